```python
import jax, jax.numpy as jnp
from jax import lax
import numpy as np

D_MODEL = 2048
BATCH = 16
SEQ = 2048
DEPTH = 1
DEC_BATCH = 16
DEC_SEQ = 16
PAST_LEN = 2048

CHUNK = 64
EPS = 1e-6
HG_DK = 128
HG_DIM = D_MODEL // 2
HG_HEADS = HG_DIM // HG_DK
HG_DV = HG_DIM // HG_HEADS
HG_BLOCK = 32
AT_HD = 64
AT_HEADS = (D_MODEL // 2) // AT_HD
AT_KV_HEADS = 2
AT_GROUP = AT_HEADS // AT_KV_HEADS
AT_Q_DIM = AT_HEADS * AT_HD
AT_KV_DIM = AT_KV_HEADS * AT_HD
WINDOW = 128
W_CHUNKS = WINDOW // CHUNK
ROPE_DIM = AT_HD // 4
ROPE_THETA = 500000.0
IN_DIM = 4 * HG_DIM + AT_Q_DIM + 2 * AT_KV_DIM + 2 * D_MODEL
N_EXPERTS = 256
TOP_K = 8
N_GROUPS = 8
TOPK_GROUPS = 4
D_EXPERT = D_MODEL // 4
D_SHARED = D_MODEL // 4
ROUTED_SCALE = 2.5
MOE_BLOCK = 128

kernel_name = 'hybrid_hgrn2_swa_sink_moe_stream_step'


def rms_norm(x, w):
    xf = x.astype(jnp.float32)
    y = xf * lax.rsqrt(jnp.mean(xf * xf, axis=-1, keepdims=True) + EPS)
    return (y * w.astype(jnp.float32)).astype(x.dtype)


def partial_rope(x, positions):
    half = ROPE_DIM // 2
    inv_freq = ROPE_THETA ** (-jnp.arange(0, ROPE_DIM, 2, dtype=jnp.float32) / ROPE_DIM)
    ang = positions.astype(jnp.float32)[:, None] * inv_freq[None, :]
    cos = jnp.cos(ang)[None, :, None, :]
    sin = jnp.sin(ang)[None, :, None, :]
    xr = x[..., :ROPE_DIM].astype(jnp.float32)
    x1, x2 = xr[..., :half], xr[..., half:]
    rot = jnp.concatenate([x1 * cos - x2 * sin, x2 * cos + x1 * sin], axis=-1)
    return jnp.concatenate([rot.astype(x.dtype), x[..., ROPE_DIM:]], axis=-1)


def hgrn2_recurrence(q, k, v, g, s0):
    B, T, H, _ = q.shape
    blk = HG_BLOCK if T % HG_BLOCK == 0 else T
    nb = T // blk

    def to_blocks(a):
        return a.reshape(B, nb, blk, H, a.shape[-1]).transpose(1, 0, 3, 2, 4)

    tril = jnp.tril(jnp.ones((blk, blk), dtype=bool))

    def step(S, inp):
        qb, kb, vb, gb = inp
        A = jnp.cumsum(gb, axis=2)
        o = jnp.einsum('bhtc,bhcv->bhtv', qb * jnp.exp(A), S)
        diff = A[:, :, :, None, :] - A[:, :, None, :, :]
        decay = jnp.exp(jnp.where(tril[:, :, None], diff, -jnp.inf))
        scores = jnp.einsum('bhtc,bhsc,bhtsc->bhts', qb, kb, decay)
        o = o + jnp.einsum('bhts,bhsv->bhtv', scores, vb)
        a_last = A[:, :, -1:, :]
        S = jnp.exp(a_last[:, :, 0, :])[..., None] * S + jnp.einsum('bhsc,bhsv->bhcv', kb * jnp.exp(a_last - A), vb)
        return S, o

    S, o = lax.scan(step, s0, (to_blocks(q), to_blocks(k), to_blocks(v), to_blocks(g)))
    o = o.transpose(1, 0, 3, 2, 4).reshape(B, T, H, v.shape[-1])
    return o, S


def sink_softmax(s, sinks):
    sk = jnp.broadcast_to(sinks.astype(jnp.float32).reshape(AT_KV_HEADS, AT_GROUP, 1, 1), s.shape[:-1] + (1,))
    return jax.nn.softmax(jnp.concatenate([s, sk], axis=-1), axis=-1)[..., :-1]


def swa_prompt(q, k, v, sinks):
    B, T, _, _ = q.shape
    nc = T // CHUNK
    qc = q.reshape(B, nc, CHUNK, AT_KV_HEADS, AT_GROUP, AT_HD)

    def band(a):
        ac = a.reshape(B, nc, CHUNK, AT_KV_HEADS, AT_HD)
        ap = jnp.concatenate([jnp.zeros((B, W_CHUNKS, CHUNK, AT_KV_HEADS, AT_HD), a.dtype), ac], axis=1)
        return jnp.concatenate([ap[:, j:j + nc] for j in range(W_CHUNKS + 1)], axis=2)

    kb, vb = band(k), band(v)
    valid = (jnp.arange(nc)[:, None] + jnp.arange(W_CHUNKS + 1)[None, :] - W_CHUNKS) >= 0
    valid = jnp.repeat(valid, CHUNK, axis=1)
    s = jnp.einsum('bnqkgd,bnskd->bnkgqs', qc, kb).astype(jnp.float32) * (AT_HD ** -0.5)
    s = jnp.where(valid[None, :, None, None, None, :], s, -jnp.inf)
    p = sink_softmax(s, sinks)
    o = jnp.einsum('bnkgqs,bnskd->bnqkgd', p.astype(vb.dtype), vb)
    return o.reshape(B, T, AT_HEADS, AT_HD)


def swa_sample(q, k, v, past_k, past_v, sinks):
    B, T, _, _ = q.shape
    keys = jnp.concatenate([past_k.astype(k.dtype), k], axis=1)
    vals = jnp.concatenate([past_v.astype(v.dtype), v], axis=1)
    qg = q.reshape(B, T, AT_KV_HEADS, AT_GROUP, AT_HD)
    s = jnp.einsum('bqkgd,bskd->bkgqs', qg, keys).astype(jnp.float32) * (AT_HD ** -0.5)
    p = sink_softmax(s, sinks)
    o = jnp.einsum('bkgqs,bskd->bqkgd', p.astype(vals.dtype), vals).reshape(B, T, AT_HEADS, AT_HD)
    return o, keys[:, -WINDOW:], vals[:, -WINDOW:]


def token_mixer(h, positions, s0, past_k, past_v, lb, w_in, hg_norm_w, q_norm_w, k_norm_w, sinks, w_hg_out, w_at_out, w_o):
    B, T, _ = h.shape
    sizes = [HG_DIM, HG_DIM, HG_DIM, HG_DIM, AT_Q_DIM, AT_KV_DIM, AT_KV_DIM, D_MODEL, D_MODEL]
    z = h @ w_in
    hq, hf, hi, hgate, aq, ak, av, gate_a, gate_b = jnp.split(z, [int(c) for c in np.cumsum(sizes)[:-1]], axis=-1)
    f = lb + (1.0 - lb) * jax.nn.sigmoid(hf.astype(jnp.float32))
    g = jnp.log(f)
    kk = 1.0 - f
    shp_k = (B, T, HG_HEADS, HG_DK)
    shp_v = (B, T, HG_HEADS, HG_DV)
    o_hg, s_new = hgrn2_recurrence(hq.astype(jnp.float32).reshape(shp_k), kk.reshape(shp_k),
                                   hi.astype(jnp.float32).reshape(shp_v), g.reshape(shp_k), s0.astype(jnp.float32))
    o_hg = rms_norm(o_hg, hg_norm_w) * jax.nn.silu(hgate.astype(jnp.float32).reshape(shp_v))
    o_hg = o_hg.reshape(B, T, HG_DIM).astype(h.dtype)
    q = partial_rope(rms_norm(aq.reshape(B, T, AT_HEADS, AT_HD), q_norm_w), positions)
    k = partial_rope(rms_norm(ak.reshape(B, T, AT_KV_HEADS, AT_HD), k_norm_w), positions)
    v = av.reshape(B, T, AT_KV_HEADS, AT_HD)
    if past_k is None:
        o_at = swa_prompt(q, k, v, sinks)
        k_keep, v_keep = k[:, -WINDOW:], v[:, -WINDOW:]
    else:
        o_at, k_keep, v_keep = swa_sample(q, k, v, past_k, past_v, sinks)
    o_at = o_at.reshape(B, T, AT_Q_DIM)
    merged = jax.nn.sigmoid(gate_a) * (o_hg @ w_hg_out) + jax.nn.sigmoid(gate_b) * (o_at @ w_at_out)
    return merged @ w_o, s_new.astype(h.dtype), k_keep, v_keep


def route(h, w_router, router_bias):
    n = h.shape[0]
    scores = jax.nn.sigmoid((h @ w_router).astype(jnp.float32))
    choice = scores + router_bias.astype(jnp.float32)
    grp = choice.reshape(n, N_GROUPS, N_EXPERTS // N_GROUPS)
    grp_score = lax.top_k(grp, 2)[0].sum(-1)
    _, gidx = lax.top_k(grp_score, TOPK_GROUPS)
    gmask = jax.nn.one_hot(gidx, N_GROUPS, dtype=jnp.float32).sum(1) > 0
    emask = jnp.repeat(gmask, N_EXPERTS // N_GROUPS, axis=-1)
    _, eidx = lax.top_k(jnp.where(emask, choice, -jnp.inf), TOP_K)
    w = jnp.take_along_axis(scores, eidx, axis=-1)
    w = w / jnp.sum(w, axis=-1, keepdims=True) * ROUTED_SCALE
    return eidx, w


def routed_experts(h, eidx, ew, w1, w3, w2):
    n, d = h.shape
    m = n * TOP_K
    bm = max(8, min(MOE_BLOCK, m // N_EXPERTS))
    nb = -(-m // bm) + N_EXPERTS
    e_flat = eidx.reshape(m)
    order = jnp.argsort(e_flat)
    e_sorted = e_flat[order]
    counts = jnp.bincount(e_flat, length=N_EXPERTS)
    start = jnp.cumsum(counts) - counts
    pcounts = (counts + bm - 1) // bm * bm
    pend = jnp.cumsum(pcounts)
    pstart = pend - pcounts
    dest = pstart[e_sorted] + jnp.arange(m) - start[e_sorted]
    row_tok = jnp.full((nb * bm,), n, jnp.int32).at[dest].set((order // TOP_K).astype(jnp.int32))
    row_w = jnp.zeros((nb * bm,), h.dtype).at[dest].set(ew.reshape(m)[order].astype(h.dtype))
    block_e = jnp.searchsorted(pend, jnp.arange(nb) * bm, side='right')
    block_valid = block_e < N_EXPERTS
    block_e = jnp.minimum(block_e, N_EXPERTS - 1)
    h_pad = jnp.concatenate([h, jnp.zeros((1, d), h.dtype)], axis=0)

    def step(y, blk):
        e, valid, toks, wts = blk

        def compute(y):
            xb = h_pad[toks]
            out = (jax.nn.silu(xb @ w1[e]) * (xb @ w3[e])) @ w2[e]
            return y.at[toks].add(out * wts[:, None])

        return lax.cond(valid, compute, lambda y: y, y), None

    y0 = jnp.zeros((n + 1, d), h.dtype)
    y, _ = lax.scan(step, y0, (block_e, block_valid, row_tok.reshape(nb, bm), row_w.reshape(nb, bm)))
    return y[:n]


def moe_ffn(h, w_router, router_bias, w1, w3, w2, ws1, ws3, ws2):
    eidx, ew = route(h, w_router, router_bias)
    shared = (jax.nn.silu(h @ ws1) * (h @ ws3)) @ ws2
    return shared + routed_experts(h, eidx, ew, w1, w3, w2)


def trunk_layer(x, c, positions, s0, past_k, past_v, lb, norm1_w, norm2_w, w_ada, b_ada, w_in, hg_norm_w, q_norm_w,
                k_norm_w, sinks, w_hg_out, w_at_out, w_o, w_router, router_bias, w1, w3, w2, ws1, ws3, ws2):
    B, T, D = x.shape
    mod = (jax.nn.silu(c) @ w_ada + b_ada)[:, None, :]
    sh1, sc1, g1, sh2, sc2, g2 = jnp.split(mod, 6, axis=-1)
    h = rms_norm(x, norm1_w) * (1 + sc1) + sh1
    mix, s_new, k_keep, v_keep = token_mixer(h, positions, s0, past_k, past_v, lb, w_in, hg_norm_w, q_norm_w, k_norm_w,
                                             sinks, w_hg_out, w_at_out, w_o)
    x = x + g1 * mix
    h = rms_norm(x, norm2_w) * (1 + sc2) + sh2
    ff = moe_ffn(h.reshape(B * T, D), w_router, router_bias, w1, w3, w2, ws1, ws3, ws2).reshape(B, T, D)
    x = x + g2 * ff
    return x, s_new, k_keep, v_keep


def setup_inputs(seed: int = 0) -> dict:
    key = jax.random.key(seed)
    ks = jax.random.split(key, 32)
    f32 = jnp.float32
    nrm = lambda k, shape, s: jax.random.normal(k, shape, f32) * s
    return {
        'x_prompt': nrm(ks[0], (BATCH, SEQ, D_MODEL), 1.0),
        'x_sample': nrm(ks[1], (DEC_BATCH, DEC_SEQ, D_MODEL), 1.0),
        'cache_k': nrm(ks[2], (DEPTH, DEC_BATCH, WINDOW, AT_KV_HEADS, AT_HD), 1.0),
        'cache_v': nrm(ks[3], (DEPTH, DEC_BATCH, WINDOW, AT_KV_HEADS, AT_HD), 1.0),
        'state_hgrn': nrm(ks[4], (DEPTH, DEC_BATCH, HG_HEADS, HG_DK, HG_DV), 0.5),
        'c_prompt': nrm(ks[5], (BATCH, D_MODEL), 1.0),
        'c_sample': nrm(ks[6], (DEC_BATCH, D_MODEL), 1.0),
        'norm1_w': 1.0 + nrm(ks[7], (DEPTH, D_MODEL), 0.02),
        'norm2_w': 1.0 + nrm(ks[8], (DEPTH, D_MODEL), 0.02),
        'w_ada': nrm(ks[9], (DEPTH, D_MODEL, 6 * D_MODEL), 0.5 * D_MODEL ** -0.5),
        'b_ada': nrm(ks[10], (DEPTH, 6 * D_MODEL), 0.01),
        'w_in': nrm(ks[11], (DEPTH, D_MODEL, IN_DIM), D_MODEL ** -0.5),
        'hg_lower_bounds': nrm(ks[12], (DEPTH + 1, HG_DIM), 0.1),
        'hg_norm_w': 1.0 + nrm(ks[13], (DEPTH, HG_DV), 0.02),
        'q_norm_w': 1.0 + nrm(ks[14], (DEPTH, AT_HD), 0.02),
        'k_norm_w': 1.0 + nrm(ks[15], (DEPTH, AT_HD), 0.02),
        'attn_sinks': nrm(ks[16], (DEPTH, AT_HEADS), 0.5),
        'w_hg_out': nrm(ks[17], (DEPTH, HG_DIM, D_MODEL), HG_DIM ** -0.5),
        'w_at_out': nrm(ks[18], (DEPTH, AT_Q_DIM, D_MODEL), AT_Q_DIM ** -0.5),
        'w_o': nrm(ks[19], (DEPTH, D_MODEL, D_MODEL), D_MODEL ** -0.5),
        'w_router': nrm(ks[20], (DEPTH, D_MODEL, N_EXPERTS), D_MODEL ** -0.5),
        'router_bias': nrm(ks[21], (DEPTH, N_EXPERTS), 0.01),
        'w_exp_gate': nrm(ks[22], (DEPTH, N_EXPERTS, D_MODEL, D_EXPERT), D_MODEL ** -0.5),
        'w_exp_up': nrm(ks[23], (DEPTH, N_EXPERTS, D_MODEL, D_EXPERT), D_MODEL ** -0.5),
        'w_exp_down': nrm(ks[24], (DEPTH, N_EXPERTS, D_EXPERT, D_MODEL), D_EXPERT ** -0.5),
        'w_sh_gate': nrm(ks[25], (DEPTH, D_MODEL, D_SHARED), D_MODEL ** -0.5),
        'w_sh_up': nrm(ks[26], (DEPTH, D_MODEL, D_SHARED), D_MODEL ** -0.5),
        'w_sh_down': nrm(ks[27], (DEPTH, D_SHARED, D_MODEL), D_SHARED ** -0.5),
    }


def reference(x_prompt, x_sample, cache_k, cache_v, state_hgrn, c_prompt, c_sample, norm1_w, norm2_w, w_ada, b_ada,
              w_in, hg_lower_bounds, hg_norm_w, q_norm_w, k_norm_w, attn_sinks, w_hg_out, w_at_out, w_o, w_router,
              router_bias, w_exp_gate, w_exp_up, w_exp_down, w_sh_gate, w_sh_up, w_sh_down):
    lbs = jnp.cumsum(jax.nn.softmax(hg_lower_bounds.astype(jnp.float32), axis=0), axis=0)
    pos_p = jnp.arange(x_prompt.shape[1], dtype=jnp.int32)
    pos_s = PAST_LEN + jnp.arange(x_sample.shape[1], dtype=jnp.int32)
    xp, xs = x_prompt, x_sample
    kp_l, vp_l, sp_l, ks_l, vs_l, ss_l = [], [], [], [], [], []
    for l in range(DEPTH):
        s0_p = jnp.zeros((xp.shape[0], HG_HEADS, HG_DK, HG_DV), jnp.float32)
        xp, sp, kp, vp = trunk_layer(xp, c_prompt, pos_p, s0_p, None, None, lbs[l], norm1_w[l], norm2_w[l], w_ada[l],
                                     b_ada[l], w_in[l], hg_norm_w[l], q_norm_w[l], k_norm_w[l], attn_sinks[l],
                                     w_hg_out[l], w_at_out[l], w_o[l], w_router[l], router_bias[l], w_exp_gate[l],
                                     w_exp_up[l], w_exp_down[l], w_sh_gate[l], w_sh_up[l], w_sh_down[l])
        xs, ss, ks, vs = trunk_layer(xs, c_sample, pos_s, state_hgrn[l], cache_k[l], cache_v[l], lbs[l], norm1_w[l],
                                     norm2_w[l], w_ada[l], b_ada[l], w_in[l], hg_norm_w[l], q_norm_w[l], k_norm_w[l],
                                     attn_sinks[l], w_hg_out[l], w_at_out[l], w_o[l], w_router[l], router_bias[l],
                                     w_exp_gate[l], w_exp_up[l], w_exp_down[l], w_sh_gate[l], w_sh_up[l], w_sh_down[l])
        kp_l.append(kp); vp_l.append(vp); sp_l.append(sp)
        ks_l.append(ks); vs_l.append(vs); ss_l.append(ss)
    new_k_prompt = jnp.stack(kp_l)
    new_v_prompt = jnp.stack(vp_l)
    new_state_prompt = jnp.stack(sp_l)
    new_k_sample = jnp.stack(ks_l)
    new_v_sample = jnp.stack(vs_l)
    new_state_sample = jnp.stack(ss_l)
    return (xp, xs, new_k_prompt, new_v_prompt, new_state_prompt, new_k_sample, new_v_sample, new_state_sample)
```

```python
import functools
import math

import numpy as np
import jax
import jax.numpy as jnp
from jax import lax
from jax.experimental import pallas as pl
from jax.experimental.pallas import tpu as pltpu

EPS = 1e-6
CHUNK = 64
HG_DK = 128
AT_HD = 64
ROPE_DIM = 16
ROPE_THETA = 500000.0
TOP_K = 8
N_GROUPS = 8
TOPK_GROUPS = 4
ROUTED_SCALE = 2.5
PAST_LEN = 2048

LANES = 128
MOE_ROWS = 256
TOK_TILE = 128
VMEM_LIMIT = 56 * 1024 * 1024

F32 = jnp.float32
BF16 = jnp.bfloat16


def _cparams(semantics, vmem=VMEM_LIMIT):
    return pltpu.CompilerParams(dimension_semantics=semantics, vmem_limit_bytes=vmem)


def _sigmoid(x):
    return 1.0 / (1.0 + jnp.exp(-x))


def _silu(x):
    return x * _sigmoid(x)


def _dot(a, b):
    return jnp.dot(a, b, preferred_element_type=F32)


def _dot_nt(a, b):
    return lax.dot_general(a, b, (((1,), (1,)), ((), ())), preferred_element_type=F32)


def _dot_tn(a, b):
    return lax.dot_general(a, b, (((0,), (0,)), ((), ())), preferred_element_type=F32)


def _split_bf16(x):
    hi = x.astype(BF16)
    lo = (x - hi.astype(F32)).astype(BF16)
    return hi, lo


def _seq_rows(m, rows):
    s, _, d = m.shape
    if s == 1:
        return m[0]
    return jnp.broadcast_to(m, (s, rows // s, d)).reshape(rows, d)


def _mod_spec(tm, seq_len, d):
    if tm <= seq_len:
        return pl.BlockSpec((1, 1, d), lambda i: ((i * tm) // seq_len, 0, 0))
    s = tm // seq_len
    return pl.BlockSpec((s, 1, d), lambda i: (i, 0, 0))


def _ada_kernel(c_ref, w_ref, b_ref, o_ref):
    s = _silu(c_ref[...]).astype(BF16)
    o_ref[...] = _dot(s, w_ref[...].astype(BF16)) + b_ref[...]


def _ada(c, w, b):
    n, d = c.shape
    m = w.shape[1]
    tn = min(m, 1024)
    return pl.pallas_call(
        _ada_kernel,
        grid=(m // tn,),
        in_specs=[pl.BlockSpec((n, d), lambda j: (0, 0)),
                  pl.BlockSpec((d, tn), lambda j: (0, j)),
                  pl.BlockSpec((1, tn), lambda j: (0, j))],
        out_specs=pl.BlockSpec((n, tn), lambda j: (0, j)),
        out_shape=jax.ShapeDtypeStruct((n, m), F32),
        compiler_params=_cparams(("parallel",)),
        name="ada_mod",
    )(c, w, b.reshape(1, m))


def _norm_mod_kernel(x_ref, w_ref, sc_ref, sh_ref, o_ref):
    x = x_ref[...]
    tm = x.shape[0]
    y = x * lax.rsqrt(jnp.mean(x * x, axis=-1, keepdims=True) + EPS) * w_ref[...]
    o_ref[...] = (y * (1.0 + _seq_rows(sc_ref[...], tm)) + _seq_rows(sh_ref[...], tm)).astype(o_ref.dtype)


def _norm_mod(x, w, sc, sh, seq_len, tm):
    n, d = x.shape
    return pl.pallas_call(
        _norm_mod_kernel,
        grid=(n // tm,),
        in_specs=[pl.BlockSpec((tm, d), lambda i: (i, 0)),
                  pl.BlockSpec((1, d), lambda i: (0, 0)),
                  _mod_spec(tm, seq_len, d), _mod_spec(tm, seq_len, d)],
        out_specs=pl.BlockSpec((tm, d), lambda i: (i, 0)),
        out_shape=jax.ShapeDtypeStruct((n, d), BF16),
        compiler_params=_cparams(("parallel",)),
        name="norm1_mod",
    )(x, w.reshape(1, d), sc, sh)


def _mm_kernel(x_ref, w_ref, o_ref, *, act):
    y = _dot(x_ref[...], w_ref[...])
    if act == "sigmoid":
        y = _sigmoid(y)
    o_ref[...] = y.astype(o_ref.dtype)


def _matmul(x, w, tm, tn, act=None, out_dtype=F32, name="matmul"):
    n, k = x.shape
    m = w.shape[1]
    return pl.pallas_call(
        functools.partial(_mm_kernel, act=act),
        grid=(n // tm, m // tn),
        in_specs=[pl.BlockSpec((tm, k), lambda i, j: (i, 0)),
                  pl.BlockSpec((k, tn), lambda i, j: (0, j))],
        out_specs=pl.BlockSpec((tm, tn), lambda i, j: (i, j)),
        out_shape=jax.ShapeDtypeStruct((n, m), out_dtype),
        compiler_params=_cparams(("parallel", "arbitrary")),
        name=name,
    )(x, w)


def _hgrn_tables(c):
    nlev = int(math.log2(c))
    t = np.arange(c)[:, None]
    s = np.arange(c)[None, :]
    seg = []
    for l in range(1, nlev + 1):
        b = 1 << l
        seg.append(((s >= (t // b) * b) & (s <= t)).astype(np.float32))
        seg.append(((s > t) & (s <= (t // b) * b + b - 1)).astype(np.float32))
    masks = [(t == s).astype(np.float32)]
    for l in range(nlev):
        b = 1 << l
        masks.append(((t // (2 * b) == s // (2 * b)) & (t % (2 * b) >= b) & (s % (2 * b) < b)).astype(np.float32))
    return np.concatenate(seg, axis=0), np.stack(masks, axis=0)


def _hgrn_kernel(q_ref, f_ref, i_ref, g_ref, lb_ref, nw_ref, s0_ref, seg_ref, msk_ref,
                 o_ref, sn_ref, st_ref, *, c, nchunks):
    nlev = int(math.log2(c))
    tstep = pl.program_id(2)

    @pl.when(tstep == 0)
    def _():
        st_ref[...] = s0_ref[0, 0].T

    lb = lb_ref[...]
    nw = nw_ref[...]
    seg = seg_ref[...]

    def chunk(ci, carry):
        r0 = pl.multiple_of(ci * c, c)
        q = q_ref[pl.ds(r0, c), :]
        f = lb + (1.0 - lb) * _sigmoid(f_ref[pl.ds(r0, c), :])
        v = i_ref[pl.ds(r0, c), :]
        gate = g_ref[pl.ds(r0, c), :]
        g = jnp.log(f)
        k = 1.0 - f
        g_hi, g_lo = _split_bf16(g)
        ex = _dot(seg, g_hi) + _dot(seg, g_lo)

        def w_of(l):
            return g if l == 0 else ex[(2 * (l - 1)) * c:(2 * (l - 1) + 1) * c, :]

        def v_of(l):
            return None if l == 0 else ex[(2 * (l - 1) + 1) * c:(2 * l) * c, :]

        qb = q.astype(BF16)
        kb = k.astype(BF16)
        scores = jnp.where(msk_ref[0] > 0, _dot_nt(qb, kb), 0.0)
        for l in range(nlev):
            ql = (q * jnp.exp(w_of(l))).astype(BF16)
            kl = kb if l == 0 else (k * jnp.exp(v_of(l))).astype(BF16)
            scores = scores + jnp.where(msk_ref[l + 1] > 0, _dot_nt(ql, kl), 0.0)
        a_inc = w_of(nlev)
        st = st_ref[...]
        vb = v.astype(BF16)
        o = _dot_nt((q * jnp.exp(a_inc)).astype(BF16), st.astype(BF16)) + _dot(scores.astype(BF16), vb)
        k_end = (k * jnp.exp(v_of(nlev))).astype(BF16)
        st_ref[...] = st * jnp.exp(a_inc[c - 1:c, :]) + _dot_tn(vb, k_end)
        on = o * lax.rsqrt(jnp.mean(o * o, axis=-1, keepdims=True) + EPS) * nw
        o_ref[pl.ds(r0, c), :] = (on * _silu(gate)).astype(o_ref.dtype)
        return carry

    lax.fori_loop(0, nchunks, chunk, 0)

    @pl.when(tstep == pl.num_programs(2) - 1)
    def _():
        sn_ref[0, 0] = st_ref[...].T


def _hgrn(z_hg, lb, norm_w, s0, bsz, seq_len):
    n, w4 = z_hg.shape
    nh = w4 // (4 * HG_DK)
    c = min(CHUNK, seq_len)
    tb = min(seq_len, 512)
    nt = seq_len // tb
    seg, masks = _hgrn_tables(c)

    def col(part):
        return pl.BlockSpec((tb, HG_DK), lambda b, h, t: (b * nt + t, part * nh + h))

    return pl.pallas_call(
        functools.partial(_hgrn_kernel, c=c, nchunks=tb // c),
        grid=(bsz, nh, nt),
        in_specs=[col(0), col(1), col(2), col(3),
                  pl.BlockSpec((1, HG_DK), lambda b, h, t: (0, h)),
                  pl.BlockSpec((1, HG_DK), lambda b, h, t: (0, 0)),
                  pl.BlockSpec((1, 1, HG_DK, HG_DK), lambda b, h, t: (b, h, 0, 0)),
                  pl.BlockSpec(seg.shape, lambda b, h, t: (0, 0)),
                  pl.BlockSpec(masks.shape, lambda b, h, t: (0, 0, 0))],
        out_specs=[pl.BlockSpec((tb, HG_DK), lambda b, h, t: (b * nt + t, h)),
                   pl.BlockSpec((1, 1, HG_DK, HG_DK), lambda b, h, t: (b, h, 0, 0))],
        out_shape=[jax.ShapeDtypeStruct((n, nh * HG_DK), BF16),
                   jax.ShapeDtypeStruct((bsz, nh, HG_DK, HG_DK), F32)],
        scratch_shapes=[pltpu.VMEM((HG_DK, HG_DK), F32)],
        compiler_params=_cparams(("parallel", "parallel", "arbitrary")),
        name="hgrn2",
    )(z_hg, z_hg, z_hg, z_hg, lb.reshape(1, -1), norm_w.reshape(1, HG_DK), s0,
      jnp.asarray(seg, BF16), jnp.asarray(masks, F32))


def _rope_tables(positions):
    half = ROPE_DIM // 2
    inv_freq = ROPE_THETA ** (-jnp.arange(0, ROPE_DIM, 2, dtype=F32) / ROPE_DIM)
    ang = positions.astype(F32)[:, None] * inv_freq[None, :]
    cos, sin = jnp.cos(ang), jnp.sin(ang)
    t = positions.shape[0]
    rest = AT_HD - ROPE_DIM
    c64 = jnp.concatenate([cos, cos, jnp.ones((t, rest), F32)], axis=1)
    sa64 = jnp.concatenate([-sin, jnp.zeros((t, half + rest), F32)], axis=1)
    sb64 = jnp.concatenate([jnp.zeros((t, half), F32), sin, jnp.zeros((t, rest), F32)], axis=1)
    rep = LANES // AT_HD
    return tuple(jnp.tile(a, (1, rep)) for a in (c64, sa64, sb64))


def _qk_prep_kernel(z_ref, cos_ref, sa_ref, sb_ref, qw_ref, kw_ref, bd_ref, q_ref, kv_ref, *, qd, kd):
    half = ROPE_DIM // 2
    z = z_ref[...]
    cos, sa, sb = cos_ref[...], sa_ref[...], sb_ref[...]

    def norm_rope(x, w, bd, reps):
        x2 = x * x
        hi, lo = _split_bf16(x2)
        ss = _dot(hi, bd) + _dot(lo, bd)
        xn = x * lax.rsqrt(ss * (1.0 / AT_HD) + EPS) * w
        width = x.shape[1]
        tile = lambda a: jnp.concatenate([a] * reps, axis=1) if reps > 1 else a
        return (xn * tile(cos) + pltpu.roll(xn, width - half, 1) * tile(sa)
                + pltpu.roll(xn, half, 1) * tile(sb))

    q = norm_rope(z[:, :qd], qw_ref[...], bd_ref[...], qd // LANES)
    k = norm_rope(z[:, qd:qd + kd], kw_ref[...], bd_ref[:kd, :kd], kd // LANES)
    q_ref[...] = q.astype(q_ref.dtype)
    kv_ref[:, :kd] = k
    kv_ref[:, kd:] = z[:, qd + kd:]


def _qk_prep(z_at, positions, q_norm_w, k_norm_w, seq_len, qd, kd, tm):
    n, w = z_at.shape
    cos, sa, sb = _rope_tables(positions)
    nt = seq_len // tm
    bd = np.kron(np.eye(qd // AT_HD, dtype=np.float32), np.ones((AT_HD, AT_HD), np.float32))
    tab = pl.BlockSpec((tm, LANES), lambda i: (i % nt, 0))
    return pl.pallas_call(
        functools.partial(_qk_prep_kernel, qd=qd, kd=kd),
        grid=(n // tm,),
        in_specs=[pl.BlockSpec((tm, w), lambda i: (i, 0)), tab, tab, tab,
                  pl.BlockSpec((1, qd), lambda i: (0, 0)),
                  pl.BlockSpec((1, kd), lambda i: (0, 0)),
                  pl.BlockSpec((qd, qd), lambda i: (0, 0))],
        out_specs=[pl.BlockSpec((tm, qd), lambda i: (i, 0)),
                   pl.BlockSpec((tm, 2 * kd), lambda i: (i, 0))],
        out_shape=[jax.ShapeDtypeStruct((n, qd), BF16),
                   jax.ShapeDtypeStruct((n, 2 * kd), F32)],
        compiler_params=_cparams(("parallel",)),
        name="qk_norm_rope",
    )(z_at, cos, sa, sb, jnp.tile(q_norm_w, qd // AT_HD).reshape(1, qd),
      jnp.tile(k_norm_w, kd // AT_HD).reshape(1, kd), jnp.asarray(bd, BF16))


def _attend(q, keys, vals, sinks_ref, col_valid, o_ref, n_kv, group):
    tq = q.shape[0]
    for j in range(n_kv):
        kj = keys[:, j * AT_HD:(j + 1) * AT_HD]
        vj = vals[:, j * AT_HD:(j + 1) * AT_HD]
        qs = jnp.concatenate([q[:, (j * group + g) * AT_HD:(j * group + g + 1) * AT_HD] for g in range(group)], axis=0)
        s = _dot_nt(qs, kj) * (AT_HD ** -0.5)
        if col_valid is not None:
            s = jnp.where(col_valid, s, -jnp.inf)
        sink = jnp.concatenate([jnp.full((tq, 1), sinks_ref[j * group + g], F32) for g in range(group)], axis=0)
        m = jnp.maximum(jnp.max(s, axis=-1, keepdims=True), sink)
        e = jnp.exp(s - m)
        den = jnp.sum(e, axis=-1, keepdims=True) + jnp.exp(sink - m)
        o = _dot((e / den).astype(BF16), vj)
        for g in range(group):
            h = j * group + g
            o_ref[:, h * AT_HD:(h + 1) * AT_HD] = o[g * tq:(g + 1) * tq, :].astype(o_ref.dtype)


def _swa_prompt_kernel(sinks_ref, q_ref, kv0_ref, kv1_ref, kv2_ref, o_ref, *, n_kv, group, w_chunks):
    kd = n_kv * AT_HD
    n = pl.program_id(1)
    blocks = [kv0_ref[...], kv1_ref[...], kv2_ref[...]]
    keys = jnp.concatenate([b[:, :kd] for b in blocks], axis=0).astype(BF16)
    vals = jnp.concatenate([b[:, kd:] for b in blocks], axis=0).astype(BF16)
    col_chunk = lax.broadcasted_iota(jnp.int32, (1, keys.shape[0]), 1) // CHUNK
    col_valid = (col_chunk + n - w_chunks) >= 0
    _attend(q_ref[...], keys, vals, sinks_ref, col_valid, o_ref, n_kv, group)


def _swa_prompt(qn, kvn, sinks, bsz, seq_len, n_kv):
    n, qd = qn.shape
    kd = n_kv * AT_HD
    nc = seq_len // CHUNK
    group = qd // AT_HD // n_kv
    w_chunks = 2

    def kv_spec(j):
        return pl.BlockSpec((CHUNK, 2 * kd), lambda b, c, s: (b * nc + jnp.maximum(c - w_chunks + j, 0), 0))

    return pl.pallas_call(
        functools.partial(_swa_prompt_kernel, n_kv=n_kv, group=group, w_chunks=w_chunks),
        grid_spec=pltpu.PrefetchScalarGridSpec(
            num_scalar_prefetch=1,
            grid=(bsz, nc),
            in_specs=[pl.BlockSpec((CHUNK, qd), lambda b, c, s: (b * nc + c, 0)),
                      kv_spec(0), kv_spec(1), kv_spec(2)],
            out_specs=pl.BlockSpec((CHUNK, qd), lambda b, c, s: (b * nc + c, 0))),
        out_shape=jax.ShapeDtypeStruct((n, qd), BF16),
        compiler_params=_cparams(("parallel", "arbitrary")),
        name="swa_prompt",
    )(sinks, qn, kvn, kvn, kvn)


def _swa_sample_kernel(sinks_ref, q_ref, pk_ref, pv_ref, kv_ref, o_ref, *, n_kv, group):
    kd = n_kv * AT_HD
    kv = kv_ref[...]
    keys = jnp.concatenate([pk_ref[0], kv[:, :kd]], axis=0).astype(BF16)
    vals = jnp.concatenate([pv_ref[0], kv[:, kd:]], axis=0).astype(BF16)
    _attend(q_ref[...], keys, vals, sinks_ref, None, o_ref, n_kv, group)


def _swa_sample(qn, kvn, past_k, past_v, sinks, bsz, seq_len, n_kv):
    n, qd = qn.shape
    kd = n_kv * AT_HD
    window = past_k.shape[1]
    group = qd // AT_HD // n_kv
    return pl.pallas_call(
        functools.partial(_swa_sample_kernel, n_kv=n_kv, group=group),
        grid_spec=pltpu.PrefetchScalarGridSpec(
            num_scalar_prefetch=1,
            grid=(bsz,),
            in_specs=[pl.BlockSpec((seq_len, qd), lambda b, s: (b, 0)),
                      pl.BlockSpec((1, window, kd), lambda b, s: (b, 0, 0)),
                      pl.BlockSpec((1, window, kd), lambda b, s: (b, 0, 0)),
                      pl.BlockSpec((seq_len, 2 * kd), lambda b, s: (b, 0))],
            out_specs=pl.BlockSpec((seq_len, qd), lambda b, s: (b, 0))),
        out_shape=jax.ShapeDtypeStruct((n, qd), BF16),
        compiler_params=_cparams(("parallel",)),
        name="swa_sample",
    )(sinks, qn, past_k, past_v, kvn)


def _store_slabs(ref, x):
    rows, width = x.shape
    nslab = width // LANES
    for s in range(nslab):
        ref[pl.ds(s, rows, stride=nslab), :] = x[:, s * LANES:(s + 1) * LANES]


def _load_slabs(ref, rows, nslab, row0=0):
    return jnp.concatenate([ref[pl.ds(row0 * nslab + s, rows, stride=nslab), :] for s in range(nslab)], axis=1)


def _merge_kernel(ohg_ref, oat_ref, ga_ref, gb_ref, x_ref, g1_ref, whg_ref, wat_ref, wo_ref,
                  nw_ref, sc_ref, sh_ref, x1_ref, h2_ref):
    tm = x_ref.shape[0]
    merged = ga_ref[...] * _dot(ohg_ref[...], whg_ref[...]) + gb_ref[...] * _dot(oat_ref[...], wat_ref[...])
    mix = _dot(merged.astype(BF16), wo_ref[...])
    x1 = x_ref[...] + _seq_rows(g1_ref[...], tm) * mix
    x1_ref[...] = x1
    y = x1 * lax.rsqrt(jnp.mean(x1 * x1, axis=-1, keepdims=True) + EPS) * nw_ref[...]
    h2 = y * (1.0 + _seq_rows(sc_ref[...], tm)) + _seq_rows(sh_ref[...], tm)
    _store_slabs(h2_ref, h2)


def _merge(o_hg, o_at, gates, x, g1, w_hg_out, w_at_out, w_o, norm2_w, sc2, sh2, seq_len, tm):
    n, d = x.shape
    hd = o_hg.shape[1]
    ad = o_at.shape[1]
    nslab = d // LANES
    const = lambda shape: pl.BlockSpec(shape, lambda i: (0,) * len(shape), pipeline_mode=pl.Buffered(1))
    mod = lambda: _mod_spec(tm, seq_len, d)
    return pl.pallas_call(
        _merge_kernel,
        grid=(n // tm,),
        in_specs=[pl.BlockSpec((tm, hd), lambda i: (i, 0)),
                  pl.BlockSpec((tm, ad), lambda i: (i, 0)),
                  pl.BlockSpec((tm, d), lambda i: (i, 0)),
                  pl.BlockSpec((tm, d), lambda i: (i, 1)),
                  pl.BlockSpec((tm, d), lambda i: (i, 0)),
                  mod(), const((hd, d)), const((ad, d)), const((d, d)), const((1, d)), mod(), mod()],
        out_specs=[pl.BlockSpec((tm, d), lambda i: (i, 0)),
                   pl.BlockSpec((tm * nslab, LANES), lambda i: (i, 0))],
        out_shape=[jax.ShapeDtypeStruct((n, d), F32),
                   jax.ShapeDtypeStruct((n * nslab, LANES), F32)],
        compiler_params=_cparams(("parallel",)),
        name="merge_norm2",
    )(o_hg, o_at, gates, gates, x, g1, w_hg_out, w_at_out, w_o, norm2_w.reshape(1, d), sc2, sh2)


def _router_kernel(h_ref, wr_ref, bias_ref, tri_ref, eidx_ref, wts_ref, rank_ref, cnt_ref, run_ref,
                   *, nslab, n_exp):
    tm = h_ref.shape[0] // nslab
    gsz = n_exp // N_GROUPS
    step = pl.program_id(0)

    @pl.when(step == 0)
    def _():
        run_ref[...] = jnp.zeros_like(run_ref)

    h = _load_slabs(h_ref, tm, nslab).astype(BF16)
    scores = _sigmoid(_dot_nt(wr_ref[...], h))
    choice = scores + bias_ref[...]
    neg = -jnp.inf
    row = lax.broadcasted_iota(jnp.int32, (gsz, tm), 0)

    gscore = []
    for gi in range(N_GROUPS):
        cg = choice[gi * gsz:(gi + 1) * gsz, :]
        m1 = jnp.max(cg, axis=0, keepdims=True)
        i1 = jnp.min(jnp.where(cg == m1, row, gsz), axis=0, keepdims=True)
        m2 = jnp.max(jnp.where(row == i1, neg, cg), axis=0, keepdims=True)
        gscore.append(m1 + m2)
    gs = jnp.concatenate(gscore, axis=0)
    grow = lax.broadcasted_iota(jnp.int32, (N_GROUPS, tm), 0)
    gsel = jnp.zeros((N_GROUPS, tm), F32)
    for _ in range(TOPK_GROUPS):
        gm = jnp.max(gs, axis=0, keepdims=True)
        gi = jnp.min(jnp.where(gs == gm, grow, N_GROUPS), axis=0, keepdims=True)
        hit = grow == gi
        gsel = jnp.where(hit, 1.0, gsel)
        gs = jnp.where(hit, neg, gs)
    masked = jnp.concatenate(
        [jnp.where(gsel[gi:gi + 1, :] > 0, choice[gi * gsz:(gi + 1) * gsz, :], neg) for gi in range(N_GROUPS)],
        axis=0)

    erow = lax.broadcasted_iota(jnp.int32, (n_exp, tm), 0)
    idxs, raw = [], []
    for _ in range(TOP_K):
        m = jnp.max(masked, axis=0, keepdims=True)
        i = jnp.min(jnp.where(masked == m, erow, n_exp), axis=0, keepdims=True)
        hit = erow == i
        raw.append(jnp.sum(jnp.where(hit, scores, 0.0), axis=0, keepdims=True))
        masked = jnp.where(hit, neg, masked)
        idxs.append(i)
    total = raw[0]
    for r in raw[1:]:
        total = total + r
    onehot = jnp.zeros((n_exp, tm), F32)
    for i in idxs:
        onehot = onehot + jnp.where(erow == i, 1.0, 0.0)
    before = _dot(onehot.astype(BF16), tri_ref[...]) + run_ref[:, 0:1]
    for kk in range(TOP_K):
        eidx_ref[kk:kk + 1, :] = idxs[kk]
        wts_ref[kk:kk + 1, :] = raw[kk] / total * ROUTED_SCALE
        rank_ref[kk:kk + 1, :] = jnp.sum(jnp.where(erow == idxs[kk], before, 0.0), axis=0, keepdims=True).astype(jnp.int32)
    run_ref[...] = run_ref[...] + jnp.sum(onehot, axis=1, keepdims=True)

    @pl.when(step == pl.num_programs(0) - 1)
    def _():
        cnt_ref[...] = run_ref[...].astype(jnp.int32)


def _router(h2s, w_router, router_bias, n_tok, d, tm):
    n_exp = w_router.shape[1]
    nslab = d // LANES
    tri = np.triu(np.ones((tm, tm), np.float32), 1)
    out_tok = lambda dt: jax.ShapeDtypeStruct((TOP_K, n_tok), dt)
    tok_spec = pl.BlockSpec((TOP_K, tm), lambda i: (0, i))
    return pl.pallas_call(
        functools.partial(_router_kernel, nslab=nslab, n_exp=n_exp),
        grid=(n_tok // tm,),
        in_specs=[pl.BlockSpec((tm * nslab, LANES), lambda i: (i, 0)),
                  pl.BlockSpec((n_exp, d), lambda i: (0, 0)),
                  pl.BlockSpec((n_exp, 1), lambda i: (0, 0)),
                  pl.BlockSpec((tm, tm), lambda i: (0, 0))],
        out_specs=[tok_spec, tok_spec, tok_spec, pl.BlockSpec((n_exp, LANES), lambda i: (0, 0))],
        out_shape=[out_tok(jnp.int32), out_tok(F32), out_tok(jnp.int32),
                   jax.ShapeDtypeStruct((n_exp, LANES), jnp.int32)],
        scratch_shapes=[pltpu.VMEM((n_exp, LANES), F32)],
        compiler_params=_cparams(("arbitrary",)),
        name="router_topk",
    )(h2s, w_router.T.astype(BF16), router_bias.reshape(n_exp, 1), jnp.asarray(tri, BF16))


def _shared_kernel(h_ref, w1_ref, w3_ref, w2_ref, o_ref, *, nslab):
    tm = h_ref.shape[0] // nslab
    h = _load_slabs(h_ref, tm, nslab).astype(BF16)
    a = (_silu(_dot(h, w1_ref[...])) * _dot(h, w3_ref[...])).astype(BF16)
    o_ref[...] = _dot(a, w2_ref[...])


def _shared_expert(h2s, ws1, ws3, ws2, n_tok, d, tm):
    nslab = d // LANES
    ds = ws1.shape[1]
    return pl.pallas_call(
        functools.partial(_shared_kernel, nslab=nslab),
        grid=(n_tok // tm,),
        in_specs=[pl.BlockSpec((tm * nslab, LANES), lambda i: (i, 0)),
                  pl.BlockSpec((d, ds), lambda i: (0, 0)),
                  pl.BlockSpec((d, ds), lambda i: (0, 0)),
                  pl.BlockSpec((ds, d), lambda i: (0, 0))],
        out_specs=pl.BlockSpec((tm, d), lambda i: (i, 0)),
        out_shape=jax.ShapeDtypeStruct((n_tok, d), F32),
        compiler_params=_cparams(("parallel",)),
        name="shared_expert",
    )(h2s, ws1, ws3, ws2)


def _dispatch_kernel(slot_hbm, h_ref, init_ref, xs_hbm, slot_smem, ssem, dsem):
    del init_ref
    i = pl.program_id(0)
    cp = pltpu.make_async_copy(slot_hbm.at[i], slot_smem, ssem)
    cp.start()
    cp.wait()
    nslab = xs_hbm.shape[1]
    tt = h_ref.shape[0] // nslab

    def row_copy(t, kk):
        src = h_ref.at[pl.ds(pl.multiple_of(t * nslab, nslab), nslab), :]
        return pltpu.make_async_copy(src, xs_hbm.at[slot_smem[kk, t]], dsem)

    def issue(t, carry):
        for kk in range(TOP_K):
            row_copy(t, kk).start()
        return carry

    lax.fori_loop(0, tt, issue, 0)

    def drain(t, carry):
        for kk in range(TOP_K):
            row_copy(t, kk).wait()
        return carry

    lax.fori_loop(0, tt, drain, 0)


def _dispatch(slots, h2s, n_tok, nslab, n_rows):
    xs0 = jnp.zeros((n_rows, nslab, LANES), F32)
    return pl.pallas_call(
        _dispatch_kernel,
        grid=(n_tok // TOK_TILE,),
        in_specs=[pl.BlockSpec(memory_space=pl.ANY),
                  pl.BlockSpec((TOK_TILE * nslab, LANES), lambda i: (i, 0)),
                  pl.BlockSpec(memory_space=pl.ANY)],
        out_specs=pl.BlockSpec(memory_space=pl.ANY),
        out_shape=jax.ShapeDtypeStruct((n_rows, nslab, LANES), F32),
        scratch_shapes=[pltpu.SMEM((TOP_K, TOK_TILE), jnp.int32),
                        pltpu.SemaphoreType.DMA, pltpu.SemaphoreType.DMA],
        input_output_aliases={2: 0},
        compiler_params=_cparams(("arbitrary",)),
        name="moe_dispatch",
    )(slots, h2s, xs0)


def _experts_kernel(be_ref, nu_ref, x_ref, w1_ref, w3_ref, w2_ref, o_ref, w1b, w3b, w2b, *, nslab):
    r = pl.program_id(0)
    bm = x_ref.shape[0] // nslab

    @pl.when(r < nu_ref[0])
    def _():
        prev = be_ref[jnp.maximum(r - 1, 0)]

        @pl.when(jnp.logical_or(r == 0, be_ref[r] != prev))
        def _():
            w1b[...] = w1_ref[0].astype(BF16)
            w3b[...] = w3_ref[0].astype(BF16)
            w2b[...] = w2_ref[0].astype(BF16)

        x = _load_slabs(x_ref, bm, nslab).astype(BF16)
        a = (_silu(_dot(x, w1b[...])) * _dot(x, w3b[...])).astype(BF16)
        _store_slabs(o_ref, _dot(a, w2b[...]))


def _experts(block_e, n_used, xs2, w1, w3, w2, d, bm):
    n_exp, _, de = w1.shape
    nslab = d // LANES
    nb = xs2.shape[0] // (bm * nslab)
    row_spec = pl.BlockSpec((bm * nslab, LANES), lambda r, be, nu: (jnp.minimum(r, nu[0] - 1), 0))
    return pl.pallas_call(
        functools.partial(_experts_kernel, nslab=nslab),
        grid_spec=pltpu.PrefetchScalarGridSpec(
            num_scalar_prefetch=2,
            grid=(nb,),
            in_specs=[row_spec,
                      pl.BlockSpec((1, d, de), lambda r, be, nu: (be[r], 0, 0)),
                      pl.BlockSpec((1, d, de), lambda r, be, nu: (be[r], 0, 0)),
                      pl.BlockSpec((1, de, d), lambda r, be, nu: (be[r], 0, 0))],
            out_specs=row_spec,
            scratch_shapes=[pltpu.VMEM((d, de), BF16), pltpu.VMEM((d, de), BF16), pltpu.VMEM((de, d), BF16)]),
        out_shape=jax.ShapeDtypeStruct(xs2.shape, F32),
        compiler_params=_cparams(("arbitrary",)),
        name="routed_experts",
    )(block_e, n_used, xs2, w1, w3, w2)


def _combine_kernel(slot_hbm, os_hbm, wt_ref, sh_ref, x1_ref, g2_ref, o_ref, slot_smem, buf, ssem, gsem,
                    *, nslab, tile0):
    i = pl.program_id(0)
    cp = pltpu.make_async_copy(slot_hbm.at[tile0 + i], slot_smem, ssem)
    cp.start()
    cp.wait()
    tt = x1_ref.shape[0]

    def row_copy(t, kk):
        dst = buf.at[pl.ds(pl.multiple_of((kk * tt + t) * nslab, nslab), nslab), :]
        return pltpu.make_async_copy(os_hbm.at[slot_smem[kk, t]], dst, gsem)

    def issue(t, carry):
        for kk in range(TOP_K):
            row_copy(t, kk).start()
        return carry

    lax.fori_loop(0, tt, issue, 0)

    def drain(t, carry):
        for kk in range(TOP_K):
            row_copy(t, kk).wait()
        return carry

    lax.fori_loop(0, tt, drain, 0)

    wt = wt_ref[...]
    g2 = _seq_rows(g2_ref[...], tt)
    for s in range(nslab):
        cols = slice(s * LANES, (s + 1) * LANES)
        y = sh_ref[:, cols]
        for kk in range(TOP_K):
            y = y + wt[:, kk:kk + 1] * buf[pl.ds(kk * tt * nslab + s, tt, stride=nslab), :]
        o_ref[:, cols] = x1_ref[:, cols] + g2[:, cols] * y


def _combine(slots, os3, wts_t, shared, x1, g2, seq_len, tok0):
    n, d = x1.shape
    nslab = d // LANES
    tt = TOK_TILE
    tile0 = tok0 // tt
    return pl.pallas_call(
        functools.partial(_combine_kernel, nslab=nslab, tile0=tile0),
        grid=(n // tt,),
        in_specs=[pl.BlockSpec(memory_space=pl.ANY),
                  pl.BlockSpec(memory_space=pl.ANY),
                  pl.BlockSpec((tt, TOP_K), lambda i: (tile0 + i, 0)),
                  pl.BlockSpec((tt, d), lambda i: (tile0 + i, 0)),
                  pl.BlockSpec((tt, d), lambda i: (i, 0)),
                  _mod_spec(tt, seq_len, d)],
        out_specs=pl.BlockSpec((tt, d), lambda i: (i, 0)),
        out_shape=jax.ShapeDtypeStruct((n, d), F32),
        scratch_shapes=[pltpu.SMEM((TOP_K, tt), jnp.int32),
                        pltpu.VMEM((TOP_K * tt * nslab, LANES), F32),
                        pltpu.SemaphoreType.DMA, pltpu.SemaphoreType.DMA],
        compiler_params=_cparams(("arbitrary",)),
        name="moe_combine",
    )(slots, os3, wts_t, shared, x1, g2)


def _mixer(x, mod, positions, s0, past_k, past_v, lb, p, w):
    bsz, seq_len, d = x.shape
    n = bsz * seq_len
    sh1, sc1, g1, sh2, sc2, g2 = mod
    tm = 256 if n % 256 == 0 else n
    x2 = x.reshape(n, d)
    h1 = _norm_mod(x2, p["norm1_w"], sc1, sh1, seq_len, tm)
    tmm = 1024 if n % 1024 == 0 else tm
    hg_w = w["w_in_hg"].shape[1]
    at_w = w["w_in_at"].shape[1]
    z_hg = _matmul(h1, w["w_in_hg"], tmm, min(hg_w, 1024), name="w_in_hgrn")
    z_at = _matmul(h1, w["w_in_at"], tmm, at_w, name="w_in_attn")
    gates = _matmul(h1, w["w_in_gate"], tmm, min(2 * d, 1024), act="sigmoid", name="w_in_gates")

    o_hg, s_new = _hgrn(z_hg, lb, p["hg_norm_w"], s0, bsz, seq_len)

    n_kv = p["n_kv"]
    kd = n_kv * AT_HD
    qd = at_w - 2 * kd
    tq = min(seq_len, 256)
    qn, kvn = _qk_prep(z_at, positions, p["q_norm_w"], p["k_norm_w"], seq_len, qd, kd, tq)
    if past_k is None:
        o_at = _swa_prompt(qn, kvn, p["attn_sinks"], bsz, seq_len, n_kv)
    else:
        o_at = _swa_sample(qn, kvn, past_k, past_v, p["attn_sinks"], bsz, seq_len, n_kv)

    x1, h2s = _merge(o_hg, o_at, gates, x2, g1, w["w_hg_out"], w["w_at_out"], w["w_o"],
                     p["norm2_w"], sc2, sh2, seq_len, tm)
    kv3 = kvn.reshape(bsz, seq_len, 2 * kd)
    return x1, h2s, s_new, kv3[:, :, :kd], kv3[:, :, kd:]


def kernel(x_prompt, x_sample, cache_k, cache_v, state_hgrn, c_prompt, c_sample, norm1_w, norm2_w, w_ada, b_ada,
           w_in, hg_lower_bounds, hg_norm_w, q_norm_w, k_norm_w, attn_sinks, w_hg_out, w_at_out, w_o, w_router,
           router_bias, w_exp_gate, w_exp_up, w_exp_down, w_sh_gate, w_sh_up, w_sh_down):
    depth = norm1_w.shape[0]
    assert depth == 1, "single trunk layer"
    bp, tp, d = x_prompt.shape
    bs, ts, _ = x_sample.shape
    window, n_kv = cache_k.shape[2], cache_k.shape[3]
    kd = n_kv * AT_HD
    hg_dim = w_hg_out.shape[1]
    qd = w_at_out.shape[1]
    n_exp = w_router.shape[2]
    nslab = d // LANES
    l = 0

    lbs = jnp.cumsum(jax.nn.softmax(hg_lower_bounds.astype(F32), axis=0), axis=0)
    win = w_in[l]
    w = {
        "w_in_hg": win[:, :4 * hg_dim].astype(BF16),
        "w_in_at": win[:, 4 * hg_dim:4 * hg_dim + qd + 2 * kd].astype(BF16),
        "w_in_gate": win[:, 4 * hg_dim + qd + 2 * kd:].astype(BF16),
        "w_hg_out": w_hg_out[l].astype(BF16),
        "w_at_out": w_at_out[l].astype(BF16),
        "w_o": w_o[l].astype(BF16),
    }
    p = {"norm1_w": norm1_w[l], "norm2_w": norm2_w[l], "hg_norm_w": hg_norm_w[l], "q_norm_w": q_norm_w[l],
         "k_norm_w": k_norm_w[l], "attn_sinks": attn_sinks[l], "n_kv": n_kv}

    c_all = jnp.concatenate([c_prompt, c_sample], axis=0)
    mod_all = _ada(c_all, w_ada[l], b_ada[l])
    mod_all = mod_all.reshape(bp + bs, 6, 1, d)
    mod_p = tuple(mod_all[:bp, j] for j in range(6))
    mod_s = tuple(mod_all[bp:, j] for j in range(6))

    pos_p = jnp.arange(tp, dtype=jnp.int32)
    pos_s = PAST_LEN + jnp.arange(ts, dtype=jnp.int32)
    s0_p = jnp.zeros((bp,) + state_hgrn.shape[2:], F32)
    x1_p, h2_p, sp, kp, vp = _mixer(x_prompt, mod_p, pos_p, s0_p, None, None, lbs[l], p, w)
    pk = cache_k[l].reshape(bs, window, kd)
    pv = cache_v[l].reshape(bs, window, kd)
    x1_s, h2_s, ss, ks, vs = _mixer(x_sample, mod_s, pos_s, state_hgrn[l], pk, pv, lbs[l], p, w)

    n_p, n_s = bp * tp, bs * ts
    n_tok = n_p + n_s
    h2s = jnp.concatenate([h2_p, h2_s], axis=0)
    tr = 256 if n_tok % 256 == 0 else TOK_TILE
    eidx, wts, rank, counts = _router(h2s, w_router[l], router_bias[l], n_tok, d, tr)
    shared = _shared_expert(h2s, w_sh_gate[l].astype(BF16), w_sh_up[l].astype(BF16), w_sh_down[l].astype(BF16),
                            n_tok, d, tr)

    bm = MOE_ROWS
    counts = counts[:, 0]
    pcounts = (counts + bm - 1) // bm * bm
    pend = jnp.cumsum(pcounts)
    pstart = pend - pcounts
    nb = -(-(n_tok * TOP_K) // bm) + n_exp
    slot = pstart[eidx] + rank
    slots = slot.reshape(TOP_K, n_tok // TOK_TILE, TOK_TILE).transpose(1, 0, 2)
    block_e = jnp.minimum(jnp.searchsorted(pend, jnp.arange(nb, dtype=jnp.int32) * bm, side="right"),
                          n_exp - 1).astype(jnp.int32)
    n_used = (pend[-1:] // bm).astype(jnp.int32)

    xs = _dispatch(slots, h2s, n_tok, nslab, nb * bm)
    os_ = _experts(block_e, n_used, xs.reshape(nb * bm * nslab, LANES), w_exp_gate[l], w_exp_up[l], w_exp_down[l],
                   d, bm)
    os3 = os_.reshape(nb * bm, nslab, LANES)
    wts_t = wts.T
    y_p = _combine(slots, os3, wts_t, shared, x1_p, mod_p[5], tp, 0)
    y_s = _combine(slots, os3, wts_t, shared, x1_s, mod_s[5], ts, n_p)

    def cache_out(a, b_, t):
        return a[:, t - window:].reshape(1, b_, window, n_kv, AT_HD)

    new_k_p = cache_out(kp, bp, tp)
    new_v_p = cache_out(vp, bp, tp)
    keys_s = jnp.concatenate([pk, ks], axis=1)
    vals_s = jnp.concatenate([pv, vs], axis=1)
    new_k_s = cache_out(keys_s, bs, window + ts)
    new_v_s = cache_out(vals_s, bs, window + ts)
    return (y_p.reshape(bp, tp, d), y_s.reshape(bs, ts, d), new_k_p, new_v_p, sp[None],
            new_k_s, new_v_s, ss[None])
```

```python
import functools
import math

import numpy as np
import jax
import jax.numpy as jnp
from jax import lax
from jax.experimental import pallas as pl
from jax.experimental.pallas import tpu as pltpu

EPS = 1e-6
CHUNK = 64
HG_DK = 128
AT_HD = 64
ROPE_DIM = 16
ROPE_THETA = 500000.0
TOP_K = 8
N_GROUPS = 8
TOPK_GROUPS = 4
ROUTED_SCALE = 2.5
PAST_LEN = 2048

LANES = 128
MOE_ROWS = 256
TOK_TILE = 128
VMEM_LIMIT = 56 * 1024 * 1024

F32 = jnp.float32
BF16 = jnp.bfloat16


def _cparams(semantics, vmem=VMEM_LIMIT):
    return pltpu.CompilerParams(dimension_semantics=semantics, vmem_limit_bytes=vmem)


def _sigmoid(x):
    return 1.0 / (1.0 + jnp.exp(-x))


def _silu(x):
    return x * _sigmoid(x)


def _dot(a, b):
    return jnp.dot(a, b, preferred_element_type=F32)


def _dot_nt(a, b):
    return lax.dot_general(a, b, (((1,), (1,)), ((), ())), preferred_element_type=F32)


def _dot_tn(a, b):
    return lax.dot_general(a, b, (((0,), (0,)), ((), ())), preferred_element_type=F32)


def _split_bf16(x):
    hi = x.astype(BF16)
    lo = (x - hi.astype(F32)).astype(BF16)
    return hi, lo


def _seq_rows(m, rows):
    s, _, d = m.shape
    if s == 1:
        return m[0]
    return jnp.broadcast_to(m, (s, rows // s, d)).reshape(rows, d)


def _mod_spec(tm, seq_len, d):
    if tm <= seq_len:
        return pl.BlockSpec((1, 1, d), lambda i: ((i * tm) // seq_len, 0, 0))
    s = tm // seq_len
    return pl.BlockSpec((s, 1, d), lambda i: (i, 0, 0))


def _ada_kernel(c_ref, w_ref, b_ref, o_ref):
    s = _silu(c_ref[...]).astype(BF16)
    o_ref[...] = _dot(s, w_ref[...].astype(BF16)) + b_ref[...]


def _ada(c, w, b):
    n, d = c.shape
    m = w.shape[1]
    tn = min(m, 1024)
    return pl.pallas_call(
        _ada_kernel,
        grid=(m // tn,),
        in_specs=[pl.BlockSpec((n, d), lambda j: (0, 0)),
                  pl.BlockSpec((d, tn), lambda j: (0, j)),
                  pl.BlockSpec((1, tn), lambda j: (0, j))],
        out_specs=pl.BlockSpec((n, tn), lambda j: (0, j)),
        out_shape=jax.ShapeDtypeStruct((n, m), F32),
        compiler_params=_cparams(("parallel",)),
        name="ada_mod",
    )(c, w, b.reshape(1, m))


def _norm_mod_kernel(x_ref, w_ref, sc_ref, sh_ref, o_ref):
    x = x_ref[...]
    tm = x.shape[0]
    y = x * lax.rsqrt(jnp.mean(x * x, axis=-1, keepdims=True) + EPS) * w_ref[...]
    o_ref[...] = (y * (1.0 + _seq_rows(sc_ref[...], tm)) + _seq_rows(sh_ref[...], tm)).astype(o_ref.dtype)


def _norm_mod(x, w, sc, sh, seq_len, tm):
    n, d = x.shape
    return pl.pallas_call(
        _norm_mod_kernel,
        grid=(n // tm,),
        in_specs=[pl.BlockSpec((tm, d), lambda i: (i, 0)),
                  pl.BlockSpec((1, d), lambda i: (0, 0)),
                  _mod_spec(tm, seq_len, d), _mod_spec(tm, seq_len, d)],
        out_specs=pl.BlockSpec((tm, d), lambda i: (i, 0)),
        out_shape=jax.ShapeDtypeStruct((n, d), BF16),
        compiler_params=_cparams(("parallel",)),
        name="norm1_mod",
    )(x, w.reshape(1, d), sc, sh)


def _mm_kernel(x_ref, w_ref, o_ref, *, act):
    y = _dot(x_ref[...], w_ref[...])
    if act == "sigmoid":
        y = _sigmoid(y)
    o_ref[...] = y.astype(o_ref.dtype)


def _matmul(x, w, tm, tn, act=None, out_dtype=F32, name="matmul"):
    n, k = x.shape
    m = w.shape[1]
    return pl.pallas_call(
        functools.partial(_mm_kernel, act=act),
        grid=(n // tm, m // tn),
        in_specs=[pl.BlockSpec((tm, k), lambda i, j: (i, 0)),
                  pl.BlockSpec((k, tn), lambda i, j: (0, j))],
        out_specs=pl.BlockSpec((tm, tn), lambda i, j: (i, j)),
        out_shape=jax.ShapeDtypeStruct((n, m), out_dtype),
        compiler_params=_cparams(("parallel", "arbitrary")),
        name=name,
    )(x, w)


def _hgrn_tables(c):
    nlev = int(math.log2(c))
    t = np.arange(c)[:, None]
    s = np.arange(c)[None, :]
    seg = []
    for l in range(1, nlev + 1):
        b = 1 << l
        seg.append(((s >= (t // b) * b) & (s <= t)).astype(np.float32))
        seg.append(((s > t) & (s <= (t // b) * b + b - 1)).astype(np.float32))
    masks = [(t == s).astype(np.float32)]
    for l in range(nlev):
        b = 1 << l
        masks.append(((t // (2 * b) == s // (2 * b)) & (t % (2 * b) >= b) & (s % (2 * b) < b)).astype(np.float32))
    return np.concatenate(seg, axis=0), np.stack(masks, axis=0)


def _hgrn_kernel(q_ref, f_ref, i_ref, g_ref, lb_ref, nw_ref, s0_ref, seg_ref, msk_ref,
                 o_ref, sn_ref, st_ref, *, c, nchunks, hb):
    nlev = int(math.log2(c))
    tstep = pl.program_id(2)

    @pl.when(tstep == 0)
    def _():
        for hh in range(hb):
            st_ref[hh] = s0_ref[0, hh].T

    nw = nw_ref[...]
    seg = seg_ref[...]

    def head_chunk(r0, hh):
        cols = slice(hh * HG_DK, (hh + 1) * HG_DK)
        lb = lb_ref[:, cols]
        q = q_ref[pl.ds(r0, c), cols]
        f = lb + (1.0 - lb) * _sigmoid(f_ref[pl.ds(r0, c), cols])
        v = i_ref[pl.ds(r0, c), cols]
        gate = g_ref[pl.ds(r0, c), cols]
        g = jnp.log(f)
        k = 1.0 - f
        g_hi, g_lo = _split_bf16(g)
        ex = _dot(seg, g_hi) + _dot(seg, g_lo)

        def w_of(l):
            return g if l == 0 else ex[(2 * (l - 1)) * c:(2 * (l - 1) + 1) * c, :]

        def v_of(l):
            return None if l == 0 else ex[(2 * (l - 1) + 1) * c:(2 * l) * c, :]

        qb = q.astype(BF16)
        kb = k.astype(BF16)
        scores = jnp.where(msk_ref[0] > 0, _dot_nt(qb, kb), 0.0)
        for l in range(nlev):
            ql = (q * jnp.exp(w_of(l))).astype(BF16)
            kl = kb if l == 0 else (k * jnp.exp(v_of(l))).astype(BF16)
            scores = scores + jnp.where(msk_ref[l + 1] > 0, _dot_nt(ql, kl), 0.0)
        a_inc = w_of(nlev)
        st = st_ref[hh]
        vb = v.astype(BF16)
        o = _dot_nt((q * jnp.exp(a_inc)).astype(BF16), st.astype(BF16)) + _dot(scores.astype(BF16), vb)
        k_end = (k * jnp.exp(v_of(nlev))).astype(BF16)
        st_ref[hh] = st * jnp.exp(a_inc[c - 1:c, :]) + _dot_tn(vb, k_end)
        on = o * lax.rsqrt(jnp.mean(o * o, axis=-1, keepdims=True) + EPS) * nw
        o_ref[pl.ds(r0, c), cols] = (on * _silu(gate)).astype(o_ref.dtype)

    def chunk(ci, carry):
        r0 = pl.multiple_of(ci * c, c)
        for hh in range(hb):
            head_chunk(r0, hh)
        return carry

    lax.fori_loop(0, nchunks, chunk, 0)

    @pl.when(tstep == pl.num_programs(2) - 1)
    def _():
        for hh in range(hb):
            sn_ref[0, hh] = st_ref[hh].T


def _hgrn(z_hg, lb, norm_w, s0, bsz, seq_len):
    n, w4 = z_hg.shape
    nh = w4 // (4 * HG_DK)
    hb = min(nh, 4)
    ng = nh // hb
    c = min(CHUNK, seq_len)
    tb = min(seq_len, 512)
    nt = seq_len // tb
    seg, masks = _hgrn_tables(c)

    def col(part):
        return pl.BlockSpec((tb, hb * HG_DK), lambda b, h, t: (b * nt + t, part * ng + h))

    return pl.pallas_call(
        functools.partial(_hgrn_kernel, c=c, nchunks=tb // c, hb=hb),
        grid=(bsz, ng, nt),
        in_specs=[col(0), col(1), col(2), col(3),
                  pl.BlockSpec((1, hb * HG_DK), lambda b, h, t: (0, h)),
                  pl.BlockSpec((1, HG_DK), lambda b, h, t: (0, 0)),
                  pl.BlockSpec((1, hb, HG_DK, HG_DK), lambda b, h, t: (b, h, 0, 0)),
                  pl.BlockSpec(seg.shape, lambda b, h, t: (0, 0)),
                  pl.BlockSpec(masks.shape, lambda b, h, t: (0, 0, 0))],
        out_specs=[pl.BlockSpec((tb, hb * HG_DK), lambda b, h, t: (b * nt + t, h)),
                   pl.BlockSpec((1, hb, HG_DK, HG_DK), lambda b, h, t: (b, h, 0, 0))],
        out_shape=[jax.ShapeDtypeStruct((n, nh * HG_DK), BF16),
                   jax.ShapeDtypeStruct((bsz, nh, HG_DK, HG_DK), F32)],
        scratch_shapes=[pltpu.VMEM((hb, HG_DK, HG_DK), F32)],
        compiler_params=_cparams(("parallel", "parallel", "arbitrary")),
        name="hgrn2",
    )(z_hg, z_hg, z_hg, z_hg, lb.reshape(1, -1), norm_w.reshape(1, HG_DK), s0,
      jnp.asarray(seg, BF16), jnp.asarray(masks, F32))


def _rope_tables(positions):
    half = ROPE_DIM // 2
    inv_freq = ROPE_THETA ** (-jnp.arange(0, ROPE_DIM, 2, dtype=F32) / ROPE_DIM)
    ang = positions.astype(F32)[:, None] * inv_freq[None, :]
    cos, sin = jnp.cos(ang), jnp.sin(ang)
    t = positions.shape[0]
    rest = AT_HD - ROPE_DIM
    c64 = jnp.concatenate([cos, cos, jnp.ones((t, rest), F32)], axis=1)
    sa64 = jnp.concatenate([-sin, jnp.zeros((t, half + rest), F32)], axis=1)
    sb64 = jnp.concatenate([jnp.zeros((t, half), F32), sin, jnp.zeros((t, rest), F32)], axis=1)
    rep = LANES // AT_HD
    return tuple(jnp.tile(a, (1, rep)) for a in (c64, sa64, sb64))


def _qk_prep_kernel(z_ref, cos_ref, sa_ref, sb_ref, qw_ref, kw_ref, bd_ref, q_ref, kv_ref, *, qd, kd):
    half = ROPE_DIM // 2
    z = z_ref[...]
    cos, sa, sb = cos_ref[...], sa_ref[...], sb_ref[...]

    def norm_rope(x, w, bd, reps):
        x2 = x * x
        hi, lo = _split_bf16(x2)
        ss = _dot(hi, bd) + _dot(lo, bd)
        xn = x * lax.rsqrt(ss * (1.0 / AT_HD) + EPS) * w
        width = x.shape[1]
        tile = lambda a: jnp.concatenate([a] * reps, axis=1) if reps > 1 else a
        return (xn * tile(cos) + pltpu.roll(xn, width - half, 1) * tile(sa)
                + pltpu.roll(xn, half, 1) * tile(sb))

    q = norm_rope(z[:, :qd], qw_ref[...], bd_ref[...], qd // LANES)
    k = norm_rope(z[:, qd:qd + kd], kw_ref[...], bd_ref[:kd, :kd], kd // LANES)
    q_ref[...] = q.astype(q_ref.dtype)
    kv_ref[:, :kd] = k
    kv_ref[:, kd:] = z[:, qd + kd:]


def _qk_prep(z_at, positions, q_norm_w, k_norm_w, seq_len, qd, kd, tm):
    n, w = z_at.shape
    cos, sa, sb = _rope_tables(positions)
    nt = seq_len // tm
    bd = np.kron(np.eye(qd // AT_HD, dtype=np.float32), np.ones((AT_HD, AT_HD), np.float32))
    tab = pl.BlockSpec((tm, LANES), lambda i: (i % nt, 0))
    return pl.pallas_call(
        functools.partial(_qk_prep_kernel, qd=qd, kd=kd),
        grid=(n // tm,),
        in_specs=[pl.BlockSpec((tm, w), lambda i: (i, 0)), tab, tab, tab,
                  pl.BlockSpec((1, qd), lambda i: (0, 0)),
                  pl.BlockSpec((1, kd), lambda i: (0, 0)),
                  pl.BlockSpec((qd, qd), lambda i: (0, 0))],
        out_specs=[pl.BlockSpec((tm, qd), lambda i: (i, 0)),
                   pl.BlockSpec((tm, 2 * kd), lambda i: (i, 0))],
        out_shape=[jax.ShapeDtypeStruct((n, qd), BF16),
                   jax.ShapeDtypeStruct((n, 2 * kd), F32)],
        compiler_params=_cparams(("parallel",)),
        name="qk_norm_rope",
    )(z_at, cos, sa, sb, jnp.tile(q_norm_w, qd // AT_HD).reshape(1, qd),
      jnp.tile(k_norm_w, kd // AT_HD).reshape(1, kd), jnp.asarray(bd, BF16))


def _attend(q, keys, vals, sinks_ref, col_valid, o_ref, n_kv, group):
    tq = q.shape[0]
    for j in range(n_kv):
        kj = keys[:, j * AT_HD:(j + 1) * AT_HD]
        vj = vals[:, j * AT_HD:(j + 1) * AT_HD]
        qs = jnp.concatenate([q[:, (j * group + g) * AT_HD:(j * group + g + 1) * AT_HD] for g in range(group)], axis=0)
        s = _dot_nt(qs, kj) * (AT_HD ** -0.5)
        if col_valid is not None:
            s = jnp.where(col_valid, s, -jnp.inf)
        sink = jnp.concatenate([jnp.full((tq, 1), sinks_ref[j * group + g], F32) for g in range(group)], axis=0)
        m = jnp.maximum(jnp.max(s, axis=-1, keepdims=True), sink)
        e = jnp.exp(s - m)
        den = jnp.sum(e, axis=-1, keepdims=True) + jnp.exp(sink - m)
        o = _dot((e / den).astype(BF16), vj)
        for g in range(group):
            h = j * group + g
            o_ref[:, h * AT_HD:(h + 1) * AT_HD] = o[g * tq:(g + 1) * tq, :].astype(o_ref.dtype)


def _swa_prompt_kernel(sinks_ref, q_ref, kv0_ref, kv1_ref, kv2_ref, o_ref, *, n_kv, group, w_chunks):
    kd = n_kv * AT_HD
    n = pl.program_id(1)
    blocks = [kv0_ref[...], kv1_ref[...], kv2_ref[...]]
    keys = jnp.concatenate([b[:, :kd] for b in blocks], axis=0).astype(BF16)
    vals = jnp.concatenate([b[:, kd:] for b in blocks], axis=0).astype(BF16)
    col_chunk = lax.broadcasted_iota(jnp.int32, (1, keys.shape[0]), 1) // CHUNK
    col_valid = (col_chunk + n - w_chunks) >= 0
    _attend(q_ref[...], keys, vals, sinks_ref, col_valid, o_ref, n_kv, group)


def _swa_prompt(qn, kvn, sinks, bsz, seq_len, n_kv):
    n, qd = qn.shape
    kd = n_kv * AT_HD
    nc = seq_len // CHUNK
    group = qd // AT_HD // n_kv
    w_chunks = 2

    def kv_spec(j):
        return pl.BlockSpec((CHUNK, 2 * kd), lambda b, c, s: (b * nc + jnp.maximum(c - w_chunks + j, 0), 0))

    return pl.pallas_call(
        functools.partial(_swa_prompt_kernel, n_kv=n_kv, group=group, w_chunks=w_chunks),
        grid_spec=pltpu.PrefetchScalarGridSpec(
            num_scalar_prefetch=1,
            grid=(bsz, nc),
            in_specs=[pl.BlockSpec((CHUNK, qd), lambda b, c, s: (b * nc + c, 0)),
                      kv_spec(0), kv_spec(1), kv_spec(2)],
            out_specs=pl.BlockSpec((CHUNK, qd), lambda b, c, s: (b * nc + c, 0))),
        out_shape=jax.ShapeDtypeStruct((n, qd), BF16),
        compiler_params=_cparams(("parallel", "arbitrary")),
        name="swa_prompt",
    )(sinks, qn, kvn, kvn, kvn)


def _swa_sample_kernel(sinks_ref, q_ref, pk_ref, pv_ref, kv_ref, o_ref, *, n_kv, group):
    kd = n_kv * AT_HD
    kv = kv_ref[...]
    keys = jnp.concatenate([pk_ref[0], kv[:, :kd]], axis=0).astype(BF16)
    vals = jnp.concatenate([pv_ref[0], kv[:, kd:]], axis=0).astype(BF16)
    _attend(q_ref[...], keys, vals, sinks_ref, None, o_ref, n_kv, group)


def _swa_sample(qn, kvn, past_k, past_v, sinks, bsz, seq_len, n_kv):
    n, qd = qn.shape
    kd = n_kv * AT_HD
    window = past_k.shape[1]
    group = qd // AT_HD // n_kv
    return pl.pallas_call(
        functools.partial(_swa_sample_kernel, n_kv=n_kv, group=group),
        grid_spec=pltpu.PrefetchScalarGridSpec(
            num_scalar_prefetch=1,
            grid=(bsz,),
            in_specs=[pl.BlockSpec((seq_len, qd), lambda b, s: (b, 0)),
                      pl.BlockSpec((1, window, kd), lambda b, s: (b, 0, 0)),
                      pl.BlockSpec((1, window, kd), lambda b, s: (b, 0, 0)),
                      pl.BlockSpec((seq_len, 2 * kd), lambda b, s: (b, 0))],
            out_specs=pl.BlockSpec((seq_len, qd), lambda b, s: (b, 0))),
        out_shape=jax.ShapeDtypeStruct((n, qd), BF16),
        compiler_params=_cparams(("parallel",)),
        name="swa_sample",
    )(sinks, qn, past_k, past_v, kvn)


HI_MASK = np.uint32(0xFFFF0000)


def _pack_rows(x):
    half = x.shape[1] // 2
    bits = lambda a: lax.bitcast_convert_type(a.astype(BF16).astype(F32), jnp.uint32)
    word = (bits(x[:, half:]) & HI_MASK) | (bits(x[:, :half]) >> 16)
    return lax.bitcast_convert_type(word, jnp.int32)


def _unpack_words(w):
    u = lax.bitcast_convert_type(w, jnp.uint32)
    return lax.bitcast_convert_type(u << 16, F32), lax.bitcast_convert_type(u & HI_MASK, F32)


def _store_slabs(ref, words):
    rows, width = words.shape
    nslab = width // LANES
    for s in range(nslab):
        ref[pl.ds(s, rows, stride=nslab), :] = words[:, s * LANES:(s + 1) * LANES]


def _load_rows(ref, rows, nslab):
    lo, hi = [], []
    for s in range(nslab):
        a, b = _unpack_words(ref[pl.ds(s, rows, stride=nslab), :])
        lo.append(a)
        hi.append(b)
    return jnp.concatenate(lo + hi, axis=1).astype(BF16)


def _merge_kernel(ohg_ref, oat_ref, ga_ref, gb_ref, x_ref, g1_ref, whg_ref, wat_ref, wo_ref,
                  nw_ref, sc_ref, sh_ref, x1_ref, h2_ref):
    tm = x_ref.shape[0]
    merged = ga_ref[...] * _dot(ohg_ref[...], whg_ref[...]) + gb_ref[...] * _dot(oat_ref[...], wat_ref[...])
    mix = _dot(merged.astype(BF16), wo_ref[...])
    x1 = x_ref[...] + _seq_rows(g1_ref[...], tm) * mix
    x1_ref[...] = x1
    y = x1 * lax.rsqrt(jnp.mean(x1 * x1, axis=-1, keepdims=True) + EPS) * nw_ref[...]
    h2 = y * (1.0 + _seq_rows(sc_ref[...], tm)) + _seq_rows(sh_ref[...], tm)
    _store_slabs(h2_ref, _pack_rows(h2))


def _merge(o_hg, o_at, gates, x, g1, w_hg_out, w_at_out, w_o, norm2_w, sc2, sh2, seq_len, tm):
    n, d = x.shape
    hd = o_hg.shape[1]
    ad = o_at.shape[1]
    nslab = d // (2 * LANES)
    const = lambda shape: pl.BlockSpec(shape, lambda i: (0,) * len(shape), pipeline_mode=pl.Buffered(1))
    mod = lambda: _mod_spec(tm, seq_len, d)
    return pl.pallas_call(
        _merge_kernel,
        grid=(n // tm,),
        in_specs=[pl.BlockSpec((tm, hd), lambda i: (i, 0)),
                  pl.BlockSpec((tm, ad), lambda i: (i, 0)),
                  pl.BlockSpec((tm, d), lambda i: (i, 0)),
                  pl.BlockSpec((tm, d), lambda i: (i, 1)),
                  pl.BlockSpec((tm, d), lambda i: (i, 0)),
                  mod(), const((hd, d)), const((ad, d)), const((d, d)), const((1, d)), mod(), mod()],
        out_specs=[pl.BlockSpec((tm, d), lambda i: (i, 0)),
                   pl.BlockSpec((tm * nslab, LANES), lambda i: (i, 0))],
        out_shape=[jax.ShapeDtypeStruct((n, d), F32),
                   jax.ShapeDtypeStruct((n * nslab, LANES), jnp.int32)],
        compiler_params=_cparams(("parallel",)),
        name="merge_norm2",
    )(o_hg, o_at, gates, gates, x, g1, w_hg_out, w_at_out, w_o, norm2_w.reshape(1, d), sc2, sh2)


def _router_kernel(h_ref, wr_ref, bias_ref, tri_ref, eidx_ref, wts_ref, rank_ref, cnt_ref, run_ref,
                   *, nslab, n_exp):
    tm = h_ref.shape[0] // nslab
    gsz = n_exp // N_GROUPS
    step = pl.program_id(0)

    @pl.when(step == 0)
    def _():
        run_ref[...] = jnp.zeros_like(run_ref)

    h = _load_rows(h_ref, tm, nslab)
    scores = _sigmoid(_dot_nt(wr_ref[...], h))
    choice = scores + bias_ref[...]
    neg = -jnp.inf
    row = lax.broadcasted_iota(jnp.int32, (gsz, tm), 0)

    gscore = []
    for gi in range(N_GROUPS):
        cg = choice[gi * gsz:(gi + 1) * gsz, :]
        m1 = jnp.max(cg, axis=0, keepdims=True)
        i1 = jnp.min(jnp.where(cg == m1, row, gsz), axis=0, keepdims=True)
        m2 = jnp.max(jnp.where(row == i1, neg, cg), axis=0, keepdims=True)
        gscore.append(m1 + m2)
    gs = jnp.concatenate(gscore, axis=0)
    grow = lax.broadcasted_iota(jnp.int32, (N_GROUPS, tm), 0)
    gsel = jnp.zeros((N_GROUPS, tm), F32)
    for _ in range(TOPK_GROUPS):
        gm = jnp.max(gs, axis=0, keepdims=True)
        gi = jnp.min(jnp.where(gs == gm, grow, N_GROUPS), axis=0, keepdims=True)
        hit = grow == gi
        gsel = jnp.where(hit, 1.0, gsel)
        gs = jnp.where(hit, neg, gs)
    masked = jnp.concatenate(
        [jnp.where(gsel[gi:gi + 1, :] > 0, choice[gi * gsz:(gi + 1) * gsz, :], neg) for gi in range(N_GROUPS)],
        axis=0)

    erow = lax.broadcasted_iota(jnp.int32, (n_exp, tm), 0)
    idxs, raw = [], []
    for _ in range(TOP_K):
        m = jnp.max(masked, axis=0, keepdims=True)
        i = jnp.min(jnp.where(masked == m, erow, n_exp), axis=0, keepdims=True)
        hit = erow == i
        raw.append(jnp.sum(jnp.where(hit, scores, 0.0), axis=0, keepdims=True))
        masked = jnp.where(hit, neg, masked)
        idxs.append(i)
    total = raw[0]
    for r in raw[1:]:
        total = total + r
    onehot = jnp.zeros((n_exp, tm), F32)
    for i in idxs:
        onehot = onehot + jnp.where(erow == i, 1.0, 0.0)
    before = _dot(onehot.astype(BF16), tri_ref[...]) + run_ref[:, 0:1]
    for kk in range(TOP_K):
        eidx_ref[kk:kk + 1, :] = idxs[kk]
        wts_ref[kk:kk + 1, :] = raw[kk] / total * ROUTED_SCALE
        rank_ref[kk:kk + 1, :] = jnp.sum(jnp.where(erow == idxs[kk], before, 0.0), axis=0, keepdims=True).astype(jnp.int32)
    run_ref[...] = run_ref[...] + jnp.sum(onehot, axis=1, keepdims=True)

    @pl.when(step == pl.num_programs(0) - 1)
    def _():
        cnt_ref[...] = run_ref[...].astype(jnp.int32)


def _slot_kernel(eidx_ref, rank_ref, pstart_ref, slot_ref):
    n_exp = pstart_ref.shape[0]
    tt = eidx_ref.shape[1]
    erow = lax.broadcasted_iota(jnp.int32, (n_exp, tt), 0)
    pstart = pstart_ref[:, 0:1]
    for kk in range(TOP_K):
        base = jnp.sum(jnp.where(erow == eidx_ref[kk:kk + 1, :], pstart, 0), axis=0, keepdims=True)
        slot_ref[0, kk:kk + 1, :] = base + rank_ref[kk:kk + 1, :]


def _slots(eidx, rank, pstart, n_tok):
    n_exp = pstart.shape[0]
    tt = TOK_TILE
    tok_spec = pl.BlockSpec((TOP_K, tt), lambda i: (0, i))
    return pl.pallas_call(
        _slot_kernel,
        grid=(n_tok // tt,),
        in_specs=[tok_spec, tok_spec, pl.BlockSpec((n_exp, LANES), lambda i: (0, 0))],
        out_specs=pl.BlockSpec((1, TOP_K, tt), lambda i: (i, 0, 0)),
        out_shape=jax.ShapeDtypeStruct((n_tok // tt, TOP_K, tt), jnp.int32),
        compiler_params=_cparams(("parallel",)),
        name="moe_slots",
    )(eidx, rank, jnp.broadcast_to(pstart[:, None], (n_exp, LANES)))


def _router(h2s, w_router, router_bias, n_tok, d, tm):
    n_exp = w_router.shape[1]
    nslab = d // (2 * LANES)
    tri = np.triu(np.ones((tm, tm), np.float32), 1)
    out_tok = lambda dt: jax.ShapeDtypeStruct((TOP_K, n_tok), dt)
    tok_spec = pl.BlockSpec((TOP_K, tm), lambda i: (0, i))
    return pl.pallas_call(
        functools.partial(_router_kernel, nslab=nslab, n_exp=n_exp),
        grid=(n_tok // tm,),
        in_specs=[pl.BlockSpec((tm * nslab, LANES), lambda i: (i, 0)),
                  pl.BlockSpec((n_exp, d), lambda i: (0, 0)),
                  pl.BlockSpec((n_exp, 1), lambda i: (0, 0)),
                  pl.BlockSpec((tm, tm), lambda i: (0, 0))],
        out_specs=[tok_spec, tok_spec, tok_spec, pl.BlockSpec((n_exp, LANES), lambda i: (0, 0))],
        out_shape=[out_tok(jnp.int32), out_tok(F32), out_tok(jnp.int32),
                   jax.ShapeDtypeStruct((n_exp, LANES), jnp.int32)],
        scratch_shapes=[pltpu.VMEM((n_exp, LANES), F32)],
        compiler_params=_cparams(("arbitrary",)),
        name="router_topk",
    )(h2s, w_router.T.astype(BF16), router_bias.reshape(n_exp, 1), jnp.asarray(tri, BF16))


def _shared_kernel(h_ref, w1_ref, w3_ref, w2_ref, o_ref, *, nslab):
    tm = h_ref.shape[0] // nslab
    h = _load_rows(h_ref, tm, nslab)
    a = (_silu(_dot(h, w1_ref[...])) * _dot(h, w3_ref[...])).astype(BF16)
    o_ref[...] = _dot(a, w2_ref[...])


def _shared_expert(h2s, ws1, ws3, ws2, n_tok, d, tm):
    nslab = d // (2 * LANES)
    ds = ws1.shape[1]
    return pl.pallas_call(
        functools.partial(_shared_kernel, nslab=nslab),
        grid=(n_tok // tm,),
        in_specs=[pl.BlockSpec((tm * nslab, LANES), lambda i: (i, 0)),
                  pl.BlockSpec((d, ds), lambda i: (0, 0)),
                  pl.BlockSpec((d, ds), lambda i: (0, 0)),
                  pl.BlockSpec((ds, d), lambda i: (0, 0))],
        out_specs=pl.BlockSpec((tm, d), lambda i: (i, 0)),
        out_shape=jax.ShapeDtypeStruct((n_tok, d), F32),
        compiler_params=_cparams(("parallel",)),
        name="shared_expert",
    )(h2s, ws1, ws3, ws2)


def _dispatch_kernel(cnt_ref, pstart_ref, slot_hbm, h_ref, xs_hbm, slot_smem, zbuf, ssem, dsem, zsem, *, bm):
    i = pl.program_id(0)
    nslab = xs_hbm.shape[1]
    tt = h_ref.shape[0] // nslab
    n_exp = cnt_ref.shape[0]

    @pl.when(i == 0)
    def _():
        zbuf[...] = jnp.zeros_like(zbuf)

        def walk(e, start):
            cnt = cnt_ref[e]
            pad = lax.rem(bm - lax.rem(cnt, bm), bm)
            base = pstart_ref[e] + cnt
            size = bm // 2
            while size >= 1:
                take = (pad & size) != 0

                @pl.when(take)
                def _(base=base, size=size):
                    cp = pltpu.make_async_copy(zbuf.at[pl.ds(0, size)], xs_hbm.at[pl.ds(base, size)], zsem)
                    if start:
                        cp.start()
                    else:
                        cp.wait()

                base = base + jnp.where(take, size, 0)
                size //= 2

        def start_e(e, carry):
            walk(e, True)
            return carry

        def wait_e(e, carry):
            walk(e, False)
            return carry

        lax.fori_loop(0, n_exp, start_e, 0)
        lax.fori_loop(0, n_exp, wait_e, 0)

    cp = pltpu.make_async_copy(slot_hbm.at[i], slot_smem, ssem)
    cp.start()
    cp.wait()

    def row_copy(t, kk):
        src = h_ref.at[pl.ds(pl.multiple_of(t * nslab, nslab), nslab), :]
        return pltpu.make_async_copy(src, xs_hbm.at[slot_smem[kk, t]], dsem)

    def issue(t, carry):
        for kk in range(TOP_K):
            row_copy(t, kk).start()
        return carry

    lax.fori_loop(0, tt, issue, 0)

    def drain(t, carry):
        for kk in range(TOP_K):
            row_copy(t, kk).wait()
        return carry

    lax.fori_loop(0, tt, drain, 0)


def _dispatch(counts, pstart, slots, h2s, n_tok, nslab, n_rows, bm):
    return pl.pallas_call(
        functools.partial(_dispatch_kernel, bm=bm),
        grid_spec=pltpu.PrefetchScalarGridSpec(
            num_scalar_prefetch=2,
            grid=(n_tok // TOK_TILE,),
            in_specs=[pl.BlockSpec(memory_space=pl.ANY),
                      pl.BlockSpec((TOK_TILE * nslab, LANES), lambda i, c, p: (i, 0))],
            out_specs=pl.BlockSpec(memory_space=pl.ANY),
            scratch_shapes=[pltpu.SMEM((TOP_K, TOK_TILE), jnp.int32),
                            pltpu.VMEM((bm // 2, nslab, LANES), jnp.int32),
                            pltpu.SemaphoreType.DMA, pltpu.SemaphoreType.DMA, pltpu.SemaphoreType.DMA]),
        out_shape=jax.ShapeDtypeStruct((n_rows, nslab, LANES), jnp.int32),
        compiler_params=_cparams(("arbitrary",)),
        name="moe_dispatch",
    )(counts, pstart, slots, h2s)


def _experts_kernel(be_ref, nu_ref, x_ref, w1_ref, w3_ref, w2_ref, o_ref, w1b, w3b, w2b, *, nslab):
    r = pl.program_id(0)
    bm = x_ref.shape[0] // nslab

    @pl.when(r < nu_ref[0])
    def _():
        prev = be_ref[jnp.maximum(r - 1, 0)]

        @pl.when(jnp.logical_or(r == 0, be_ref[r] != prev))
        def _():
            w1b[...] = w1_ref[0].astype(BF16)
            w3b[...] = w3_ref[0].astype(BF16)
            w2b[...] = w2_ref[0].astype(BF16)

        x = _load_rows(x_ref, bm, nslab)
        a = (_silu(_dot(x, w1b[...])) * _dot(x, w3b[...])).astype(BF16)
        _store_slabs(o_ref, _pack_rows(_dot(a, w2b[...])))


def _experts(block_e, n_used, xs2, w1, w3, w2, d, bm):
    n_exp, _, de = w1.shape
    nslab = d // (2 * LANES)
    nb = xs2.shape[0] // (bm * nslab)
    row_spec = pl.BlockSpec((bm * nslab, LANES), lambda r, be, nu: (jnp.minimum(r, nu[0] - 1), 0))
    return pl.pallas_call(
        functools.partial(_experts_kernel, nslab=nslab),
        grid_spec=pltpu.PrefetchScalarGridSpec(
            num_scalar_prefetch=2,
            grid=(nb,),
            in_specs=[row_spec,
                      pl.BlockSpec((1, d, de), lambda r, be, nu: (be[r], 0, 0)),
                      pl.BlockSpec((1, d, de), lambda r, be, nu: (be[r], 0, 0)),
                      pl.BlockSpec((1, de, d), lambda r, be, nu: (be[r], 0, 0))],
            out_specs=row_spec,
            scratch_shapes=[pltpu.VMEM((d, de), BF16), pltpu.VMEM((d, de), BF16), pltpu.VMEM((de, d), BF16)]),
        out_shape=jax.ShapeDtypeStruct(xs2.shape, jnp.int32),
        compiler_params=_cparams(("arbitrary",)),
        name="routed_experts",
    )(block_e, n_used, xs2, w1, w3, w2)


def _combine_kernel(slot_hbm, os_hbm, wt_ref, sh_ref, x1_ref, g2_ref, o_ref, slot_smem, buf, ssem, gsem,
                    *, nslab, tile0):
    i = pl.program_id(0)
    tt = x1_ref.shape[0]
    rows = TOP_K * tt * nslab

    def row_copy(half, t, kk):
        dst = buf.at[pl.ds(pl.multiple_of(half * rows + (kk * tt + t) * nslab, nslab), nslab), :]
        return pltpu.make_async_copy(os_hbm.at[slot_smem[half, kk, t]], dst, gsem.at[half])

    def fetch(tile, half):
        cp = pltpu.make_async_copy(slot_hbm.at[tile0 + tile], slot_smem.at[half], ssem)
        cp.start()
        cp.wait()

        def issue(t, carry):
            for kk in range(TOP_K):
                row_copy(half, t, kk).start()
            return carry

        lax.fori_loop(0, tt, issue, 0)

    @pl.when(i == 0)
    def _():
        fetch(0, 0)

    @pl.when(i + 1 < pl.num_programs(0))
    def _():
        fetch(i + 1, lax.rem(i + 1, 2))

    half = lax.rem(i, 2)

    def drain(t, carry):
        for kk in range(TOP_K):
            row_copy(half, t, kk).wait()
        return carry

    lax.fori_loop(0, tt, drain, 0)

    wt = wt_ref[...]
    g2 = _seq_rows(g2_ref[...], tt)
    hw = nslab * LANES
    for s in range(nslab):
        y_lo = sh_ref[:, s * LANES:(s + 1) * LANES]
        y_hi = sh_ref[:, hw + s * LANES:hw + (s + 1) * LANES]
        for kk in range(TOP_K):
            lo, hi = _unpack_words(buf[pl.ds(half * rows + kk * tt * nslab + s, tt, stride=nslab), :])
            y_lo = y_lo + wt[:, kk:kk + 1] * lo
            y_hi = y_hi + wt[:, kk:kk + 1] * hi
        for y, c0 in ((y_lo, s * LANES), (y_hi, hw + s * LANES)):
            cols = slice(c0, c0 + LANES)
            o_ref[:, cols] = x1_ref[:, cols] + g2[:, cols] * y


def _combine(slots, os3, wts_t, shared, x1, g2, seq_len, tok0):
    n, d = x1.shape
    nslab = d // (2 * LANES)
    tt = TOK_TILE
    tile0 = tok0 // tt
    return pl.pallas_call(
        functools.partial(_combine_kernel, nslab=nslab, tile0=tile0),
        grid=(n // tt,),
        in_specs=[pl.BlockSpec(memory_space=pl.ANY),
                  pl.BlockSpec(memory_space=pl.ANY),
                  pl.BlockSpec((tt, TOP_K), lambda i: (tile0 + i, 0)),
                  pl.BlockSpec((tt, d), lambda i: (tile0 + i, 0)),
                  pl.BlockSpec((tt, d), lambda i: (i, 0)),
                  _mod_spec(tt, seq_len, d)],
        out_specs=pl.BlockSpec((tt, d), lambda i: (i, 0)),
        out_shape=jax.ShapeDtypeStruct((n, d), F32),
        scratch_shapes=[pltpu.SMEM((2, TOP_K, tt), jnp.int32),
                        pltpu.VMEM((2 * TOP_K * tt * nslab, LANES), jnp.int32),
                        pltpu.SemaphoreType.DMA, pltpu.SemaphoreType.DMA((2,))],
        compiler_params=_cparams(("arbitrary",)),
        name="moe_combine",
    )(slots, os3, wts_t, shared, x1, g2)


def _mixer(x, mod, positions, s0, past_k, past_v, lb, p, w):
    bsz, seq_len, d = x.shape
    n = bsz * seq_len
    sh1, sc1, g1, sh2, sc2, g2 = mod
    tm = 256 if n % 256 == 0 else n
    x2 = x.reshape(n, d)
    h1 = _norm_mod(x2, p["norm1_w"], sc1, sh1, seq_len, tm)
    tmm = 1024 if n % 1024 == 0 else tm
    hg_w = w["w_in_hg"].shape[1]
    at_w = w["w_in_at"].shape[1]
    z_hg = _matmul(h1, w["w_in_hg"], tmm, min(hg_w, 1024), name="w_in_hgrn")
    z_at = _matmul(h1, w["w_in_at"], tmm, at_w, name="w_in_attn")
    gates = _matmul(h1, w["w_in_gate"], tmm, min(2 * d, 1024), act="sigmoid", name="w_in_gates")

    o_hg, s_new = _hgrn(z_hg, lb, p["hg_norm_w"], s0, bsz, seq_len)

    n_kv = p["n_kv"]
    kd = n_kv * AT_HD
    qd = at_w - 2 * kd
    tq = min(seq_len, 256)
    qn, kvn = _qk_prep(z_at, positions, p["q_norm_w"], p["k_norm_w"], seq_len, qd, kd, tq)
    if past_k is None:
        o_at = _swa_prompt(qn, kvn, p["attn_sinks"], bsz, seq_len, n_kv)
    else:
        o_at = _swa_sample(qn, kvn, past_k, past_v, p["attn_sinks"], bsz, seq_len, n_kv)

    x1, h2s = _merge(o_hg, o_at, gates, x2, g1, w["w_hg_out"], w["w_at_out"], w["w_o"],
                     p["norm2_w"], sc2, sh2, seq_len, tm)
    kv3 = kvn.reshape(bsz, seq_len, 2 * kd)
    return x1, h2s, s_new, kv3[:, :, :kd], kv3[:, :, kd:]


def kernel(x_prompt, x_sample, cache_k, cache_v, state_hgrn, c_prompt, c_sample, norm1_w, norm2_w, w_ada, b_ada,
           w_in, hg_lower_bounds, hg_norm_w, q_norm_w, k_norm_w, attn_sinks, w_hg_out, w_at_out, w_o, w_router,
           router_bias, w_exp_gate, w_exp_up, w_exp_down, w_sh_gate, w_sh_up, w_sh_down):
    depth = norm1_w.shape[0]
    assert depth == 1, "single trunk layer"
    bp, tp, d = x_prompt.shape
    bs, ts, _ = x_sample.shape
    window, n_kv = cache_k.shape[2], cache_k.shape[3]
    kd = n_kv * AT_HD
    hg_dim = w_hg_out.shape[1]
    qd = w_at_out.shape[1]
    n_exp = w_router.shape[2]
    nslab = d // (2 * LANES)
    l = 0

    lbs = jnp.cumsum(jax.nn.softmax(hg_lower_bounds.astype(F32), axis=0), axis=0)
    win = w_in[l]
    w = {
        "w_in_hg": win[:, :4 * hg_dim].astype(BF16),
        "w_in_at": win[:, 4 * hg_dim:4 * hg_dim + qd + 2 * kd].astype(BF16),
        "w_in_gate": win[:, 4 * hg_dim + qd + 2 * kd:].astype(BF16),
        "w_hg_out": w_hg_out[l].astype(BF16),
        "w_at_out": w_at_out[l].astype(BF16),
        "w_o": w_o[l].astype(BF16),
    }
    p = {"norm1_w": norm1_w[l], "norm2_w": norm2_w[l], "hg_norm_w": hg_norm_w[l], "q_norm_w": q_norm_w[l],
         "k_norm_w": k_norm_w[l], "attn_sinks": attn_sinks[l], "n_kv": n_kv}

    c_all = jnp.concatenate([c_prompt, c_sample], axis=0)
    mod_all = _ada(c_all, w_ada[l], b_ada[l])
    mod_all = mod_all.reshape(bp + bs, 6, 1, d)
    mod_p = tuple(mod_all[:bp, j] for j in range(6))
    mod_s = tuple(mod_all[bp:, j] for j in range(6))

    pos_p = jnp.arange(tp, dtype=jnp.int32)
    pos_s = PAST_LEN + jnp.arange(ts, dtype=jnp.int32)
    s0_p = jnp.zeros((bp,) + state_hgrn.shape[2:], F32)
    x1_p, h2_p, sp, kp, vp = _mixer(x_prompt, mod_p, pos_p, s0_p, None, None, lbs[l], p, w)
    pk = cache_k[l].reshape(bs, window, kd)
    pv = cache_v[l].reshape(bs, window, kd)
    x1_s, h2_s, ss, ks, vs = _mixer(x_sample, mod_s, pos_s, state_hgrn[l], pk, pv, lbs[l], p, w)

    n_p, n_s = bp * tp, bs * ts
    n_tok = n_p + n_s
    h2s = jnp.concatenate([h2_p, h2_s], axis=0)
    tr = 256 if n_tok % 256 == 0 else TOK_TILE
    eidx, wts, rank, counts = _router(h2s, w_router[l], router_bias[l], n_tok, d, tr)
    shared = _shared_expert(h2s, w_sh_gate[l].astype(BF16), w_sh_up[l].astype(BF16), w_sh_down[l].astype(BF16),
                            n_tok, d, tr)

    bm = MOE_ROWS
    counts = counts[:, 0]
    pcounts = (counts + bm - 1) // bm * bm
    pend = jnp.cumsum(pcounts)
    pstart = pend - pcounts
    nb = -(-(n_tok * TOP_K) // bm) + n_exp
    slots = _slots(eidx, rank, pstart, n_tok)
    block_e = jnp.minimum(jnp.searchsorted(pend, jnp.arange(nb, dtype=jnp.int32) * bm, side="right"),
                          n_exp - 1).astype(jnp.int32)
    n_used = (pend[-1:] // bm).astype(jnp.int32)

    xs = _dispatch(counts, pstart, slots, h2s, n_tok, nslab, nb * bm, bm)
    os_ = _experts(block_e, n_used, xs.reshape(nb * bm * nslab, LANES), w_exp_gate[l], w_exp_up[l], w_exp_down[l],
                   d, bm)
    os3 = os_.reshape(nb * bm, nslab, LANES)
    wts_t = wts.T
    y_p = _combine(slots, os3, wts_t, shared, x1_p, mod_p[5], tp, 0)
    y_s = _combine(slots, os3, wts_t, shared, x1_s, mod_s[5], ts, n_p)

    def cache_out(a, b_, t):
        return a[:, t - window:].reshape(1, b_, window, n_kv, AT_HD)

    new_k_p = cache_out(kp, bp, tp)
    new_v_p = cache_out(vp, bp, tp)
    keys_s = jnp.concatenate([pk, ks], axis=1)
    vals_s = jnp.concatenate([pv, vs], axis=1)
    new_k_s = cache_out(keys_s, bs, window + ts)
    new_v_s = cache_out(vals_s, bs, window + ts)
    return (y_p.reshape(bp, tp, d), y_s.reshape(bs, ts, d), new_k_p, new_v_p, sp[None],
            new_k_s, new_v_s, ss[None])
```

```python
import functools
import math

import numpy as np
import jax
import jax.numpy as jnp
from jax import lax
from jax.experimental import pallas as pl
from jax.experimental.pallas import tpu as pltpu

EPS = 1e-6
CHUNK = 64
HG_DK = 128
AT_HD = 64
ROPE_DIM = 16
ROPE_THETA = 500000.0
TOP_K = 8
N_GROUPS = 8
TOPK_GROUPS = 4
ROUTED_SCALE = 2.5
PAST_LEN = 2048

LANES = 128
MOE_ROWS = 256
TOK_TILE = 128
VMEM_LIMIT = 56 * 1024 * 1024

F32 = jnp.float32
BF16 = jnp.bfloat16


def _cparams(semantics, vmem=VMEM_LIMIT):
    return pltpu.CompilerParams(dimension_semantics=semantics, vmem_limit_bytes=vmem)


def _sigmoid(x):
    return 1.0 / (1.0 + jnp.exp(-x))


def _silu(x):
    return x * _sigmoid(x)


def _dot(a, b):
    return jnp.dot(a, b, preferred_element_type=F32)


def _dot_nt(a, b):
    return lax.dot_general(a, b, (((1,), (1,)), ((), ())), preferred_element_type=F32)


def _dot_tn(a, b):
    return lax.dot_general(a, b, (((0,), (0,)), ((), ())), preferred_element_type=F32)


def _split_bf16(x):
    hi = x.astype(BF16)
    lo = (x - hi.astype(F32)).astype(BF16)
    return hi, lo


def _seq_rows(m, rows):
    s, _, d = m.shape
    if s == 1:
        return m[0]
    return jnp.broadcast_to(m, (s, rows // s, d)).reshape(rows, d)


def _mod_spec(tm, seq_len, d):
    if tm <= seq_len:
        return pl.BlockSpec((1, 1, d), lambda i: ((i * tm) // seq_len, 0, 0))
    s = tm // seq_len
    return pl.BlockSpec((s, 1, d), lambda i: (i, 0, 0))


def _ada_kernel(c_ref, w_ref, b_ref, o_ref):
    s = _silu(c_ref[...]).astype(BF16)
    o_ref[...] = _dot(s, w_ref[...].astype(BF16)) + b_ref[...]


def _ada(c, w, b):
    n, d = c.shape
    m = w.shape[1]
    tn = min(m, 1024)
    return pl.pallas_call(
        _ada_kernel,
        grid=(m // tn,),
        in_specs=[pl.BlockSpec((n, d), lambda j: (0, 0)),
                  pl.BlockSpec((d, tn), lambda j: (0, j)),
                  pl.BlockSpec((1, tn), lambda j: (0, j))],
        out_specs=pl.BlockSpec((n, tn), lambda j: (0, j)),
        out_shape=jax.ShapeDtypeStruct((n, m), F32),
        compiler_params=_cparams(("parallel",)),
        name="ada_mod",
    )(c, w, b.reshape(1, m))


def _norm_mod_kernel(x_ref, w_ref, sc_ref, sh_ref, o_ref):
    x = x_ref[...]
    tm = x.shape[0]
    y = x * lax.rsqrt(jnp.mean(x * x, axis=-1, keepdims=True) + EPS) * w_ref[...]
    o_ref[...] = (y * (1.0 + _seq_rows(sc_ref[...], tm)) + _seq_rows(sh_ref[...], tm)).astype(o_ref.dtype)


def _norm_mod(x, w, sc, sh, seq_len, tm):
    n, d = x.shape
    return pl.pallas_call(
        _norm_mod_kernel,
        grid=(n // tm,),
        in_specs=[pl.BlockSpec((tm, d), lambda i: (i, 0)),
                  pl.BlockSpec((1, d), lambda i: (0, 0)),
                  _mod_spec(tm, seq_len, d), _mod_spec(tm, seq_len, d)],
        out_specs=pl.BlockSpec((tm, d), lambda i: (i, 0)),
        out_shape=jax.ShapeDtypeStruct((n, d), BF16),
        compiler_params=_cparams(("parallel",)),
        name="norm1_mod",
    )(x, w.reshape(1, d), sc, sh)


def _mm_kernel(x_ref, w_ref, o_ref, *, act):
    y = _dot(x_ref[...], w_ref[...])
    if act == "sigmoid":
        y = _sigmoid(y)
    o_ref[...] = y.astype(o_ref.dtype)


def _matmul(x, w, tm, tn, act=None, out_dtype=F32, name="matmul"):
    n, k = x.shape
    m = w.shape[1]
    return pl.pallas_call(
        functools.partial(_mm_kernel, act=act),
        grid=(n // tm, m // tn),
        in_specs=[pl.BlockSpec((tm, k), lambda i, j: (i, 0)),
                  pl.BlockSpec((k, tn), lambda i, j: (0, j))],
        out_specs=pl.BlockSpec((tm, tn), lambda i, j: (i, j)),
        out_shape=jax.ShapeDtypeStruct((n, m), out_dtype),
        compiler_params=_cparams(("parallel", "arbitrary")),
        name=name,
    )(x, w)


def _hgrn_tables(c):
    nlev = int(math.log2(c))
    t = np.arange(c)[:, None]
    s = np.arange(c)[None, :]
    seg = []
    for l in range(1, nlev + 1):
        b = 1 << l
        seg.append(((s >= (t // b) * b) & (s <= t)).astype(np.float32))
        seg.append(((s > t) & (s <= (t // b) * b + b - 1)).astype(np.float32))
    masks = [(t == s).astype(np.float32)]
    for l in range(nlev):
        b = 1 << l
        masks.append(((t // (2 * b) == s // (2 * b)) & (t % (2 * b) >= b) & (s % (2 * b) < b)).astype(np.float32))
    return np.concatenate(seg, axis=0), np.stack(masks, axis=0)


def _hgrn_kernel(q_ref, f_ref, i_ref, g_ref, lb_ref, nw_ref, s0_ref, seg_ref, msk_ref,
                 o_ref, sn_ref, st_ref, *, c, nchunks, hb):
    nlev = int(math.log2(c))
    tstep = pl.program_id(2)

    @pl.when(tstep == 0)
    def _():
        for hh in range(hb):
            st_ref[hh] = s0_ref[0, hh].T

    nw = nw_ref[...]
    lb = lb_ref[...]
    head = lambda a, hh: a[:, hh * HG_DK:(hh + 1) * HG_DK]

    def decays(l, g_hi, g_lo):
        seg = seg_ref[2 * (l - 1) * c:2 * l * c, :]
        e = _dot(seg, g_hi) + _dot(seg, g_lo)
        return e[:c, :], e[c:, :]

    def chunk(ci, carry):
        rows = pl.ds(pl.multiple_of(ci * c, c), c)
        q = q_ref[rows, :]
        f = lb + (1.0 - lb) * _sigmoid(f_ref[rows, :])
        g = jnp.log(f)
        k = 1.0 - f
        g_hi, g_lo = _split_bf16(g)
        qb = q.astype(BF16)
        kb = k.astype(BF16)
        scores = [jnp.where(msk_ref[0] > 0, _dot_nt(head(qb, hh), head(kb, hh)), 0.0) for hh in range(hb)]
        for l in range(nlev):
            if l == 0:
                ql, kl = (q * f).astype(BF16), kb
            else:
                wl, vl = decays(l, g_hi, g_lo)
                ql = (q * jnp.exp(wl)).astype(BF16)
                kl = (k * jnp.exp(vl)).astype(BF16)
            for hh in range(hb):
                scores[hh] = scores[hh] + jnp.where(msk_ref[l + 1] > 0, _dot_nt(head(ql, hh), head(kl, hh)), 0.0)
        a_inc, v_end = decays(nlev, g_hi, g_lo)
        qa = (q * jnp.exp(a_inc)).astype(BF16)
        k_end = (k * jnp.exp(v_end)).astype(BF16)
        vb = i_ref[rows, :].astype(BF16)
        carry_decay = jnp.exp(a_inc[c - 1:c, :])
        gate = _silu(g_ref[rows, :])
        for hh in range(hb):
            st = st_ref[hh]
            o = _dot_nt(head(qa, hh), st.astype(BF16)) + _dot(scores[hh].astype(BF16), head(vb, hh))
            st_ref[hh] = st * head(carry_decay, hh) + _dot_tn(head(vb, hh), head(k_end, hh))
            on = o * lax.rsqrt(jnp.mean(o * o, axis=-1, keepdims=True) + EPS) * nw
            o_ref[rows, hh * HG_DK:(hh + 1) * HG_DK] = (on * head(gate, hh)).astype(o_ref.dtype)
        return carry

    lax.fori_loop(0, nchunks, chunk, 0)

    @pl.when(tstep == pl.num_programs(2) - 1)
    def _():
        for hh in range(hb):
            sn_ref[0, hh] = st_ref[hh].T


def _hgrn(z_hg, lb, norm_w, s0, bsz, seq_len):
    n, w4 = z_hg.shape
    nh = w4 // (4 * HG_DK)
    hb = min(nh, 8)
    ng = nh // hb
    c = min(CHUNK, seq_len)
    tb = min(seq_len, 512)
    nt = seq_len // tb
    seg, masks = _hgrn_tables(c)

    def col(part):
        return pl.BlockSpec((tb, hb * HG_DK), lambda b, h, t: (b * nt + t, part * ng + h))

    return pl.pallas_call(
        functools.partial(_hgrn_kernel, c=c, nchunks=tb // c, hb=hb),
        grid=(bsz, ng, nt),
        in_specs=[col(0), col(1), col(2), col(3),
                  pl.BlockSpec((1, hb * HG_DK), lambda b, h, t: (0, h)),
                  pl.BlockSpec((1, HG_DK), lambda b, h, t: (0, 0)),
                  pl.BlockSpec((1, hb, HG_DK, HG_DK), lambda b, h, t: (b, h, 0, 0)),
                  pl.BlockSpec(seg.shape, lambda b, h, t: (0, 0)),
                  pl.BlockSpec(masks.shape, lambda b, h, t: (0, 0, 0))],
        out_specs=[pl.BlockSpec((tb, hb * HG_DK), lambda b, h, t: (b * nt + t, h)),
                   pl.BlockSpec((1, hb, HG_DK, HG_DK), lambda b, h, t: (b, h, 0, 0))],
        out_shape=[jax.ShapeDtypeStruct((n, nh * HG_DK), BF16),
                   jax.ShapeDtypeStruct((bsz, nh, HG_DK, HG_DK), F32)],
        scratch_shapes=[pltpu.VMEM((hb, HG_DK, HG_DK), F32)],
        compiler_params=_cparams(("parallel", "parallel", "arbitrary")),
        name="hgrn2",
    )(z_hg, z_hg, z_hg, z_hg, lb.reshape(1, -1), norm_w.reshape(1, HG_DK), s0,
      jnp.asarray(seg, BF16), jnp.asarray(masks, F32))


def _rope_tables(positions):
    half = ROPE_DIM // 2
    inv_freq = ROPE_THETA ** (-jnp.arange(0, ROPE_DIM, 2, dtype=F32) / ROPE_DIM)
    ang = positions.astype(F32)[:, None] * inv_freq[None, :]
    cos, sin = jnp.cos(ang), jnp.sin(ang)
    t = positions.shape[0]
    rest = AT_HD - ROPE_DIM
    c64 = jnp.concatenate([cos, cos, jnp.ones((t, rest), F32)], axis=1)
    sa64 = jnp.concatenate([-sin, jnp.zeros((t, half + rest), F32)], axis=1)
    sb64 = jnp.concatenate([jnp.zeros((t, half), F32), sin, jnp.zeros((t, rest), F32)], axis=1)
    rep = LANES // AT_HD
    return tuple(jnp.tile(a, (1, rep)) for a in (c64, sa64, sb64))


def _qk_prep_kernel(z_ref, cos_ref, sa_ref, sb_ref, qw_ref, kw_ref, bd_ref, q_ref, kv_ref, *, qd, kd):
    half = ROPE_DIM // 2
    z = z_ref[...]
    cos, sa, sb = cos_ref[...], sa_ref[...], sb_ref[...]

    def norm_rope(x, w, bd, reps):
        x2 = x * x
        hi, lo = _split_bf16(x2)
        ss = _dot(hi, bd) + _dot(lo, bd)
        xn = x * lax.rsqrt(ss * (1.0 / AT_HD) + EPS) * w
        width = x.shape[1]
        tile = lambda a: jnp.concatenate([a] * reps, axis=1) if reps > 1 else a
        return (xn * tile(cos) + pltpu.roll(xn, width - half, 1) * tile(sa)
                + pltpu.roll(xn, half, 1) * tile(sb))

    q = norm_rope(z[:, :qd], qw_ref[...], bd_ref[...], qd // LANES)
    k = norm_rope(z[:, qd:qd + kd], kw_ref[...], bd_ref[:kd, :kd], kd // LANES)
    q_ref[...] = q.astype(q_ref.dtype)
    kv_ref[:, :kd] = k
    kv_ref[:, kd:] = z[:, qd + kd:]


def _qk_prep(z_at, positions, q_norm_w, k_norm_w, seq_len, qd, kd, tm):
    n, w = z_at.shape
    cos, sa, sb = _rope_tables(positions)
    nt = seq_len // tm
    bd = np.kron(np.eye(qd // AT_HD, dtype=np.float32), np.ones((AT_HD, AT_HD), np.float32))
    tab = pl.BlockSpec((tm, LANES), lambda i: (i % nt, 0))
    return pl.pallas_call(
        functools.partial(_qk_prep_kernel, qd=qd, kd=kd),
        grid=(n // tm,),
        in_specs=[pl.BlockSpec((tm, w), lambda i: (i, 0)), tab, tab, tab,
                  pl.BlockSpec((1, qd), lambda i: (0, 0)),
                  pl.BlockSpec((1, kd), lambda i: (0, 0)),
                  pl.BlockSpec((qd, qd), lambda i: (0, 0))],
        out_specs=[pl.BlockSpec((tm, qd), lambda i: (i, 0)),
                   pl.BlockSpec((tm, 2 * kd), lambda i: (i, 0))],
        out_shape=[jax.ShapeDtypeStruct((n, qd), BF16),
                   jax.ShapeDtypeStruct((n, 2 * kd), F32)],
        compiler_params=_cparams(("parallel",)),
        name="qk_norm_rope",
    )(z_at, cos, sa, sb, jnp.tile(q_norm_w, qd // AT_HD).reshape(1, qd),
      jnp.tile(k_norm_w, kd // AT_HD).reshape(1, kd), jnp.asarray(bd, BF16))


def _attend(q, keys, vals, sinks_ref, col_valid, o_ref, n_kv, group):
    n_heads = n_kv * group
    batch = 16
    for h0 in range(0, n_heads, batch):
        heads = range(h0, min(h0 + batch, n_heads))
        scores = []
        for h in heads:
            j = h // group
            s = _dot_nt(q[:, h * AT_HD:(h + 1) * AT_HD], keys[:, j * AT_HD:(j + 1) * AT_HD]) * (AT_HD ** -0.5)
            scores.append(s if col_valid is None else jnp.where(col_valid, s, -jnp.inf))
        probs = []
        for h, s in zip(heads, scores):
            sink = sinks_ref[h]
            m = jnp.maximum(jnp.max(s, axis=-1, keepdims=True), sink)
            e = jnp.exp(s - m)
            den = jnp.sum(e, axis=-1, keepdims=True) + jnp.exp(sink - m)
            probs.append((e / den).astype(BF16))
        for h, p in zip(heads, probs):
            j = h // group
            o_ref[:, h * AT_HD:(h + 1) * AT_HD] = _dot(p, vals[:, j * AT_HD:(j + 1) * AT_HD]).astype(o_ref.dtype)


def _swa_prompt_kernel(sinks_ref, q_ref, kv0_ref, kv1_ref, kv2_ref, o_ref, *, n_kv, group, w_chunks):
    kd = n_kv * AT_HD
    n = pl.program_id(1)
    blocks = [kv0_ref[...], kv1_ref[...], kv2_ref[...]]
    keys = jnp.concatenate([b[:, :kd] for b in blocks], axis=0).astype(BF16)
    vals = jnp.concatenate([b[:, kd:] for b in blocks], axis=0).astype(BF16)
    col_chunk = lax.broadcasted_iota(jnp.int32, (1, keys.shape[0]), 1) // CHUNK
    col_valid = (col_chunk + n - w_chunks) >= 0
    _attend(q_ref[...], keys, vals, sinks_ref, col_valid, o_ref, n_kv, group)


def _swa_prompt(qn, kvn, sinks, bsz, seq_len, n_kv):
    n, qd = qn.shape
    kd = n_kv * AT_HD
    nc = seq_len // CHUNK
    group = qd // AT_HD // n_kv
    w_chunks = 2

    def kv_spec(j):
        return pl.BlockSpec((CHUNK, 2 * kd), lambda b, c, s: (b * nc + jnp.maximum(c - w_chunks + j, 0), 0))

    return pl.pallas_call(
        functools.partial(_swa_prompt_kernel, n_kv=n_kv, group=group, w_chunks=w_chunks),
        grid_spec=pltpu.PrefetchScalarGridSpec(
            num_scalar_prefetch=1,
            grid=(bsz, nc),
            in_specs=[pl.BlockSpec((CHUNK, qd), lambda b, c, s: (b * nc + c, 0)),
                      kv_spec(0), kv_spec(1), kv_spec(2)],
            out_specs=pl.BlockSpec((CHUNK, qd), lambda b, c, s: (b * nc + c, 0))),
        out_shape=jax.ShapeDtypeStruct((n, qd), BF16),
        compiler_params=_cparams(("parallel", "arbitrary")),
        name="swa_prompt",
    )(sinks, qn, kvn, kvn, kvn)


def _swa_sample_kernel(sinks_ref, q_ref, pk_ref, pv_ref, kv_ref, o_ref, *, n_kv, group):
    kd = n_kv * AT_HD
    kv = kv_ref[...]
    keys = jnp.concatenate([pk_ref[0], kv[:, :kd]], axis=0).astype(BF16)
    vals = jnp.concatenate([pv_ref[0], kv[:, kd:]], axis=0).astype(BF16)
    _attend(q_ref[...], keys, vals, sinks_ref, None, o_ref, n_kv, group)


def _swa_sample(qn, kvn, past_k, past_v, sinks, bsz, seq_len, n_kv):
    n, qd = qn.shape
    kd = n_kv * AT_HD
    window = past_k.shape[1]
    group = qd // AT_HD // n_kv
    return pl.pallas_call(
        functools.partial(_swa_sample_kernel, n_kv=n_kv, group=group),
        grid_spec=pltpu.PrefetchScalarGridSpec(
            num_scalar_prefetch=1,
            grid=(bsz,),
            in_specs=[pl.BlockSpec((seq_len, qd), lambda b, s: (b, 0)),
                      pl.BlockSpec((1, window, kd), lambda b, s: (b, 0, 0)),
                      pl.BlockSpec((1, window, kd), lambda b, s: (b, 0, 0)),
                      pl.BlockSpec((seq_len, 2 * kd), lambda b, s: (b, 0))],
            out_specs=pl.BlockSpec((seq_len, qd), lambda b, s: (b, 0))),
        out_shape=jax.ShapeDtypeStruct((n, qd), BF16),
        compiler_params=_cparams(("parallel",)),
        name="swa_sample",
    )(sinks, qn, past_k, past_v, kvn)


HI_MASK = np.uint32(0xFFFF0000)


def _pack_rows(x):
    half = x.shape[1] // 2
    bits = lambda a: lax.bitcast_convert_type(a.astype(BF16).astype(F32), jnp.uint32)
    word = (bits(x[:, half:]) & HI_MASK) | (bits(x[:, :half]) >> 16)
    return lax.bitcast_convert_type(word, jnp.int32)


def _unpack_words(w):
    u = lax.bitcast_convert_type(w, jnp.uint32)
    return lax.bitcast_convert_type(u << 16, F32), lax.bitcast_convert_type(u & HI_MASK, F32)


def _store_slabs(ref, words):
    rows, width = words.shape
    nslab = width // LANES
    for s in range(nslab):
        ref[pl.ds(s, rows, stride=nslab), :] = words[:, s * LANES:(s + 1) * LANES]


def _load_rows(ref, rows, nslab):
    lo, hi = [], []
    for s in range(nslab):
        a, b = _unpack_words(ref[pl.ds(s, rows, stride=nslab), :])
        lo.append(a)
        hi.append(b)
    return jnp.concatenate(lo + hi, axis=1).astype(BF16)


def _merge_kernel(ohg_ref, oat_ref, ga_ref, gb_ref, x_ref, g1_ref, whg_ref, wat_ref, wo_ref,
                  nw_ref, sc_ref, sh_ref, x1_ref, h2_ref):
    tm = x_ref.shape[0]
    merged = ga_ref[...] * _dot(ohg_ref[...], whg_ref[...]) + gb_ref[...] * _dot(oat_ref[...], wat_ref[...])
    mix = _dot(merged.astype(BF16), wo_ref[...])
    x1 = x_ref[...] + _seq_rows(g1_ref[...], tm) * mix
    x1_ref[...] = x1
    y = x1 * lax.rsqrt(jnp.mean(x1 * x1, axis=-1, keepdims=True) + EPS) * nw_ref[...]
    h2 = y * (1.0 + _seq_rows(sc_ref[...], tm)) + _seq_rows(sh_ref[...], tm)
    _store_slabs(h2_ref, _pack_rows(h2))


def _merge(o_hg, o_at, gates, x, g1, w_hg_out, w_at_out, w_o, norm2_w, sc2, sh2, seq_len, tm):
    n, d = x.shape
    hd = o_hg.shape[1]
    ad = o_at.shape[1]
    nslab = d // (2 * LANES)
    const = lambda shape: pl.BlockSpec(shape, lambda i: (0,) * len(shape), pipeline_mode=pl.Buffered(1))
    mod = lambda: _mod_spec(tm, seq_len, d)
    return pl.pallas_call(
        _merge_kernel,
        grid=(n // tm,),
        in_specs=[pl.BlockSpec((tm, hd), lambda i: (i, 0)),
                  pl.BlockSpec((tm, ad), lambda i: (i, 0)),
                  pl.BlockSpec((tm, d), lambda i: (i, 0)),
                  pl.BlockSpec((tm, d), lambda i: (i, 1)),
                  pl.BlockSpec((tm, d), lambda i: (i, 0)),
                  mod(), const((hd, d)), const((ad, d)), const((d, d)), const((1, d)), mod(), mod()],
        out_specs=[pl.BlockSpec((tm, d), lambda i: (i, 0)),
                   pl.BlockSpec((tm * nslab, LANES), lambda i: (i, 0))],
        out_shape=[jax.ShapeDtypeStruct((n, d), F32),
                   jax.ShapeDtypeStruct((n * nslab, LANES), jnp.int32)],
        compiler_params=_cparams(("parallel",)),
        name="merge_norm2",
    )(o_hg, o_at, gates, gates, x, g1, w_hg_out, w_at_out, w_o, norm2_w.reshape(1, d), sc2, sh2)


def _router_kernel(h_ref, wr_ref, bias_ref, tri_ref, eidx_ref, wts_ref, rank_ref, cnt_ref, run_ref,
                   *, nslab, n_exp):
    tm = h_ref.shape[0] // nslab
    gsz = n_exp // N_GROUPS
    step = pl.program_id(0)

    @pl.when(step == 0)
    def _():
        run_ref[...] = jnp.zeros_like(run_ref)

    h = _load_rows(h_ref, tm, nslab)
    scores = _sigmoid(_dot_nt(wr_ref[...], h))
    choice = scores + bias_ref[...]
    neg = -jnp.inf
    row = lax.broadcasted_iota(jnp.int32, (gsz, tm), 0)

    gscore = []
    for gi in range(N_GROUPS):
        cg = choice[gi * gsz:(gi + 1) * gsz, :]
        m1 = jnp.max(cg, axis=0, keepdims=True)
        i1 = jnp.min(jnp.where(cg == m1, row, gsz), axis=0, keepdims=True)
        m2 = jnp.max(jnp.where(row == i1, neg, cg), axis=0, keepdims=True)
        gscore.append(m1 + m2)
    gs = jnp.concatenate(gscore, axis=0)
    grow = lax.broadcasted_iota(jnp.int32, (N_GROUPS, tm), 0)
    gsel = jnp.zeros((N_GROUPS, tm), F32)
    for _ in range(TOPK_GROUPS):
        gm = jnp.max(gs, axis=0, keepdims=True)
        gi = jnp.min(jnp.where(gs == gm, grow, N_GROUPS), axis=0, keepdims=True)
        hit = grow == gi
        gsel = jnp.where(hit, 1.0, gsel)
        gs = jnp.where(hit, neg, gs)
    masked = jnp.concatenate(
        [jnp.where(gsel[gi:gi + 1, :] > 0, choice[gi * gsz:(gi + 1) * gsz, :], neg) for gi in range(N_GROUPS)],
        axis=0)

    erow = lax.broadcasted_iota(jnp.int32, (n_exp, tm), 0)
    idxs, raw = [], []
    for _ in range(TOP_K):
        m = jnp.max(masked, axis=0, keepdims=True)
        i = jnp.min(jnp.where(masked == m, erow, n_exp), axis=0, keepdims=True)
        hit = erow == i
        raw.append(jnp.sum(jnp.where(hit, scores, 0.0), axis=0, keepdims=True))
        masked = jnp.where(hit, neg, masked)
        idxs.append(i)
    total = raw[0]
    for r in raw[1:]:
        total = total + r
    onehot = jnp.zeros((n_exp, tm), F32)
    for i in idxs:
        onehot = onehot + jnp.where(erow == i, 1.0, 0.0)
    before = _dot(onehot.astype(BF16), tri_ref[...]) + run_ref[:, 0:1]
    for kk in range(TOP_K):
        eidx_ref[kk:kk + 1, :] = idxs[kk]
        wts_ref[kk:kk + 1, :] = raw[kk] / total * ROUTED_SCALE
        rank_ref[kk:kk + 1, :] = jnp.sum(jnp.where(erow == idxs[kk], before, 0.0), axis=0, keepdims=True).astype(jnp.int32)
    run_ref[...] = run_ref[...] + jnp.sum(onehot, axis=1, keepdims=True)

    @pl.when(step == pl.num_programs(0) - 1)
    def _():
        cnt_ref[...] = run_ref[...].astype(jnp.int32)


def _slot_kernel(eidx_ref, rank_ref, pstart_ref, slot_ref):
    n_exp = pstart_ref.shape[0]
    tt = eidx_ref.shape[1]
    erow = lax.broadcasted_iota(jnp.int32, (n_exp, tt), 0)
    pstart = pstart_ref[:, 0:1]
    for kk in range(TOP_K):
        base = jnp.sum(jnp.where(erow == eidx_ref[kk:kk + 1, :], pstart, 0), axis=0, keepdims=True)
        slot_ref[0, kk:kk + 1, :] = base + rank_ref[kk:kk + 1, :]


def _slots(eidx, rank, pstart, n_tok):
    n_exp = pstart.shape[0]
    tt = TOK_TILE
    tok_spec = pl.BlockSpec((TOP_K, tt), lambda i: (0, i))
    return pl.pallas_call(
        _slot_kernel,
        grid=(n_tok // tt,),
        in_specs=[tok_spec, tok_spec, pl.BlockSpec((n_exp, LANES), lambda i: (0, 0))],
        out_specs=pl.BlockSpec((1, TOP_K, tt), lambda i: (i, 0, 0)),
        out_shape=jax.ShapeDtypeStruct((n_tok // tt, TOP_K, tt), jnp.int32),
        compiler_params=_cparams(("parallel",)),
        name="moe_slots",
    )(eidx, rank, jnp.broadcast_to(pstart[:, None], (n_exp, LANES)))


def _router(h2s, w_router, router_bias, n_tok, d, tm):
    n_exp = w_router.shape[1]
    nslab = d // (2 * LANES)
    tri = np.triu(np.ones((tm, tm), np.float32), 1)
    out_tok = lambda dt: jax.ShapeDtypeStruct((TOP_K, n_tok), dt)
    tok_spec = pl.BlockSpec((TOP_K, tm), lambda i: (0, i))
    return pl.pallas_call(
        functools.partial(_router_kernel, nslab=nslab, n_exp=n_exp),
        grid=(n_tok // tm,),
        in_specs=[pl.BlockSpec((tm * nslab, LANES), lambda i: (i, 0)),
                  pl.BlockSpec((n_exp, d), lambda i: (0, 0)),
                  pl.BlockSpec((n_exp, 1), lambda i: (0, 0)),
                  pl.BlockSpec((tm, tm), lambda i: (0, 0))],
        out_specs=[tok_spec, tok_spec, tok_spec, pl.BlockSpec((n_exp, LANES), lambda i: (0, 0))],
        out_shape=[out_tok(jnp.int32), out_tok(F32), out_tok(jnp.int32),
                   jax.ShapeDtypeStruct((n_exp, LANES), jnp.int32)],
        scratch_shapes=[pltpu.VMEM((n_exp, LANES), F32)],
        compiler_params=_cparams(("arbitrary",)),
        name="router_topk",
    )(h2s, w_router.T.astype(BF16), router_bias.reshape(n_exp, 1), jnp.asarray(tri, BF16))


def _shared_kernel(h_ref, w1_ref, w3_ref, w2_ref, o_ref, *, nslab):
    tm = h_ref.shape[0] // nslab
    h = _load_rows(h_ref, tm, nslab)
    a = (_silu(_dot(h, w1_ref[...])) * _dot(h, w3_ref[...])).astype(BF16)
    o_ref[...] = _dot(a, w2_ref[...])


def _shared_expert(h2s, ws1, ws3, ws2, n_tok, d, tm):
    nslab = d // (2 * LANES)
    ds = ws1.shape[1]
    return pl.pallas_call(
        functools.partial(_shared_kernel, nslab=nslab),
        grid=(n_tok // tm,),
        in_specs=[pl.BlockSpec((tm * nslab, LANES), lambda i: (i, 0)),
                  pl.BlockSpec((d, ds), lambda i: (0, 0)),
                  pl.BlockSpec((d, ds), lambda i: (0, 0)),
                  pl.BlockSpec((ds, d), lambda i: (0, 0))],
        out_specs=pl.BlockSpec((tm, d), lambda i: (i, 0)),
        out_shape=jax.ShapeDtypeStruct((n_tok, d), F32),
        compiler_params=_cparams(("parallel",)),
        name="shared_expert",
    )(h2s, ws1, ws3, ws2)


def _dispatch_kernel(cnt_ref, pstart_ref, slot_hbm, h_ref, xs_hbm, slot_smem, zbuf, ssem, dsem, zsem, *, bm):
    i = pl.program_id(0)
    nslab = xs_hbm.shape[1]
    tt = h_ref.shape[0] // nslab
    n_exp = cnt_ref.shape[0]

    @pl.when(i == 0)
    def _():
        zbuf[...] = jnp.zeros_like(zbuf)

        def walk(e, start):
            cnt = cnt_ref[e]
            pad = lax.rem(bm - lax.rem(cnt, bm), bm)
            base = pstart_ref[e] + cnt
            size = bm // 2
            while size >= 1:
                take = (pad & size) != 0

                @pl.when(take)
                def _(base=base, size=size):
                    cp = pltpu.make_async_copy(zbuf.at[pl.ds(0, size)], xs_hbm.at[pl.ds(base, size)], zsem)
                    if start:
                        cp.start()
                    else:
                        cp.wait()

                base = base + jnp.where(take, size, 0)
                size //= 2

        def start_e(e, carry):
            walk(e, True)
            return carry

        def wait_e(e, carry):
            walk(e, False)
            return carry

        lax.fori_loop(0, n_exp, start_e, 0)
        lax.fori_loop(0, n_exp, wait_e, 0)

    cp = pltpu.make_async_copy(slot_hbm.at[i], slot_smem, ssem)
    cp.start()
    cp.wait()

    def row_copy(t, kk):
        src = h_ref.at[pl.ds(pl.multiple_of(t * nslab, nslab), nslab), :]
        return pltpu.make_async_copy(src, xs_hbm.at[slot_smem[kk, t]], dsem)

    def issue(t, carry):
        for kk in range(TOP_K):
            row_copy(t, kk).start()
        return carry

    lax.fori_loop(0, tt, issue, 0)

    def drain(t, carry):
        for kk in range(TOP_K):
            row_copy(t, kk).wait()
        return carry

    lax.fori_loop(0, tt, drain, 0)


def _dispatch(counts, pstart, slots, h2s, n_tok, nslab, n_rows, bm):
    return pl.pallas_call(
        functools.partial(_dispatch_kernel, bm=bm),
        grid_spec=pltpu.PrefetchScalarGridSpec(
            num_scalar_prefetch=2,
            grid=(n_tok // TOK_TILE,),
            in_specs=[pl.BlockSpec(memory_space=pl.ANY),
                      pl.BlockSpec((TOK_TILE * nslab, LANES), lambda i, c, p: (i, 0))],
            out_specs=pl.BlockSpec(memory_space=pl.ANY),
            scratch_shapes=[pltpu.SMEM((TOP_K, TOK_TILE), jnp.int32),
                            pltpu.VMEM((bm // 2, nslab, LANES), jnp.int32),
                            pltpu.SemaphoreType.DMA, pltpu.SemaphoreType.DMA, pltpu.SemaphoreType.DMA]),
        out_shape=jax.ShapeDtypeStruct((n_rows, nslab, LANES), jnp.int32),
        compiler_params=_cparams(("arbitrary",)),
        name="moe_dispatch",
    )(counts, pstart, slots, h2s)


def _experts_kernel(be_ref, nu_ref, vis_ref, nv_ref, x_ref, w1_hbm, w3_hbm, w2_hbm, o_ref,
                    w1s, w3s, w2s, w1b, w3b, w2b, vcount, wsem, *, nslab):
    r = pl.program_id(0)
    bm = x_ref.shape[0] // nslab

    def weight_copies(j, slot):
        e = vis_ref[j]
        return (pltpu.make_async_copy(w1_hbm.at[e], w1s.at[slot], wsem.at[slot, 0]),
                pltpu.make_async_copy(w3_hbm.at[e], w3s.at[slot], wsem.at[slot, 1]),
                pltpu.make_async_copy(w2_hbm.at[e], w2s.at[slot], wsem.at[slot, 2]))

    @pl.when(r == 0)
    def _():
        vcount[0] = 0
        for cp in weight_copies(0, 0):
            cp.start()

    @pl.when(r < nu_ref[0])
    def _():
        prev = be_ref[jnp.maximum(r - 1, 0)]

        @pl.when(jnp.logical_or(r == 0, be_ref[r] != prev))
        def _():
            j = vcount[0]
            slot = lax.rem(j, 2)
            for cp in weight_copies(j, slot):
                cp.wait()

            @pl.when(j + 1 < nv_ref[0])
            def _():
                for cp in weight_copies(j + 1, 1 - slot):
                    cp.start()

            w1b[...] = w1s[slot].astype(BF16)
            w3b[...] = w3s[slot].astype(BF16)
            w2b[...] = w2s[slot].astype(BF16)
            vcount[0] = j + 1

        x = _load_rows(x_ref, bm, nslab)
        a = (_silu(_dot(x, w1b[...])) * _dot(x, w3b[...])).astype(BF16)
        _store_slabs(o_ref, _pack_rows(_dot(a, w2b[...])))


def _experts(block_e, n_used, visit, n_visit, xs2, w1, w3, w2, d, bm):
    n_exp, _, de = w1.shape
    nslab = d // (2 * LANES)
    nb = xs2.shape[0] // (bm * nslab)
    row_spec = pl.BlockSpec((bm * nslab, LANES), lambda r, be, nu, vis, nv: (jnp.minimum(r, nu[0] - 1), 0))
    hbm = pl.BlockSpec(memory_space=pl.ANY)
    return pl.pallas_call(
        functools.partial(_experts_kernel, nslab=nslab),
        grid_spec=pltpu.PrefetchScalarGridSpec(
            num_scalar_prefetch=4,
            grid=(nb,),
            in_specs=[row_spec, hbm, hbm, hbm],
            out_specs=row_spec,
            scratch_shapes=[pltpu.VMEM((2, d, de), F32), pltpu.VMEM((2, d, de), F32), pltpu.VMEM((2, de, d), F32),
                            pltpu.VMEM((d, de), BF16), pltpu.VMEM((d, de), BF16), pltpu.VMEM((de, d), BF16),
                            pltpu.SMEM((1,), jnp.int32), pltpu.SemaphoreType.DMA((2, 3))]),
        out_shape=jax.ShapeDtypeStruct(xs2.shape, jnp.int32),
        compiler_params=_cparams(("arbitrary",)),
        name="routed_experts",
    )(block_e, n_used, visit, n_visit, xs2, w1, w3, w2)


def _combine_kernel(slot_hbm, os_hbm, wt_ref, sh_ref, x1_ref, g2_ref, o_ref, slot_smem, buf, ssem, gsem,
                    *, nslab, tile0):
    i = pl.program_id(0)
    tt = x1_ref.shape[0]
    rows = TOP_K * tt * nslab

    def row_copy(half, t, kk):
        dst = buf.at[pl.ds(pl.multiple_of(half * rows + (kk * tt + t) * nslab, nslab), nslab), :]
        return pltpu.make_async_copy(os_hbm.at[slot_smem[half, kk, t]], dst, gsem.at[half])

    def fetch(tile, half):
        cp = pltpu.make_async_copy(slot_hbm.at[tile0 + tile], slot_smem.at[half], ssem)
        cp.start()
        cp.wait()

        def issue(t, carry):
            for kk in range(TOP_K):
                row_copy(half, t, kk).start()
            return carry

        lax.fori_loop(0, tt, issue, 0)

    @pl.when(i == 0)
    def _():
        fetch(0, 0)

    @pl.when(i + 1 < pl.num_programs(0))
    def _():
        fetch(i + 1, lax.rem(i + 1, 2))

    half = lax.rem(i, 2)

    def drain(t, carry):
        for kk in range(TOP_K):
            row_copy(half, t, kk).wait()
        return carry

    lax.fori_loop(0, tt, drain, 0)

    wt = wt_ref[...]
    g2 = _seq_rows(g2_ref[...], tt)
    hw = nslab * LANES
    for s in range(nslab):
        y_lo = sh_ref[:, s * LANES:(s + 1) * LANES]
        y_hi = sh_ref[:, hw + s * LANES:hw + (s + 1) * LANES]
        for kk in range(TOP_K):
            lo, hi = _unpack_words(buf[pl.ds(half * rows + kk * tt * nslab + s, tt, stride=nslab), :])
            y_lo = y_lo + wt[:, kk:kk + 1] * lo
            y_hi = y_hi + wt[:, kk:kk + 1] * hi
        for y, c0 in ((y_lo, s * LANES), (y_hi, hw + s * LANES)):
            cols = slice(c0, c0 + LANES)
            o_ref[:, cols] = x1_ref[:, cols] + g2[:, cols] * y


def _combine(slots, os3, wts_t, shared, x1, g2, seq_len, tok0):
    n, d = x1.shape
    nslab = d // (2 * LANES)
    tt = TOK_TILE
    tile0 = tok0 // tt
    return pl.pallas_call(
        functools.partial(_combine_kernel, nslab=nslab, tile0=tile0),
        grid=(n // tt,),
        in_specs=[pl.BlockSpec(memory_space=pl.ANY),
                  pl.BlockSpec(memory_space=pl.ANY),
                  pl.BlockSpec((tt, TOP_K), lambda i: (tile0 + i, 0)),
                  pl.BlockSpec((tt, d), lambda i: (tile0 + i, 0)),
                  pl.BlockSpec((tt, d), lambda i: (i, 0)),
                  _mod_spec(tt, seq_len, d)],
        out_specs=pl.BlockSpec((tt, d), lambda i: (i, 0)),
        out_shape=jax.ShapeDtypeStruct((n, d), F32),
        scratch_shapes=[pltpu.SMEM((2, TOP_K, tt), jnp.int32),
                        pltpu.VMEM((2 * TOP_K * tt * nslab, LANES), jnp.int32),
                        pltpu.SemaphoreType.DMA, pltpu.SemaphoreType.DMA((2,))],
        compiler_params=_cparams(("arbitrary",)),
        name="moe_combine",
    )(slots, os3, wts_t, shared, x1, g2)


def _mixer(x, mod, positions, s0, past_k, past_v, lb, p, w):
    bsz, seq_len, d = x.shape
    n = bsz * seq_len
    sh1, sc1, g1, sh2, sc2, g2 = mod
    tm = 256 if n % 256 == 0 else n
    x2 = x.reshape(n, d)
    h1 = _norm_mod(x2, p["norm1_w"], sc1, sh1, seq_len, tm)
    tmm = 1024 if n % 1024 == 0 else tm
    hg_w = w["w_in_hg"].shape[1]
    at_w = w["w_in_at"].shape[1]
    z_hg = _matmul(h1, w["w_in_hg"], tmm, min(hg_w, 1024), name="w_in_hgrn")
    z_at = _matmul(h1, w["w_in_at"], tmm, at_w, name="w_in_attn")
    gates = _matmul(h1, w["w_in_gate"], tmm, min(2 * d, 1024), act="sigmoid", name="w_in_gates")

    o_hg, s_new = _hgrn(z_hg, lb, p["hg_norm_w"], s0, bsz, seq_len)

    n_kv = p["n_kv"]
    kd = n_kv * AT_HD
    qd = at_w - 2 * kd
    tq = min(seq_len, 256)
    qn, kvn = _qk_prep(z_at, positions, p["q_norm_w"], p["k_norm_w"], seq_len, qd, kd, tq)
    if past_k is None:
        o_at = _swa_prompt(qn, kvn, p["attn_sinks"], bsz, seq_len, n_kv)
    else:
        o_at = _swa_sample(qn, kvn, past_k, past_v, p["attn_sinks"], bsz, seq_len, n_kv)

    x1, h2s = _merge(o_hg, o_at, gates, x2, g1, w["w_hg_out"], w["w_at_out"], w["w_o"],
                     p["norm2_w"], sc2, sh2, seq_len, tm)
    kv3 = kvn.reshape(bsz, seq_len, 2 * kd)
    return x1, h2s, s_new, kv3[:, :, :kd], kv3[:, :, kd:]


def kernel(x_prompt, x_sample, cache_k, cache_v, state_hgrn, c_prompt, c_sample, norm1_w, norm2_w, w_ada, b_ada,
           w_in, hg_lower_bounds, hg_norm_w, q_norm_w, k_norm_w, attn_sinks, w_hg_out, w_at_out, w_o, w_router,
           router_bias, w_exp_gate, w_exp_up, w_exp_down, w_sh_gate, w_sh_up, w_sh_down):
    depth = norm1_w.shape[0]
    assert depth == 1, "single trunk layer"
    bp, tp, d = x_prompt.shape
    bs, ts, _ = x_sample.shape
    window, n_kv = cache_k.shape[2], cache_k.shape[3]
    kd = n_kv * AT_HD
    hg_dim = w_hg_out.shape[1]
    qd = w_at_out.shape[1]
    n_exp = w_router.shape[2]
    nslab = d // (2 * LANES)
    l = 0

    lbs = jnp.cumsum(jax.nn.softmax(hg_lower_bounds.astype(F32), axis=0), axis=0)
    win = w_in[l]
    w = {
        "w_in_hg": win[:, :4 * hg_dim].astype(BF16),
        "w_in_at": win[:, 4 * hg_dim:4 * hg_dim + qd + 2 * kd].astype(BF16),
        "w_in_gate": win[:, 4 * hg_dim + qd + 2 * kd:].astype(BF16),
        "w_hg_out": w_hg_out[l].astype(BF16),
        "w_at_out": w_at_out[l].astype(BF16),
        "w_o": w_o[l].astype(BF16),
    }
    p = {"norm1_w": norm1_w[l], "norm2_w": norm2_w[l], "hg_norm_w": hg_norm_w[l], "q_norm_w": q_norm_w[l],
         "k_norm_w": k_norm_w[l], "attn_sinks": attn_sinks[l], "n_kv": n_kv}

    c_all = jnp.concatenate([c_prompt, c_sample], axis=0)
    mod_all = _ada(c_all, w_ada[l], b_ada[l])
    mod_all = mod_all.reshape(bp + bs, 6, 1, d)
    mod_p = tuple(mod_all[:bp, j] for j in range(6))
    mod_s = tuple(mod_all[bp:, j] for j in range(6))

    pos_p = jnp.arange(tp, dtype=jnp.int32)
    pos_s = PAST_LEN + jnp.arange(ts, dtype=jnp.int32)
    s0_p = jnp.zeros((bp,) + state_hgrn.shape[2:], F32)
    x1_p, h2_p, sp, kp, vp = _mixer(x_prompt, mod_p, pos_p, s0_p, None, None, lbs[l], p, w)
    pk = cache_k[l].reshape(bs, window, kd)
    pv = cache_v[l].reshape(bs, window, kd)
    x1_s, h2_s, ss, ks, vs = _mixer(x_sample, mod_s, pos_s, state_hgrn[l], pk, pv, lbs[l], p, w)

    n_p, n_s = bp * tp, bs * ts
    n_tok = n_p + n_s
    h2s = jnp.concatenate([h2_p, h2_s], axis=0)
    tr = 256 if n_tok % 256 == 0 else TOK_TILE
    eidx, wts, rank, counts = _router(h2s, w_router[l], router_bias[l], n_tok, d, tr)
    shared = _shared_expert(h2s, w_sh_gate[l].astype(BF16), w_sh_up[l].astype(BF16), w_sh_down[l].astype(BF16),
                            n_tok, d, tr)

    bm = MOE_ROWS
    counts = counts[:, 0]
    pcounts = (counts + bm - 1) // bm * bm
    pend = jnp.cumsum(pcounts)
    pstart = pend - pcounts
    nb = -(-(n_tok * TOP_K) // bm) + n_exp
    slots = _slots(eidx, rank, pstart, n_tok)
    block_e = jnp.minimum(jnp.searchsorted(pend, jnp.arange(nb, dtype=jnp.int32) * bm, side="right"),
                          n_exp - 1).astype(jnp.int32)
    n_used = (pend[-1:] // bm).astype(jnp.int32)

    xs = _dispatch(counts, pstart, slots, h2s, n_tok, nslab, nb * bm, bm)
    visit = jnp.argsort(counts == 0, stable=True).astype(jnp.int32)
    n_visit = jnp.sum(counts > 0).astype(jnp.int32).reshape(1)
    os_ = _experts(block_e, n_used, visit, n_visit, xs.reshape(nb * bm * nslab, LANES),
                   w_exp_gate[l], w_exp_up[l], w_exp_down[l], d, bm)
    os3 = os_.reshape(nb * bm, nslab, LANES)
    wts_t = wts.T
    y_p = _combine(slots, os3, wts_t, shared, x1_p, mod_p[5], tp, 0)
    y_s = _combine(slots, os3, wts_t, shared, x1_s, mod_s[5], ts, n_p)

    def cache_out(a, b_, t):
        return a[:, t - window:].reshape(1, b_, window, n_kv, AT_HD)

    new_k_p = cache_out(kp, bp, tp)
    new_v_p = cache_out(vp, bp, tp)
    keys_s = jnp.concatenate([pk, ks], axis=1)
    vals_s = jnp.concatenate([pv, vs], axis=1)
    new_k_s = cache_out(keys_s, bs, window + ts)
    new_v_s = cache_out(vals_s, bs, window + ts)
    return (y_p.reshape(bp, tp, d), y_s.reshape(bs, ts, d), new_k_p, new_v_p, sp[None],
            new_k_s, new_v_s, ss[None])
```

```python
import functools
import math

import numpy as np
import jax
import jax.numpy as jnp
from jax import lax
from jax.experimental import pallas as pl
from jax.experimental.pallas import tpu as pltpu

EPS = 1e-6
CHUNK = 64
HG_DK = 128
AT_HD = 64
ROPE_DIM = 16
ROPE_THETA = 500000.0
TOP_K = 8
N_GROUPS = 8
TOPK_GROUPS = 4
ROUTED_SCALE = 2.5
PAST_LEN = 2048

LANES = 128
MOE_ROWS = 256
TOK_TILE = 128
VMEM_LIMIT = 56 * 1024 * 1024

F32 = jnp.float32
BF16 = jnp.bfloat16


def _cparams(semantics, vmem=VMEM_LIMIT):
    return pltpu.CompilerParams(dimension_semantics=semantics, vmem_limit_bytes=vmem)


def _sigmoid(x):
    return 1.0 / (1.0 + jnp.exp(-x))


def _silu(x):
    return x * _sigmoid(x)


def _dot(a, b):
    return jnp.dot(a, b, preferred_element_type=F32)


def _dot_nt(a, b):
    return lax.dot_general(a, b, (((1,), (1,)), ((), ())), preferred_element_type=F32)


def _dot_tn(a, b):
    return lax.dot_general(a, b, (((0,), (0,)), ((), ())), preferred_element_type=F32)


def _split_bf16(x):
    hi = x.astype(BF16)
    lo = (x - hi.astype(F32)).astype(BF16)
    return hi, lo


def _seq_rows(m, rows):
    s, _, d = m.shape
    if s == 1:
        return m[0]
    return jnp.broadcast_to(m, (s, rows // s, d)).reshape(rows, d)


def _mod_spec(tm, seq_len, d):
    if tm <= seq_len:
        return pl.BlockSpec((1, 1, d), lambda i: ((i * tm) // seq_len, 0, 0))
    s = tm // seq_len
    return pl.BlockSpec((s, 1, d), lambda i: (i, 0, 0))


def _ada_kernel(c_ref, w_ref, b_ref, o_ref):
    s = _silu(c_ref[...]).astype(BF16)
    o_ref[...] = _dot(s, w_ref[...].astype(BF16)) + b_ref[...]


def _ada(c, w, b):
    n, d = c.shape
    m = w.shape[1]
    tn = min(m, 1024)
    return pl.pallas_call(
        _ada_kernel,
        grid=(m // tn,),
        in_specs=[pl.BlockSpec((n, d), lambda j: (0, 0)),
                  pl.BlockSpec((d, tn), lambda j: (0, j)),
                  pl.BlockSpec((1, tn), lambda j: (0, j))],
        out_specs=pl.BlockSpec((n, tn), lambda j: (0, j)),
        out_shape=jax.ShapeDtypeStruct((n, m), F32),
        compiler_params=_cparams(("parallel",)),
        name="ada_mod",
    )(c, w, b.reshape(1, m))


def _norm_mod_kernel(x_ref, w_ref, sc_ref, sh_ref, o_ref):
    x = x_ref[...]
    tm = x.shape[0]
    y = x * lax.rsqrt(jnp.mean(x * x, axis=-1, keepdims=True) + EPS) * w_ref[...]
    o_ref[...] = (y * (1.0 + _seq_rows(sc_ref[...], tm)) + _seq_rows(sh_ref[...], tm)).astype(o_ref.dtype)


def _norm_mod(x, w, sc, sh, seq_len, tm):
    n, d = x.shape
    return pl.pallas_call(
        _norm_mod_kernel,
        grid=(n // tm,),
        in_specs=[pl.BlockSpec((tm, d), lambda i: (i, 0)),
                  pl.BlockSpec((1, d), lambda i: (0, 0)),
                  _mod_spec(tm, seq_len, d), _mod_spec(tm, seq_len, d)],
        out_specs=pl.BlockSpec((tm, d), lambda i: (i, 0)),
        out_shape=jax.ShapeDtypeStruct((n, d), BF16),
        compiler_params=_cparams(("parallel",)),
        name="norm1_mod",
    )(x, w.reshape(1, d), sc, sh)


def _mm_kernel(x_ref, w_ref, o_ref, *, act):
    y = _dot(x_ref[...], w_ref[...])
    if act == "sigmoid":
        y = _sigmoid(y)
    o_ref[...] = y.astype(o_ref.dtype)


def _matmul(x, w, tm, tn, act=None, out_dtype=F32, name="matmul"):
    n, k = x.shape
    m = w.shape[1]
    return pl.pallas_call(
        functools.partial(_mm_kernel, act=act),
        grid=(n // tm, m // tn),
        in_specs=[pl.BlockSpec((tm, k), lambda i, j: (i, 0)),
                  pl.BlockSpec((k, tn), lambda i, j: (0, j))],
        out_specs=pl.BlockSpec((tm, tn), lambda i, j: (i, j)),
        out_shape=jax.ShapeDtypeStruct((n, m), out_dtype),
        compiler_params=_cparams(("parallel", "arbitrary")),
        name=name,
    )(x, w)


def _hgrn_tables(c):
    nlev = int(math.log2(c))
    t = np.arange(c)[:, None]
    s = np.arange(c)[None, :]
    seg = []
    for l in range(1, nlev + 1):
        b = 1 << l
        seg.append(((s >= (t // b) * b) & (s <= t)).astype(np.float32))
        seg.append(((s > t) & (s <= (t // b) * b + b - 1)).astype(np.float32))
    masks = [(t == s).astype(np.float32)]
    for l in range(nlev):
        b = 1 << l
        masks.append(((t // (2 * b) == s // (2 * b)) & (t % (2 * b) >= b) & (s % (2 * b) < b)).astype(np.float32))
    return np.concatenate(seg, axis=0), np.stack(masks, axis=0)


def _hgrn_kernel(q_ref, f_ref, i_ref, g_ref, lb_ref, nw_ref, s0_ref, seg_ref, msk_ref,
                 o_ref, sn_ref, st_ref, *, c, nchunks, hb):
    nlev = int(math.log2(c))
    tstep = pl.program_id(2)

    @pl.when(tstep == 0)
    def _():
        for hh in range(hb):
            st_ref[hh] = s0_ref[0, hh].T

    nw = nw_ref[...]
    lb = lb_ref[...]
    head = lambda a, hh: a[:, hh * HG_DK:(hh + 1) * HG_DK]

    def decays(l, g_hi, g_lo):
        seg = seg_ref[2 * (l - 1) * c:2 * l * c, :]
        e = _dot(seg, g_hi) + _dot(seg, g_lo)
        return e[:c, :], e[c:, :]

    def chunk(ci, carry):
        rows = pl.ds(pl.multiple_of(ci * c, c), c)
        q = q_ref[rows, :]
        f = lb + (1.0 - lb) * _sigmoid(f_ref[rows, :])
        g = jnp.log(f)
        k = 1.0 - f
        g_hi, g_lo = _split_bf16(g)
        qb = q.astype(BF16)
        kb = k.astype(BF16)
        scores = [jnp.where(msk_ref[0] > 0, _dot_nt(head(qb, hh), head(kb, hh)), 0.0) for hh in range(hb)]
        for l in range(nlev):
            if l == 0:
                ql, kl = (q * f).astype(BF16), kb
            else:
                wl, vl = decays(l, g_hi, g_lo)
                ql = (q * jnp.exp(wl)).astype(BF16)
                kl = (k * jnp.exp(vl)).astype(BF16)
            for hh in range(hb):
                scores[hh] = scores[hh] + jnp.where(msk_ref[l + 1] > 0, _dot_nt(head(ql, hh), head(kl, hh)), 0.0)
        a_inc, v_end = decays(nlev, g_hi, g_lo)
        qa = (q * jnp.exp(a_inc)).astype(BF16)
        k_end = (k * jnp.exp(v_end)).astype(BF16)
        vb = i_ref[rows, :].astype(BF16)
        carry_decay = jnp.exp(a_inc[c - 1:c, :])
        gate = _silu(g_ref[rows, :])
        for hh in range(hb):
            st = st_ref[hh]
            o = _dot_nt(head(qa, hh), st.astype(BF16)) + _dot(scores[hh].astype(BF16), head(vb, hh))
            st_ref[hh] = st * head(carry_decay, hh) + _dot_tn(head(vb, hh), head(k_end, hh))
            on = o * lax.rsqrt(jnp.mean(o * o, axis=-1, keepdims=True) + EPS) * nw
            o_ref[rows, hh * HG_DK:(hh + 1) * HG_DK] = (on * head(gate, hh)).astype(o_ref.dtype)
        return carry

    lax.fori_loop(0, nchunks, chunk, 0)

    @pl.when(tstep == pl.num_programs(2) - 1)
    def _():
        for hh in range(hb):
            sn_ref[0, hh] = st_ref[hh].T


def _hgrn(z_hg, lb, norm_w, s0, bsz, seq_len):
    n, w4 = z_hg.shape
    nh = w4 // (4 * HG_DK)
    hb = min(nh, 8)
    ng = nh // hb
    c = min(CHUNK, seq_len)
    tb = min(seq_len, 512)
    nt = seq_len // tb
    seg, masks = _hgrn_tables(c)

    def col(part):
        return pl.BlockSpec((tb, hb * HG_DK), lambda b, h, t: (b * nt + t, part * ng + h))

    return pl.pallas_call(
        functools.partial(_hgrn_kernel, c=c, nchunks=tb // c, hb=hb),
        grid=(bsz, ng, nt),
        in_specs=[col(0), col(1), col(2), col(3),
                  pl.BlockSpec((1, hb * HG_DK), lambda b, h, t: (0, h)),
                  pl.BlockSpec((1, HG_DK), lambda b, h, t: (0, 0)),
                  pl.BlockSpec((1, hb, HG_DK, HG_DK), lambda b, h, t: (b, h, 0, 0)),
                  pl.BlockSpec(seg.shape, lambda b, h, t: (0, 0)),
                  pl.BlockSpec(masks.shape, lambda b, h, t: (0, 0, 0))],
        out_specs=[pl.BlockSpec((tb, hb * HG_DK), lambda b, h, t: (b * nt + t, h)),
                   pl.BlockSpec((1, hb, HG_DK, HG_DK), lambda b, h, t: (b, h, 0, 0))],
        out_shape=[jax.ShapeDtypeStruct((n, nh * HG_DK), BF16),
                   jax.ShapeDtypeStruct((bsz, nh, HG_DK, HG_DK), F32)],
        scratch_shapes=[pltpu.VMEM((hb, HG_DK, HG_DK), F32)],
        compiler_params=_cparams(("parallel", "parallel", "arbitrary")),
        name="hgrn2",
    )(z_hg, z_hg, z_hg, z_hg, lb.reshape(1, -1), norm_w.reshape(1, HG_DK), s0,
      jnp.asarray(seg, BF16), jnp.asarray(masks, F32))


def _rope_tables(positions):
    half = ROPE_DIM // 2
    inv_freq = ROPE_THETA ** (-jnp.arange(0, ROPE_DIM, 2, dtype=F32) / ROPE_DIM)
    ang = positions.astype(F32)[:, None] * inv_freq[None, :]
    cos, sin = jnp.cos(ang), jnp.sin(ang)
    t = positions.shape[0]
    rest = AT_HD - ROPE_DIM
    c64 = jnp.concatenate([cos, cos, jnp.ones((t, rest), F32)], axis=1)
    sa64 = jnp.concatenate([-sin, jnp.zeros((t, half + rest), F32)], axis=1)
    sb64 = jnp.concatenate([jnp.zeros((t, half), F32), sin, jnp.zeros((t, rest), F32)], axis=1)
    rep = LANES // AT_HD
    return tuple(jnp.tile(a, (1, rep)) for a in (c64, sa64, sb64))


def _qk_prep_kernel(z_ref, cos_ref, sa_ref, sb_ref, qw_ref, kw_ref, bd_ref, q_ref, kv_ref, *, qd, kd):
    half = ROPE_DIM // 2
    z = z_ref[...]
    cos, sa, sb = cos_ref[...], sa_ref[...], sb_ref[...]

    def norm_rope(x, w, bd, reps):
        x2 = x * x
        hi, lo = _split_bf16(x2)
        ss = _dot(hi, bd) + _dot(lo, bd)
        xn = x * lax.rsqrt(ss * (1.0 / AT_HD) + EPS) * w
        width = x.shape[1]
        tile = lambda a: jnp.concatenate([a] * reps, axis=1) if reps > 1 else a
        return (xn * tile(cos) + pltpu.roll(xn, width - half, 1) * tile(sa)
                + pltpu.roll(xn, half, 1) * tile(sb))

    q = norm_rope(z[:, :qd], qw_ref[...], bd_ref[...], qd // LANES)
    k = norm_rope(z[:, qd:qd + kd], kw_ref[...], bd_ref[:kd, :kd], kd // LANES)
    q_ref[...] = q.astype(q_ref.dtype)
    kv_ref[:, :kd] = k
    kv_ref[:, kd:] = z[:, qd + kd:]


def _qk_prep(z_at, positions, q_norm_w, k_norm_w, seq_len, qd, kd, tm):
    n, w = z_at.shape
    cos, sa, sb = _rope_tables(positions)
    nt = seq_len // tm
    bd = np.kron(np.eye(qd // AT_HD, dtype=np.float32), np.ones((AT_HD, AT_HD), np.float32))
    tab = pl.BlockSpec((tm, LANES), lambda i: (i % nt, 0))
    return pl.pallas_call(
        functools.partial(_qk_prep_kernel, qd=qd, kd=kd),
        grid=(n // tm,),
        in_specs=[pl.BlockSpec((tm, w), lambda i: (i, 0)), tab, tab, tab,
                  pl.BlockSpec((1, qd), lambda i: (0, 0)),
                  pl.BlockSpec((1, kd), lambda i: (0, 0)),
                  pl.BlockSpec((qd, qd), lambda i: (0, 0))],
        out_specs=[pl.BlockSpec((tm, qd), lambda i: (i, 0)),
                   pl.BlockSpec((tm, 2 * kd), lambda i: (i, 0))],
        out_shape=[jax.ShapeDtypeStruct((n, qd), BF16),
                   jax.ShapeDtypeStruct((n, 2 * kd), F32)],
        compiler_params=_cparams(("parallel",)),
        name="qk_norm_rope",
    )(z_at, cos, sa, sb, jnp.tile(q_norm_w, qd // AT_HD).reshape(1, qd),
      jnp.tile(k_norm_w, kd // AT_HD).reshape(1, kd), jnp.asarray(bd, BF16))


def _attend(q, keys, vals, sinks_ref, col_valid, o_ref, n_kv, group):
    n_heads = n_kv * group
    batch = 16
    for h0 in range(0, n_heads, batch):
        heads = range(h0, min(h0 + batch, n_heads))
        scores = []
        for h in heads:
            j = h // group
            s = _dot_nt(q[:, h * AT_HD:(h + 1) * AT_HD], keys[:, j * AT_HD:(j + 1) * AT_HD]) * (AT_HD ** -0.5)
            scores.append(s if col_valid is None else jnp.where(col_valid, s, -jnp.inf))
        probs = []
        for h, s in zip(heads, scores):
            sink = sinks_ref[h]
            m = jnp.maximum(jnp.max(s, axis=-1, keepdims=True), sink)
            e = jnp.exp(s - m)
            den = jnp.sum(e, axis=-1, keepdims=True) + jnp.exp(sink - m)
            probs.append((e / den).astype(BF16))
        for h, p in zip(heads, probs):
            j = h // group
            o_ref[:, h * AT_HD:(h + 1) * AT_HD] = _dot(p, vals[:, j * AT_HD:(j + 1) * AT_HD]).astype(o_ref.dtype)


def _swa_prompt_kernel(sinks_ref, q_ref, kv0_ref, kv1_ref, kv2_ref, o_ref, *, n_kv, group, w_chunks):
    kd = n_kv * AT_HD
    n = pl.program_id(1)
    blocks = [kv0_ref[...], kv1_ref[...], kv2_ref[...]]
    keys = jnp.concatenate([b[:, :kd] for b in blocks], axis=0).astype(BF16)
    vals = jnp.concatenate([b[:, kd:] for b in blocks], axis=0).astype(BF16)
    col_chunk = lax.broadcasted_iota(jnp.int32, (1, keys.shape[0]), 1) // CHUNK
    col_valid = (col_chunk + n - w_chunks) >= 0
    _attend(q_ref[...], keys, vals, sinks_ref, col_valid, o_ref, n_kv, group)


def _swa_prompt(qn, kvn, sinks, bsz, seq_len, n_kv):
    n, qd = qn.shape
    kd = n_kv * AT_HD
    nc = seq_len // CHUNK
    group = qd // AT_HD // n_kv
    w_chunks = 2

    def kv_spec(j):
        return pl.BlockSpec((CHUNK, 2 * kd), lambda b, c, s: (b * nc + jnp.maximum(c - w_chunks + j, 0), 0))

    return pl.pallas_call(
        functools.partial(_swa_prompt_kernel, n_kv=n_kv, group=group, w_chunks=w_chunks),
        grid_spec=pltpu.PrefetchScalarGridSpec(
            num_scalar_prefetch=1,
            grid=(bsz, nc),
            in_specs=[pl.BlockSpec((CHUNK, qd), lambda b, c, s: (b * nc + c, 0)),
                      kv_spec(0), kv_spec(1), kv_spec(2)],
            out_specs=pl.BlockSpec((CHUNK, qd), lambda b, c, s: (b * nc + c, 0))),
        out_shape=jax.ShapeDtypeStruct((n, qd), BF16),
        compiler_params=_cparams(("parallel", "arbitrary")),
        name="swa_prompt",
    )(sinks, qn, kvn, kvn, kvn)


def _swa_sample_kernel(sinks_ref, q_ref, pk_ref, pv_ref, kv_ref, o_ref, *, n_kv, group):
    kd = n_kv * AT_HD
    kv = kv_ref[...]
    keys = jnp.concatenate([pk_ref[0], kv[:, :kd]], axis=0).astype(BF16)
    vals = jnp.concatenate([pv_ref[0], kv[:, kd:]], axis=0).astype(BF16)
    _attend(q_ref[...], keys, vals, sinks_ref, None, o_ref, n_kv, group)


def _swa_sample(qn, kvn, past_k, past_v, sinks, bsz, seq_len, n_kv):
    n, qd = qn.shape
    kd = n_kv * AT_HD
    window = past_k.shape[1]
    group = qd // AT_HD // n_kv
    return pl.pallas_call(
        functools.partial(_swa_sample_kernel, n_kv=n_kv, group=group),
        grid_spec=pltpu.PrefetchScalarGridSpec(
            num_scalar_prefetch=1,
            grid=(bsz,),
            in_specs=[pl.BlockSpec((seq_len, qd), lambda b, s: (b, 0)),
                      pl.BlockSpec((1, window, kd), lambda b, s: (b, 0, 0)),
                      pl.BlockSpec((1, window, kd), lambda b, s: (b, 0, 0)),
                      pl.BlockSpec((seq_len, 2 * kd), lambda b, s: (b, 0))],
            out_specs=pl.BlockSpec((seq_len, qd), lambda b, s: (b, 0))),
        out_shape=jax.ShapeDtypeStruct((n, qd), BF16),
        compiler_params=_cparams(("parallel",)),
        name="swa_sample",
    )(sinks, qn, past_k, past_v, kvn)


HI_MASK = np.uint32(0xFFFF0000)


def _pack_rows(x):
    half = x.shape[1] // 2
    bits = lambda a: lax.bitcast_convert_type(a.astype(BF16).astype(F32), jnp.uint32)
    word = (bits(x[:, half:]) & HI_MASK) | (bits(x[:, :half]) >> 16)
    return lax.bitcast_convert_type(word, jnp.int32)


def _unpack_words(w):
    u = lax.bitcast_convert_type(w, jnp.uint32)
    return lax.bitcast_convert_type(u << 16, F32), lax.bitcast_convert_type(u & HI_MASK, F32)


def _store_slabs(ref, words):
    rows, width = words.shape
    nslab = width // LANES
    for s in range(nslab):
        ref[pl.ds(s, rows, stride=nslab), :] = words[:, s * LANES:(s + 1) * LANES]


def _load_rows(ref, rows, nslab):
    lo, hi = [], []
    for s in range(nslab):
        a, b = _unpack_words(ref[pl.ds(s, rows, stride=nslab), :])
        lo.append(a)
        hi.append(b)
    return jnp.concatenate(lo + hi, axis=1).astype(BF16)


def _merge_kernel(ohg_ref, oat_ref, ga_ref, gb_ref, x_ref, g1_ref, whg_ref, wat_ref, wo_ref,
                  nw_ref, sc_ref, sh_ref, *rest):
    x1_ref, h2_ref = rest[-2:]
    tm = x_ref.shape[0]
    merged = ga_ref[...] * _dot(ohg_ref[...], whg_ref[...]) + gb_ref[...] * _dot(oat_ref[...], wat_ref[...])
    mix = _dot(merged.astype(BF16), wo_ref[...])
    x1 = x_ref[...] + _seq_rows(g1_ref[...], tm) * mix
    x1_ref[...] = x1
    y = x1 * lax.rsqrt(jnp.mean(x1 * x1, axis=-1, keepdims=True) + EPS) * nw_ref[...]
    h2 = y * (1.0 + _seq_rows(sc_ref[...], tm)) + _seq_rows(sh_ref[...], tm)
    _store_slabs(h2_ref, _pack_rows(h2))


def _merge(o_hg, o_at, gates, x, g1, w_hg_out, w_at_out, w_o, norm2_w, sc2, sh2, seq_len, tm, moe_rows, moe_buf):
    n, d = x.shape
    hd = o_hg.shape[1]
    ad = o_at.shape[1]
    nslab = d // (2 * LANES)
    tok0, n_all = moe_rows
    tile0 = tok0 // tm
    const = lambda shape: pl.BlockSpec(shape, lambda i: (0,) * len(shape), pipeline_mode=pl.Buffered(1))
    mod = lambda: _mod_spec(tm, seq_len, d)
    in_specs = [pl.BlockSpec((tm, hd), lambda i: (i, 0)),
                pl.BlockSpec((tm, ad), lambda i: (i, 0)),
                pl.BlockSpec((tm, d), lambda i: (i, 0)),
                pl.BlockSpec((tm, d), lambda i: (i, 1)),
                pl.BlockSpec((tm, d), lambda i: (i, 0)),
                mod(), const((hd, d)), const((ad, d)), const((d, d)), const((1, d)), mod(), mod()]
    args = [o_hg, o_at, gates, gates, x, g1, w_hg_out, w_at_out, w_o, norm2_w.reshape(1, d), sc2, sh2]
    aliases = {}
    if moe_buf is not None:
        in_specs.append(pl.BlockSpec(memory_space=pl.ANY))
        args.append(moe_buf)
        aliases = {len(args) - 1: 1}
    return pl.pallas_call(
        _merge_kernel,
        grid=(n // tm,),
        in_specs=in_specs,
        out_specs=[pl.BlockSpec((tm, d), lambda i: (i, 0)),
                   pl.BlockSpec((tm * nslab, LANES), lambda i: (tile0 + i, 0))],
        out_shape=[jax.ShapeDtypeStruct((n, d), F32),
                   jax.ShapeDtypeStruct((n_all * nslab, LANES), jnp.int32)],
        input_output_aliases=aliases,
        compiler_params=_cparams(("parallel",)),
        name="merge_norm2",
    )(*args)


def _router_kernel(h_ref, wr_ref, bias_ref, tri_ref, eidx_ref, wts_ref, rank_ref, cnt_ref, run_ref,
                   *, nslab, n_exp):
    tm = h_ref.shape[0] // nslab
    gsz = n_exp // N_GROUPS
    step = pl.program_id(0)

    @pl.when(step == 0)
    def _():
        run_ref[...] = jnp.zeros_like(run_ref)

    h = _load_rows(h_ref, tm, nslab)
    scores = _sigmoid(_dot_nt(wr_ref[...], h))
    choice = scores + bias_ref[...]
    neg = -jnp.inf
    row = lax.broadcasted_iota(jnp.int32, (gsz, tm), 0)

    gscore = []
    for gi in range(N_GROUPS):
        cg = choice[gi * gsz:(gi + 1) * gsz, :]
        m1 = jnp.max(cg, axis=0, keepdims=True)
        i1 = jnp.min(jnp.where(cg == m1, row, gsz), axis=0, keepdims=True)
        m2 = jnp.max(jnp.where(row == i1, neg, cg), axis=0, keepdims=True)
        gscore.append(m1 + m2)
    gs = jnp.concatenate(gscore, axis=0)
    grow = lax.broadcasted_iota(jnp.int32, (N_GROUPS, tm), 0)
    gsel = jnp.zeros((N_GROUPS, tm), F32)
    for _ in range(TOPK_GROUPS):
        gm = jnp.max(gs, axis=0, keepdims=True)
        gi = jnp.min(jnp.where(gs == gm, grow, N_GROUPS), axis=0, keepdims=True)
        hit = grow == gi
        gsel = jnp.where(hit, 1.0, gsel)
        gs = jnp.where(hit, neg, gs)
    masked = jnp.concatenate(
        [jnp.where(gsel[gi:gi + 1, :] > 0, choice[gi * gsz:(gi + 1) * gsz, :], neg) for gi in range(N_GROUPS)],
        axis=0)

    erow = lax.broadcasted_iota(jnp.int32, (n_exp, tm), 0)
    idxs, raw = [], []
    for _ in range(TOP_K):
        m = jnp.max(masked, axis=0, keepdims=True)
        i = jnp.min(jnp.where(masked == m, erow, n_exp), axis=0, keepdims=True)
        hit = erow == i
        raw.append(jnp.sum(jnp.where(hit, scores, 0.0), axis=0, keepdims=True))
        masked = jnp.where(hit, neg, masked)
        idxs.append(i)
    total = raw[0]
    for r in raw[1:]:
        total = total + r
    onehot = jnp.zeros((n_exp, tm), F32)
    for i in idxs:
        onehot = onehot + jnp.where(erow == i, 1.0, 0.0)
    before = _dot(onehot.astype(BF16), tri_ref[...]) + run_ref[:, 0:1]
    for kk in range(TOP_K):
        eidx_ref[kk:kk + 1, :] = idxs[kk]
        wts_ref[kk:kk + 1, :] = raw[kk] / total * ROUTED_SCALE
        rank_ref[kk:kk + 1, :] = jnp.sum(jnp.where(erow == idxs[kk], before, 0.0), axis=0, keepdims=True).astype(jnp.int32)
    run_ref[...] = run_ref[...] + jnp.sum(onehot, axis=1, keepdims=True)

    @pl.when(step == pl.num_programs(0) - 1)
    def _():
        cnt_ref[...] = run_ref[...].astype(jnp.int32)


def _slot_kernel(eidx_ref, rank_ref, pstart_ref, slot_ref):
    n_exp = pstart_ref.shape[0]
    tt = TOK_TILE
    erow = lax.broadcasted_iota(jnp.int32, (n_exp, tt), 0)
    pstart = pstart_ref[:, 0:1]
    for j in range(slot_ref.shape[0]):
        lanes = slice(j * tt, (j + 1) * tt)
        for kk in range(TOP_K):
            base = jnp.sum(jnp.where(erow == eidx_ref[kk:kk + 1, lanes], pstart, 0), axis=0, keepdims=True)
            slot_ref[j, kk:kk + 1, :] = base + rank_ref[kk:kk + 1, lanes]


def _slots(eidx, rank, pstart, n_tok):
    n_exp = pstart.shape[0]
    tt = TOK_TILE
    ntile = n_tok // tt
    per_step = max(g for g in range(1, 9) if ntile % g == 0)
    tok_spec = pl.BlockSpec((TOP_K, per_step * tt), lambda i: (0, i))
    return pl.pallas_call(
        _slot_kernel,
        grid=(ntile // per_step,),
        in_specs=[tok_spec, tok_spec, pl.BlockSpec((n_exp, LANES), lambda i: (0, 0))],
        out_specs=pl.BlockSpec((per_step, TOP_K, tt), lambda i: (i, 0, 0)),
        out_shape=jax.ShapeDtypeStruct((ntile, TOP_K, tt), jnp.int32),
        compiler_params=_cparams(("parallel",)),
        name="moe_slots",
    )(eidx, rank, jnp.broadcast_to(pstart[:, None], (n_exp, LANES)))


def _router(h2s, w_router, router_bias, n_tok, d, tm):
    n_exp = w_router.shape[1]
    nslab = d // (2 * LANES)
    tri = np.triu(np.ones((tm, tm), np.float32), 1)
    out_tok = lambda dt: jax.ShapeDtypeStruct((TOP_K, n_tok), dt)
    tok_spec = pl.BlockSpec((TOP_K, tm), lambda i: (0, i))
    return pl.pallas_call(
        functools.partial(_router_kernel, nslab=nslab, n_exp=n_exp),
        grid=(n_tok // tm,),
        in_specs=[pl.BlockSpec((tm * nslab, LANES), lambda i: (i, 0)),
                  pl.BlockSpec((n_exp, d), lambda i: (0, 0)),
                  pl.BlockSpec((n_exp, 1), lambda i: (0, 0)),
                  pl.BlockSpec((tm, tm), lambda i: (0, 0))],
        out_specs=[tok_spec, tok_spec, tok_spec, pl.BlockSpec((n_exp, LANES), lambda i: (0, 0))],
        out_shape=[out_tok(jnp.int32), out_tok(F32), out_tok(jnp.int32),
                   jax.ShapeDtypeStruct((n_exp, LANES), jnp.int32)],
        scratch_shapes=[pltpu.VMEM((n_exp, LANES), F32)],
        compiler_params=_cparams(("arbitrary",)),
        name="router_topk",
    )(h2s, w_router.T.astype(BF16), router_bias.reshape(n_exp, 1), jnp.asarray(tri, BF16))


def _shared_kernel(h_ref, w1_ref, w3_ref, w2_ref, o_ref, *, nslab):
    tm = h_ref.shape[0] // nslab
    h = _load_rows(h_ref, tm, nslab)
    a = (_silu(_dot(h, w1_ref[...])) * _dot(h, w3_ref[...])).astype(BF16)
    o_ref[...] = _dot(a, w2_ref[...])


def _shared_expert(h2s, ws1, ws3, ws2, n_tok, d, tm):
    nslab = d // (2 * LANES)
    ds = ws1.shape[1]
    return pl.pallas_call(
        functools.partial(_shared_kernel, nslab=nslab),
        grid=(n_tok // tm,),
        in_specs=[pl.BlockSpec((tm * nslab, LANES), lambda i: (i, 0)),
                  pl.BlockSpec((d, ds), lambda i: (0, 0)),
                  pl.BlockSpec((d, ds), lambda i: (0, 0)),
                  pl.BlockSpec((ds, d), lambda i: (0, 0))],
        out_specs=pl.BlockSpec((tm, d), lambda i: (i, 0)),
        out_shape=jax.ShapeDtypeStruct((n_tok, d), F32),
        compiler_params=_cparams(("parallel",)),
        name="shared_expert",
    )(h2s, ws1, ws3, ws2)


def _dispatch_kernel(cnt_ref, pstart_ref, slot_hbm, h_ref, xs_hbm, slot_smem, zbuf, ssem, dsem, zsem, *, bm):
    i = pl.program_id(0)
    nslab = xs_hbm.shape[1]
    tt = h_ref.shape[0] // nslab
    n_exp = cnt_ref.shape[0]

    @pl.when(i == 0)
    def _():
        zbuf[...] = jnp.zeros_like(zbuf)

        def walk(e, start):
            cnt = cnt_ref[e]
            pad = lax.rem(bm - lax.rem(cnt, bm), bm)
            base = pstart_ref[e] + cnt
            size = bm // 2
            while size >= 1:
                take = (pad & size) != 0

                @pl.when(take)
                def _(base=base, size=size):
                    cp = pltpu.make_async_copy(zbuf.at[pl.ds(0, size)], xs_hbm.at[pl.ds(base, size)], zsem)
                    if start:
                        cp.start()
                    else:
                        cp.wait()

                base = base + jnp.where(take, size, 0)
                size //= 2

        def start_e(e, carry):
            walk(e, True)
            return carry

        def wait_e(e, carry):
            walk(e, False)
            return carry

        lax.fori_loop(0, n_exp, start_e, 0)
        lax.fori_loop(0, n_exp, wait_e, 0)

    cp = pltpu.make_async_copy(slot_hbm.at[i], slot_smem, ssem)
    cp.start()
    cp.wait()

    def row_copy(t, kk):
        src = h_ref.at[pl.ds(pl.multiple_of(t * nslab, nslab), nslab), :]
        return pltpu.make_async_copy(src, xs_hbm.at[slot_smem[kk, t]], dsem)

    def issue(t, carry):
        for kk in range(TOP_K):
            row_copy(t, kk).start(priority=kk % 2)
        return carry

    lax.fori_loop(0, tt, issue, 0)

    def drain(t, carry):
        for kk in range(TOP_K):
            row_copy(t, kk).wait()
        return carry

    lax.fori_loop(0, tt, drain, 0)


def _dispatch(counts, pstart, slots, h2s, n_tok, nslab, n_rows, bm):
    return pl.pallas_call(
        functools.partial(_dispatch_kernel, bm=bm),
        grid_spec=pltpu.PrefetchScalarGridSpec(
            num_scalar_prefetch=2,
            grid=(n_tok // TOK_TILE,),
            in_specs=[pl.BlockSpec(memory_space=pl.ANY),
                      pl.BlockSpec((TOK_TILE * nslab, LANES), lambda i, c, p: (i, 0))],
            out_specs=pl.BlockSpec(memory_space=pl.ANY),
            scratch_shapes=[pltpu.SMEM((TOP_K, TOK_TILE), jnp.int32),
                            pltpu.VMEM((bm // 2, nslab, LANES), jnp.int32),
                            pltpu.SemaphoreType.DMA, pltpu.SemaphoreType.DMA, pltpu.SemaphoreType.DMA]),
        out_shape=jax.ShapeDtypeStruct((n_rows, nslab, LANES), jnp.int32),
        compiler_params=_cparams(("arbitrary",)),
        name="moe_dispatch",
    )(counts, pstart, slots, h2s)


def _experts_kernel(be_ref, nu_ref, vis_ref, nv_ref, x_ref, w1_hbm, w3_hbm, w2_hbm, o_ref,
                    w1s, w3s, w2s, w1b, w3b, w2b, vcount, wsem, *, nslab):
    r = pl.program_id(0)
    bm = x_ref.shape[0] // nslab

    def weight_copies(j, slot):
        e = vis_ref[j]
        return (pltpu.make_async_copy(w1_hbm.at[e], w1s.at[slot], wsem.at[slot, 0]),
                pltpu.make_async_copy(w3_hbm.at[e], w3s.at[slot], wsem.at[slot, 1]),
                pltpu.make_async_copy(w2_hbm.at[e], w2s.at[slot], wsem.at[slot, 2]))

    @pl.when(r == 0)
    def _():
        vcount[0] = 0
        for cp in weight_copies(0, 0):
            cp.start()

    @pl.when(r < nu_ref[0])
    def _():
        prev = be_ref[jnp.maximum(r - 1, 0)]

        @pl.when(jnp.logical_or(r == 0, be_ref[r] != prev))
        def _():
            j = vcount[0]
            slot = lax.rem(j, 2)
            for cp in weight_copies(j, slot):
                cp.wait()

            @pl.when(j + 1 < nv_ref[0])
            def _():
                for cp in weight_copies(j + 1, 1 - slot):
                    cp.start()

            w1b[...] = w1s[slot].astype(BF16)
            w3b[...] = w3s[slot].astype(BF16)
            w2b[...] = w2s[slot].astype(BF16)
            vcount[0] = j + 1

        x = _load_rows(x_ref, bm, nslab)
        a = (_silu(_dot(x, w1b[...])) * _dot(x, w3b[...])).astype(BF16)
        _store_slabs(o_ref, _pack_rows(_dot(a, w2b[...])))


def _experts(block_e, n_used, visit, n_visit, xs2, w1, w3, w2, d, bm):
    n_exp, _, de = w1.shape
    nslab = d // (2 * LANES)
    nb = xs2.shape[0] // (bm * nslab)
    row_spec = pl.BlockSpec((bm * nslab, LANES), lambda r, be, nu, vis, nv: (jnp.minimum(r, nu[0] - 1), 0))
    hbm = pl.BlockSpec(memory_space=pl.ANY)
    return pl.pallas_call(
        functools.partial(_experts_kernel, nslab=nslab),
        grid_spec=pltpu.PrefetchScalarGridSpec(
            num_scalar_prefetch=4,
            grid=(nb,),
            in_specs=[row_spec, hbm, hbm, hbm],
            out_specs=row_spec,
            scratch_shapes=[pltpu.VMEM((2, d, de), F32), pltpu.VMEM((2, d, de), F32), pltpu.VMEM((2, de, d), F32),
                            pltpu.VMEM((d, de), BF16), pltpu.VMEM((d, de), BF16), pltpu.VMEM((de, d), BF16),
                            pltpu.SMEM((1,), jnp.int32), pltpu.SemaphoreType.DMA((2, 3))]),
        out_shape=jax.ShapeDtypeStruct(xs2.shape, jnp.int32),
        compiler_params=_cparams(("arbitrary",)),
        name="routed_experts",
    )(block_e, n_used, visit, n_visit, xs2, w1, w3, w2)


def _combine_kernel(slot_hbm, os_hbm, wt_ref, sh_ref, x1_ref, g2_ref, o_ref, slot_smem, buf, ssem, gsem,
                    *, nslab, tile0):
    i = pl.program_id(0)
    tt = x1_ref.shape[0]
    rows = TOP_K * tt * nslab

    def row_copy(half, t, kk):
        dst = buf.at[pl.ds(pl.multiple_of(half * rows + (kk * tt + t) * nslab, nslab), nslab), :]
        return pltpu.make_async_copy(os_hbm.at[slot_smem[half, kk, t]], dst, gsem.at[half])

    def fetch(tile, half):
        cp = pltpu.make_async_copy(slot_hbm.at[tile0 + tile], slot_smem.at[half], ssem)
        cp.start()
        cp.wait()

        def issue(t, carry):
            for kk in range(TOP_K):
                row_copy(half, t, kk).start(priority=kk % 2)
            return carry

        lax.fori_loop(0, tt, issue, 0)

    @pl.when(i == 0)
    def _():
        fetch(0, 0)

    @pl.when(i + 1 < pl.num_programs(0))
    def _():
        fetch(i + 1, lax.rem(i + 1, 2))

    half = lax.rem(i, 2)

    def drain(t, carry):
        for kk in range(TOP_K):
            row_copy(half, t, kk).wait()
        return carry

    lax.fori_loop(0, tt, drain, 0)

    wt = wt_ref[...]
    g2 = _seq_rows(g2_ref[...], tt)
    hw = nslab * LANES
    for s in range(nslab):
        y_lo = sh_ref[:, s * LANES:(s + 1) * LANES]
        y_hi = sh_ref[:, hw + s * LANES:hw + (s + 1) * LANES]
        for kk in range(TOP_K):
            lo, hi = _unpack_words(buf[pl.ds(half * rows + kk * tt * nslab + s, tt, stride=nslab), :])
            y_lo = y_lo + wt[:, kk:kk + 1] * lo
            y_hi = y_hi + wt[:, kk:kk + 1] * hi
        for y, c0 in ((y_lo, s * LANES), (y_hi, hw + s * LANES)):
            cols = slice(c0, c0 + LANES)
            o_ref[:, cols] = x1_ref[:, cols] + g2[:, cols] * y


def _combine(slots, os3, wts_t, shared, x1, g2, seq_len, tok0):
    n, d = x1.shape
    nslab = d // (2 * LANES)
    tt = TOK_TILE
    tile0 = tok0 // tt
    return pl.pallas_call(
        functools.partial(_combine_kernel, nslab=nslab, tile0=tile0),
        grid=(n // tt,),
        in_specs=[pl.BlockSpec(memory_space=pl.ANY),
                  pl.BlockSpec(memory_space=pl.ANY),
                  pl.BlockSpec((tt, TOP_K), lambda i: (tile0 + i, 0)),
                  pl.BlockSpec((tt, d), lambda i: (tile0 + i, 0)),
                  pl.BlockSpec((tt, d), lambda i: (i, 0)),
                  _mod_spec(tt, seq_len, d)],
        out_specs=pl.BlockSpec((tt, d), lambda i: (i, 0)),
        out_shape=jax.ShapeDtypeStruct((n, d), F32),
        scratch_shapes=[pltpu.SMEM((2, TOP_K, tt), jnp.int32),
                        pltpu.VMEM((2 * TOP_K * tt * nslab, LANES), jnp.int32),
                        pltpu.SemaphoreType.DMA, pltpu.SemaphoreType.DMA((2,))],
        compiler_params=_cparams(("arbitrary",)),
        name="moe_combine",
    )(slots, os3, wts_t, shared, x1, g2)


def _mixer(x, mod, positions, s0, past_k, past_v, lb, p, w, moe_rows, moe_buf):
    bsz, seq_len, d = x.shape
    n = bsz * seq_len
    sh1, sc1, g1, sh2, sc2, g2 = mod
    tm = 256 if n % 256 == 0 else n
    x2 = x.reshape(n, d)
    h1 = _norm_mod(x2, p["norm1_w"], sc1, sh1, seq_len, tm)
    tmm = 1024 if n % 1024 == 0 else tm
    hg_w = w["w_in_hg"].shape[1]
    at_w = w["w_in_at"].shape[1]
    z_hg = _matmul(h1, w["w_in_hg"], tmm, min(hg_w, 1024), name="w_in_hgrn")
    z_at = _matmul(h1, w["w_in_at"], tmm, at_w, name="w_in_attn")
    gates = _matmul(h1, w["w_in_gate"], tmm, min(2 * d, 1024), act="sigmoid", name="w_in_gates")

    o_hg, s_new = _hgrn(z_hg, lb, p["hg_norm_w"], s0, bsz, seq_len)

    n_kv = p["n_kv"]
    kd = n_kv * AT_HD
    qd = at_w - 2 * kd
    tq = min(seq_len, 256)
    qn, kvn = _qk_prep(z_at, positions, p["q_norm_w"], p["k_norm_w"], seq_len, qd, kd, tq)
    if past_k is None:
        o_at = _swa_prompt(qn, kvn, p["attn_sinks"], bsz, seq_len, n_kv)
    else:
        o_at = _swa_sample(qn, kvn, past_k, past_v, p["attn_sinks"], bsz, seq_len, n_kv)

    x1, h2s = _merge(o_hg, o_at, gates, x2, g1, w["w_hg_out"], w["w_at_out"], w["w_o"],
                     p["norm2_w"], sc2, sh2, seq_len, tm, moe_rows, moe_buf)
    kv3 = kvn.reshape(bsz, seq_len, 2 * kd)
    return x1, h2s, s_new, kv3[:, :, :kd], kv3[:, :, kd:]


def kernel(x_prompt, x_sample, cache_k, cache_v, state_hgrn, c_prompt, c_sample, norm1_w, norm2_w, w_ada, b_ada,
           w_in, hg_lower_bounds, hg_norm_w, q_norm_w, k_norm_w, attn_sinks, w_hg_out, w_at_out, w_o, w_router,
           router_bias, w_exp_gate, w_exp_up, w_exp_down, w_sh_gate, w_sh_up, w_sh_down):
    depth = norm1_w.shape[0]
    assert depth == 1, "single trunk layer"
    bp, tp, d = x_prompt.shape
    bs, ts, _ = x_sample.shape
    window, n_kv = cache_k.shape[2], cache_k.shape[3]
    kd = n_kv * AT_HD
    hg_dim = w_hg_out.shape[1]
    qd = w_at_out.shape[1]
    n_exp = w_router.shape[2]
    nslab = d // (2 * LANES)
    l = 0

    lbs = jnp.cumsum(jax.nn.softmax(hg_lower_bounds.astype(F32), axis=0), axis=0)
    win = w_in[l]
    w = {
        "w_in_hg": win[:, :4 * hg_dim].astype(BF16),
        "w_in_at": win[:, 4 * hg_dim:4 * hg_dim + qd + 2 * kd].astype(BF16),
        "w_in_gate": win[:, 4 * hg_dim + qd + 2 * kd:].astype(BF16),
        "w_hg_out": w_hg_out[l].astype(BF16),
        "w_at_out": w_at_out[l].astype(BF16),
        "w_o": w_o[l].astype(BF16),
    }
    p = {"norm1_w": norm1_w[l], "norm2_w": norm2_w[l], "hg_norm_w": hg_norm_w[l], "q_norm_w": q_norm_w[l],
         "k_norm_w": k_norm_w[l], "attn_sinks": attn_sinks[l], "n_kv": n_kv}

    c_all = jnp.concatenate([c_prompt, c_sample], axis=0)
    mod_all = _ada(c_all, w_ada[l], b_ada[l])
    mod_all = mod_all.reshape(bp + bs, 6, 1, d)
    mod_p = tuple(mod_all[:bp, j] for j in range(6))
    mod_s = tuple(mod_all[bp:, j] for j in range(6))

    pos_p = jnp.arange(tp, dtype=jnp.int32)
    pos_s = PAST_LEN + jnp.arange(ts, dtype=jnp.int32)
    s0_p = jnp.zeros((bp,) + state_hgrn.shape[2:], F32)
    n_p, n_s = bp * tp, bs * ts
    n_tok = n_p + n_s
    x1_p, h2_p, sp, kp, vp = _mixer(x_prompt, mod_p, pos_p, s0_p, None, None, lbs[l], p, w, (0, n_tok), None)
    pk = cache_k[l].reshape(bs, window, kd)
    pv = cache_v[l].reshape(bs, window, kd)
    x1_s, h2s, ss, ks, vs = _mixer(x_sample, mod_s, pos_s, state_hgrn[l], pk, pv, lbs[l], p, w, (n_p, n_tok), h2_p)

    tr = 256 if n_tok % 256 == 0 else TOK_TILE
    eidx, wts, rank, counts = _router(h2s, w_router[l], router_bias[l], n_tok, d, tr)
    shared = _shared_expert(h2s, w_sh_gate[l].astype(BF16), w_sh_up[l].astype(BF16), w_sh_down[l].astype(BF16),
                            n_tok, d, tr)

    bm = MOE_ROWS
    counts = counts[:, 0]
    pcounts = (counts + bm - 1) // bm * bm
    pend = jnp.cumsum(pcounts)
    pstart = pend - pcounts
    nb = -(-(n_tok * TOP_K) // bm) + n_exp
    slots = _slots(eidx, rank, pstart, n_tok)
    block_row0 = jnp.arange(nb, dtype=jnp.int32) * bm
    block_e = jnp.minimum(jnp.sum(pend[None, :] <= block_row0[:, None], axis=1), n_exp - 1).astype(jnp.int32)
    n_used = (pend[-1:] // bm).astype(jnp.int32)

    xs = _dispatch(counts, pstart, slots, h2s, n_tok, nslab, nb * bm, bm)
    visit = jnp.argsort(counts == 0, stable=True).astype(jnp.int32)
    n_visit = jnp.sum(counts > 0).astype(jnp.int32).reshape(1)
    os_ = _experts(block_e, n_used, visit, n_visit, xs.reshape(nb * bm * nslab, LANES),
                   w_exp_gate[l], w_exp_up[l], w_exp_down[l], d, bm)
    os3 = os_.reshape(nb * bm, nslab, LANES)
    wts_t = wts.T
    y_p = _combine(slots, os3, wts_t, shared, x1_p, mod_p[5], tp, 0)
    y_s = _combine(slots, os3, wts_t, shared, x1_s, mod_s[5], ts, n_p)

    def cache_out(a, b_, t):
        return a[:, t - window:].reshape(1, b_, window, n_kv, AT_HD)

    new_k_p = cache_out(kp, bp, tp)
    new_v_p = cache_out(vp, bp, tp)
    keys_s = jnp.concatenate([pk, ks], axis=1)
    vals_s = jnp.concatenate([pv, vs], axis=1)
    new_k_s = cache_out(keys_s, bs, window + ts)
    new_v_s = cache_out(vals_s, bs, window + ts)
    return (y_p.reshape(bp, tp, d), y_s.reshape(bs, ts, d), new_k_p, new_v_p, sp[None],
            new_k_s, new_v_s, ss[None])
```

```python
import functools
import math

import numpy as np
import jax
import jax.numpy as jnp
from jax import lax
from jax.experimental import pallas as pl
from jax.experimental.pallas import tpu as pltpu

EPS = 1e-6
CHUNK = 64
HG_DK = 128
AT_HD = 64
ROPE_DIM = 16
ROPE_THETA = 500000.0
TOP_K = 8
N_GROUPS = 8
TOPK_GROUPS = 4
ROUTED_SCALE = 2.5
PAST_LEN = 2048

LANES = 128
MOE_ROWS = 256
TOK_TILE = 128
VMEM_LIMIT = 56 * 1024 * 1024

F32 = jnp.float32
BF16 = jnp.bfloat16


def _cparams(semantics, vmem=VMEM_LIMIT):
    return pltpu.CompilerParams(dimension_semantics=semantics, vmem_limit_bytes=vmem)


def _sigmoid(x):
    return 1.0 / (1.0 + jnp.exp(-x))


def _silu(x):
    return x * _sigmoid(x)


def _dot(a, b):
    return jnp.dot(a, b, preferred_element_type=F32)


def _dot_nt(a, b):
    return lax.dot_general(a, b, (((1,), (1,)), ((), ())), preferred_element_type=F32)


def _dot_tn(a, b):
    return lax.dot_general(a, b, (((0,), (0,)), ((), ())), preferred_element_type=F32)


def _split_bf16(x):
    hi = x.astype(BF16)
    lo = (x - hi.astype(F32)).astype(BF16)
    return hi, lo


def _seq_rows(m, rows):
    s, _, d = m.shape
    if s == 1:
        return m[0]
    return jnp.broadcast_to(m, (s, rows // s, d)).reshape(rows, d)


def _mod_spec(tm, seq_len, d):
    if tm <= seq_len:
        return pl.BlockSpec((1, 1, d), lambda i: ((i * tm) // seq_len, 0, 0))
    s = tm // seq_len
    return pl.BlockSpec((s, 1, d), lambda i: (i, 0, 0))


def _ada_kernel(c_ref, w_ref, b_ref, o_ref):
    s = _silu(c_ref[...]).astype(BF16)
    o_ref[...] = _dot(s, w_ref[...].astype(BF16)) + b_ref[...]


def _ada(c, w, b):
    n, d = c.shape
    m = w.shape[1]
    tn = min(m, 1024)
    return pl.pallas_call(
        _ada_kernel,
        grid=(m // tn,),
        in_specs=[pl.BlockSpec((n, d), lambda j: (0, 0)),
                  pl.BlockSpec((d, tn), lambda j: (0, j)),
                  pl.BlockSpec((1, tn), lambda j: (0, j))],
        out_specs=pl.BlockSpec((n, tn), lambda j: (0, j)),
        out_shape=jax.ShapeDtypeStruct((n, m), F32),
        compiler_params=_cparams(("parallel",)),
        name="ada_mod",
    )(c, w, b.reshape(1, m))


def _norm_mod_kernel(x_ref, w_ref, sc_ref, sh_ref, o_ref):
    x = x_ref[...]
    tm = x.shape[0]
    y = x * lax.rsqrt(jnp.mean(x * x, axis=-1, keepdims=True) + EPS) * w_ref[...]
    o_ref[...] = (y * (1.0 + _seq_rows(sc_ref[...], tm)) + _seq_rows(sh_ref[...], tm)).astype(o_ref.dtype)


def _norm_mod(x, w, sc, sh, seq_len, tm):
    n, d = x.shape
    return pl.pallas_call(
        _norm_mod_kernel,
        grid=(n // tm,),
        in_specs=[pl.BlockSpec((tm, d), lambda i: (i, 0)),
                  pl.BlockSpec((1, d), lambda i: (0, 0)),
                  _mod_spec(tm, seq_len, d), _mod_spec(tm, seq_len, d)],
        out_specs=pl.BlockSpec((tm, d), lambda i: (i, 0)),
        out_shape=jax.ShapeDtypeStruct((n, d), BF16),
        compiler_params=_cparams(("parallel",)),
        name="norm1_mod",
    )(x, w.reshape(1, d), sc, sh)


def _mm_kernel(x_ref, w_ref, o_ref, *, act):
    y = _dot(x_ref[...], w_ref[...])
    if act == "sigmoid":
        y = _sigmoid(y)
    o_ref[...] = y.astype(o_ref.dtype)


def _matmul(x, w, tm, tn, act=None, out_dtype=F32, name="matmul"):
    n, k = x.shape
    m = w.shape[1]
    return pl.pallas_call(
        functools.partial(_mm_kernel, act=act),
        grid=(n // tm, m // tn),
        in_specs=[pl.BlockSpec((tm, k), lambda i, j: (i, 0)),
                  pl.BlockSpec((k, tn), lambda i, j: (0, j))],
        out_specs=pl.BlockSpec((tm, tn), lambda i, j: (i, j)),
        out_shape=jax.ShapeDtypeStruct((n, m), out_dtype),
        compiler_params=_cparams(("parallel", "arbitrary")),
        name=name,
    )(x, w)


def _hgrn_tables(c):
    nlev = int(math.log2(c))
    t = np.arange(c)[:, None]
    s = np.arange(c)[None, :]
    seg = []
    for l in range(1, nlev + 1):
        b = 1 << l
        seg.append(((s >= (t // b) * b) & (s <= t)).astype(np.float32))
        seg.append(((s > t) & (s <= (t // b) * b + b - 1)).astype(np.float32))
    masks = [(t == s).astype(np.float32)]
    for l in range(nlev):
        b = 1 << l
        masks.append(((t // (2 * b) == s // (2 * b)) & (t % (2 * b) >= b) & (s % (2 * b) < b)).astype(np.float32))
    return np.concatenate(seg, axis=0), np.stack(masks, axis=0)


def _hgrn_kernel(q_ref, f_ref, i_ref, g_ref, lb_ref, nw_ref, s0_ref, seg_ref, msk_ref,
                 o_ref, sn_ref, st_ref, *, c, nchunks, hb):
    nlev = int(math.log2(c))
    tstep = pl.program_id(2)

    @pl.when(tstep == 0)
    def _():
        for hh in range(hb):
            st_ref[hh] = s0_ref[0, hh].T

    nw = nw_ref[...]
    lb = lb_ref[...]
    head = lambda a, hh: a[:, hh * HG_DK:(hh + 1) * HG_DK]

    def decays(l, g_hi, g_lo):
        seg = seg_ref[2 * (l - 1) * c:2 * l * c, :]
        e = _dot(seg, g_hi) + _dot(seg, g_lo)
        return e[:c, :], e[c:, :]

    def chunk(ci, carry):
        rows = pl.ds(pl.multiple_of(ci * c, c), c)
        q = q_ref[rows, :]
        f = lb + (1.0 - lb) * _sigmoid(f_ref[rows, :])
        g = jnp.log(f)
        k = 1.0 - f
        g_hi, g_lo = _split_bf16(g)
        qb = q.astype(BF16)
        kb = k.astype(BF16)
        scores = [jnp.where(msk_ref[0] > 0, _dot_nt(head(qb, hh), head(kb, hh)), 0.0) for hh in range(hb)]
        for l in range(nlev):
            if l == 0:
                ql, kl = (q * f).astype(BF16), kb
            else:
                wl, vl = decays(l, g_hi, g_lo)
                ql = (q * jnp.exp(wl)).astype(BF16)
                kl = (k * jnp.exp(vl)).astype(BF16)
            for hh in range(hb):
                scores[hh] = scores[hh] + jnp.where(msk_ref[l + 1] > 0, _dot_nt(head(ql, hh), head(kl, hh)), 0.0)
        a_inc, v_end = decays(nlev, g_hi, g_lo)
        qa = (q * jnp.exp(a_inc)).astype(BF16)
        k_end = (k * jnp.exp(v_end)).astype(BF16)
        vb = i_ref[rows, :].astype(BF16)
        carry_decay = jnp.exp(a_inc[c - 1:c, :])
        gate = _silu(g_ref[rows, :])
        for hh in range(hb):
            st = st_ref[hh]
            o = _dot_nt(head(qa, hh), st.astype(BF16)) + _dot(scores[hh].astype(BF16), head(vb, hh))
            st_ref[hh] = st * head(carry_decay, hh) + _dot_tn(head(vb, hh), head(k_end, hh))
            on = o * lax.rsqrt(jnp.mean(o * o, axis=-1, keepdims=True) + EPS) * nw
            o_ref[rows, hh * HG_DK:(hh + 1) * HG_DK] = (on * head(gate, hh)).astype(o_ref.dtype)
        return carry

    lax.fori_loop(0, nchunks, chunk, 0)

    @pl.when(tstep == pl.num_programs(2) - 1)
    def _():
        for hh in range(hb):
            sn_ref[0, hh] = st_ref[hh].T


def _hgrn(z_hg, lb, norm_w, s0, bsz, seq_len):
    n, w4 = z_hg.shape
    nh = w4 // (4 * HG_DK)
    hb = min(nh, 8)
    ng = nh // hb
    c = min(CHUNK, seq_len)
    tb = min(seq_len, 512)
    nt = seq_len // tb
    seg, masks = _hgrn_tables(c)

    def col(part):
        return pl.BlockSpec((tb, hb * HG_DK), lambda b, h, t: (b * nt + t, part * ng + h))

    return pl.pallas_call(
        functools.partial(_hgrn_kernel, c=c, nchunks=tb // c, hb=hb),
        grid=(bsz, ng, nt),
        in_specs=[col(0), col(1), col(2), col(3),
                  pl.BlockSpec((1, hb * HG_DK), lambda b, h, t: (0, h)),
                  pl.BlockSpec((1, HG_DK), lambda b, h, t: (0, 0)),
                  pl.BlockSpec((1, hb, HG_DK, HG_DK), lambda b, h, t: (b, h, 0, 0)),
                  pl.BlockSpec(seg.shape, lambda b, h, t: (0, 0)),
                  pl.BlockSpec(masks.shape, lambda b, h, t: (0, 0, 0))],
        out_specs=[pl.BlockSpec((tb, hb * HG_DK), lambda b, h, t: (b * nt + t, h)),
                   pl.BlockSpec((1, hb, HG_DK, HG_DK), lambda b, h, t: (b, h, 0, 0))],
        out_shape=[jax.ShapeDtypeStruct((n, nh * HG_DK), BF16),
                   jax.ShapeDtypeStruct((bsz, nh, HG_DK, HG_DK), F32)],
        scratch_shapes=[pltpu.VMEM((hb, HG_DK, HG_DK), F32)],
        compiler_params=_cparams(("parallel", "parallel", "arbitrary")),
        name="hgrn2",
    )(z_hg, z_hg, z_hg, z_hg, lb.reshape(1, -1), norm_w.reshape(1, HG_DK), s0,
      jnp.asarray(seg, BF16), jnp.asarray(masks, F32))


def _rope_tables(positions):
    half = ROPE_DIM // 2
    inv_freq = ROPE_THETA ** (-jnp.arange(0, ROPE_DIM, 2, dtype=F32) / ROPE_DIM)
    ang = positions.astype(F32)[:, None] * inv_freq[None, :]
    cos, sin = jnp.cos(ang), jnp.sin(ang)
    t = positions.shape[0]
    rest = AT_HD - ROPE_DIM
    c64 = jnp.concatenate([cos, cos, jnp.ones((t, rest), F32)], axis=1)
    sa64 = jnp.concatenate([-sin, jnp.zeros((t, half + rest), F32)], axis=1)
    sb64 = jnp.concatenate([jnp.zeros((t, half), F32), sin, jnp.zeros((t, rest), F32)], axis=1)
    rep = LANES // AT_HD
    return tuple(jnp.tile(a, (1, rep)) for a in (c64, sa64, sb64))


def _qk_prep_kernel(z_ref, cos_ref, sa_ref, sb_ref, qw_ref, kw_ref, bd_ref, q_ref, kv_ref, *, qd, kd):
    half = ROPE_DIM // 2
    z = z_ref[...]
    cos, sa, sb = cos_ref[...], sa_ref[...], sb_ref[...]

    def norm_rope(x, w, bd, reps):
        x2 = x * x
        hi, lo = _split_bf16(x2)
        ss = _dot(hi, bd) + _dot(lo, bd)
        xn = x * lax.rsqrt(ss * (1.0 / AT_HD) + EPS) * w
        width = x.shape[1]
        tile = lambda a: jnp.concatenate([a] * reps, axis=1) if reps > 1 else a
        return (xn * tile(cos) + pltpu.roll(xn, width - half, 1) * tile(sa)
                + pltpu.roll(xn, half, 1) * tile(sb))

    q = norm_rope(z[:, :qd], qw_ref[...], bd_ref[...], qd // LANES)
    k = norm_rope(z[:, qd:qd + kd], kw_ref[...], bd_ref[:kd, :kd], kd // LANES)
    q_ref[...] = q.astype(q_ref.dtype)
    kv_ref[:, :kd] = k
    kv_ref[:, kd:] = z[:, qd + kd:]


def _qk_prep(z_at, positions, q_norm_w, k_norm_w, seq_len, qd, kd, tm):
    n, w = z_at.shape
    cos, sa, sb = _rope_tables(positions)
    nt = seq_len // tm
    bd = np.kron(np.eye(qd // AT_HD, dtype=np.float32), np.ones((AT_HD, AT_HD), np.float32))
    tab = pl.BlockSpec((tm, LANES), lambda i: (i % nt, 0))
    return pl.pallas_call(
        functools.partial(_qk_prep_kernel, qd=qd, kd=kd),
        grid=(n // tm,),
        in_specs=[pl.BlockSpec((tm, w), lambda i: (i, 0)), tab, tab, tab,
                  pl.BlockSpec((1, qd), lambda i: (0, 0)),
                  pl.BlockSpec((1, kd), lambda i: (0, 0)),
                  pl.BlockSpec((qd, qd), lambda i: (0, 0))],
        out_specs=[pl.BlockSpec((tm, qd), lambda i: (i, 0)),
                   pl.BlockSpec((tm, 2 * kd), lambda i: (i, 0))],
        out_shape=[jax.ShapeDtypeStruct((n, qd), BF16),
                   jax.ShapeDtypeStruct((n, 2 * kd), F32)],
        compiler_params=_cparams(("parallel",)),
        name="qk_norm_rope",
    )(z_at, cos, sa, sb, jnp.tile(q_norm_w, qd // AT_HD).reshape(1, qd),
      jnp.tile(k_norm_w, kd // AT_HD).reshape(1, kd), jnp.asarray(bd, BF16))


def _attend(q, keys, vals, sinks_ref, col_valid, o_ref, n_kv, group):
    n_heads = n_kv * group
    batch = 16
    for h0 in range(0, n_heads, batch):
        heads = range(h0, min(h0 + batch, n_heads))
        scores = []
        for h in heads:
            j = h // group
            s = _dot_nt(q[:, h * AT_HD:(h + 1) * AT_HD], keys[:, j * AT_HD:(j + 1) * AT_HD]) * (AT_HD ** -0.5)
            scores.append(s if col_valid is None else jnp.where(col_valid, s, -jnp.inf))
        probs = []
        for h, s in zip(heads, scores):
            sink = sinks_ref[h]
            m = jnp.maximum(jnp.max(s, axis=-1, keepdims=True), sink)
            e = jnp.exp(s - m)
            den = jnp.sum(e, axis=-1, keepdims=True) + jnp.exp(sink - m)
            probs.append((e / den).astype(BF16))
        for h, p in zip(heads, probs):
            j = h // group
            o_ref[:, h * AT_HD:(h + 1) * AT_HD] = _dot(p, vals[:, j * AT_HD:(j + 1) * AT_HD]).astype(o_ref.dtype)


def _swa_prompt_kernel(sinks_ref, q_ref, kv0_ref, kv1_ref, kv2_ref, o_ref, *, n_kv, group, w_chunks):
    kd = n_kv * AT_HD
    n = pl.program_id(1)
    blocks = [kv0_ref[...], kv1_ref[...], kv2_ref[...]]
    keys = jnp.concatenate([b[:, :kd] for b in blocks], axis=0).astype(BF16)
    vals = jnp.concatenate([b[:, kd:] for b in blocks], axis=0).astype(BF16)
    col_chunk = lax.broadcasted_iota(jnp.int32, (1, keys.shape[0]), 1) // CHUNK
    col_valid = (col_chunk + n - w_chunks) >= 0
    _attend(q_ref[...], keys, vals, sinks_ref, col_valid, o_ref, n_kv, group)


def _swa_prompt(qn, kvn, sinks, bsz, seq_len, n_kv):
    n, qd = qn.shape
    kd = n_kv * AT_HD
    nc = seq_len // CHUNK
    group = qd // AT_HD // n_kv
    w_chunks = 2

    def kv_spec(j):
        return pl.BlockSpec((CHUNK, 2 * kd), lambda b, c, s: (b * nc + jnp.maximum(c - w_chunks + j, 0), 0))

    return pl.pallas_call(
        functools.partial(_swa_prompt_kernel, n_kv=n_kv, group=group, w_chunks=w_chunks),
        grid_spec=pltpu.PrefetchScalarGridSpec(
            num_scalar_prefetch=1,
            grid=(bsz, nc),
            in_specs=[pl.BlockSpec((CHUNK, qd), lambda b, c, s: (b * nc + c, 0)),
                      kv_spec(0), kv_spec(1), kv_spec(2)],
            out_specs=pl.BlockSpec((CHUNK, qd), lambda b, c, s: (b * nc + c, 0))),
        out_shape=jax.ShapeDtypeStruct((n, qd), BF16),
        compiler_params=_cparams(("parallel", "arbitrary")),
        name="swa_prompt",
    )(sinks, qn, kvn, kvn, kvn)


def _swa_sample_kernel(sinks_ref, q_ref, pk_ref, pv_ref, kv_ref, o_ref, *, n_kv, group):
    kd = n_kv * AT_HD
    kv = kv_ref[...]
    keys = jnp.concatenate([pk_ref[0], kv[:, :kd]], axis=0).astype(BF16)
    vals = jnp.concatenate([pv_ref[0], kv[:, kd:]], axis=0).astype(BF16)
    _attend(q_ref[...], keys, vals, sinks_ref, None, o_ref, n_kv, group)


def _swa_sample(qn, kvn, past_k, past_v, sinks, bsz, seq_len, n_kv):
    n, qd = qn.shape
    kd = n_kv * AT_HD
    window = past_k.shape[1]
    group = qd // AT_HD // n_kv
    return pl.pallas_call(
        functools.partial(_swa_sample_kernel, n_kv=n_kv, group=group),
        grid_spec=pltpu.PrefetchScalarGridSpec(
            num_scalar_prefetch=1,
            grid=(bsz,),
            in_specs=[pl.BlockSpec((seq_len, qd), lambda b, s: (b, 0)),
                      pl.BlockSpec((1, window, kd), lambda b, s: (b, 0, 0)),
                      pl.BlockSpec((1, window, kd), lambda b, s: (b, 0, 0)),
                      pl.BlockSpec((seq_len, 2 * kd), lambda b, s: (b, 0))],
            out_specs=pl.BlockSpec((seq_len, qd), lambda b, s: (b, 0))),
        out_shape=jax.ShapeDtypeStruct((n, qd), BF16),
        compiler_params=_cparams(("parallel",)),
        name="swa_sample",
    )(sinks, qn, past_k, past_v, kvn)


HI_MASK = np.uint32(0xFFFF0000)


def _pack_rows(x):
    half = x.shape[1] // 2
    bits = lambda a: lax.bitcast_convert_type(a.astype(BF16).astype(F32), jnp.uint32)
    word = (bits(x[:, half:]) & HI_MASK) | (bits(x[:, :half]) >> 16)
    return lax.bitcast_convert_type(word, jnp.int32)


def _unpack_words(w):
    u = lax.bitcast_convert_type(w, jnp.uint32)
    return lax.bitcast_convert_type(u << 16, F32), lax.bitcast_convert_type(u & HI_MASK, F32)


def _store_slabs(ref, words, row0=0):
    rows, width = words.shape
    nslab = width // LANES
    for s in range(nslab):
        ref[pl.ds(row0 + s, rows, stride=nslab), :] = words[:, s * LANES:(s + 1) * LANES]


def _load_rows(ref, rows, nslab, row0=0):
    lo, hi = [], []
    for s in range(nslab):
        a, b = _unpack_words(ref[pl.ds(row0 + s, rows, stride=nslab), :])
        lo.append(a)
        hi.append(b)
    return jnp.concatenate(lo + hi, axis=1).astype(BF16)


def _merge_kernel(ohg_ref, oat_ref, ga_ref, gb_ref, x_ref, g1_ref, whg_ref, wat_ref, wo_ref,
                  nw_ref, sc_ref, sh_ref, *rest):
    x1_ref, h2_ref = rest[-2:]
    tm = x_ref.shape[0]
    merged = ga_ref[...] * _dot(ohg_ref[...], whg_ref[...]) + gb_ref[...] * _dot(oat_ref[...], wat_ref[...])
    mix = _dot(merged.astype(BF16), wo_ref[...])
    x1 = x_ref[...] + _seq_rows(g1_ref[...], tm) * mix
    x1_ref[...] = x1
    y = x1 * lax.rsqrt(jnp.mean(x1 * x1, axis=-1, keepdims=True) + EPS) * nw_ref[...]
    h2 = y * (1.0 + _seq_rows(sc_ref[...], tm)) + _seq_rows(sh_ref[...], tm)
    _store_slabs(h2_ref, _pack_rows(h2))


def _merge(o_hg, o_at, gates, x, g1, w_hg_out, w_at_out, w_o, norm2_w, sc2, sh2, seq_len, tm, moe_rows, moe_buf):
    n, d = x.shape
    hd = o_hg.shape[1]
    ad = o_at.shape[1]
    nslab = d // (2 * LANES)
    tok0, n_all = moe_rows
    tile0 = tok0 // tm
    const = lambda shape: pl.BlockSpec(shape, lambda i: (0,) * len(shape), pipeline_mode=pl.Buffered(1))
    mod = lambda: _mod_spec(tm, seq_len, d)
    in_specs = [pl.BlockSpec((tm, hd), lambda i: (i, 0)),
                pl.BlockSpec((tm, ad), lambda i: (i, 0)),
                pl.BlockSpec((tm, d), lambda i: (i, 0)),
                pl.BlockSpec((tm, d), lambda i: (i, 1)),
                pl.BlockSpec((tm, d), lambda i: (i, 0)),
                mod(), const((hd, d)), const((ad, d)), const((d, d)), const((1, d)), mod(), mod()]
    args = [o_hg, o_at, gates, gates, x, g1, w_hg_out, w_at_out, w_o, norm2_w.reshape(1, d), sc2, sh2]
    aliases = {}
    if moe_buf is not None:
        in_specs.append(pl.BlockSpec(memory_space=pl.ANY))
        args.append(moe_buf)
        aliases = {len(args) - 1: 1}
    return pl.pallas_call(
        _merge_kernel,
        grid=(n // tm,),
        in_specs=in_specs,
        out_specs=[pl.BlockSpec((tm, d), lambda i: (i, 0)),
                   pl.BlockSpec((tm * nslab, LANES), lambda i: (tile0 + i, 0))],
        out_shape=[jax.ShapeDtypeStruct((n, d), F32),
                   jax.ShapeDtypeStruct((n_all * nslab, LANES), jnp.int32)],
        input_output_aliases=aliases,
        compiler_params=_cparams(("parallel",)),
        name="merge_norm2",
    )(*args)


def _router_kernel(h_ref, wr_ref, bias_ref, tri_ref, eidx_ref, wts_ref, rank_ref, cnt_ref, run_ref,
                   *, nslab, n_exp):
    tm = h_ref.shape[0] // nslab
    gsz = n_exp // N_GROUPS
    step = pl.program_id(0)

    @pl.when(step == 0)
    def _():
        run_ref[...] = jnp.zeros_like(run_ref)

    h = _load_rows(h_ref, tm, nslab)
    scores = _sigmoid(_dot_nt(wr_ref[...], h))
    choice = scores + bias_ref[...]
    neg = -jnp.inf
    row = lax.broadcasted_iota(jnp.int32, (gsz, tm), 0)

    gscore = []
    for gi in range(N_GROUPS):
        cg = choice[gi * gsz:(gi + 1) * gsz, :]
        m1 = jnp.max(cg, axis=0, keepdims=True)
        i1 = jnp.min(jnp.where(cg == m1, row, gsz), axis=0, keepdims=True)
        m2 = jnp.max(jnp.where(row == i1, neg, cg), axis=0, keepdims=True)
        gscore.append(m1 + m2)
    gs = jnp.concatenate(gscore, axis=0)
    grow = lax.broadcasted_iota(jnp.int32, (N_GROUPS, tm), 0)
    gsel = jnp.zeros((N_GROUPS, tm), F32)
    for _ in range(TOPK_GROUPS):
        gm = jnp.max(gs, axis=0, keepdims=True)
        gi = jnp.min(jnp.where(gs == gm, grow, N_GROUPS), axis=0, keepdims=True)
        hit = grow == gi
        gsel = jnp.where(hit, 1.0, gsel)
        gs = jnp.where(hit, neg, gs)
    masked = jnp.concatenate(
        [jnp.where(gsel[gi:gi + 1, :] > 0, choice[gi * gsz:(gi + 1) * gsz, :], neg) for gi in range(N_GROUPS)],
        axis=0)

    erow = lax.broadcasted_iota(jnp.int32, (n_exp, tm), 0)
    idxs, raw = [], []
    for _ in range(TOP_K):
        m = jnp.max(masked, axis=0, keepdims=True)
        i = jnp.min(jnp.where(masked == m, erow, n_exp), axis=0, keepdims=True)
        hit = erow == i
        raw.append(jnp.sum(jnp.where(hit, scores, 0.0), axis=0, keepdims=True))
        masked = jnp.where(hit, neg, masked)
        idxs.append(i)
    total = raw[0]
    for r in raw[1:]:
        total = total + r
    onehot = jnp.zeros((n_exp, tm), F32)
    for i in idxs:
        onehot = onehot + jnp.where(erow == i, 1.0, 0.0)
    before = _dot(onehot.astype(BF16), tri_ref[...]) + run_ref[:, 0:1]
    for kk in range(TOP_K):
        eidx_ref[kk:kk + 1, :] = idxs[kk]
        wts_ref[kk:kk + 1, :] = raw[kk] / total * ROUTED_SCALE
        rank_ref[kk:kk + 1, :] = jnp.sum(jnp.where(erow == idxs[kk], before, 0.0), axis=0, keepdims=True).astype(jnp.int32)
    run_ref[...] = run_ref[...] + jnp.sum(onehot, axis=1, keepdims=True)

    @pl.when(step == pl.num_programs(0) - 1)
    def _():
        cnt_ref[...] = run_ref[...].astype(jnp.int32)


def _slot_kernel(eidx_ref, rank_ref, pstart_ref, slot_ref):
    n_exp = pstart_ref.shape[0]
    tt = TOK_TILE
    erow = lax.broadcasted_iota(jnp.int32, (n_exp, tt), 0)
    pstart = pstart_ref[:, 0:1]
    for j in range(slot_ref.shape[0]):
        lanes = slice(j * tt, (j + 1) * tt)
        for kk in range(TOP_K):
            base = jnp.sum(jnp.where(erow == eidx_ref[kk:kk + 1, lanes], pstart, 0), axis=0, keepdims=True)
            slot_ref[j, kk:kk + 1, :] = base + rank_ref[kk:kk + 1, lanes]


def _slots(eidx, rank, pstart, n_tok):
    n_exp = pstart.shape[0]
    tt = TOK_TILE
    ntile = n_tok // tt
    per_step = max(g for g in range(1, 9) if ntile % g == 0)
    tok_spec = pl.BlockSpec((TOP_K, per_step * tt), lambda i: (0, i))
    return pl.pallas_call(
        _slot_kernel,
        grid=(ntile // per_step,),
        in_specs=[tok_spec, tok_spec, pl.BlockSpec((n_exp, LANES), lambda i: (0, 0))],
        out_specs=pl.BlockSpec((per_step, TOP_K, tt), lambda i: (i, 0, 0)),
        out_shape=jax.ShapeDtypeStruct((ntile, TOP_K, tt), jnp.int32),
        compiler_params=_cparams(("parallel",)),
        name="moe_slots",
    )(eidx, rank, jnp.broadcast_to(pstart[:, None], (n_exp, LANES)))


def _router(h2s, w_router, router_bias, n_tok, d, tm):
    n_exp = w_router.shape[1]
    nslab = d // (2 * LANES)
    tri = np.triu(np.ones((tm, tm), np.float32), 1)
    out_tok = lambda dt: jax.ShapeDtypeStruct((TOP_K, n_tok), dt)
    tok_spec = pl.BlockSpec((TOP_K, tm), lambda i: (0, i))
    return pl.pallas_call(
        functools.partial(_router_kernel, nslab=nslab, n_exp=n_exp),
        grid=(n_tok // tm,),
        in_specs=[pl.BlockSpec((tm * nslab, LANES), lambda i: (i, 0)),
                  pl.BlockSpec((n_exp, d), lambda i: (0, 0)),
                  pl.BlockSpec((n_exp, 1), lambda i: (0, 0)),
                  pl.BlockSpec((tm, tm), lambda i: (0, 0))],
        out_specs=[tok_spec, tok_spec, tok_spec, pl.BlockSpec((n_exp, LANES), lambda i: (0, 0))],
        out_shape=[out_tok(jnp.int32), out_tok(F32), out_tok(jnp.int32),
                   jax.ShapeDtypeStruct((n_exp, LANES), jnp.int32)],
        scratch_shapes=[pltpu.VMEM((n_exp, LANES), F32)],
        compiler_params=_cparams(("arbitrary",)),
        name="router_topk",
    )(h2s, w_router.T.astype(BF16), router_bias.reshape(n_exp, 1), jnp.asarray(tri, BF16))


def _shared_kernel(h_ref, w1_ref, w3_ref, w2_ref, o_ref, *, nslab):
    tm = h_ref.shape[0] // nslab
    h = _load_rows(h_ref, tm, nslab)
    a = (_silu(_dot(h, w1_ref[...])) * _dot(h, w3_ref[...])).astype(BF16)
    o_ref[...] = _dot(a, w2_ref[...])


def _shared_expert(h2s, ws1, ws3, ws2, n_tok, d, tm):
    nslab = d // (2 * LANES)
    ds = ws1.shape[1]
    return pl.pallas_call(
        functools.partial(_shared_kernel, nslab=nslab),
        grid=(n_tok // tm,),
        in_specs=[pl.BlockSpec((tm * nslab, LANES), lambda i: (i, 0)),
                  pl.BlockSpec((d, ds), lambda i: (0, 0)),
                  pl.BlockSpec((d, ds), lambda i: (0, 0)),
                  pl.BlockSpec((ds, d), lambda i: (0, 0))],
        out_specs=pl.BlockSpec((tm, d), lambda i: (i, 0)),
        out_shape=jax.ShapeDtypeStruct((n_tok, d), F32),
        compiler_params=_cparams(("parallel",)),
        name="shared_expert",
    )(h2s, ws1, ws3, ws2)


def _dispatch_kernel(cnt_ref, pstart_ref, slot_hbm, h_ref, xs_hbm, slot_smem, zbuf, ssem, dsem, zsem, *, bm):
    i = pl.program_id(0)
    nslab = xs_hbm.shape[1]
    tt = h_ref.shape[0] // nslab
    n_exp = cnt_ref.shape[0]

    @pl.when(i == 0)
    def _():
        zbuf[...] = jnp.zeros_like(zbuf)

        def walk(e, start):
            cnt = cnt_ref[e]
            pad = lax.rem(bm - lax.rem(cnt, bm), bm)
            base = pstart_ref[e] + cnt
            size = bm // 2
            while size >= 1:
                take = (pad & size) != 0

                @pl.when(take)
                def _(base=base, size=size):
                    cp = pltpu.make_async_copy(zbuf.at[pl.ds(0, size)], xs_hbm.at[pl.ds(base, size)], zsem)
                    if start:
                        cp.start()
                    else:
                        cp.wait()

                base = base + jnp.where(take, size, 0)
                size //= 2

        def start_e(e, carry):
            walk(e, True)
            return carry

        def wait_e(e, carry):
            walk(e, False)
            return carry

        lax.fori_loop(0, n_exp, start_e, 0)
        lax.fori_loop(0, n_exp, wait_e, 0)

    cp = pltpu.make_async_copy(slot_hbm.at[i], slot_smem, ssem)
    cp.start()
    cp.wait()

    def row_copy(t, kk):
        src = h_ref.at[pl.ds(pl.multiple_of(t * nslab, nslab), nslab), :]
        return pltpu.make_async_copy(src, xs_hbm.at[slot_smem[kk, t]], dsem)

    def issue(t, carry):
        for kk in range(TOP_K):
            row_copy(t, kk).start(priority=kk % 2)
        return carry

    lax.fori_loop(0, tt, issue, 0)

    def drain(t, carry):
        for kk in range(TOP_K):
            row_copy(t, kk).wait()
        return carry

    lax.fori_loop(0, tt, drain, 0)


def _dispatch(counts, pstart, slots, h2s, n_tok, nslab, n_rows, bm):
    return pl.pallas_call(
        functools.partial(_dispatch_kernel, bm=bm),
        grid_spec=pltpu.PrefetchScalarGridSpec(
            num_scalar_prefetch=2,
            grid=(n_tok // TOK_TILE,),
            in_specs=[pl.BlockSpec(memory_space=pl.ANY),
                      pl.BlockSpec((TOK_TILE * nslab, LANES), lambda i, c, p: (i, 0))],
            out_specs=pl.BlockSpec(memory_space=pl.ANY),
            scratch_shapes=[pltpu.SMEM((TOP_K, TOK_TILE), jnp.int32),
                            pltpu.VMEM((bm // 2, nslab, LANES), jnp.int32),
                            pltpu.SemaphoreType.DMA, pltpu.SemaphoreType.DMA, pltpu.SemaphoreType.DMA]),
        out_shape=jax.ShapeDtypeStruct((n_rows, nslab, LANES), jnp.int32),
        compiler_params=_cparams(("arbitrary",)),
        name="moe_dispatch",
    )(counts, pstart, slots, h2s)


def _experts_kernel(vis_ref, nv_ref, pstart_ref, pcnt_ref, nu_ref, xs_hbm, w1_ref, w3_ref, w2_ref, os_hbm,
                    xbuf, obuf, w1b, w3b, w2b, xsem, osem, *, nslab, bm):
    j = pl.program_id(0)
    rows = bm * nslab
    n_used = nu_ref[0]

    def x_copy(g, slot):
        src = xs_hbm.at[pl.ds(pl.multiple_of(g * rows, rows), rows), :]
        return pltpu.make_async_copy(src, xbuf.at[pl.ds(pl.multiple_of(slot * rows, rows), rows), :], xsem.at[slot])

    def o_copy(g, slot):
        dst = os_hbm.at[pl.ds(pl.multiple_of(g * rows, rows), rows), :]
        return pltpu.make_async_copy(obuf.at[pl.ds(pl.multiple_of(slot * rows, rows), rows), :], dst, osem.at[slot])

    @pl.when(j == 0)
    def _():
        x_copy(0, 0).start()

    @pl.when(j < nv_ref[0])
    def _():
        e = vis_ref[j]
        g0 = pstart_ref[e] // bm
        w1b[...] = w1_ref[0].astype(BF16)
        w3b[...] = w3_ref[0].astype(BF16)
        w2b[...] = w2_ref[0].astype(BF16)

        def block(b, carry):
            g = g0 + b
            slot = lax.rem(g, 2)
            x_copy(g, slot).wait()

            @pl.when(g + 1 < n_used)
            def _():
                x_copy(g + 1, 1 - slot).start()

            x = _load_rows(xbuf, bm, nslab, slot * rows)
            a = (_silu(_dot(x, w1b[...])) * _dot(x, w3b[...])).astype(BF16)
            out = _pack_rows(_dot(a, w2b[...]))

            @pl.when(g >= 2)
            def _():
                o_copy(g - 2, slot).wait()

            _store_slabs(obuf, out, slot * rows)
            o_copy(g, slot).start()
            return carry

        lax.fori_loop(0, pcnt_ref[e] // bm, block, 0)

    @pl.when(j == pl.num_programs(0) - 1)
    def _():
        @pl.when(n_used >= 2)
        def _():
            o_copy(n_used - 2, lax.rem(n_used, 2)).wait()

        o_copy(n_used - 1, lax.rem(n_used - 1, 2)).wait()


def _experts(visit, n_visit, pstart, pcounts, n_used, xs2, w1, w3, w2, d, bm):
    n_exp, _, de = w1.shape
    nslab = d // (2 * LANES)
    hbm = pl.BlockSpec(memory_space=pl.ANY)
    expert = lambda j, vis, nv, ps, pc, nu: (vis[jnp.minimum(j, nv[0] - 1)], 0, 0)
    return pl.pallas_call(
        functools.partial(_experts_kernel, nslab=nslab, bm=bm),
        grid_spec=pltpu.PrefetchScalarGridSpec(
            num_scalar_prefetch=5,
            grid=(n_exp,),
            in_specs=[hbm, pl.BlockSpec((1, d, de), expert), pl.BlockSpec((1, d, de), expert),
                      pl.BlockSpec((1, de, d), expert)],
            out_specs=hbm,
            scratch_shapes=[pltpu.VMEM((2 * bm * nslab, LANES), jnp.int32),
                            pltpu.VMEM((2 * bm * nslab, LANES), jnp.int32),
                            pltpu.VMEM((d, de), BF16), pltpu.VMEM((d, de), BF16), pltpu.VMEM((de, d), BF16),
                            pltpu.SemaphoreType.DMA((2,)), pltpu.SemaphoreType.DMA((2,))]),
        out_shape=jax.ShapeDtypeStruct(xs2.shape, jnp.int32),
        compiler_params=_cparams(("arbitrary",)),
        name="routed_experts",
    )(visit, n_visit, pstart, pcounts, n_used, xs2, w1, w3, w2)


def _combine_kernel(slot_hbm, os_hbm, wt_ref, sh_ref, x1_ref, g2_ref, o_ref, slot_smem, buf, ssem, gsem,
                    *, nslab, tile0):
    i = pl.program_id(0)
    tt = x1_ref.shape[0]
    rows = TOP_K * tt * nslab

    def row_copy(half, t, kk):
        dst = buf.at[pl.ds(pl.multiple_of(half * rows + (kk * tt + t) * nslab, nslab), nslab), :]
        return pltpu.make_async_copy(os_hbm.at[slot_smem[half, kk, t]], dst, gsem.at[half])

    def fetch(tile, half):
        cp = pltpu.make_async_copy(slot_hbm.at[tile0 + tile], slot_smem.at[half], ssem)
        cp.start()
        cp.wait()

        def issue(t, carry):
            for kk in range(TOP_K):
                row_copy(half, t, kk).start(priority=kk % 2)
            return carry

        lax.fori_loop(0, tt, issue, 0)

    @pl.when(i == 0)
    def _():
        fetch(0, 0)

    def reduce_tile(half):
        def drain(t, carry):
            for kk in range(TOP_K):
                row_copy(half, t, kk).wait()
            return carry

        lax.fori_loop(0, tt, drain, 0)

        wt = wt_ref[...]
        g2 = _seq_rows(g2_ref[...], tt)
        hw = nslab * LANES
        for s in range(nslab):
            y_lo = sh_ref[:, s * LANES:(s + 1) * LANES]
            y_hi = sh_ref[:, hw + s * LANES:hw + (s + 1) * LANES]
            for kk in range(TOP_K):
                lo, hi = _unpack_words(buf[pl.ds(half * rows + kk * tt * nslab + s, tt, stride=nslab), :])
                y_lo = y_lo + wt[:, kk:kk + 1] * lo
                y_hi = y_hi + wt[:, kk:kk + 1] * hi
            for y, c0 in ((y_lo, s * LANES), (y_hi, hw + s * LANES)):
                cols = slice(c0, c0 + LANES)
                o_ref[:, cols] = x1_ref[:, cols] + g2[:, cols] * y

    for half in (0, 1):
        @pl.when(jnp.logical_and(i + 1 < pl.num_programs(0), lax.rem(i + 1, 2) == half))
        def _(half=half):
            fetch(i + 1, half)

    for half in (0, 1):
        @pl.when(lax.rem(i, 2) == half)
        def _(half=half):
            reduce_tile(half)


def _combine(slots, os3, wts_t, shared, x1, g2, seq_len, tok0):
    n, d = x1.shape
    nslab = d // (2 * LANES)
    tt = TOK_TILE
    tile0 = tok0 // tt
    return pl.pallas_call(
        functools.partial(_combine_kernel, nslab=nslab, tile0=tile0),
        grid=(n // tt,),
        in_specs=[pl.BlockSpec(memory_space=pl.ANY),
                  pl.BlockSpec(memory_space=pl.ANY),
                  pl.BlockSpec((tt, TOP_K), lambda i: (tile0 + i, 0)),
                  pl.BlockSpec((tt, d), lambda i: (tile0 + i, 0)),
                  pl.BlockSpec((tt, d), lambda i: (i, 0)),
                  _mod_spec(tt, seq_len, d)],
        out_specs=pl.BlockSpec((tt, d), lambda i: (i, 0)),
        out_shape=jax.ShapeDtypeStruct((n, d), F32),
        scratch_shapes=[pltpu.SMEM((2, TOP_K, tt), jnp.int32),
                        pltpu.VMEM((2 * TOP_K * tt * nslab, LANES), jnp.int32),
                        pltpu.SemaphoreType.DMA, pltpu.SemaphoreType.DMA((2,))],
        compiler_params=_cparams(("arbitrary",)),
        name="moe_combine",
    )(slots, os3, wts_t, shared, x1, g2)


def _mixer(x, mod, positions, s0, past_k, past_v, lb, p, w, moe_rows, moe_buf):
    bsz, seq_len, d = x.shape
    n = bsz * seq_len
    sh1, sc1, g1, sh2, sc2, g2 = mod
    tm = 256 if n % 256 == 0 else n
    x2 = x.reshape(n, d)
    h1 = _norm_mod(x2, p["norm1_w"], sc1, sh1, seq_len, tm)
    tmm = 1024 if n % 1024 == 0 else tm
    hg_w = w["w_in_hg"].shape[1]
    at_w = w["w_in_at"].shape[1]
    z_hg = _matmul(h1, w["w_in_hg"], tmm, min(hg_w, 1024), name="w_in_hgrn")
    z_at = _matmul(h1, w["w_in_at"], tmm, at_w, name="w_in_attn")
    gates = _matmul(h1, w["w_in_gate"], tmm, min(2 * d, 1024), act="sigmoid", name="w_in_gates")

    o_hg, s_new = _hgrn(z_hg, lb, p["hg_norm_w"], s0, bsz, seq_len)

    n_kv = p["n_kv"]
    kd = n_kv * AT_HD
    qd = at_w - 2 * kd
    tq = min(seq_len, 256)
    qn, kvn = _qk_prep(z_at, positions, p["q_norm_w"], p["k_norm_w"], seq_len, qd, kd, tq)
    if past_k is None:
        o_at = _swa_prompt(qn, kvn, p["attn_sinks"], bsz, seq_len, n_kv)
    else:
        o_at = _swa_sample(qn, kvn, past_k, past_v, p["attn_sinks"], bsz, seq_len, n_kv)

    x1, h2s = _merge(o_hg, o_at, gates, x2, g1, w["w_hg_out"], w["w_at_out"], w["w_o"],
                     p["norm2_w"], sc2, sh2, seq_len, tm, moe_rows, moe_buf)
    kv3 = kvn.reshape(bsz, seq_len, 2 * kd)
    return x1, h2s, s_new, kv3[:, :, :kd], kv3[:, :, kd:]


def kernel(x_prompt, x_sample, cache_k, cache_v, state_hgrn, c_prompt, c_sample, norm1_w, norm2_w, w_ada, b_ada,
           w_in, hg_lower_bounds, hg_norm_w, q_norm_w, k_norm_w, attn_sinks, w_hg_out, w_at_out, w_o, w_router,
           router_bias, w_exp_gate, w_exp_up, w_exp_down, w_sh_gate, w_sh_up, w_sh_down):
    depth = norm1_w.shape[0]
    assert depth == 1, "single trunk layer"
    bp, tp, d = x_prompt.shape
    bs, ts, _ = x_sample.shape
    window, n_kv = cache_k.shape[2], cache_k.shape[3]
    kd = n_kv * AT_HD
    hg_dim = w_hg_out.shape[1]
    qd = w_at_out.shape[1]
    n_exp = w_router.shape[2]
    nslab = d // (2 * LANES)
    l = 0

    lbs = jnp.cumsum(jax.nn.softmax(hg_lower_bounds.astype(F32), axis=0), axis=0)
    win = w_in[l]
    w = {
        "w_in_hg": win[:, :4 * hg_dim].astype(BF16),
        "w_in_at": win[:, 4 * hg_dim:4 * hg_dim + qd + 2 * kd].astype(BF16),
        "w_in_gate": win[:, 4 * hg_dim + qd + 2 * kd:].astype(BF16),
        "w_hg_out": w_hg_out[l].astype(BF16),
        "w_at_out": w_at_out[l].astype(BF16),
        "w_o": w_o[l].astype(BF16),
    }
    p = {"norm1_w": norm1_w[l], "norm2_w": norm2_w[l], "hg_norm_w": hg_norm_w[l], "q_norm_w": q_norm_w[l],
         "k_norm_w": k_norm_w[l], "attn_sinks": attn_sinks[l], "n_kv": n_kv}

    c_all = jnp.concatenate([c_prompt, c_sample], axis=0)
    mod_all = _ada(c_all, w_ada[l], b_ada[l])
    mod_all = mod_all.reshape(bp + bs, 6, 1, d)
    mod_p = tuple(mod_all[:bp, j] for j in range(6))
    mod_s = tuple(mod_all[bp:, j] for j in range(6))

    pos_p = jnp.arange(tp, dtype=jnp.int32)
    pos_s = PAST_LEN + jnp.arange(ts, dtype=jnp.int32)
    s0_p = jnp.zeros((bp,) + state_hgrn.shape[2:], F32)
    n_p, n_s = bp * tp, bs * ts
    n_tok = n_p + n_s
    x1_p, h2_p, sp, kp, vp = _mixer(x_prompt, mod_p, pos_p, s0_p, None, None, lbs[l], p, w, (0, n_tok), None)
    pk = cache_k[l].reshape(bs, window, kd)
    pv = cache_v[l].reshape(bs, window, kd)
    x1_s, h2s, ss, ks, vs = _mixer(x_sample, mod_s, pos_s, state_hgrn[l], pk, pv, lbs[l], p, w, (n_p, n_tok), h2_p)

    tr = 256 if n_tok % 256 == 0 else TOK_TILE
    eidx, wts, rank, counts = _router(h2s, w_router[l], router_bias[l], n_tok, d, tr)
    shared = _shared_expert(h2s, w_sh_gate[l].astype(BF16), w_sh_up[l].astype(BF16), w_sh_down[l].astype(BF16),
                            n_tok, d, tr)

    bm = MOE_ROWS
    counts = counts[:, 0]
    pcounts = (counts + bm - 1) // bm * bm
    pend = jnp.cumsum(pcounts)
    pstart = pend - pcounts
    nb = -(-(n_tok * TOP_K) // bm) + n_exp
    slots = _slots(eidx, rank, pstart, n_tok)
    n_used = (pend[-1:] // bm).astype(jnp.int32)

    xs = _dispatch(counts, pstart, slots, h2s, n_tok, nslab, nb * bm, bm)
    visit = jnp.argsort(counts == 0, stable=True).astype(jnp.int32)
    n_visit = jnp.sum(counts > 0).astype(jnp.int32).reshape(1)
    os_ = _experts(visit, n_visit, pstart, pcounts, n_used, xs.reshape(nb * bm * nslab, LANES),
                   w_exp_gate[l], w_exp_up[l], w_exp_down[l], d, bm)
    os3 = os_.reshape(nb * bm, nslab, LANES)
    wts_t = wts.T
    y_p = _combine(slots, os3, wts_t, shared, x1_p, mod_p[5], tp, 0)
    y_s = _combine(slots, os3, wts_t, shared, x1_s, mod_s[5], ts, n_p)

    def cache_out(a, b_, t):
        return a[:, t - window:].reshape(1, b_, window, n_kv, AT_HD)

    new_k_p = cache_out(kp, bp, tp)
    new_v_p = cache_out(vp, bp, tp)
    keys_s = jnp.concatenate([pk, ks], axis=1)
    vals_s = jnp.concatenate([pv, vs], axis=1)
    new_k_s = cache_out(keys_s, bs, window + ts)
    new_v_s = cache_out(vals_s, bs, window + ts)
    return (y_p.reshape(bp, tp, d), y_s.reshape(bs, ts, d), new_k_p, new_v_p, sp[None],
            new_k_s, new_v_s, ss[None])
```

```python
import functools
import math

import numpy as np
import jax
import jax.numpy as jnp
from jax import lax
from jax.experimental import pallas as pl
from jax.experimental.pallas import tpu as pltpu

EPS = 1e-6
CHUNK = 64
HG_DK = 128
AT_HD = 64
ROPE_DIM = 16
ROPE_THETA = 500000.0
TOP_K = 8
N_GROUPS = 8
TOPK_GROUPS = 4
ROUTED_SCALE = 2.5
PAST_LEN = 2048

LANES = 128
MOE_ROWS = 256
TOK_TILE = 128
VMEM_LIMIT = 56 * 1024 * 1024

F32 = jnp.float32
BF16 = jnp.bfloat16


def _cparams(semantics, vmem=VMEM_LIMIT):
    return pltpu.CompilerParams(dimension_semantics=semantics, vmem_limit_bytes=vmem)


def _sigmoid(x):
    return 1.0 / (1.0 + jnp.exp(-x))


def _silu(x):
    return x * _sigmoid(x)


def _dot(a, b):
    return jnp.dot(a, b, preferred_element_type=F32)


def _dot_nt(a, b):
    return lax.dot_general(a, b, (((1,), (1,)), ((), ())), preferred_element_type=F32)


def _dot_tn(a, b):
    return lax.dot_general(a, b, (((0,), (0,)), ((), ())), preferred_element_type=F32)


def _split_bf16(x):
    hi = x.astype(BF16)
    lo = (x - hi.astype(F32)).astype(BF16)
    return hi, lo


def _seq_rows(m, rows):
    s, _, d = m.shape
    if s == 1:
        return m[0]
    return jnp.broadcast_to(m, (s, rows // s, d)).reshape(rows, d)


def _mod_spec(tm, seq_len, d):
    if tm <= seq_len:
        return pl.BlockSpec((1, 1, d), lambda i: ((i * tm) // seq_len, 0, 0))
    s = tm // seq_len
    return pl.BlockSpec((s, 1, d), lambda i: (i, 0, 0))


def _ada_kernel(c_ref, w_ref, b_ref, o_ref):
    s = _silu(c_ref[...]).astype(BF16)
    o_ref[...] = _dot(s, w_ref[...].astype(BF16)) + b_ref[...]


def _ada(c, w, b):
    n, d = c.shape
    m = w.shape[1]
    tn = min(m, 1024)
    return pl.pallas_call(
        _ada_kernel,
        grid=(m // tn,),
        in_specs=[pl.BlockSpec((n, d), lambda j: (0, 0)),
                  pl.BlockSpec((d, tn), lambda j: (0, j)),
                  pl.BlockSpec((1, tn), lambda j: (0, j))],
        out_specs=pl.BlockSpec((n, tn), lambda j: (0, j)),
        out_shape=jax.ShapeDtypeStruct((n, m), F32),
        compiler_params=_cparams(("parallel",)),
        name="ada_mod",
    )(c, w, b.reshape(1, m))


def _norm_mod_kernel(x_ref, w_ref, sc_ref, sh_ref, o_ref):
    x = x_ref[...]
    tm = x.shape[0]
    y = x * lax.rsqrt(jnp.mean(x * x, axis=-1, keepdims=True) + EPS) * w_ref[...]
    o_ref[...] = (y * (1.0 + _seq_rows(sc_ref[...], tm)) + _seq_rows(sh_ref[...], tm)).astype(o_ref.dtype)


def _norm_mod(x, w, sc, sh, seq_len, tm):
    n, d = x.shape
    return pl.pallas_call(
        _norm_mod_kernel,
        grid=(n // tm,),
        in_specs=[pl.BlockSpec((tm, d), lambda i: (i, 0)),
                  pl.BlockSpec((1, d), lambda i: (0, 0)),
                  _mod_spec(tm, seq_len, d), _mod_spec(tm, seq_len, d)],
        out_specs=pl.BlockSpec((tm, d), lambda i: (i, 0)),
        out_shape=jax.ShapeDtypeStruct((n, d), BF16),
        compiler_params=_cparams(("parallel",)),
        name="norm1_mod",
    )(x, w.reshape(1, d), sc, sh)


def _mm_kernel(x_ref, w_ref, o_ref, *, act):
    y = _dot(x_ref[...], w_ref[...])
    if act == "sigmoid":
        y = _sigmoid(y)
    o_ref[...] = y.astype(o_ref.dtype)


def _matmul(x, w, tm, tn, act=None, out_dtype=F32, name="matmul"):
    n, k = x.shape
    m = w.shape[1]
    return pl.pallas_call(
        functools.partial(_mm_kernel, act=act),
        grid=(n // tm, m // tn),
        in_specs=[pl.BlockSpec((tm, k), lambda i, j: (i, 0)),
                  pl.BlockSpec((k, tn), lambda i, j: (0, j))],
        out_specs=pl.BlockSpec((tm, tn), lambda i, j: (i, j)),
        out_shape=jax.ShapeDtypeStruct((n, m), out_dtype),
        compiler_params=_cparams(("parallel", "arbitrary")),
        name=name,
    )(x, w)


def _hgrn_tables(c):
    nlev = int(math.log2(c))
    t = np.arange(c)[:, None]
    s = np.arange(c)[None, :]
    seg = []
    for l in range(1, nlev + 1):
        b = 1 << l
        seg.append(((s >= (t // b) * b) & (s <= t)).astype(np.float32))
        seg.append(((s > t) & (s <= (t // b) * b + b - 1)).astype(np.float32))
    masks = [(t == s).astype(np.float32)]
    for l in range(nlev):
        b = 1 << l
        masks.append(((t // (2 * b) == s // (2 * b)) & (t % (2 * b) >= b) & (s % (2 * b) < b)).astype(np.float32))
    return np.concatenate(seg, axis=0), np.stack(masks, axis=0)


def _hgrn_kernel(q_ref, f_ref, i_ref, g_ref, lb_ref, nw_ref, s0_ref, seg_ref, msk_ref,
                 o_ref, sn_ref, st_ref, *, c, nchunks, hb):
    nlev = int(math.log2(c))
    tstep = pl.program_id(2)

    @pl.when(tstep == 0)
    def _():
        for hh in range(hb):
            st_ref[hh] = s0_ref[0, hh].T

    nw = nw_ref[...]
    lb = lb_ref[...]
    head = lambda a, hh: a[:, hh * HG_DK:(hh + 1) * HG_DK]

    def decays(l, g_hi, g_lo):
        seg = seg_ref[2 * (l - 1) * c:2 * l * c, :]
        e = _dot(seg, g_hi) + _dot(seg, g_lo)
        return e[:c, :], e[c:, :]

    def chunk(ci, carry):
        rows = pl.ds(pl.multiple_of(ci * c, c), c)
        q = q_ref[rows, :]
        f = lb + (1.0 - lb) * _sigmoid(f_ref[rows, :])
        g = jnp.log(f)
        k = 1.0 - f
        g_hi, g_lo = _split_bf16(g)
        qb = q.astype(BF16)
        kb = k.astype(BF16)
        scores = [jnp.where(msk_ref[0] > 0, _dot_nt(head(qb, hh), head(kb, hh)), 0.0) for hh in range(hb)]
        for l in range(nlev):
            if l == 0:
                ql, kl = (q * f).astype(BF16), kb
            else:
                wl, vl = decays(l, g_hi, g_lo)
                ql = (q * jnp.exp(wl)).astype(BF16)
                kl = (k * jnp.exp(vl)).astype(BF16)
            for hh in range(hb):
                scores[hh] = scores[hh] + jnp.where(msk_ref[l + 1] > 0, _dot_nt(head(ql, hh), head(kl, hh)), 0.0)
        a_inc, v_end = decays(nlev, g_hi, g_lo)
        qa = (q * jnp.exp(a_inc)).astype(BF16)
        k_end = (k * jnp.exp(v_end)).astype(BF16)
        vb = i_ref[rows, :].astype(BF16)
        carry_decay = jnp.exp(a_inc[c - 1:c, :])
        gate = _silu(g_ref[rows, :])
        for hh in range(hb):
            st = st_ref[hh]
            o = _dot_nt(head(qa, hh), st.astype(BF16)) + _dot(scores[hh].astype(BF16), head(vb, hh))
            st_ref[hh] = st * head(carry_decay, hh) + _dot_tn(head(vb, hh), head(k_end, hh))
            on = o * lax.rsqrt(jnp.mean(o * o, axis=-1, keepdims=True) + EPS) * nw
            o_ref[rows, hh * HG_DK:(hh + 1) * HG_DK] = (on * head(gate, hh)).astype(o_ref.dtype)
        return carry

    lax.fori_loop(0, nchunks, chunk, 0)

    @pl.when(tstep == pl.num_programs(2) - 1)
    def _():
        for hh in range(hb):
            sn_ref[0, hh] = st_ref[hh].T


def _hgrn(z_hg, lb, norm_w, s0, bsz, seq_len):
    n, w4 = z_hg.shape
    nh = w4 // (4 * HG_DK)
    hb = min(nh, 8)
    ng = nh // hb
    c = min(CHUNK, seq_len)
    tb = min(seq_len, 512)
    nt = seq_len // tb
    seg, masks = _hgrn_tables(c)

    def col(part):
        return pl.BlockSpec((tb, hb * HG_DK), lambda b, h, t: (b * nt + t, part * ng + h))

    return pl.pallas_call(
        functools.partial(_hgrn_kernel, c=c, nchunks=tb // c, hb=hb),
        grid=(bsz, ng, nt),
        in_specs=[col(0), col(1), col(2), col(3),
                  pl.BlockSpec((1, hb * HG_DK), lambda b, h, t: (0, h)),
                  pl.BlockSpec((1, HG_DK), lambda b, h, t: (0, 0)),
                  pl.BlockSpec((1, hb, HG_DK, HG_DK), lambda b, h, t: (b, h, 0, 0)),
                  pl.BlockSpec(seg.shape, lambda b, h, t: (0, 0)),
                  pl.BlockSpec(masks.shape, lambda b, h, t: (0, 0, 0))],
        out_specs=[pl.BlockSpec((tb, hb * HG_DK), lambda b, h, t: (b * nt + t, h)),
                   pl.BlockSpec((1, hb, HG_DK, HG_DK), lambda b, h, t: (b, h, 0, 0))],
        out_shape=[jax.ShapeDtypeStruct((n, nh * HG_DK), BF16),
                   jax.ShapeDtypeStruct((bsz, nh, HG_DK, HG_DK), F32)],
        scratch_shapes=[pltpu.VMEM((hb, HG_DK, HG_DK), F32)],
        compiler_params=_cparams(("parallel", "parallel", "arbitrary")),
        name="hgrn2",
    )(z_hg, z_hg, z_hg, z_hg, lb.reshape(1, -1), norm_w.reshape(1, HG_DK), s0,
      jnp.asarray(seg, BF16), jnp.asarray(masks, F32))


def _rope_tables(positions):
    half = ROPE_DIM // 2
    inv_freq = ROPE_THETA ** (-jnp.arange(0, ROPE_DIM, 2, dtype=F32) / ROPE_DIM)
    ang = positions.astype(F32)[:, None] * inv_freq[None, :]
    cos, sin = jnp.cos(ang), jnp.sin(ang)
    t = positions.shape[0]
    rest = AT_HD - ROPE_DIM
    c64 = jnp.concatenate([cos, cos, jnp.ones((t, rest), F32)], axis=1)
    sa64 = jnp.concatenate([-sin, jnp.zeros((t, half + rest), F32)], axis=1)
    sb64 = jnp.concatenate([jnp.zeros((t, half), F32), sin, jnp.zeros((t, rest), F32)], axis=1)
    rep = LANES // AT_HD
    return tuple(jnp.tile(a, (1, rep)) for a in (c64, sa64, sb64))


def _qk_prep_kernel(z_ref, cos_ref, sa_ref, sb_ref, qw_ref, kw_ref, bd_ref, q_ref, kv_ref, *, qd, kd):
    half = ROPE_DIM // 2
    z = z_ref[...]
    cos, sa, sb = cos_ref[...], sa_ref[...], sb_ref[...]

    def norm_rope(x, w, bd, reps):
        x2 = x * x
        hi, lo = _split_bf16(x2)
        ss = _dot(hi, bd) + _dot(lo, bd)
        xn = x * lax.rsqrt(ss * (1.0 / AT_HD) + EPS) * w
        width = x.shape[1]
        tile = lambda a: jnp.concatenate([a] * reps, axis=1) if reps > 1 else a
        return (xn * tile(cos) + pltpu.roll(xn, width - half, 1) * tile(sa)
                + pltpu.roll(xn, half, 1) * tile(sb))

    q = norm_rope(z[:, :qd], qw_ref[...], bd_ref[...], qd // LANES)
    k = norm_rope(z[:, qd:qd + kd], kw_ref[...], bd_ref[:kd, :kd], kd // LANES)
    q_ref[...] = q.astype(q_ref.dtype)
    kv_ref[:, :kd] = k
    kv_ref[:, kd:] = z[:, qd + kd:]


def _qk_prep(z_at, positions, q_norm_w, k_norm_w, seq_len, qd, kd, tm):
    n, w = z_at.shape
    cos, sa, sb = _rope_tables(positions)
    nt = seq_len // tm
    bd = np.kron(np.eye(qd // AT_HD, dtype=np.float32), np.ones((AT_HD, AT_HD), np.float32))
    tab = pl.BlockSpec((tm, LANES), lambda i: (i % nt, 0))
    return pl.pallas_call(
        functools.partial(_qk_prep_kernel, qd=qd, kd=kd),
        grid=(n // tm,),
        in_specs=[pl.BlockSpec((tm, w), lambda i: (i, 0)), tab, tab, tab,
                  pl.BlockSpec((1, qd), lambda i: (0, 0)),
                  pl.BlockSpec((1, kd), lambda i: (0, 0)),
                  pl.BlockSpec((qd, qd), lambda i: (0, 0))],
        out_specs=[pl.BlockSpec((tm, qd), lambda i: (i, 0)),
                   pl.BlockSpec((tm, 2 * kd), lambda i: (i, 0))],
        out_shape=[jax.ShapeDtypeStruct((n, qd), BF16),
                   jax.ShapeDtypeStruct((n, 2 * kd), F32)],
        compiler_params=_cparams(("parallel",)),
        name="qk_norm_rope",
    )(z_at, cos, sa, sb, jnp.tile(q_norm_w, qd // AT_HD).reshape(1, qd),
      jnp.tile(k_norm_w, kd // AT_HD).reshape(1, kd), jnp.asarray(bd, BF16))


def _attend(q, keys, vals, sinks_ref, col_valid, o_ref, n_kv, group):
    n_heads = n_kv * group
    batch = 16
    for h0 in range(0, n_heads, batch):
        heads = range(h0, min(h0 + batch, n_heads))
        scores = []
        for h in heads:
            j = h // group
            s = _dot_nt(q[:, h * AT_HD:(h + 1) * AT_HD], keys[:, j * AT_HD:(j + 1) * AT_HD]) * (AT_HD ** -0.5)
            scores.append(s if col_valid is None else jnp.where(col_valid, s, -jnp.inf))
        probs = []
        for h, s in zip(heads, scores):
            sink = sinks_ref[h]
            m = jnp.maximum(jnp.max(s, axis=-1, keepdims=True), sink)
            e = jnp.exp(s - m)
            den = jnp.sum(e, axis=-1, keepdims=True) + jnp.exp(sink - m)
            probs.append((e / den).astype(BF16))
        for h, p in zip(heads, probs):
            j = h // group
            o_ref[:, h * AT_HD:(h + 1) * AT_HD] = _dot(p, vals[:, j * AT_HD:(j + 1) * AT_HD]).astype(o_ref.dtype)


def _swa_prompt_kernel(sinks_ref, q_ref, kv0_ref, kv1_ref, kv2_ref, o_ref, *, n_kv, group, w_chunks):
    kd = n_kv * AT_HD
    n = pl.program_id(1)
    blocks = [kv0_ref[...], kv1_ref[...], kv2_ref[...]]
    keys = jnp.concatenate([b[:, :kd] for b in blocks], axis=0).astype(BF16)
    vals = jnp.concatenate([b[:, kd:] for b in blocks], axis=0).astype(BF16)
    col_chunk = lax.broadcasted_iota(jnp.int32, (1, keys.shape[0]), 1) // CHUNK
    col_valid = (col_chunk + n - w_chunks) >= 0
    _attend(q_ref[...], keys, vals, sinks_ref, col_valid, o_ref, n_kv, group)


def _swa_prompt(qn, kvn, sinks, bsz, seq_len, n_kv):
    n, qd = qn.shape
    kd = n_kv * AT_HD
    nc = seq_len // CHUNK
    group = qd // AT_HD // n_kv
    w_chunks = 2

    def kv_spec(j):
        return pl.BlockSpec((CHUNK, 2 * kd), lambda b, c, s: (b * nc + jnp.maximum(c - w_chunks + j, 0), 0))

    return pl.pallas_call(
        functools.partial(_swa_prompt_kernel, n_kv=n_kv, group=group, w_chunks=w_chunks),
        grid_spec=pltpu.PrefetchScalarGridSpec(
            num_scalar_prefetch=1,
            grid=(bsz, nc),
            in_specs=[pl.BlockSpec((CHUNK, qd), lambda b, c, s: (b * nc + c, 0)),
                      kv_spec(0), kv_spec(1), kv_spec(2)],
            out_specs=pl.BlockSpec((CHUNK, qd), lambda b, c, s: (b * nc + c, 0))),
        out_shape=jax.ShapeDtypeStruct((n, qd), BF16),
        compiler_params=_cparams(("parallel", "arbitrary")),
        name="swa_prompt",
    )(sinks, qn, kvn, kvn, kvn)


def _swa_sample_kernel(sinks_ref, q_ref, pk_ref, pv_ref, kv_ref, o_ref, *, n_kv, group):
    kd = n_kv * AT_HD
    kv = kv_ref[...]
    keys = jnp.concatenate([pk_ref[0], kv[:, :kd]], axis=0).astype(BF16)
    vals = jnp.concatenate([pv_ref[0], kv[:, kd:]], axis=0).astype(BF16)
    _attend(q_ref[...], keys, vals, sinks_ref, None, o_ref, n_kv, group)


def _swa_sample(qn, kvn, past_k, past_v, sinks, bsz, seq_len, n_kv):
    n, qd = qn.shape
    kd = n_kv * AT_HD
    window = past_k.shape[1]
    group = qd // AT_HD // n_kv
    return pl.pallas_call(
        functools.partial(_swa_sample_kernel, n_kv=n_kv, group=group),
        grid_spec=pltpu.PrefetchScalarGridSpec(
            num_scalar_prefetch=1,
            grid=(bsz,),
            in_specs=[pl.BlockSpec((seq_len, qd), lambda b, s: (b, 0)),
                      pl.BlockSpec((1, window, kd), lambda b, s: (b, 0, 0)),
                      pl.BlockSpec((1, window, kd), lambda b, s: (b, 0, 0)),
                      pl.BlockSpec((seq_len, 2 * kd), lambda b, s: (b, 0))],
            out_specs=pl.BlockSpec((seq_len, qd), lambda b, s: (b, 0))),
        out_shape=jax.ShapeDtypeStruct((n, qd), BF16),
        compiler_params=_cparams(("parallel",)),
        name="swa_sample",
    )(sinks, qn, past_k, past_v, kvn)


HI_MASK = np.uint32(0xFFFF0000)


def _pack_rows(x):
    half = x.shape[1] // 2
    bits = lambda a: lax.bitcast_convert_type(a.astype(BF16).astype(F32), jnp.uint32)
    word = (bits(x[:, half:]) & HI_MASK) | (bits(x[:, :half]) >> 16)
    return lax.bitcast_convert_type(word, jnp.int32)


def _unpack_words(w):
    u = lax.bitcast_convert_type(w, jnp.uint32)
    return lax.bitcast_convert_type(u << 16, F32), lax.bitcast_convert_type(u & HI_MASK, F32)


def _store_slabs(ref, words, row0=0):
    rows, width = words.shape
    nslab = width // LANES
    for s in range(nslab):
        ref[pl.ds(row0 + s, rows, stride=nslab), :] = words[:, s * LANES:(s + 1) * LANES]


def _load_rows(ref, rows, nslab, row0=0):
    lo, hi = [], []
    for s in range(nslab):
        a, b = _unpack_words(ref[pl.ds(row0 + s, rows, stride=nslab), :])
        lo.append(a)
        hi.append(b)
    return jnp.concatenate(lo + hi, axis=1).astype(BF16)


def _merge_kernel(ohg_ref, oat_ref, ga_ref, gb_ref, x_ref, g1_ref, whg_ref, wat_ref, wo_ref,
                  nw_ref, sc_ref, sh_ref, *rest):
    x1_ref, h2_ref = rest[-2:]
    tm = x_ref.shape[0]
    merged = ga_ref[...] * _dot(ohg_ref[...], whg_ref[...]) + gb_ref[...] * _dot(oat_ref[...], wat_ref[...])
    mix = _dot(merged.astype(BF16), wo_ref[...])
    x1 = x_ref[...] + _seq_rows(g1_ref[...], tm) * mix
    x1_ref[...] = x1
    y = x1 * lax.rsqrt(jnp.mean(x1 * x1, axis=-1, keepdims=True) + EPS) * nw_ref[...]
    h2 = y * (1.0 + _seq_rows(sc_ref[...], tm)) + _seq_rows(sh_ref[...], tm)
    _store_slabs(h2_ref, _pack_rows(h2))


def _merge(o_hg, o_at, gates, x, g1, w_hg_out, w_at_out, w_o, norm2_w, sc2, sh2, seq_len, tm, moe_rows, moe_buf):
    n, d = x.shape
    hd = o_hg.shape[1]
    ad = o_at.shape[1]
    nslab = d // (2 * LANES)
    tok0, n_all = moe_rows
    tile0 = tok0 // tm
    const = lambda shape: pl.BlockSpec(shape, lambda i: (0,) * len(shape), pipeline_mode=pl.Buffered(1))
    mod = lambda: _mod_spec(tm, seq_len, d)
    in_specs = [pl.BlockSpec((tm, hd), lambda i: (i, 0)),
                pl.BlockSpec((tm, ad), lambda i: (i, 0)),
                pl.BlockSpec((tm, d), lambda i: (i, 0)),
                pl.BlockSpec((tm, d), lambda i: (i, 1)),
                pl.BlockSpec((tm, d), lambda i: (i, 0)),
                mod(), const((hd, d)), const((ad, d)), const((d, d)), const((1, d)), mod(), mod()]
    args = [o_hg, o_at, gates, gates, x, g1, w_hg_out, w_at_out, w_o, norm2_w.reshape(1, d), sc2, sh2]
    aliases = {}
    if moe_buf is not None:
        in_specs.append(pl.BlockSpec(memory_space=pl.ANY))
        args.append(moe_buf)
        aliases = {len(args) - 1: 1}
    return pl.pallas_call(
        _merge_kernel,
        grid=(n // tm,),
        in_specs=in_specs,
        out_specs=[pl.BlockSpec((tm, d), lambda i: (i, 0)),
                   pl.BlockSpec((tm * nslab, LANES), lambda i: (tile0 + i, 0))],
        out_shape=[jax.ShapeDtypeStruct((n, d), F32),
                   jax.ShapeDtypeStruct((n_all * nslab, LANES), jnp.int32)],
        input_output_aliases=aliases,
        compiler_params=_cparams(("parallel",)),
        name="merge_norm2",
    )(*args)


def _router_kernel(h_ref, wr_ref, bias_ref, tri_ref, eidx_ref, wts_ref, rank_ref, cnt_ref, run_ref,
                   *, nslab, n_exp):
    tm = h_ref.shape[0] // nslab
    gsz = n_exp // N_GROUPS
    step = pl.program_id(0)

    @pl.when(step == 0)
    def _():
        run_ref[...] = jnp.zeros_like(run_ref)

    h = _load_rows(h_ref, tm, nslab)
    scores = _sigmoid(_dot_nt(wr_ref[...], h))
    choice = scores + bias_ref[...]
    neg = -jnp.inf
    row = lax.broadcasted_iota(jnp.int32, (gsz, tm), 0)

    gscore = []
    for gi in range(N_GROUPS):
        cg = choice[gi * gsz:(gi + 1) * gsz, :]
        m1 = jnp.max(cg, axis=0, keepdims=True)
        i1 = jnp.min(jnp.where(cg == m1, row, gsz), axis=0, keepdims=True)
        m2 = jnp.max(jnp.where(row == i1, neg, cg), axis=0, keepdims=True)
        gscore.append(m1 + m2)
    gs = jnp.concatenate(gscore, axis=0)
    grow = lax.broadcasted_iota(jnp.int32, (N_GROUPS, tm), 0)
    gsel = jnp.zeros((N_GROUPS, tm), F32)
    for _ in range(TOPK_GROUPS):
        gm = jnp.max(gs, axis=0, keepdims=True)
        gi = jnp.min(jnp.where(gs == gm, grow, N_GROUPS), axis=0, keepdims=True)
        hit = grow == gi
        gsel = jnp.where(hit, 1.0, gsel)
        gs = jnp.where(hit, neg, gs)
    masked = jnp.concatenate(
        [jnp.where(gsel[gi:gi + 1, :] > 0, choice[gi * gsz:(gi + 1) * gsz, :], neg) for gi in range(N_GROUPS)],
        axis=0)

    erow = lax.broadcasted_iota(jnp.int32, (n_exp, tm), 0)
    idxs, raw = [], []
    for _ in range(TOP_K):
        m = jnp.max(masked, axis=0, keepdims=True)
        i = jnp.min(jnp.where(masked == m, erow, n_exp), axis=0, keepdims=True)
        hit = erow == i
        raw.append(jnp.sum(jnp.where(hit, scores, 0.0), axis=0, keepdims=True))
        masked = jnp.where(hit, neg, masked)
        idxs.append(i)
    total = raw[0]
    for r in raw[1:]:
        total = total + r
    onehot = jnp.zeros((n_exp, tm), F32)
    for i in idxs:
        onehot = onehot + jnp.where(erow == i, 1.0, 0.0)
    before = _dot(onehot.astype(BF16), tri_ref[...]) + run_ref[:, 0:1]
    for kk in range(TOP_K):
        eidx_ref[kk:kk + 1, :] = idxs[kk]
        wts_ref[kk:kk + 1, :] = raw[kk] / total * ROUTED_SCALE
        rank_ref[kk:kk + 1, :] = jnp.sum(jnp.where(erow == idxs[kk], before, 0.0), axis=0, keepdims=True).astype(jnp.int32)
    run_ref[...] = run_ref[...] + jnp.sum(onehot, axis=1, keepdims=True)

    @pl.when(step == pl.num_programs(0) - 1)
    def _():
        cnt_ref[...] = run_ref[...].astype(jnp.int32)


def _slot_kernel(eidx_ref, rank_ref, pstart_ref, slot_ref):
    n_exp = pstart_ref.shape[0]
    tt = TOK_TILE
    erow = lax.broadcasted_iota(jnp.int32, (n_exp, tt), 0)
    pstart = pstart_ref[:, 0:1]
    for j in range(slot_ref.shape[0]):
        lanes = slice(j * tt, (j + 1) * tt)
        for kk in range(TOP_K):
            base = jnp.sum(jnp.where(erow == eidx_ref[kk:kk + 1, lanes], pstart, 0), axis=0, keepdims=True)
            slot_ref[j, kk:kk + 1, :] = base + rank_ref[kk:kk + 1, lanes]


def _slots(eidx, rank, pstart, n_tok):
    n_exp = pstart.shape[0]
    tt = TOK_TILE
    ntile = n_tok // tt
    per_step = max(g for g in range(1, 9) if ntile % g == 0)
    tok_spec = pl.BlockSpec((TOP_K, per_step * tt), lambda i: (0, i))
    return pl.pallas_call(
        _slot_kernel,
        grid=(ntile // per_step,),
        in_specs=[tok_spec, tok_spec, pl.BlockSpec((n_exp, LANES), lambda i: (0, 0))],
        out_specs=pl.BlockSpec((per_step, TOP_K, tt), lambda i: (i, 0, 0)),
        out_shape=jax.ShapeDtypeStruct((ntile, TOP_K, tt), jnp.int32),
        compiler_params=_cparams(("parallel",)),
        name="moe_slots",
    )(eidx, rank, jnp.broadcast_to(pstart[:, None], (n_exp, LANES)))


def _router(h2s, w_router, router_bias, n_tok, d, tm):
    n_exp = w_router.shape[1]
    nslab = d // (2 * LANES)
    tri = np.triu(np.ones((tm, tm), np.float32), 1)
    out_tok = lambda dt: jax.ShapeDtypeStruct((TOP_K, n_tok), dt)
    tok_spec = pl.BlockSpec((TOP_K, tm), lambda i: (0, i))
    return pl.pallas_call(
        functools.partial(_router_kernel, nslab=nslab, n_exp=n_exp),
        grid=(n_tok // tm,),
        in_specs=[pl.BlockSpec((tm * nslab, LANES), lambda i: (i, 0)),
                  pl.BlockSpec((n_exp, d), lambda i: (0, 0)),
                  pl.BlockSpec((n_exp, 1), lambda i: (0, 0)),
                  pl.BlockSpec((tm, tm), lambda i: (0, 0))],
        out_specs=[tok_spec, tok_spec, tok_spec, pl.BlockSpec((n_exp, LANES), lambda i: (0, 0))],
        out_shape=[out_tok(jnp.int32), out_tok(F32), out_tok(jnp.int32),
                   jax.ShapeDtypeStruct((n_exp, LANES), jnp.int32)],
        scratch_shapes=[pltpu.VMEM((n_exp, LANES), F32)],
        compiler_params=_cparams(("arbitrary",)),
        name="router_topk",
    )(h2s, w_router.T.astype(BF16), router_bias.reshape(n_exp, 1), jnp.asarray(tri, BF16))


def _shared_kernel(h_ref, w1_ref, w3_ref, w2_ref, o_ref, *, nslab):
    tm = h_ref.shape[0] // nslab
    h = _load_rows(h_ref, tm, nslab)
    a = (_silu(_dot(h, w1_ref[...])) * _dot(h, w3_ref[...])).astype(BF16)
    o_ref[...] = _dot(a, w2_ref[...])


def _shared_expert(h2s, ws1, ws3, ws2, n_tok, d, tm):
    nslab = d // (2 * LANES)
    ds = ws1.shape[1]
    return pl.pallas_call(
        functools.partial(_shared_kernel, nslab=nslab),
        grid=(n_tok // tm,),
        in_specs=[pl.BlockSpec((tm * nslab, LANES), lambda i: (i, 0)),
                  pl.BlockSpec((d, ds), lambda i: (0, 0)),
                  pl.BlockSpec((d, ds), lambda i: (0, 0)),
                  pl.BlockSpec((ds, d), lambda i: (0, 0))],
        out_specs=pl.BlockSpec((tm, d), lambda i: (i, 0)),
        out_shape=jax.ShapeDtypeStruct((n_tok, d), F32),
        compiler_params=_cparams(("parallel",)),
        name="shared_expert",
    )(h2s, ws1, ws3, ws2)


def _dispatch_kernel(cnt_ref, pstart_ref, slot_hbm, h_ref, xs_hbm, slot_smem, zbuf, ssem, dsem, zsem, *, bm):
    i = pl.program_id(0)
    nslab = xs_hbm.shape[1]
    tt = h_ref.shape[0] // nslab
    n_exp = cnt_ref.shape[0]

    @pl.when(i == 0)
    def _():
        zbuf[...] = jnp.zeros_like(zbuf)

        def walk(e, start):
            cnt = cnt_ref[e]
            pad = lax.rem(bm - lax.rem(cnt, bm), bm)
            base = pstart_ref[e] + cnt
            size = bm // 2
            while size >= 1:
                take = (pad & size) != 0

                @pl.when(take)
                def _(base=base, size=size):
                    cp = pltpu.make_async_copy(zbuf.at[pl.ds(0, size)], xs_hbm.at[pl.ds(base, size)], zsem)
                    if start:
                        cp.start()
                    else:
                        cp.wait()

                base = base + jnp.where(take, size, 0)
                size //= 2

        def start_e(e, carry):
            walk(e, True)
            return carry

        def wait_e(e, carry):
            walk(e, False)
            return carry

        lax.fori_loop(0, n_exp, start_e, 0)
        lax.fori_loop(0, n_exp, wait_e, 0)

    cp = pltpu.make_async_copy(slot_hbm.at[i], slot_smem, ssem)
    cp.start()
    cp.wait()

    def row_copy(t, kk):
        src = h_ref.at[pl.ds(pl.multiple_of(t * nslab, nslab), nslab), :]
        return pltpu.make_async_copy(src, xs_hbm.at[slot_smem[kk, t]], dsem)

    def issue(t, carry):
        for kk in range(TOP_K):
            row_copy(t, kk).start(priority=kk % 2)
        return carry

    lax.fori_loop(0, tt, issue, 0)

    def drain(t, carry):
        for kk in range(TOP_K):
            row_copy(t, kk).wait()
        return carry

    lax.fori_loop(0, tt, drain, 0)


def _dispatch(counts, pstart, slots, h2s, n_tok, nslab, n_rows, bm):
    return pl.pallas_call(
        functools.partial(_dispatch_kernel, bm=bm),
        grid_spec=pltpu.PrefetchScalarGridSpec(
            num_scalar_prefetch=2,
            grid=(n_tok // TOK_TILE,),
            in_specs=[pl.BlockSpec(memory_space=pl.ANY),
                      pl.BlockSpec((TOK_TILE * nslab, LANES), lambda i, c, p: (i, 0))],
            out_specs=pl.BlockSpec(memory_space=pl.ANY),
            scratch_shapes=[pltpu.SMEM((TOP_K, TOK_TILE), jnp.int32),
                            pltpu.VMEM((bm // 2, nslab, LANES), jnp.int32),
                            pltpu.SemaphoreType.DMA, pltpu.SemaphoreType.DMA, pltpu.SemaphoreType.DMA]),
        out_shape=jax.ShapeDtypeStruct((n_rows, nslab, LANES), jnp.int32),
        compiler_params=_cparams(("arbitrary",)),
        name="moe_dispatch",
    )(counts, pstart, slots, h2s)


def _experts_kernel(vis_ref, nv_ref, pstart_ref, pcnt_ref, nu_ref, xs_hbm, w1_hbm, w3_hbm, w2_hbm, os_hbm,
                    xbuf, obuf, w1s, w3s, w2s, w1b, w3b, w2b, xsem, osem, wsem, *, nslab, bm):
    j = pl.program_id(0)
    rows = bm * nslab
    n_used = nu_ref[0]
    n_visit = nv_ref[0]

    def weight_copies(jj, slot):
        e = vis_ref[jj]
        return (pltpu.make_async_copy(w1_hbm.at[e], w1s.at[slot], wsem.at[slot, 0]),
                pltpu.make_async_copy(w3_hbm.at[e], w3s.at[slot], wsem.at[slot, 1]),
                pltpu.make_async_copy(w2_hbm.at[e], w2s.at[slot], wsem.at[slot, 2]))

    def x_copy(g, slot):
        src = xs_hbm.at[pl.ds(pl.multiple_of(g * rows, rows), rows), :]
        return pltpu.make_async_copy(src, xbuf.at[pl.ds(pl.multiple_of(slot * rows, rows), rows), :], xsem.at[slot])

    def o_copy(g, slot):
        dst = os_hbm.at[pl.ds(pl.multiple_of(g * rows, rows), rows), :]
        return pltpu.make_async_copy(obuf.at[pl.ds(pl.multiple_of(slot * rows, rows), rows), :], dst, osem.at[slot])

    @pl.when(j == 0)
    def _():
        x_copy(0, 0).start()
        for cp in weight_copies(0, 0):
            cp.start(priority=1)

    @pl.when(j < n_visit)
    def _():
        e = vis_ref[j]
        g0 = pstart_ref[e] // bm
        wslot = lax.rem(j, 2)
        for cp in weight_copies(j, wslot):
            cp.wait()

        @pl.when(j + 1 < n_visit)
        def _():
            for cp in weight_copies(j + 1, 1 - wslot):
                cp.start(priority=1)

        w1b[...] = w1s[wslot].astype(BF16)
        w3b[...] = w3s[wslot].astype(BF16)
        w2b[...] = w2s[wslot].astype(BF16)

        def block(b, carry):
            g = g0 + b
            slot = lax.rem(g, 2)
            x_copy(g, slot).wait()

            @pl.when(g + 1 < n_used)
            def _():
                x_copy(g + 1, 1 - slot).start()

            @pl.when(g >= 2)
            def _():
                o_copy(g - 2, slot).wait()

            x = _load_rows(xbuf, bm, nslab, slot * rows)
            a = (_silu(_dot(x, w1b[...])) * _dot(x, w3b[...])).astype(BF16)
            _store_slabs(obuf, _pack_rows(_dot(a, w2b[...])), slot * rows)
            o_copy(g, slot).start()
            return carry

        lax.fori_loop(0, pcnt_ref[e] // bm, block, 0)

    @pl.when(j == pl.num_programs(0) - 1)
    def _():
        @pl.when(n_used >= 2)
        def _():
            o_copy(n_used - 2, lax.rem(n_used, 2)).wait()

        o_copy(n_used - 1, lax.rem(n_used - 1, 2)).wait()


def _experts(visit, n_visit, pstart, pcounts, n_used, xs2, w1, w3, w2, d, bm):
    n_exp, _, de = w1.shape
    nslab = d // (2 * LANES)
    hbm = pl.BlockSpec(memory_space=pl.ANY)
    return pl.pallas_call(
        functools.partial(_experts_kernel, nslab=nslab, bm=bm),
        grid_spec=pltpu.PrefetchScalarGridSpec(
            num_scalar_prefetch=5,
            grid=(n_exp,),
            in_specs=[hbm, hbm, hbm, hbm],
            out_specs=hbm,
            scratch_shapes=[pltpu.VMEM((2 * bm * nslab, LANES), jnp.int32),
                            pltpu.VMEM((2 * bm * nslab, LANES), jnp.int32),
                            pltpu.VMEM((2, d, de), F32), pltpu.VMEM((2, d, de), F32), pltpu.VMEM((2, de, d), F32),
                            pltpu.VMEM((d, de), BF16), pltpu.VMEM((d, de), BF16), pltpu.VMEM((de, d), BF16),
                            pltpu.SemaphoreType.DMA((2,)), pltpu.SemaphoreType.DMA((2,)),
                            pltpu.SemaphoreType.DMA((2, 3))]),
        out_shape=jax.ShapeDtypeStruct(xs2.shape, jnp.int32),
        compiler_params=_cparams(("arbitrary",)),
        name="routed_experts",
    )(visit, n_visit, pstart, pcounts, n_used, xs2, w1, w3, w2)


def _combine_kernel(slot_hbm, os_hbm, wt_ref, sh_ref, x1_ref, g2_ref, o_ref, slot_smem, buf, ssem, gsem,
                    *, nslab, tile0):
    i = pl.program_id(0)
    tt = x1_ref.shape[0]
    rows = TOP_K * tt * nslab

    def row_copy(half, t, kk):
        dst = buf.at[pl.ds(pl.multiple_of(half * rows + (kk * tt + t) * nslab, nslab), nslab), :]
        return pltpu.make_async_copy(os_hbm.at[slot_smem[half, kk, t]], dst, gsem.at[half])

    def fetch(tile, half):
        cp = pltpu.make_async_copy(slot_hbm.at[tile0 + tile], slot_smem.at[half], ssem)
        cp.start()
        cp.wait()

        def issue(t, carry):
            for kk in range(TOP_K):
                row_copy(half, t, kk).start(priority=kk % 2)
            return carry

        lax.fori_loop(0, tt, issue, 0)

    @pl.when(i == 0)
    def _():
        fetch(0, 0)

    def reduce_tile(half):
        def drain(t, carry):
            for kk in range(TOP_K):
                row_copy(half, t, kk).wait()
            return carry

        lax.fori_loop(0, tt, drain, 0)

        wt = wt_ref[...]
        g2 = _seq_rows(g2_ref[...], tt)
        hw = nslab * LANES
        for s in range(nslab):
            y_lo = sh_ref[:, s * LANES:(s + 1) * LANES]
            y_hi = sh_ref[:, hw + s * LANES:hw + (s + 1) * LANES]
            for kk in range(TOP_K):
                lo, hi = _unpack_words(buf[pl.ds(half * rows + kk * tt * nslab + s, tt, stride=nslab), :])
                y_lo = y_lo + wt[:, kk:kk + 1] * lo
                y_hi = y_hi + wt[:, kk:kk + 1] * hi
            for y, c0 in ((y_lo, s * LANES), (y_hi, hw + s * LANES)):
                cols = slice(c0, c0 + LANES)
                o_ref[:, cols] = x1_ref[:, cols] + g2[:, cols] * y

    for half in (0, 1):
        @pl.when(jnp.logical_and(i + 1 < pl.num_programs(0), lax.rem(i + 1, 2) == half))
        def _(half=half):
            fetch(i + 1, half)

    for half in (0, 1):
        @pl.when(lax.rem(i, 2) == half)
        def _(half=half):
            reduce_tile(half)


def _combine(slots, os3, wts_t, shared, x1, g2, seq_len, tok0):
    n, d = x1.shape
    nslab = d // (2 * LANES)
    tt = TOK_TILE
    tile0 = tok0 // tt
    return pl.pallas_call(
        functools.partial(_combine_kernel, nslab=nslab, tile0=tile0),
        grid=(n // tt,),
        in_specs=[pl.BlockSpec(memory_space=pl.ANY),
                  pl.BlockSpec(memory_space=pl.ANY),
                  pl.BlockSpec((tt, TOP_K), lambda i: (tile0 + i, 0)),
                  pl.BlockSpec((tt, d), lambda i: (tile0 + i, 0)),
                  pl.BlockSpec((tt, d), lambda i: (i, 0)),
                  _mod_spec(tt, seq_len, d)],
        out_specs=pl.BlockSpec((tt, d), lambda i: (i, 0)),
        out_shape=jax.ShapeDtypeStruct((n, d), F32),
        scratch_shapes=[pltpu.SMEM((2, TOP_K, tt), jnp.int32),
                        pltpu.VMEM((2 * TOP_K * tt * nslab, LANES), jnp.int32),
                        pltpu.SemaphoreType.DMA, pltpu.SemaphoreType.DMA((2,))],
        compiler_params=_cparams(("arbitrary",)),
        name="moe_combine",
    )(slots, os3, wts_t, shared, x1, g2)


def _mixer(x, mod, positions, s0, past_k, past_v, lb, p, w, moe_rows, moe_buf):
    bsz, seq_len, d = x.shape
    n = bsz * seq_len
    sh1, sc1, g1, sh2, sc2, g2 = mod
    tm = 256 if n % 256 == 0 else n
    x2 = x.reshape(n, d)
    h1 = _norm_mod(x2, p["norm1_w"], sc1, sh1, seq_len, tm)
    tmm = 1024 if n % 1024 == 0 else tm
    hg_w = w["w_in_hg"].shape[1]
    at_w = w["w_in_at"].shape[1]
    z_hg = _matmul(h1, w["w_in_hg"], tmm, min(hg_w, 1024), name="w_in_hgrn")
    z_at = _matmul(h1, w["w_in_at"], tmm, at_w, name="w_in_attn")
    gates = _matmul(h1, w["w_in_gate"], tmm, min(2 * d, 1024), act="sigmoid", out_dtype=BF16, name="w_in_gates")

    o_hg, s_new = _hgrn(z_hg, lb, p["hg_norm_w"], s0, bsz, seq_len)

    n_kv = p["n_kv"]
    kd = n_kv * AT_HD
    qd = at_w - 2 * kd
    tq = min(seq_len, 256)
    qn, kvn = _qk_prep(z_at, positions, p["q_norm_w"], p["k_norm_w"], seq_len, qd, kd, tq)
    if past_k is None:
        o_at = _swa_prompt(qn, kvn, p["attn_sinks"], bsz, seq_len, n_kv)
    else:
        o_at = _swa_sample(qn, kvn, past_k, past_v, p["attn_sinks"], bsz, seq_len, n_kv)

    x1, h2s = _merge(o_hg, o_at, gates, x2, g1, w["w_hg_out"], w["w_at_out"], w["w_o"],
                     p["norm2_w"], sc2, sh2, seq_len, tm, moe_rows, moe_buf)
    kv3 = kvn.reshape(bsz, seq_len, 2 * kd)
    return x1, h2s, s_new, kv3[:, :, :kd], kv3[:, :, kd:]


def kernel(x_prompt, x_sample, cache_k, cache_v, state_hgrn, c_prompt, c_sample, norm1_w, norm2_w, w_ada, b_ada,
           w_in, hg_lower_bounds, hg_norm_w, q_norm_w, k_norm_w, attn_sinks, w_hg_out, w_at_out, w_o, w_router,
           router_bias, w_exp_gate, w_exp_up, w_exp_down, w_sh_gate, w_sh_up, w_sh_down):
    depth = norm1_w.shape[0]
    assert depth == 1, "single trunk layer"
    bp, tp, d = x_prompt.shape
    bs, ts, _ = x_sample.shape
    window, n_kv = cache_k.shape[2], cache_k.shape[3]
    kd = n_kv * AT_HD
    hg_dim = w_hg_out.shape[1]
    qd = w_at_out.shape[1]
    n_exp = w_router.shape[2]
    nslab = d // (2 * LANES)
    l = 0

    lbs = jnp.cumsum(jax.nn.softmax(hg_lower_bounds.astype(F32), axis=0), axis=0)
    win = w_in[l]
    w = {
        "w_in_hg": win[:, :4 * hg_dim].astype(BF16),
        "w_in_at": win[:, 4 * hg_dim:4 * hg_dim + qd + 2 * kd].astype(BF16),
        "w_in_gate": win[:, 4 * hg_dim + qd + 2 * kd:].astype(BF16),
        "w_hg_out": w_hg_out[l].astype(BF16),
        "w_at_out": w_at_out[l].astype(BF16),
        "w_o": w_o[l].astype(BF16),
    }
    p = {"norm1_w": norm1_w[l], "norm2_w": norm2_w[l], "hg_norm_w": hg_norm_w[l], "q_norm_w": q_norm_w[l],
         "k_norm_w": k_norm_w[l], "attn_sinks": attn_sinks[l], "n_kv": n_kv}

    c_all = jnp.concatenate([c_prompt, c_sample], axis=0)
    mod_all = _ada(c_all, w_ada[l], b_ada[l])
    mod_all = mod_all.reshape(bp + bs, 6, 1, d)
    mod_p = tuple(mod_all[:bp, j] for j in range(6))
    mod_s = tuple(mod_all[bp:, j] for j in range(6))

    pos_p = jnp.arange(tp, dtype=jnp.int32)
    pos_s = PAST_LEN + jnp.arange(ts, dtype=jnp.int32)
    s0_p = jnp.zeros((bp,) + state_hgrn.shape[2:], F32)
    n_p, n_s = bp * tp, bs * ts
    n_tok = n_p + n_s
    x1_p, h2_p, sp, kp, vp = _mixer(x_prompt, mod_p, pos_p, s0_p, None, None, lbs[l], p, w, (0, n_tok), None)
    pk = cache_k[l].reshape(bs, window, kd)
    pv = cache_v[l].reshape(bs, window, kd)
    x1_s, h2s, ss, ks, vs = _mixer(x_sample, mod_s, pos_s, state_hgrn[l], pk, pv, lbs[l], p, w, (n_p, n_tok), h2_p)

    tr = 256 if n_tok % 256 == 0 else TOK_TILE
    eidx, wts, rank, counts = _router(h2s, w_router[l], router_bias[l], n_tok, d, tr)
    shared = _shared_expert(h2s, w_sh_gate[l].astype(BF16), w_sh_up[l].astype(BF16), w_sh_down[l].astype(BF16),
                            n_tok, d, tr)

    bm = MOE_ROWS
    counts = counts[:, 0]
    pcounts = (counts + bm - 1) // bm * bm
    pend = jnp.cumsum(pcounts)
    pstart = pend - pcounts
    nb = -(-(n_tok * TOP_K) // bm) + n_exp
    slots = _slots(eidx, rank, pstart, n_tok)
    n_used = (pend[-1:] // bm).astype(jnp.int32)

    xs = _dispatch(counts, pstart, slots, h2s, n_tok, nslab, nb * bm, bm)
    visit = jnp.argsort(counts == 0, stable=True).astype(jnp.int32)
    n_visit = jnp.sum(counts > 0).astype(jnp.int32).reshape(1)
    os_ = _experts(visit, n_visit, pstart, pcounts, n_used, xs.reshape(nb * bm * nslab, LANES),
                   w_exp_gate[l], w_exp_up[l], w_exp_down[l], d, bm)
    os3 = os_.reshape(nb * bm, nslab, LANES)
    wts_t = wts.T
    y_p = _combine(slots, os3, wts_t, shared, x1_p, mod_p[5], tp, 0)
    y_s = _combine(slots, os3, wts_t, shared, x1_s, mod_s[5], ts, n_p)

    def cache_out(a, b_, t):
        return a[:, t - window:].reshape(1, b_, window, n_kv, AT_HD)

    new_k_p = cache_out(kp, bp, tp)
    new_v_p = cache_out(vp, bp, tp)
    keys_s = jnp.concatenate([pk, ks], axis=1)
    vals_s = jnp.concatenate([pv, vs], axis=1)
    new_k_s = cache_out(keys_s, bs, window + ts)
    new_v_s = cache_out(vals_s, bs, window + ts)
    return (y_p.reshape(bp, tp, d), y_s.reshape(bs, ts, d), new_k_p, new_v_p, sp[None],
            new_k_s, new_v_s, ss[None])
```

```python
import functools
import math

import numpy as np
import jax
import jax.numpy as jnp
from jax import lax
from jax.experimental import pallas as pl
from jax.experimental.pallas import tpu as pltpu

EPS = 1e-6
CHUNK = 64
HG_DK = 128
AT_HD = 64
ROPE_DIM = 16
ROPE_THETA = 500000.0
TOP_K = 8
N_GROUPS = 8
TOPK_GROUPS = 4
ROUTED_SCALE = 2.5
PAST_LEN = 2048

LANES = 128
MOE_ROWS = 256
TOK_TILE = 128
VMEM_LIMIT = 56 * 1024 * 1024

F32 = jnp.float32
BF16 = jnp.bfloat16


def _cparams(semantics, vmem=VMEM_LIMIT):
    return pltpu.CompilerParams(dimension_semantics=semantics, vmem_limit_bytes=vmem)


def _sigmoid(x):
    return 1.0 / (1.0 + jnp.exp(-x))


def _silu(x):
    return x * _sigmoid(x)


def _dot(a, b):
    return jnp.dot(a, b, preferred_element_type=F32)


def _dot_nt(a, b):
    return lax.dot_general(a, b, (((1,), (1,)), ((), ())), preferred_element_type=F32)


def _dot_tn(a, b):
    return lax.dot_general(a, b, (((0,), (0,)), ((), ())), preferred_element_type=F32)


def _split_bf16(x):
    hi = x.astype(BF16)
    lo = (x - hi.astype(F32)).astype(BF16)
    return hi, lo


def _seq_rows(m, rows):
    s, _, d = m.shape
    if s == 1:
        return m[0]
    return jnp.broadcast_to(m, (s, rows // s, d)).reshape(rows, d)


def _mod_spec(tm, seq_len, d):
    if tm <= seq_len:
        return pl.BlockSpec((1, 1, d), lambda i: ((i * tm) // seq_len, 0, 0))
    s = tm // seq_len
    return pl.BlockSpec((s, 1, d), lambda i: (i, 0, 0))


def _ada_kernel(c_ref, w_ref, b_ref, o_ref):
    s = _silu(c_ref[...]).astype(BF16)
    o_ref[...] = _dot(s, w_ref[...].astype(BF16)) + b_ref[...]


def _ada(c, w, b):
    n, d = c.shape
    m = w.shape[1]
    tn = min(m, 1024)
    return pl.pallas_call(
        _ada_kernel,
        grid=(m // tn,),
        in_specs=[pl.BlockSpec((n, d), lambda j: (0, 0)),
                  pl.BlockSpec((d, tn), lambda j: (0, j)),
                  pl.BlockSpec((1, tn), lambda j: (0, j))],
        out_specs=pl.BlockSpec((n, tn), lambda j: (0, j)),
        out_shape=jax.ShapeDtypeStruct((n, m), F32),
        compiler_params=_cparams(("parallel",)),
        name="ada_mod",
    )(c, w, b.reshape(1, m))


def _mm_kernel(x_ref, w_ref, o_ref, *, act):
    y = _dot(x_ref[...], w_ref[...])
    if act == "sigmoid":
        y = _sigmoid(y)
    o_ref[...] = y.astype(o_ref.dtype)


def _matmul(x, w, tm, tn, act=None, out_dtype=F32, name="matmul"):
    n, k = x.shape
    m = w.shape[1]
    return pl.pallas_call(
        functools.partial(_mm_kernel, act=act),
        grid=(n // tm, m // tn),
        in_specs=[pl.BlockSpec((tm, k), lambda i, j: (i, 0)),
                  pl.BlockSpec((k, tn), lambda i, j: (0, j))],
        out_specs=pl.BlockSpec((tm, tn), lambda i, j: (i, j)),
        out_shape=jax.ShapeDtypeStruct((n, m), out_dtype),
        compiler_params=_cparams(("parallel", "arbitrary")),
        name=name,
    )(x, w)


def _hgrn_tables(c):
    nlev = int(math.log2(c))
    t = np.arange(c)[:, None]
    s = np.arange(c)[None, :]
    seg = []
    for l in range(1, nlev + 1):
        b = 1 << l
        seg.append(((s >= (t // b) * b) & (s <= t)).astype(np.float32))
        seg.append(((s > t) & (s <= (t // b) * b + b - 1)).astype(np.float32))
    masks = [(t == s).astype(np.float32)]
    for l in range(nlev):
        b = 1 << l
        masks.append(((t // (2 * b) == s // (2 * b)) & (t % (2 * b) >= b) & (s % (2 * b) < b)).astype(np.float32))
    return np.concatenate(seg, axis=0), np.stack(masks, axis=0)


def _hgrn_kernel(q_ref, f_ref, i_ref, g_ref, lb_ref, nw_ref, s0_ref, seg_ref, msk_ref,
                 o_ref, sn_ref, st_ref, *, c, nchunks, hb):
    nlev = int(math.log2(c))
    tstep = pl.program_id(2)

    @pl.when(tstep == 0)
    def _():
        for hh in range(hb):
            st_ref[hh] = s0_ref[0, hh].T

    nw = nw_ref[...]
    lb = lb_ref[...]
    head = lambda a, hh: a[:, hh * HG_DK:(hh + 1) * HG_DK]

    def decays(l, g_hi, g_lo):
        seg = seg_ref[2 * (l - 1) * c:2 * l * c, :]
        e = _dot(seg, g_hi) + _dot(seg, g_lo)
        return e[:c, :], e[c:, :]

    def chunk(ci, carry):
        rows = pl.ds(pl.multiple_of(ci * c, c), c)
        q = q_ref[rows, :]
        f = lb + (1.0 - lb) * _sigmoid(f_ref[rows, :])
        g = jnp.log(f)
        k = 1.0 - f
        g_hi, g_lo = _split_bf16(g)
        qb = q.astype(BF16)
        kb = k.astype(BF16)
        scores = [jnp.where(msk_ref[0] > 0, _dot_nt(head(qb, hh), head(kb, hh)), 0.0) for hh in range(hb)]
        for l in range(nlev):
            if l == 0:
                ql, kl = (q * f).astype(BF16), kb
            else:
                wl, vl = decays(l, g_hi, g_lo)
                ql = (q * jnp.exp(wl)).astype(BF16)
                kl = (k * jnp.exp(vl)).astype(BF16)
            for hh in range(hb):
                scores[hh] = scores[hh] + jnp.where(msk_ref[l + 1] > 0, _dot_nt(head(ql, hh), head(kl, hh)), 0.0)
        a_inc, v_end = decays(nlev, g_hi, g_lo)
        qa = (q * jnp.exp(a_inc)).astype(BF16)
        k_end = (k * jnp.exp(v_end)).astype(BF16)
        vb = i_ref[rows, :].astype(BF16)
        carry_decay = jnp.exp(a_inc[c - 1:c, :])
        gate = _silu(g_ref[rows, :])
        for hh in range(hb):
            st = st_ref[hh]
            o = _dot_nt(head(qa, hh), st.astype(BF16)) + _dot(scores[hh].astype(BF16), head(vb, hh))
            st_ref[hh] = st * head(carry_decay, hh) + _dot_tn(head(vb, hh), head(k_end, hh))
            on = o * lax.rsqrt(jnp.mean(o * o, axis=-1, keepdims=True) + EPS) * nw
            o_ref[rows, hh * HG_DK:(hh + 1) * HG_DK] = (on * head(gate, hh)).astype(o_ref.dtype)
        return carry

    lax.fori_loop(0, nchunks, chunk, 0)

    @pl.when(tstep == pl.num_programs(2) - 1)
    def _():
        for hh in range(hb):
            sn_ref[0, hh] = st_ref[hh].T


def _hgrn(z_hg, lb, norm_w, s0, bsz, seq_len):
    n, w4 = z_hg.shape
    nh = w4 // (4 * HG_DK)
    hb = min(nh, 8)
    ng = nh // hb
    c = min(CHUNK, seq_len)
    tb = min(seq_len, 512)
    nt = seq_len // tb
    seg, masks = _hgrn_tables(c)

    def col(part):
        return pl.BlockSpec((tb, hb * HG_DK), lambda b, h, t: (b * nt + t, part * ng + h))

    return pl.pallas_call(
        functools.partial(_hgrn_kernel, c=c, nchunks=tb // c, hb=hb),
        grid=(bsz, ng, nt),
        in_specs=[col(0), col(1), col(2), col(3),
                  pl.BlockSpec((1, hb * HG_DK), lambda b, h, t: (0, h)),
                  pl.BlockSpec((1, HG_DK), lambda b, h, t: (0, 0)),
                  pl.BlockSpec((1, hb, HG_DK, HG_DK), lambda b, h, t: (b, h, 0, 0)),
                  pl.BlockSpec(seg.shape, lambda b, h, t: (0, 0)),
                  pl.BlockSpec(masks.shape, lambda b, h, t: (0, 0, 0))],
        out_specs=[pl.BlockSpec((tb, hb * HG_DK), lambda b, h, t: (b * nt + t, h)),
                   pl.BlockSpec((1, hb, HG_DK, HG_DK), lambda b, h, t: (b, h, 0, 0))],
        out_shape=[jax.ShapeDtypeStruct((n, nh * HG_DK), BF16),
                   jax.ShapeDtypeStruct((bsz, nh, HG_DK, HG_DK), F32)],
        scratch_shapes=[pltpu.VMEM((hb, HG_DK, HG_DK), F32)],
        compiler_params=_cparams(("parallel", "parallel", "arbitrary")),
        name="hgrn2",
    )(z_hg, z_hg, z_hg, z_hg, lb.reshape(1, -1), norm_w.reshape(1, HG_DK), s0,
      jnp.asarray(seg, BF16), jnp.asarray(masks, F32))


def _rope_tables(positions):
    half = ROPE_DIM // 2
    inv_freq = ROPE_THETA ** (-jnp.arange(0, ROPE_DIM, 2, dtype=F32) / ROPE_DIM)
    ang = positions.astype(F32)[:, None] * inv_freq[None, :]
    cos, sin = jnp.cos(ang), jnp.sin(ang)
    t = positions.shape[0]
    rest = AT_HD - ROPE_DIM
    c64 = jnp.concatenate([cos, cos, jnp.ones((t, rest), F32)], axis=1)
    sa64 = jnp.concatenate([-sin, jnp.zeros((t, half + rest), F32)], axis=1)
    sb64 = jnp.concatenate([jnp.zeros((t, half), F32), sin, jnp.zeros((t, rest), F32)], axis=1)
    rep = LANES // AT_HD
    return tuple(jnp.tile(a, (1, rep)) for a in (c64, sa64, sb64))


def _attn_in_kernel(x_ref, nw_ref, sc_ref, sh_ref, w_ref, cos_ref, sa_ref, sb_ref, qw_ref, kw_ref, bd_ref,
                    h_ref, q_ref, kv_ref, *, qd, kd):
    half = ROPE_DIM // 2
    x = x_ref[...]
    tm = x.shape[0]
    y = x * lax.rsqrt(jnp.mean(x * x, axis=-1, keepdims=True) + EPS) * nw_ref[...]
    h = (y * (1.0 + _seq_rows(sc_ref[...], tm)) + _seq_rows(sh_ref[...], tm)).astype(BF16)
    h_ref[...] = h
    z = _dot(h, w_ref[...])
    cos, sa, sb = cos_ref[...], sa_ref[...], sb_ref[...]

    def norm_rope(x, w, bd, reps):
        x2 = x * x
        hi, lo = _split_bf16(x2)
        ss = _dot(hi, bd) + _dot(lo, bd)
        xn = x * lax.rsqrt(ss * (1.0 / AT_HD) + EPS) * w
        width = x.shape[1]
        tile = lambda a: jnp.concatenate([a] * reps, axis=1) if reps > 1 else a
        return (xn * tile(cos) + pltpu.roll(xn, width - half, 1) * tile(sa)
                + pltpu.roll(xn, half, 1) * tile(sb))

    q = norm_rope(z[:, :qd], qw_ref[...], bd_ref[...], qd // LANES)
    k = norm_rope(z[:, qd:qd + kd], kw_ref[...], bd_ref[:kd, :kd], kd // LANES)
    q_ref[...] = q.astype(q_ref.dtype)
    kv_ref[:, :kd] = k
    kv_ref[:, kd:] = z[:, qd + kd:]


def _attn_in(x, norm_w, sc, sh, w_at, positions, q_norm_w, k_norm_w, seq_len, qd, kd, tm):
    n, d = x.shape
    cos, sa, sb = _rope_tables(positions)
    nt = seq_len // tm
    bd = np.kron(np.eye(qd // AT_HD, dtype=np.float32), np.ones((AT_HD, AT_HD), np.float32))
    tab = pl.BlockSpec((tm, LANES), lambda i: (i % nt, 0))
    const = lambda shape: pl.BlockSpec(shape, lambda i: (0,) * len(shape), pipeline_mode=pl.Buffered(1))
    return pl.pallas_call(
        functools.partial(_attn_in_kernel, qd=qd, kd=kd),
        grid=(n // tm,),
        in_specs=[pl.BlockSpec((tm, d), lambda i: (i, 0)), const((1, d)),
                  _mod_spec(tm, seq_len, d), _mod_spec(tm, seq_len, d),
                  const(w_at.shape), tab, tab, tab, const((1, qd)), const((1, kd)), const((qd, qd))],
        out_specs=[pl.BlockSpec((tm, d), lambda i: (i, 0)),
                   pl.BlockSpec((tm, qd), lambda i: (i, 0)),
                   pl.BlockSpec((tm, 2 * kd), lambda i: (i, 0))],
        out_shape=[jax.ShapeDtypeStruct((n, d), BF16),
                   jax.ShapeDtypeStruct((n, qd), BF16),
                   jax.ShapeDtypeStruct((n, 2 * kd), F32)],
        compiler_params=_cparams(("parallel",)),
        name="attn_in",
    )(x, norm_w.reshape(1, d), sc, sh, w_at, cos, sa, sb, jnp.tile(q_norm_w, qd // AT_HD).reshape(1, qd),
      jnp.tile(k_norm_w, kd // AT_HD).reshape(1, kd), jnp.asarray(bd, BF16))


def _attend(q, keys, vals, sinks_ref, col_valid, o_ref, n_kv, group):
    n_heads = n_kv * group
    batch = 16
    for h0 in range(0, n_heads, batch):
        heads = range(h0, min(h0 + batch, n_heads))
        scores = []
        for h in heads:
            j = h // group
            s = _dot_nt(q[:, h * AT_HD:(h + 1) * AT_HD], keys[:, j * AT_HD:(j + 1) * AT_HD]) * (AT_HD ** -0.5)
            scores.append(s if col_valid is None else jnp.where(col_valid, s, -jnp.inf))
        probs = []
        for h, s in zip(heads, scores):
            sink = sinks_ref[h]
            m = jnp.maximum(jnp.max(s, axis=-1, keepdims=True), sink)
            e = jnp.exp(s - m)
            den = jnp.sum(e, axis=-1, keepdims=True) + jnp.exp(sink - m)
            probs.append((e / den).astype(BF16))
        for h, p in zip(heads, probs):
            j = h // group
            o_ref[:, h * AT_HD:(h + 1) * AT_HD] = _dot(p, vals[:, j * AT_HD:(j + 1) * AT_HD]).astype(o_ref.dtype)


def _swa_prompt_kernel(sinks_ref, q_ref, kv0_ref, kv1_ref, kv2_ref, o_ref, *, n_kv, group, w_chunks):
    kd = n_kv * AT_HD
    n = pl.program_id(1)
    blocks = [kv0_ref[...], kv1_ref[...], kv2_ref[...]]
    keys = jnp.concatenate([b[:, :kd] for b in blocks], axis=0).astype(BF16)
    vals = jnp.concatenate([b[:, kd:] for b in blocks], axis=0).astype(BF16)
    col_chunk = lax.broadcasted_iota(jnp.int32, (1, keys.shape[0]), 1) // CHUNK
    col_valid = (col_chunk + n - w_chunks) >= 0
    _attend(q_ref[...], keys, vals, sinks_ref, col_valid, o_ref, n_kv, group)


def _swa_prompt(qn, kvn, sinks, bsz, seq_len, n_kv):
    n, qd = qn.shape
    kd = n_kv * AT_HD
    nc = seq_len // CHUNK
    group = qd // AT_HD // n_kv
    w_chunks = 2

    def kv_spec(j):
        return pl.BlockSpec((CHUNK, 2 * kd), lambda b, c, s: (b * nc + jnp.maximum(c - w_chunks + j, 0), 0))

    return pl.pallas_call(
        functools.partial(_swa_prompt_kernel, n_kv=n_kv, group=group, w_chunks=w_chunks),
        grid_spec=pltpu.PrefetchScalarGridSpec(
            num_scalar_prefetch=1,
            grid=(bsz, nc),
            in_specs=[pl.BlockSpec((CHUNK, qd), lambda b, c, s: (b * nc + c, 0)),
                      kv_spec(0), kv_spec(1), kv_spec(2)],
            out_specs=pl.BlockSpec((CHUNK, qd), lambda b, c, s: (b * nc + c, 0))),
        out_shape=jax.ShapeDtypeStruct((n, qd), BF16),
        compiler_params=_cparams(("parallel", "arbitrary")),
        name="swa_prompt",
    )(sinks, qn, kvn, kvn, kvn)


def _swa_sample_kernel(sinks_ref, q_ref, pk_ref, pv_ref, kv_ref, o_ref, *, n_kv, group):
    kd = n_kv * AT_HD
    kv = kv_ref[...]
    keys = jnp.concatenate([pk_ref[0], kv[:, :kd]], axis=0).astype(BF16)
    vals = jnp.concatenate([pv_ref[0], kv[:, kd:]], axis=0).astype(BF16)
    _attend(q_ref[...], keys, vals, sinks_ref, None, o_ref, n_kv, group)


def _swa_sample(qn, kvn, past_k, past_v, sinks, bsz, seq_len, n_kv):
    n, qd = qn.shape
    kd = n_kv * AT_HD
    window = past_k.shape[1]
    group = qd // AT_HD // n_kv
    return pl.pallas_call(
        functools.partial(_swa_sample_kernel, n_kv=n_kv, group=group),
        grid_spec=pltpu.PrefetchScalarGridSpec(
            num_scalar_prefetch=1,
            grid=(bsz,),
            in_specs=[pl.BlockSpec((seq_len, qd), lambda b, s: (b, 0)),
                      pl.BlockSpec((1, window, kd), lambda b, s: (b, 0, 0)),
                      pl.BlockSpec((1, window, kd), lambda b, s: (b, 0, 0)),
                      pl.BlockSpec((seq_len, 2 * kd), lambda b, s: (b, 0))],
            out_specs=pl.BlockSpec((seq_len, qd), lambda b, s: (b, 0))),
        out_shape=jax.ShapeDtypeStruct((n, qd), BF16),
        compiler_params=_cparams(("parallel",)),
        name="swa_sample",
    )(sinks, qn, past_k, past_v, kvn)


HI_MASK = np.uint32(0xFFFF0000)


def _pack_rows(x):
    half = x.shape[1] // 2
    bits = lambda a: lax.bitcast_convert_type(a.astype(BF16).astype(F32), jnp.uint32)
    word = (bits(x[:, half:]) & HI_MASK) | (bits(x[:, :half]) >> 16)
    return lax.bitcast_convert_type(word, jnp.int32)


def _unpack_words(w):
    u = lax.bitcast_convert_type(w, jnp.uint32)
    return lax.bitcast_convert_type(u << 16, F32), lax.bitcast_convert_type(u & HI_MASK, F32)


def _store_slabs(ref, words, row0=0):
    rows, width = words.shape
    nslab = width // LANES
    for s in range(nslab):
        ref[pl.ds(row0 + s, rows, stride=nslab), :] = words[:, s * LANES:(s + 1) * LANES]


def _load_rows(ref, rows, nslab, row0=0):
    lo, hi = [], []
    for s in range(nslab):
        a, b = _unpack_words(ref[pl.ds(row0 + s, rows, stride=nslab), :])
        lo.append(a)
        hi.append(b)
    return jnp.concatenate(lo + hi, axis=1).astype(BF16)


def _merge_kernel(ohg_ref, oat_ref, ga_ref, gb_ref, x_ref, g1_ref, whg_ref, wat_ref, wo_ref,
                  nw_ref, sc_ref, sh_ref, *rest):
    x1_ref, h2_ref = rest[-2:]
    tm = x_ref.shape[0]
    merged = ga_ref[...] * _dot(ohg_ref[...], whg_ref[...]) + gb_ref[...] * _dot(oat_ref[...], wat_ref[...])
    mix = _dot(merged.astype(BF16), wo_ref[...])
    x1 = x_ref[...] + _seq_rows(g1_ref[...], tm) * mix
    x1_ref[...] = x1
    y = x1 * lax.rsqrt(jnp.mean(x1 * x1, axis=-1, keepdims=True) + EPS) * nw_ref[...]
    h2 = y * (1.0 + _seq_rows(sc_ref[...], tm)) + _seq_rows(sh_ref[...], tm)
    _store_slabs(h2_ref, _pack_rows(h2))


def _merge(o_hg, o_at, gates, x, g1, w_hg_out, w_at_out, w_o, norm2_w, sc2, sh2, seq_len, tm, moe_rows, moe_buf):
    n, d = x.shape
    hd = o_hg.shape[1]
    ad = o_at.shape[1]
    nslab = d // (2 * LANES)
    tok0, n_all = moe_rows
    tile0 = tok0 // tm
    const = lambda shape: pl.BlockSpec(shape, lambda i: (0,) * len(shape), pipeline_mode=pl.Buffered(1))
    mod = lambda: _mod_spec(tm, seq_len, d)
    in_specs = [pl.BlockSpec((tm, hd), lambda i: (i, 0)),
                pl.BlockSpec((tm, ad), lambda i: (i, 0)),
                pl.BlockSpec((tm, d), lambda i: (i, 0)),
                pl.BlockSpec((tm, d), lambda i: (i, 1)),
                pl.BlockSpec((tm, d), lambda i: (i, 0)),
                mod(), const((hd, d)), const((ad, d)), const((d, d)), const((1, d)), mod(), mod()]
    args = [o_hg, o_at, gates, gates, x, g1, w_hg_out, w_at_out, w_o, norm2_w.reshape(1, d), sc2, sh2]
    aliases = {}
    if moe_buf is not None:
        in_specs.append(pl.BlockSpec(memory_space=pl.ANY))
        args.append(moe_buf)
        aliases = {len(args) - 1: 1}
    return pl.pallas_call(
        _merge_kernel,
        grid=(n // tm,),
        in_specs=in_specs,
        out_specs=[pl.BlockSpec((tm, d), lambda i: (i, 0)),
                   pl.BlockSpec((tm * nslab, LANES), lambda i: (tile0 + i, 0))],
        out_shape=[jax.ShapeDtypeStruct((n, d), F32),
                   jax.ShapeDtypeStruct((n_all * nslab, LANES), jnp.int32)],
        input_output_aliases=aliases,
        compiler_params=_cparams(("parallel",)),
        name="merge_norm2",
    )(*args)


def _router_kernel(h_ref, wr_ref, bias_ref, tri_ref, eidx_ref, wts_ref, rank_ref, cnt_ref, run_ref,
                   *, nslab, n_exp):
    tm = h_ref.shape[0] // nslab
    gsz = n_exp // N_GROUPS
    step = pl.program_id(0)

    @pl.when(step == 0)
    def _():
        run_ref[...] = jnp.zeros_like(run_ref)

    h = _load_rows(h_ref, tm, nslab)
    scores = _sigmoid(_dot_nt(wr_ref[...], h))
    choice = scores + bias_ref[...]
    neg = -jnp.inf
    row = lax.broadcasted_iota(jnp.int32, (gsz, tm), 0)

    gscore = []
    for gi in range(N_GROUPS):
        cg = choice[gi * gsz:(gi + 1) * gsz, :]
        m1 = jnp.max(cg, axis=0, keepdims=True)
        i1 = jnp.min(jnp.where(cg == m1, row, gsz), axis=0, keepdims=True)
        m2 = jnp.max(jnp.where(row == i1, neg, cg), axis=0, keepdims=True)
        gscore.append(m1 + m2)
    gs = jnp.concatenate(gscore, axis=0)
    grow = lax.broadcasted_iota(jnp.int32, (N_GROUPS, tm), 0)
    gsel = jnp.zeros((N_GROUPS, tm), F32)
    for _ in range(TOPK_GROUPS):
        gm = jnp.max(gs, axis=0, keepdims=True)
        gi = jnp.min(jnp.where(gs == gm, grow, N_GROUPS), axis=0, keepdims=True)
        hit = grow == gi
        gsel = jnp.where(hit, 1.0, gsel)
        gs = jnp.where(hit, neg, gs)
    masked = jnp.concatenate(
        [jnp.where(gsel[gi:gi + 1, :] > 0, choice[gi * gsz:(gi + 1) * gsz, :], neg) for gi in range(N_GROUPS)],
        axis=0)

    erow = lax.broadcasted_iota(jnp.int32, (n_exp, tm), 0)
    idxs, raw = [], []
    for _ in range(TOP_K):
        m = jnp.max(masked, axis=0, keepdims=True)
        i = jnp.min(jnp.where(masked == m, erow, n_exp), axis=0, keepdims=True)
        hit = erow == i
        raw.append(jnp.sum(jnp.where(hit, scores, 0.0), axis=0, keepdims=True))
        masked = jnp.where(hit, neg, masked)
        idxs.append(i)
    total = raw[0]
    for r in raw[1:]:
        total = total + r
    onehot = jnp.zeros((n_exp, tm), F32)
    for i in idxs:
        onehot = onehot + jnp.where(erow == i, 1.0, 0.0)
    before = _dot(onehot.astype(BF16), tri_ref[...]) + run_ref[:, 0:1]
    for kk in range(TOP_K):
        eidx_ref[kk:kk + 1, :] = idxs[kk]
        wts_ref[kk:kk + 1, :] = raw[kk] / total * ROUTED_SCALE
        rank_ref[kk:kk + 1, :] = jnp.sum(jnp.where(erow == idxs[kk], before, 0.0), axis=0, keepdims=True).astype(jnp.int32)
    run_ref[...] = run_ref[...] + jnp.sum(onehot, axis=1, keepdims=True)

    @pl.when(step == pl.num_programs(0) - 1)
    def _():
        cnt_ref[...] = run_ref[...].astype(jnp.int32)


def _slot_kernel(eidx_ref, rank_ref, pstart_ref, slot_ref):
    n_exp = pstart_ref.shape[0]
    tt = TOK_TILE
    erow = lax.broadcasted_iota(jnp.int32, (n_exp, tt), 0)
    pstart = pstart_ref[:, 0:1]
    for j in range(slot_ref.shape[0]):
        lanes = slice(j * tt, (j + 1) * tt)
        for kk in range(TOP_K):
            base = jnp.sum(jnp.where(erow == eidx_ref[kk:kk + 1, lanes], pstart, 0), axis=0, keepdims=True)
            slot_ref[j, kk:kk + 1, :] = base + rank_ref[kk:kk + 1, lanes]


def _slots(eidx, rank, pstart, n_tok):
    n_exp = pstart.shape[0]
    tt = TOK_TILE
    ntile = n_tok // tt
    per_step = max(g for g in range(1, 9) if ntile % g == 0)
    tok_spec = pl.BlockSpec((TOP_K, per_step * tt), lambda i: (0, i))
    return pl.pallas_call(
        _slot_kernel,
        grid=(ntile // per_step,),
        in_specs=[tok_spec, tok_spec, pl.BlockSpec((n_exp, LANES), lambda i: (0, 0))],
        out_specs=pl.BlockSpec((per_step, TOP_K, tt), lambda i: (i, 0, 0)),
        out_shape=jax.ShapeDtypeStruct((ntile, TOP_K, tt), jnp.int32),
        compiler_params=_cparams(("parallel",)),
        name="moe_slots",
    )(eidx, rank, jnp.broadcast_to(pstart[:, None], (n_exp, LANES)))


def _router(h2s, w_router, router_bias, n_tok, d, tm):
    n_exp = w_router.shape[1]
    nslab = d // (2 * LANES)
    tri = np.triu(np.ones((tm, tm), np.float32), 1)
    out_tok = lambda dt: jax.ShapeDtypeStruct((TOP_K, n_tok), dt)
    tok_spec = pl.BlockSpec((TOP_K, tm), lambda i: (0, i))
    return pl.pallas_call(
        functools.partial(_router_kernel, nslab=nslab, n_exp=n_exp),
        grid=(n_tok // tm,),
        in_specs=[pl.BlockSpec((tm * nslab, LANES), lambda i: (i, 0)),
                  pl.BlockSpec((n_exp, d), lambda i: (0, 0)),
                  pl.BlockSpec((n_exp, 1), lambda i: (0, 0)),
                  pl.BlockSpec((tm, tm), lambda i: (0, 0))],
        out_specs=[tok_spec, tok_spec, tok_spec, pl.BlockSpec((n_exp, LANES), lambda i: (0, 0))],
        out_shape=[out_tok(jnp.int32), out_tok(F32), out_tok(jnp.int32),
                   jax.ShapeDtypeStruct((n_exp, LANES), jnp.int32)],
        scratch_shapes=[pltpu.VMEM((n_exp, LANES), F32)],
        compiler_params=_cparams(("arbitrary",)),
        name="router_topk",
    )(h2s, w_router.T.astype(BF16), router_bias.reshape(n_exp, 1), jnp.asarray(tri, BF16))


def _shared_kernel(h_ref, w1_ref, w3_ref, w2_ref, o_ref, *, nslab):
    tm = h_ref.shape[0] // nslab
    h = _load_rows(h_ref, tm, nslab)
    a = (_silu(_dot(h, w1_ref[...])) * _dot(h, w3_ref[...])).astype(BF16)
    o_ref[...] = _dot(a, w2_ref[...])


def _shared_expert(h2s, ws1, ws3, ws2, n_tok, d, tm):
    nslab = d // (2 * LANES)
    ds = ws1.shape[1]
    return pl.pallas_call(
        functools.partial(_shared_kernel, nslab=nslab),
        grid=(n_tok // tm,),
        in_specs=[pl.BlockSpec((tm * nslab, LANES), lambda i: (i, 0)),
                  pl.BlockSpec((d, ds), lambda i: (0, 0)),
                  pl.BlockSpec((d, ds), lambda i: (0, 0)),
                  pl.BlockSpec((ds, d), lambda i: (0, 0))],
        out_specs=pl.BlockSpec((tm, d), lambda i: (i, 0)),
        out_shape=jax.ShapeDtypeStruct((n_tok, d), F32),
        compiler_params=_cparams(("parallel",)),
        name="shared_expert",
    )(h2s, ws1, ws3, ws2)


def _dispatch_kernel(cnt_ref, pstart_ref, slot_hbm, h_ref, xs_hbm, slot_smem, zbuf, ssem, dsem, zsem, *, bm):
    i = pl.program_id(0)
    nslab = xs_hbm.shape[1]
    tt = h_ref.shape[0] // nslab
    n_exp = cnt_ref.shape[0]

    @pl.when(i == 0)
    def _():
        zbuf[...] = jnp.zeros_like(zbuf)

        def walk(e, start):
            cnt = cnt_ref[e]
            pad = lax.rem(bm - lax.rem(cnt, bm), bm)
            base = pstart_ref[e] + cnt
            size = bm // 2
            while size >= 1:
                take = (pad & size) != 0

                @pl.when(take)
                def _(base=base, size=size):
                    cp = pltpu.make_async_copy(zbuf.at[pl.ds(0, size)], xs_hbm.at[pl.ds(base, size)], zsem)
                    if start:
                        cp.start()
                    else:
                        cp.wait()

                base = base + jnp.where(take, size, 0)
                size //= 2

        def start_e(e, carry):
            walk(e, True)
            return carry

        def wait_e(e, carry):
            walk(e, False)
            return carry

        lax.fori_loop(0, n_exp, start_e, 0)
        lax.fori_loop(0, n_exp, wait_e, 0)

    cp = pltpu.make_async_copy(slot_hbm.at[i], slot_smem, ssem)
    cp.start()
    cp.wait()

    def row_copy(t, kk):
        src = h_ref.at[pl.ds(pl.multiple_of(t * nslab, nslab), nslab), :]
        return pltpu.make_async_copy(src, xs_hbm.at[slot_smem[kk, t]], dsem)

    def issue(t, carry):
        for kk in range(TOP_K):
            row_copy(t, kk).start(priority=kk % 2)
        return carry

    lax.fori_loop(0, tt, issue, 0)

    def drain(t, carry):
        for kk in range(TOP_K):
            row_copy(t, kk).wait()
        return carry

    lax.fori_loop(0, tt, drain, 0)


def _dispatch(counts, pstart, slots, h2s, n_tok, nslab, n_rows, bm):
    return pl.pallas_call(
        functools.partial(_dispatch_kernel, bm=bm),
        grid_spec=pltpu.PrefetchScalarGridSpec(
            num_scalar_prefetch=2,
            grid=(n_tok // TOK_TILE,),
            in_specs=[pl.BlockSpec(memory_space=pl.ANY),
                      pl.BlockSpec((TOK_TILE * nslab, LANES), lambda i, c, p: (i, 0))],
            out_specs=pl.BlockSpec(memory_space=pl.ANY),
            scratch_shapes=[pltpu.SMEM((TOP_K, TOK_TILE), jnp.int32),
                            pltpu.VMEM((bm // 2, nslab, LANES), jnp.int32),
                            pltpu.SemaphoreType.DMA, pltpu.SemaphoreType.DMA, pltpu.SemaphoreType.DMA]),
        out_shape=jax.ShapeDtypeStruct((n_rows, nslab, LANES), jnp.int32),
        compiler_params=_cparams(("arbitrary",)),
        name="moe_dispatch",
    )(counts, pstart, slots, h2s)


def _experts_kernel(vis_ref, nv_ref, pstart_ref, pcnt_ref, nu_ref, xs_hbm, w1_hbm, w3_hbm, w2_hbm, os_hbm,
                    xbuf, obuf, w1s, w3s, w2s, w1b, w3b, w2b, xsem, osem, wsem, *, nslab, bm):
    j = pl.program_id(0)
    rows = bm * nslab
    n_used = nu_ref[0]
    n_visit = nv_ref[0]

    def weight_copies(jj, slot):
        e = vis_ref[jj]
        return (pltpu.make_async_copy(w1_hbm.at[e], w1s.at[slot], wsem.at[slot, 0]),
                pltpu.make_async_copy(w3_hbm.at[e], w3s.at[slot], wsem.at[slot, 1]),
                pltpu.make_async_copy(w2_hbm.at[e], w2s.at[slot], wsem.at[slot, 2]))

    def x_copy(g, slot):
        src = xs_hbm.at[pl.ds(pl.multiple_of(g * rows, rows), rows), :]
        return pltpu.make_async_copy(src, xbuf.at[pl.ds(pl.multiple_of(slot * rows, rows), rows), :], xsem.at[slot])

    def o_copy(g, slot):
        dst = os_hbm.at[pl.ds(pl.multiple_of(g * rows, rows), rows), :]
        return pltpu.make_async_copy(obuf.at[pl.ds(pl.multiple_of(slot * rows, rows), rows), :], dst, osem.at[slot])

    @pl.when(j == 0)
    def _():
        x_copy(0, 0).start()
        for cp in weight_copies(0, 0):
            cp.start(priority=1)

    @pl.when(j < n_visit)
    def _():
        e = vis_ref[j]
        g0 = pstart_ref[e] // bm
        wslot = lax.rem(j, 2)
        for cp in weight_copies(j, wslot):
            cp.wait()

        @pl.when(j + 1 < n_visit)
        def _():
            for cp in weight_copies(j + 1, 1 - wslot):
                cp.start(priority=1)

        w1b[...] = w1s[wslot].astype(BF16)
        w3b[...] = w3s[wslot].astype(BF16)
        w2b[...] = w2s[wslot].astype(BF16)

        def block(b, carry):
            g = g0 + b
            slot = lax.rem(g, 2)
            x_copy(g, slot).wait()

            @pl.when(g + 1 < n_used)
            def _():
                x_copy(g + 1, 1 - slot).start()

            @pl.when(g >= 2)
            def _():
                o_copy(g - 2, slot).wait()

            x = _load_rows(xbuf, bm, nslab, slot * rows)
            a = (_silu(_dot(x, w1b[...])) * _dot(x, w3b[...])).astype(BF16)
            _store_slabs(obuf, _pack_rows(_dot(a, w2b[...])), slot * rows)
            o_copy(g, slot).start()
            return carry

        lax.fori_loop(0, pcnt_ref[e] // bm, block, 0)

    @pl.when(j == pl.num_programs(0) - 1)
    def _():
        @pl.when(n_used >= 2)
        def _():
            o_copy(n_used - 2, lax.rem(n_used, 2)).wait()

        o_copy(n_used - 1, lax.rem(n_used - 1, 2)).wait()


def _experts(visit, n_visit, pstart, pcounts, n_used, xs2, w1, w3, w2, d, bm):
    n_exp, _, de = w1.shape
    nslab = d // (2 * LANES)
    hbm = pl.BlockSpec(memory_space=pl.ANY)
    return pl.pallas_call(
        functools.partial(_experts_kernel, nslab=nslab, bm=bm),
        grid_spec=pltpu.PrefetchScalarGridSpec(
            num_scalar_prefetch=5,
            grid=(n_exp,),
            in_specs=[hbm, hbm, hbm, hbm],
            out_specs=hbm,
            scratch_shapes=[pltpu.VMEM((2 * bm * nslab, LANES), jnp.int32),
                            pltpu.VMEM((2 * bm * nslab, LANES), jnp.int32),
                            pltpu.VMEM((2, d, de), F32), pltpu.VMEM((2, d, de), F32), pltpu.VMEM((2, de, d), F32),
                            pltpu.VMEM((d, de), BF16), pltpu.VMEM((d, de), BF16), pltpu.VMEM((de, d), BF16),
                            pltpu.SemaphoreType.DMA((2,)), pltpu.SemaphoreType.DMA((2,)),
                            pltpu.SemaphoreType.DMA((2, 3))]),
        out_shape=jax.ShapeDtypeStruct(xs2.shape, jnp.int32),
        compiler_params=_cparams(("arbitrary",)),
        name="routed_experts",
    )(visit, n_visit, pstart, pcounts, n_used, xs2, w1, w3, w2)


def _combine_kernel(slot_hbm, os_hbm, wt_ref, sh_ref, x1_ref, g2_ref, o_ref, slot_smem, buf, ssem, gsem,
                    *, nslab, tile0):
    i = pl.program_id(0)
    tt = x1_ref.shape[0]
    rows = TOP_K * tt * nslab

    def row_copy(half, t, kk):
        dst = buf.at[pl.ds(pl.multiple_of(half * rows + (kk * tt + t) * nslab, nslab), nslab), :]
        return pltpu.make_async_copy(os_hbm.at[slot_smem[half, kk, t]], dst, gsem.at[half])

    def fetch(tile, half):
        cp = pltpu.make_async_copy(slot_hbm.at[tile0 + tile], slot_smem.at[half], ssem)
        cp.start()
        cp.wait()

        def issue(t, carry):
            for kk in range(TOP_K):
                row_copy(half, t, kk).start(priority=kk % 2)
            return carry

        lax.fori_loop(0, tt, issue, 0)

    @pl.when(i == 0)
    def _():
        fetch(0, 0)

    def reduce_tile(half):
        def drain(t, carry):
            for kk in range(TOP_K):
                row_copy(half, t, kk).wait()
            return carry

        lax.fori_loop(0, tt, drain, 0)

        wt = wt_ref[...]
        g2 = _seq_rows(g2_ref[...], tt)
        hw = nslab * LANES
        for s in range(nslab):
            y_lo = sh_ref[:, s * LANES:(s + 1) * LANES]
            y_hi = sh_ref[:, hw + s * LANES:hw + (s + 1) * LANES]
            for kk in range(TOP_K):
                lo, hi = _unpack_words(buf[pl.ds(half * rows + kk * tt * nslab + s, tt, stride=nslab), :])
                y_lo = y_lo + wt[:, kk:kk + 1] * lo
                y_hi = y_hi + wt[:, kk:kk + 1] * hi
            for y, c0 in ((y_lo, s * LANES), (y_hi, hw + s * LANES)):
                cols = slice(c0, c0 + LANES)
                o_ref[:, cols] = x1_ref[:, cols] + g2[:, cols] * y

    for half in (0, 1):
        @pl.when(jnp.logical_and(i + 1 < pl.num_programs(0), lax.rem(i + 1, 2) == half))
        def _(half=half):
            fetch(i + 1, half)

    for half in (0, 1):
        @pl.when(lax.rem(i, 2) == half)
        def _(half=half):
            reduce_tile(half)


def _combine(slots, os3, wts_t, shared, x1, g2, seq_len, tok0):
    n, d = x1.shape
    nslab = d // (2 * LANES)
    tt = TOK_TILE
    tile0 = tok0 // tt
    return pl.pallas_call(
        functools.partial(_combine_kernel, nslab=nslab, tile0=tile0),
        grid=(n // tt,),
        in_specs=[pl.BlockSpec(memory_space=pl.ANY),
                  pl.BlockSpec(memory_space=pl.ANY),
                  pl.BlockSpec((tt, TOP_K), lambda i: (tile0 + i, 0)),
                  pl.BlockSpec((tt, d), lambda i: (tile0 + i, 0)),
                  pl.BlockSpec((tt, d), lambda i: (i, 0)),
                  _mod_spec(tt, seq_len, d)],
        out_specs=pl.BlockSpec((tt, d), lambda i: (i, 0)),
        out_shape=jax.ShapeDtypeStruct((n, d), F32),
        scratch_shapes=[pltpu.SMEM((2, TOP_K, tt), jnp.int32),
                        pltpu.VMEM((2 * TOP_K * tt * nslab, LANES), jnp.int32),
                        pltpu.SemaphoreType.DMA, pltpu.SemaphoreType.DMA((2,))],
        compiler_params=_cparams(("arbitrary",)),
        name="moe_combine",
    )(slots, os3, wts_t, shared, x1, g2)


def _mixer(x, mod, positions, s0, past_k, past_v, lb, p, w, moe_rows, moe_buf):
    bsz, seq_len, d = x.shape
    n = bsz * seq_len
    sh1, sc1, g1, sh2, sc2, g2 = mod
    tm = 256 if n % 256 == 0 else n
    x2 = x.reshape(n, d)
    n_kv = p["n_kv"]
    kd = n_kv * AT_HD
    hg_w = w["w_in_hg"].shape[1]
    qd = w["w_in_at"].shape[1] - 2 * kd
    h1, qn, kvn = _attn_in(x2, p["norm1_w"], sc1, sh1, w["w_in_at"], positions, p["q_norm_w"], p["k_norm_w"],
                           seq_len, qd, kd, min(seq_len, 512))
    tmm = 1024 if n % 1024 == 0 else tm
    z_hg = _matmul(h1, w["w_in_hg"], tmm, min(hg_w, 1024), name="w_in_hgrn")
    gates = _matmul(h1, w["w_in_gate"], tmm, min(2 * d, 1024), act="sigmoid", out_dtype=BF16, name="w_in_gates")

    o_hg, s_new = _hgrn(z_hg, lb, p["hg_norm_w"], s0, bsz, seq_len)

    if past_k is None:
        o_at = _swa_prompt(qn, kvn, p["attn_sinks"], bsz, seq_len, n_kv)
    else:
        o_at = _swa_sample(qn, kvn, past_k, past_v, p["attn_sinks"], bsz, seq_len, n_kv)

    x1, h2s = _merge(o_hg, o_at, gates, x2, g1, w["w_hg_out"], w["w_at_out"], w["w_o"],
                     p["norm2_w"], sc2, sh2, seq_len, tm, moe_rows, moe_buf)
    kv3 = kvn.reshape(bsz, seq_len, 2 * kd)
    return x1, h2s, s_new, kv3[:, :, :kd], kv3[:, :, kd:]


def kernel(x_prompt, x_sample, cache_k, cache_v, state_hgrn, c_prompt, c_sample, norm1_w, norm2_w, w_ada, b_ada,
           w_in, hg_lower_bounds, hg_norm_w, q_norm_w, k_norm_w, attn_sinks, w_hg_out, w_at_out, w_o, w_router,
           router_bias, w_exp_gate, w_exp_up, w_exp_down, w_sh_gate, w_sh_up, w_sh_down):
    depth = norm1_w.shape[0]
    assert depth == 1, "single trunk layer"
    bp, tp, d = x_prompt.shape
    bs, ts, _ = x_sample.shape
    window, n_kv = cache_k.shape[2], cache_k.shape[3]
    kd = n_kv * AT_HD
    hg_dim = w_hg_out.shape[1]
    qd = w_at_out.shape[1]
    n_exp = w_router.shape[2]
    nslab = d // (2 * LANES)
    l = 0

    lbs = jnp.cumsum(jax.nn.softmax(hg_lower_bounds.astype(F32), axis=0), axis=0)
    win = w_in[l]
    w = {
        "w_in_hg": win[:, :4 * hg_dim].astype(BF16),
        "w_in_at": win[:, 4 * hg_dim:4 * hg_dim + qd + 2 * kd].astype(BF16),
        "w_in_gate": win[:, 4 * hg_dim + qd + 2 * kd:].astype(BF16),
        "w_hg_out": w_hg_out[l].astype(BF16),
        "w_at_out": w_at_out[l].astype(BF16),
        "w_o": w_o[l].astype(BF16),
    }
    p = {"norm1_w": norm1_w[l], "norm2_w": norm2_w[l], "hg_norm_w": hg_norm_w[l], "q_norm_w": q_norm_w[l],
         "k_norm_w": k_norm_w[l], "attn_sinks": attn_sinks[l], "n_kv": n_kv}

    c_all = jnp.concatenate([c_prompt, c_sample], axis=0)
    mod_all = _ada(c_all, w_ada[l], b_ada[l])
    mod_all = mod_all.reshape(bp + bs, 6, 1, d)
    mod_p = tuple(mod_all[:bp, j] for j in range(6))
    mod_s = tuple(mod_all[bp:, j] for j in range(6))

    pos_p = jnp.arange(tp, dtype=jnp.int32)
    pos_s = PAST_LEN + jnp.arange(ts, dtype=jnp.int32)
    s0_p = jnp.zeros((bp,) + state_hgrn.shape[2:], F32)
    n_p, n_s = bp * tp, bs * ts
    n_tok = n_p + n_s
    x1_p, h2_p, sp, kp, vp = _mixer(x_prompt, mod_p, pos_p, s0_p, None, None, lbs[l], p, w, (0, n_tok), None)
    pk = cache_k[l].reshape(bs, window, kd)
    pv = cache_v[l].reshape(bs, window, kd)
    x1_s, h2s, ss, ks, vs = _mixer(x_sample, mod_s, pos_s, state_hgrn[l], pk, pv, lbs[l], p, w, (n_p, n_tok), h2_p)

    tr = 256 if n_tok % 256 == 0 else TOK_TILE
    eidx, wts, rank, counts = _router(h2s, w_router[l], router_bias[l], n_tok, d, tr)
    shared = _shared_expert(h2s, w_sh_gate[l].astype(BF16), w_sh_up[l].astype(BF16), w_sh_down[l].astype(BF16),
                            n_tok, d, tr)

    bm = MOE_ROWS
    counts = counts[:, 0]
    pcounts = (counts + bm - 1) // bm * bm
    pend = jnp.cumsum(pcounts)
    pstart = pend - pcounts
    nb = -(-(n_tok * TOP_K) // bm) + n_exp
    slots = _slots(eidx, rank, pstart, n_tok)
    n_used = (pend[-1:] // bm).astype(jnp.int32)

    xs = _dispatch(counts, pstart, slots, h2s, n_tok, nslab, nb * bm, bm)
    visit = jnp.argsort(counts == 0, stable=True).astype(jnp.int32)
    n_visit = jnp.sum(counts > 0).astype(jnp.int32).reshape(1)
    os_ = _experts(visit, n_visit, pstart, pcounts, n_used, xs.reshape(nb * bm * nslab, LANES),
                   w_exp_gate[l], w_exp_up[l], w_exp_down[l], d, bm)
    os3 = os_.reshape(nb * bm, nslab, LANES)
    wts_t = wts.T
    y_p = _combine(slots, os3, wts_t, shared, x1_p, mod_p[5], tp, 0)
    y_s = _combine(slots, os3, wts_t, shared, x1_s, mod_s[5], ts, n_p)

    def cache_out(a, b_, t):
        return a[:, t - window:].reshape(1, b_, window, n_kv, AT_HD)

    new_k_p = cache_out(kp, bp, tp)
    new_v_p = cache_out(vp, bp, tp)
    keys_s = jnp.concatenate([pk, ks], axis=1)
    vals_s = jnp.concatenate([pv, vs], axis=1)
    new_k_s = cache_out(keys_s, bs, window + ts)
    new_v_s = cache_out(vals_s, bs, window + ts)
    return (y_p.reshape(bp, tp, d), y_s.reshape(bs, ts, d), new_k_p, new_v_p, sp[None],
            new_k_s, new_v_s, ss[None])
```

```python
import functools
import math

import numpy as np
import jax
import jax.numpy as jnp
from jax import lax
from jax.experimental import pallas as pl
from jax.experimental.pallas import tpu as pltpu

EPS = 1e-6
CHUNK = 64
HG_CHUNK = 128
HG_DK = 128
AT_HD = 64
ROPE_DIM = 16
ROPE_THETA = 500000.0
TOP_K = 8
N_GROUPS = 8
TOPK_GROUPS = 4
ROUTED_SCALE = 2.5
PAST_LEN = 2048

LANES = 128
MOE_ROWS = 256
TOK_TILE = 128
VMEM_LIMIT = 56 * 1024 * 1024

F32 = jnp.float32
BF16 = jnp.bfloat16


def _cparams(semantics, vmem=VMEM_LIMIT):
    return pltpu.CompilerParams(dimension_semantics=semantics, vmem_limit_bytes=vmem)


def _sigmoid(x):
    return 1.0 / (1.0 + jnp.exp(-x))


def _silu(x):
    return x * _sigmoid(x)


def _dot(a, b):
    return jnp.dot(a, b, preferred_element_type=F32)


def _dot_nt(a, b):
    return lax.dot_general(a, b, (((1,), (1,)), ((), ())), preferred_element_type=F32)


def _dot_tn(a, b):
    return lax.dot_general(a, b, (((0,), (0,)), ((), ())), preferred_element_type=F32)


def _split_bf16(x):
    hi = x.astype(BF16)
    lo = (x - hi.astype(F32)).astype(BF16)
    return hi, lo


def _seq_rows(m, rows):
    s, _, d = m.shape
    if s == 1:
        return m[0]
    return jnp.broadcast_to(m, (s, rows // s, d)).reshape(rows, d)


def _mod_spec(tm, seq_len, d):
    if tm <= seq_len:
        return pl.BlockSpec((1, 1, d), lambda i: ((i * tm) // seq_len, 0, 0))
    s = tm // seq_len
    return pl.BlockSpec((s, 1, d), lambda i: (i, 0, 0))


def _ada_kernel(c_ref, w_ref, b_ref, o_ref):
    s = _silu(c_ref[...]).astype(BF16)
    o_ref[...] = _dot(s, w_ref[...].astype(BF16)) + b_ref[...]


def _ada(c, w, b):
    n, d = c.shape
    m = w.shape[1]
    tn = min(m, 1024)
    return pl.pallas_call(
        _ada_kernel,
        grid=(m // tn,),
        in_specs=[pl.BlockSpec((n, d), lambda j: (0, 0)),
                  pl.BlockSpec((d, tn), lambda j: (0, j)),
                  pl.BlockSpec((1, tn), lambda j: (0, j))],
        out_specs=pl.BlockSpec((n, tn), lambda j: (0, j)),
        out_shape=jax.ShapeDtypeStruct((n, m), F32),
        compiler_params=_cparams(("parallel",)),
        name="ada_mod",
    )(c, w, b.reshape(1, m))


def _mm_kernel(x_ref, w_ref, o_ref, *, act):
    y = _dot(x_ref[...], w_ref[...])
    if act == "sigmoid":
        y = _sigmoid(y)
    o_ref[...] = y.astype(o_ref.dtype)


def _matmul(x, w, tm, tn, act=None, out_dtype=F32, name="matmul"):
    n, k = x.shape
    m = w.shape[1]
    return pl.pallas_call(
        functools.partial(_mm_kernel, act=act),
        grid=(n // tm, m // tn),
        in_specs=[pl.BlockSpec((tm, k), lambda i, j: (i, 0)),
                  pl.BlockSpec((k, tn), lambda i, j: (0, j))],
        out_specs=pl.BlockSpec((tm, tn), lambda i, j: (i, j)),
        out_shape=jax.ShapeDtypeStruct((n, m), out_dtype),
        compiler_params=_cparams(("parallel", "arbitrary")),
        name=name,
    )(x, w)


def _hgrn_tables(c):
    nlev = int(math.log2(c))
    t = np.arange(c)[:, None]
    s = np.arange(c)[None, :]
    seg = []
    for l in range(1, nlev + 1):
        b = 1 << l
        seg.append(((s >= (t // b) * b) & (s <= t)).astype(np.float32))
        seg.append(((s > t) & (s <= (t // b) * b + b - 1)).astype(np.float32))
    masks = [(t == s).astype(np.float32)]
    for l in range(nlev):
        b = 1 << l
        masks.append(((t // (2 * b) == s // (2 * b)) & (t % (2 * b) >= b) & (s % (2 * b) < b)).astype(np.float32))
    return np.concatenate(seg, axis=0), np.stack(masks, axis=0)


def _hgrn_kernel(q_ref, f_ref, i_ref, g_ref, lb_ref, nw_ref, s0_ref, seg_ref, msk_ref,
                 o_ref, sn_ref, st_ref, *, c, nchunks, hb):
    nlev = int(math.log2(c))
    tstep = pl.program_id(2)

    @pl.when(tstep == 0)
    def _():
        for hh in range(hb):
            st_ref[hh] = s0_ref[0, hh].T

    nw = nw_ref[...]
    lb = lb_ref[...]
    head = lambda a, hh: a[:, hh * HG_DK:(hh + 1) * HG_DK]

    def decays(l, g_hi, g_lo):
        seg = seg_ref[2 * (l - 1) * c:2 * l * c, :]
        e = _dot(seg, g_hi) + _dot(seg, g_lo)
        return e[:c, :], e[c:, :]

    def chunk(ci, carry):
        rows = pl.ds(pl.multiple_of(ci * c, c), c)
        q = q_ref[rows, :]
        f = lb + (1.0 - lb) * _sigmoid(f_ref[rows, :])
        g = jnp.log(f)
        k = 1.0 - f
        g_hi, g_lo = _split_bf16(g)
        qb = q.astype(BF16)
        kb = k.astype(BF16)
        scores = [jnp.where(msk_ref[0] > 0, _dot_nt(head(qb, hh), head(kb, hh)), 0.0) for hh in range(hb)]
        for l in range(nlev):
            if l == 0:
                ql, kl = (q * f).astype(BF16), kb
            else:
                wl, vl = decays(l, g_hi, g_lo)
                ql = (q * jnp.exp(wl)).astype(BF16)
                kl = (k * jnp.exp(vl)).astype(BF16)
            for hh in range(hb):
                scores[hh] = scores[hh] + jnp.where(msk_ref[l + 1] > 0, _dot_nt(head(ql, hh), head(kl, hh)), 0.0)
        a_inc, v_end = decays(nlev, g_hi, g_lo)
        qa = (q * jnp.exp(a_inc)).astype(BF16)
        k_end = (k * jnp.exp(v_end)).astype(BF16)
        vb = i_ref[rows, :].astype(BF16)
        carry_decay = jnp.exp(a_inc[c - 1:c, :])
        gate = _silu(g_ref[rows, :])
        for hh in range(hb):
            st = st_ref[hh]
            o = _dot_nt(head(qa, hh), st.astype(BF16)) + _dot(scores[hh].astype(BF16), head(vb, hh))
            st_ref[hh] = st * head(carry_decay, hh) + _dot_tn(head(vb, hh), head(k_end, hh))
            on = o * lax.rsqrt(jnp.mean(o * o, axis=-1, keepdims=True) + EPS) * nw
            o_ref[rows, hh * HG_DK:(hh + 1) * HG_DK] = (on * head(gate, hh)).astype(o_ref.dtype)
        return carry

    lax.fori_loop(0, nchunks, chunk, 0)

    @pl.when(tstep == pl.num_programs(2) - 1)
    def _():
        for hh in range(hb):
            sn_ref[0, hh] = st_ref[hh].T


def _hgrn(z_hg, lb, norm_w, s0, bsz, seq_len):
    n, w4 = z_hg.shape
    nh = w4 // (4 * HG_DK)
    hb = min(nh, 8)
    ng = nh // hb
    c = min(HG_CHUNK, seq_len)
    tb = min(seq_len, 512)
    nt = seq_len // tb
    seg, masks = _hgrn_tables(c)

    def col(part):
        return pl.BlockSpec((tb, hb * HG_DK), lambda b, h, t: (b * nt + t, part * ng + h))

    return pl.pallas_call(
        functools.partial(_hgrn_kernel, c=c, nchunks=tb // c, hb=hb),
        grid=(bsz, ng, nt),
        in_specs=[col(0), col(1), col(2), col(3),
                  pl.BlockSpec((1, hb * HG_DK), lambda b, h, t: (0, h)),
                  pl.BlockSpec((1, HG_DK), lambda b, h, t: (0, 0)),
                  pl.BlockSpec((1, hb, HG_DK, HG_DK), lambda b, h, t: (b, h, 0, 0)),
                  pl.BlockSpec(seg.shape, lambda b, h, t: (0, 0)),
                  pl.BlockSpec(masks.shape, lambda b, h, t: (0, 0, 0))],
        out_specs=[pl.BlockSpec((tb, hb * HG_DK), lambda b, h, t: (b * nt + t, h)),
                   pl.BlockSpec((1, hb, HG_DK, HG_DK), lambda b, h, t: (b, h, 0, 0))],
        out_shape=[jax.ShapeDtypeStruct((n, nh * HG_DK), BF16),
                   jax.ShapeDtypeStruct((bsz, nh, HG_DK, HG_DK), F32)],
        scratch_shapes=[pltpu.VMEM((hb, HG_DK, HG_DK), F32)],
        compiler_params=_cparams(("parallel", "parallel", "arbitrary")),
        name="hgrn2",
    )(z_hg, z_hg, z_hg, z_hg, lb.reshape(1, -1), norm_w.reshape(1, HG_DK), s0,
      jnp.asarray(seg, BF16), jnp.asarray(masks, F32))


def _rope_tables(positions):
    half = ROPE_DIM // 2
    inv_freq = ROPE_THETA ** (-jnp.arange(0, ROPE_DIM, 2, dtype=F32) / ROPE_DIM)
    ang = positions.astype(F32)[:, None] * inv_freq[None, :]
    cos, sin = jnp.cos(ang), jnp.sin(ang)
    t = positions.shape[0]
    rest = AT_HD - ROPE_DIM
    c64 = jnp.concatenate([cos, cos, jnp.ones((t, rest), F32)], axis=1)
    sa64 = jnp.concatenate([-sin, jnp.zeros((t, half + rest), F32)], axis=1)
    sb64 = jnp.concatenate([jnp.zeros((t, half), F32), sin, jnp.zeros((t, rest), F32)], axis=1)
    rep = LANES // AT_HD
    return tuple(jnp.tile(a, (1, rep)) for a in (c64, sa64, sb64))


def _attn_in_kernel(x_ref, nw_ref, sc_ref, sh_ref, w_ref, cos_ref, sa_ref, sb_ref, qw_ref, kw_ref, bd_ref,
                    h_ref, q_ref, kv_ref, *, qd, kd):
    half = ROPE_DIM // 2
    x = x_ref[...]
    tm = x.shape[0]
    y = x * lax.rsqrt(jnp.mean(x * x, axis=-1, keepdims=True) + EPS) * nw_ref[...]
    h = (y * (1.0 + _seq_rows(sc_ref[...], tm)) + _seq_rows(sh_ref[...], tm)).astype(BF16)
    h_ref[...] = h
    z = _dot(h, w_ref[...])
    cos, sa, sb = cos_ref[...], sa_ref[...], sb_ref[...]

    def norm_rope(x, w, bd, reps):
        x2 = x * x
        hi, lo = _split_bf16(x2)
        ss = _dot(hi, bd) + _dot(lo, bd)
        xn = x * lax.rsqrt(ss * (1.0 / AT_HD) + EPS) * w
        width = x.shape[1]
        tile = lambda a: jnp.concatenate([a] * reps, axis=1) if reps > 1 else a
        return (xn * tile(cos) + pltpu.roll(xn, width - half, 1) * tile(sa)
                + pltpu.roll(xn, half, 1) * tile(sb))

    q = norm_rope(z[:, :qd], qw_ref[...], bd_ref[...], qd // LANES)
    k = norm_rope(z[:, qd:qd + kd], kw_ref[...], bd_ref[:kd, :kd], kd // LANES)
    q_ref[...] = q.astype(q_ref.dtype)
    kv_ref[:, :kd] = k
    kv_ref[:, kd:] = z[:, qd + kd:]


def _attn_in(x, norm_w, sc, sh, w_at, positions, q_norm_w, k_norm_w, seq_len, qd, kd, tm):
    n, d = x.shape
    cos, sa, sb = _rope_tables(positions)
    nt = seq_len // tm
    bd = np.kron(np.eye(qd // AT_HD, dtype=np.float32), np.ones((AT_HD, AT_HD), np.float32))
    tab = pl.BlockSpec((tm, LANES), lambda i: (i % nt, 0))
    const = lambda shape: pl.BlockSpec(shape, lambda i: (0,) * len(shape), pipeline_mode=pl.Buffered(1))
    return pl.pallas_call(
        functools.partial(_attn_in_kernel, qd=qd, kd=kd),
        grid=(n // tm,),
        in_specs=[pl.BlockSpec((tm, d), lambda i: (i, 0)), const((1, d)),
                  _mod_spec(tm, seq_len, d), _mod_spec(tm, seq_len, d),
                  const(w_at.shape), tab, tab, tab, const((1, qd)), const((1, kd)), const((qd, qd))],
        out_specs=[pl.BlockSpec((tm, d), lambda i: (i, 0)),
                   pl.BlockSpec((tm, qd), lambda i: (i, 0)),
                   pl.BlockSpec((tm, 2 * kd), lambda i: (i, 0))],
        out_shape=[jax.ShapeDtypeStruct((n, d), BF16),
                   jax.ShapeDtypeStruct((n, qd), BF16),
                   jax.ShapeDtypeStruct((n, 2 * kd), F32)],
        compiler_params=_cparams(("parallel",)),
        name="attn_in",
    )(x, norm_w.reshape(1, d), sc, sh, w_at, cos, sa, sb, jnp.tile(q_norm_w, qd // AT_HD).reshape(1, qd),
      jnp.tile(k_norm_w, kd // AT_HD).reshape(1, kd), jnp.asarray(bd, BF16))


def _attend(q, keys, vals, sinks_ref, col_valid, o_ref, n_kv, group):
    n_heads = n_kv * group
    batch = 16
    for h0 in range(0, n_heads, batch):
        heads = range(h0, min(h0 + batch, n_heads))
        scores = []
        for h in heads:
            j = h // group
            s = _dot_nt(q[:, h * AT_HD:(h + 1) * AT_HD], keys[:, j * AT_HD:(j + 1) * AT_HD]) * (AT_HD ** -0.5)
            scores.append(s if col_valid is None else jnp.where(col_valid, s, -jnp.inf))
        probs = []
        for h, s in zip(heads, scores):
            sink = sinks_ref[h]
            m = jnp.maximum(jnp.max(s, axis=-1, keepdims=True), sink)
            e = jnp.exp(s - m)
            den = jnp.sum(e, axis=-1, keepdims=True) + jnp.exp(sink - m)
            probs.append((e / den).astype(BF16))
        for h, p in zip(heads, probs):
            j = h // group
            o_ref[:, h * AT_HD:(h + 1) * AT_HD] = _dot(p, vals[:, j * AT_HD:(j + 1) * AT_HD]).astype(o_ref.dtype)


def _swa_prompt_kernel(sinks_ref, q_ref, kv0_ref, kv1_ref, kv2_ref, o_ref, *, n_kv, group, w_chunks):
    kd = n_kv * AT_HD
    n = pl.program_id(1)
    blocks = [kv0_ref[...], kv1_ref[...], kv2_ref[...]]
    keys = jnp.concatenate([b[:, :kd] for b in blocks], axis=0).astype(BF16)
    vals = jnp.concatenate([b[:, kd:] for b in blocks], axis=0).astype(BF16)
    col_chunk = lax.broadcasted_iota(jnp.int32, (1, keys.shape[0]), 1) // CHUNK
    col_valid = (col_chunk + n - w_chunks) >= 0
    _attend(q_ref[...], keys, vals, sinks_ref, col_valid, o_ref, n_kv, group)


def _swa_prompt(qn, kvn, sinks, bsz, seq_len, n_kv):
    n, qd = qn.shape
    kd = n_kv * AT_HD
    nc = seq_len // CHUNK
    group = qd // AT_HD // n_kv
    w_chunks = 2

    def kv_spec(j):
        return pl.BlockSpec((CHUNK, 2 * kd), lambda b, c, s: (b * nc + jnp.maximum(c - w_chunks + j, 0), 0))

    return pl.pallas_call(
        functools.partial(_swa_prompt_kernel, n_kv=n_kv, group=group, w_chunks=w_chunks),
        grid_spec=pltpu.PrefetchScalarGridSpec(
            num_scalar_prefetch=1,
            grid=(bsz, nc),
            in_specs=[pl.BlockSpec((CHUNK, qd), lambda b, c, s: (b * nc + c, 0)),
                      kv_spec(0), kv_spec(1), kv_spec(2)],
            out_specs=pl.BlockSpec((CHUNK, qd), lambda b, c, s: (b * nc + c, 0))),
        out_shape=jax.ShapeDtypeStruct((n, qd), BF16),
        compiler_params=_cparams(("parallel", "arbitrary")),
        name="swa_prompt",
    )(sinks, qn, kvn, kvn, kvn)


def _swa_sample_kernel(sinks_ref, q_ref, pk_ref, pv_ref, kv_ref, o_ref, *, n_kv, group):
    kd = n_kv * AT_HD
    kv = kv_ref[...]
    keys = jnp.concatenate([pk_ref[0], kv[:, :kd]], axis=0).astype(BF16)
    vals = jnp.concatenate([pv_ref[0], kv[:, kd:]], axis=0).astype(BF16)
    _attend(q_ref[...], keys, vals, sinks_ref, None, o_ref, n_kv, group)


def _swa_sample(qn, kvn, past_k, past_v, sinks, bsz, seq_len, n_kv):
    n, qd = qn.shape
    kd = n_kv * AT_HD
    window = past_k.shape[1]
    group = qd // AT_HD // n_kv
    return pl.pallas_call(
        functools.partial(_swa_sample_kernel, n_kv=n_kv, group=group),
        grid_spec=pltpu.PrefetchScalarGridSpec(
            num_scalar_prefetch=1,
            grid=(bsz,),
            in_specs=[pl.BlockSpec((seq_len, qd), lambda b, s: (b, 0)),
                      pl.BlockSpec((1, window, kd), lambda b, s: (b, 0, 0)),
                      pl.BlockSpec((1, window, kd), lambda b, s: (b, 0, 0)),
                      pl.BlockSpec((seq_len, 2 * kd), lambda b, s: (b, 0))],
            out_specs=pl.BlockSpec((seq_len, qd), lambda b, s: (b, 0))),
        out_shape=jax.ShapeDtypeStruct((n, qd), BF16),
        compiler_params=_cparams(("parallel",)),
        name="swa_sample",
    )(sinks, qn, past_k, past_v, kvn)


HI_MASK = np.uint32(0xFFFF0000)


def _pack_rows(x):
    half = x.shape[1] // 2
    bits = lambda a: lax.bitcast_convert_type(a.astype(BF16).astype(F32), jnp.uint32)
    word = (bits(x[:, half:]) & HI_MASK) | (bits(x[:, :half]) >> 16)
    return lax.bitcast_convert_type(word, jnp.int32)


def _unpack_words(w):
    u = lax.bitcast_convert_type(w, jnp.uint32)
    return lax.bitcast_convert_type(u << 16, F32), lax.bitcast_convert_type(u & HI_MASK, F32)


def _store_slabs(ref, words, row0=0):
    rows, width = words.shape
    nslab = width // LANES
    for s in range(nslab):
        ref[pl.ds(row0 + s, rows, stride=nslab), :] = words[:, s * LANES:(s + 1) * LANES]


def _load_rows(ref, rows, nslab, row0=0):
    lo, hi = [], []
    for s in range(nslab):
        a, b = _unpack_words(ref[pl.ds(row0 + s, rows, stride=nslab), :])
        lo.append(a)
        hi.append(b)
    return jnp.concatenate(lo + hi, axis=1).astype(BF16)


def _merge_kernel(ohg_ref, oat_ref, ga_ref, gb_ref, x_ref, g1_ref, whg_ref, wat_ref, wo_ref,
                  nw_ref, sc_ref, sh_ref, *rest):
    x1_ref, h2_ref = rest[-2:]
    tm = x_ref.shape[0]
    merged = ga_ref[...] * _dot(ohg_ref[...], whg_ref[...]) + gb_ref[...] * _dot(oat_ref[...], wat_ref[...])
    mix = _dot(merged.astype(BF16), wo_ref[...])
    x1 = x_ref[...] + _seq_rows(g1_ref[...], tm) * mix
    x1_ref[...] = x1
    y = x1 * lax.rsqrt(jnp.mean(x1 * x1, axis=-1, keepdims=True) + EPS) * nw_ref[...]
    h2 = y * (1.0 + _seq_rows(sc_ref[...], tm)) + _seq_rows(sh_ref[...], tm)
    _store_slabs(h2_ref, _pack_rows(h2))


def _merge(o_hg, o_at, gates, x, g1, w_hg_out, w_at_out, w_o, norm2_w, sc2, sh2, seq_len, tm, moe_rows, moe_buf):
    n, d = x.shape
    hd = o_hg.shape[1]
    ad = o_at.shape[1]
    nslab = d // (2 * LANES)
    tok0, n_all = moe_rows
    tile0 = tok0 // tm
    const = lambda shape: pl.BlockSpec(shape, lambda i: (0,) * len(shape), pipeline_mode=pl.Buffered(1))
    mod = lambda: _mod_spec(tm, seq_len, d)
    in_specs = [pl.BlockSpec((tm, hd), lambda i: (i, 0)),
                pl.BlockSpec((tm, ad), lambda i: (i, 0)),
                pl.BlockSpec((tm, d), lambda i: (i, 0)),
                pl.BlockSpec((tm, d), lambda i: (i, 1)),
                pl.BlockSpec((tm, d), lambda i: (i, 0)),
                mod(), const((hd, d)), const((ad, d)), const((d, d)), const((1, d)), mod(), mod()]
    args = [o_hg, o_at, gates, gates, x, g1, w_hg_out, w_at_out, w_o, norm2_w.reshape(1, d), sc2, sh2]
    aliases = {}
    if moe_buf is not None:
        in_specs.append(pl.BlockSpec(memory_space=pl.ANY))
        args.append(moe_buf)
        aliases = {len(args) - 1: 1}
    return pl.pallas_call(
        _merge_kernel,
        grid=(n // tm,),
        in_specs=in_specs,
        out_specs=[pl.BlockSpec((tm, d), lambda i: (i, 0)),
                   pl.BlockSpec((tm * nslab, LANES), lambda i: (tile0 + i, 0))],
        out_shape=[jax.ShapeDtypeStruct((n, d), F32),
                   jax.ShapeDtypeStruct((n_all * nslab, LANES), jnp.int32)],
        input_output_aliases=aliases,
        compiler_params=_cparams(("parallel",)),
        name="merge_norm2",
    )(*args)


def _router_kernel(h_ref, wr_ref, bias_ref, tri_ref, eidx_ref, wts_ref, rank_ref, cnt_ref, run_ref,
                   *, nslab, n_exp):
    tm = h_ref.shape[0] // nslab
    gsz = n_exp // N_GROUPS
    step = pl.program_id(0)

    @pl.when(step == 0)
    def _():
        run_ref[...] = jnp.zeros_like(run_ref)

    h = _load_rows(h_ref, tm, nslab)
    scores = _sigmoid(_dot_nt(wr_ref[...], h))
    choice = scores + bias_ref[...]
    neg = -jnp.inf
    row = lax.broadcasted_iota(jnp.int32, (gsz, tm), 0)

    gscore = []
    for gi in range(N_GROUPS):
        cg = choice[gi * gsz:(gi + 1) * gsz, :]
        m1 = jnp.max(cg, axis=0, keepdims=True)
        i1 = jnp.min(jnp.where(cg == m1, row, gsz), axis=0, keepdims=True)
        m2 = jnp.max(jnp.where(row == i1, neg, cg), axis=0, keepdims=True)
        gscore.append(m1 + m2)
    gs = jnp.concatenate(gscore, axis=0)
    grow = lax.broadcasted_iota(jnp.int32, (N_GROUPS, tm), 0)
    gsel = jnp.zeros((N_GROUPS, tm), F32)
    for _ in range(TOPK_GROUPS):
        gm = jnp.max(gs, axis=0, keepdims=True)
        gi = jnp.min(jnp.where(gs == gm, grow, N_GROUPS), axis=0, keepdims=True)
        hit = grow == gi
        gsel = jnp.where(hit, 1.0, gsel)
        gs = jnp.where(hit, neg, gs)
    masked = jnp.concatenate(
        [jnp.where(gsel[gi:gi + 1, :] > 0, choice[gi * gsz:(gi + 1) * gsz, :], neg) for gi in range(N_GROUPS)],
        axis=0)

    erow = lax.broadcasted_iota(jnp.int32, (n_exp, tm), 0)
    idxs, raw = [], []
    for _ in range(TOP_K):
        m = jnp.max(masked, axis=0, keepdims=True)
        i = jnp.min(jnp.where(masked == m, erow, n_exp), axis=0, keepdims=True)
        hit = erow == i
        raw.append(jnp.sum(jnp.where(hit, scores, 0.0), axis=0, keepdims=True))
        masked = jnp.where(hit, neg, masked)
        idxs.append(i)
    total = raw[0]
    for r in raw[1:]:
        total = total + r
    onehot = jnp.zeros((n_exp, tm), F32)
    for i in idxs:
        onehot = onehot + jnp.where(erow == i, 1.0, 0.0)
    before = _dot(onehot.astype(BF16), tri_ref[...]) + run_ref[:, 0:1]
    for kk in range(TOP_K):
        eidx_ref[kk:kk + 1, :] = idxs[kk]
        wts_ref[kk:kk + 1, :] = raw[kk] / total * ROUTED_SCALE
        rank_ref[kk:kk + 1, :] = jnp.sum(jnp.where(erow == idxs[kk], before, 0.0), axis=0, keepdims=True).astype(jnp.int32)
    run_ref[...] = run_ref[...] + jnp.sum(onehot, axis=1, keepdims=True)

    @pl.when(step == pl.num_programs(0) - 1)
    def _():
        cnt_ref[...] = run_ref[...].astype(jnp.int32)


def _slot_kernel(eidx_ref, rank_ref, pstart_ref, slot_ref):
    n_exp = pstart_ref.shape[0]
    tt = TOK_TILE
    erow = lax.broadcasted_iota(jnp.int32, (n_exp, tt), 0)
    pstart = pstart_ref[:, 0:1]
    for j in range(slot_ref.shape[0]):
        lanes = slice(j * tt, (j + 1) * tt)
        for kk in range(TOP_K):
            base = jnp.sum(jnp.where(erow == eidx_ref[kk:kk + 1, lanes], pstart, 0), axis=0, keepdims=True)
            slot_ref[j, kk:kk + 1, :] = base + rank_ref[kk:kk + 1, lanes]


def _slots(eidx, rank, pstart, n_tok):
    n_exp = pstart.shape[0]
    tt = TOK_TILE
    ntile = n_tok // tt
    per_step = max(g for g in range(1, 9) if ntile % g == 0)
    tok_spec = pl.BlockSpec((TOP_K, per_step * tt), lambda i: (0, i))
    return pl.pallas_call(
        _slot_kernel,
        grid=(ntile // per_step,),
        in_specs=[tok_spec, tok_spec, pl.BlockSpec((n_exp, LANES), lambda i: (0, 0))],
        out_specs=pl.BlockSpec((per_step, TOP_K, tt), lambda i: (i, 0, 0)),
        out_shape=jax.ShapeDtypeStruct((ntile, TOP_K, tt), jnp.int32),
        compiler_params=_cparams(("parallel",)),
        name="moe_slots",
    )(eidx, rank, jnp.broadcast_to(pstart[:, None], (n_exp, LANES)))


def _router(h2s, w_router, router_bias, n_tok, d, tm):
    n_exp = w_router.shape[1]
    nslab = d // (2 * LANES)
    tri = np.triu(np.ones((tm, tm), np.float32), 1)
    out_tok = lambda dt: jax.ShapeDtypeStruct((TOP_K, n_tok), dt)
    tok_spec = pl.BlockSpec((TOP_K, tm), lambda i: (0, i))
    return pl.pallas_call(
        functools.partial(_router_kernel, nslab=nslab, n_exp=n_exp),
        grid=(n_tok // tm,),
        in_specs=[pl.BlockSpec((tm * nslab, LANES), lambda i: (i, 0)),
                  pl.BlockSpec((n_exp, d), lambda i: (0, 0)),
                  pl.BlockSpec((n_exp, 1), lambda i: (0, 0)),
                  pl.BlockSpec((tm, tm), lambda i: (0, 0))],
        out_specs=[tok_spec, tok_spec, tok_spec, pl.BlockSpec((n_exp, LANES), lambda i: (0, 0))],
        out_shape=[out_tok(jnp.int32), out_tok(F32), out_tok(jnp.int32),
                   jax.ShapeDtypeStruct((n_exp, LANES), jnp.int32)],
        scratch_shapes=[pltpu.VMEM((n_exp, LANES), F32)],
        compiler_params=_cparams(("arbitrary",)),
        name="router_topk",
    )(h2s, w_router.T.astype(BF16), router_bias.reshape(n_exp, 1), jnp.asarray(tri, BF16))


def _shared_kernel(h_ref, w1_ref, w3_ref, w2_ref, o_ref, *, nslab):
    tm = h_ref.shape[0] // nslab
    h = _load_rows(h_ref, tm, nslab)
    a = (_silu(_dot(h, w1_ref[...])) * _dot(h, w3_ref[...])).astype(BF16)
    o_ref[...] = _dot(a, w2_ref[...])


def _shared_expert(h2s, ws1, ws3, ws2, n_tok, d, tm):
    nslab = d // (2 * LANES)
    ds = ws1.shape[1]
    return pl.pallas_call(
        functools.partial(_shared_kernel, nslab=nslab),
        grid=(n_tok // tm,),
        in_specs=[pl.BlockSpec((tm * nslab, LANES), lambda i: (i, 0)),
                  pl.BlockSpec((d, ds), lambda i: (0, 0)),
                  pl.BlockSpec((d, ds), lambda i: (0, 0)),
                  pl.BlockSpec((ds, d), lambda i: (0, 0))],
        out_specs=pl.BlockSpec((tm, d), lambda i: (i, 0)),
        out_shape=jax.ShapeDtypeStruct((n_tok, d), F32),
        compiler_params=_cparams(("parallel",)),
        name="shared_expert",
    )(h2s, ws1, ws3, ws2)


def _dispatch_kernel(cnt_ref, pstart_ref, slot_hbm, h_ref, xs_hbm, slot_smem, zbuf, ssem, dsem, zsem, *, bm):
    i = pl.program_id(0)
    nslab = xs_hbm.shape[1]
    tt = h_ref.shape[0] // nslab
    n_exp = cnt_ref.shape[0]

    @pl.when(i == 0)
    def _():
        zbuf[...] = jnp.zeros_like(zbuf)

        def walk(e, start):
            cnt = cnt_ref[e]
            pad = lax.rem(bm - lax.rem(cnt, bm), bm)
            base = pstart_ref[e] + cnt
            size = bm // 2
            while size >= 1:
                take = (pad & size) != 0

                @pl.when(take)
                def _(base=base, size=size):
                    cp = pltpu.make_async_copy(zbuf.at[pl.ds(0, size)], xs_hbm.at[pl.ds(base, size)], zsem)
                    if start:
                        cp.start()
                    else:
                        cp.wait()

                base = base + jnp.where(take, size, 0)
                size //= 2

        def start_e(e, carry):
            walk(e, True)
            return carry

        def wait_e(e, carry):
            walk(e, False)
            return carry

        lax.fori_loop(0, n_exp, start_e, 0)
        lax.fori_loop(0, n_exp, wait_e, 0)

    cp = pltpu.make_async_copy(slot_hbm.at[i], slot_smem, ssem)
    cp.start()
    cp.wait()

    def row_copy(t, kk):
        src = h_ref.at[pl.ds(pl.multiple_of(t * nslab, nslab), nslab), :]
        return pltpu.make_async_copy(src, xs_hbm.at[slot_smem[t, kk]], dsem)

    def issue(t, carry):
        for kk in range(TOP_K):
            row_copy(t, kk).start(priority=kk % 2)
        return carry

    lax.fori_loop(0, tt, issue, 0)

    def drain(t, carry):
        for kk in range(TOP_K):
            row_copy(t, kk).wait()
        return carry

    lax.fori_loop(0, tt, drain, 0)


def _dispatch(counts, pstart, slots, h2s, n_tok, nslab, n_rows, bm):
    return pl.pallas_call(
        functools.partial(_dispatch_kernel, bm=bm),
        grid_spec=pltpu.PrefetchScalarGridSpec(
            num_scalar_prefetch=2,
            grid=(n_tok // TOK_TILE,),
            in_specs=[pl.BlockSpec(memory_space=pl.ANY),
                      pl.BlockSpec((TOK_TILE * nslab, LANES), lambda i, c, p: (i, 0))],
            out_specs=pl.BlockSpec(memory_space=pl.ANY),
            scratch_shapes=[pltpu.SMEM((TOK_TILE, TOP_K), jnp.int32),
                            pltpu.VMEM((bm // 2, nslab, LANES), jnp.int32),
                            pltpu.SemaphoreType.DMA, pltpu.SemaphoreType.DMA, pltpu.SemaphoreType.DMA]),
        out_shape=jax.ShapeDtypeStruct((n_rows, nslab, LANES), jnp.int32),
        compiler_params=_cparams(("arbitrary",)),
        name="moe_dispatch",
    )(counts, pstart, slots, h2s)


def _experts_kernel(vis_ref, nv_ref, pstart_ref, pcnt_ref, nu_ref, xs_hbm, w1_hbm, w3_hbm, w2_hbm, os_hbm,
                    xbuf, obuf, w1s, w3s, w2s, w1b, w3b, w2b, xsem, osem, wsem, *, nslab, bm):
    j = pl.program_id(0)
    rows = bm * nslab
    n_used = nu_ref[0]
    n_visit = nv_ref[0]

    def weight_copies(jj, slot):
        e = vis_ref[jj]
        return (pltpu.make_async_copy(w1_hbm.at[e], w1s.at[slot], wsem.at[slot, 0]),
                pltpu.make_async_copy(w3_hbm.at[e], w3s.at[slot], wsem.at[slot, 1]),
                pltpu.make_async_copy(w2_hbm.at[e], w2s.at[slot], wsem.at[slot, 2]))

    def x_copy(g, slot):
        src = xs_hbm.at[pl.ds(pl.multiple_of(g * rows, rows), rows), :]
        return pltpu.make_async_copy(src, xbuf.at[pl.ds(pl.multiple_of(slot * rows, rows), rows), :], xsem.at[slot])

    def o_copy(g, slot):
        dst = os_hbm.at[pl.ds(pl.multiple_of(g * rows, rows), rows), :]
        return pltpu.make_async_copy(obuf.at[pl.ds(pl.multiple_of(slot * rows, rows), rows), :], dst, osem.at[slot])

    @pl.when(j == 0)
    def _():
        x_copy(0, 0).start()
        for cp in weight_copies(0, 0):
            cp.start(priority=1)

    @pl.when(j < n_visit)
    def _():
        e = vis_ref[j]
        g0 = pstart_ref[e] // bm
        wslot = lax.rem(j, 2)
        for cp in weight_copies(j, wslot):
            cp.wait()

        @pl.when(j + 1 < n_visit)
        def _():
            for cp in weight_copies(j + 1, 1 - wslot):
                cp.start(priority=1)

        w1b[...] = w1s[wslot].astype(BF16)
        w3b[...] = w3s[wslot].astype(BF16)
        w2b[...] = w2s[wslot].astype(BF16)

        def block(b, carry):
            g = g0 + b
            slot = lax.rem(g, 2)
            x_copy(g, slot).wait()

            @pl.when(g + 1 < n_used)
            def _():
                x_copy(g + 1, 1 - slot).start()

            @pl.when(g >= 2)
            def _():
                o_copy(g - 2, slot).wait()

            x = _load_rows(xbuf, bm, nslab, slot * rows)
            a = (_silu(_dot(x, w1b[...])) * _dot(x, w3b[...])).astype(BF16)
            _store_slabs(obuf, _pack_rows(_dot(a, w2b[...])), slot * rows)
            o_copy(g, slot).start()
            return carry

        lax.fori_loop(0, pcnt_ref[e] // bm, block, 0)

    @pl.when(j == pl.num_programs(0) - 1)
    def _():
        @pl.when(n_used >= 2)
        def _():
            o_copy(n_used - 2, lax.rem(n_used, 2)).wait()

        o_copy(n_used - 1, lax.rem(n_used - 1, 2)).wait()


def _experts(visit, n_visit, pstart, pcounts, n_used, xs2, w1, w3, w2, d, bm):
    n_exp, _, de = w1.shape
    nslab = d // (2 * LANES)
    hbm = pl.BlockSpec(memory_space=pl.ANY)
    return pl.pallas_call(
        functools.partial(_experts_kernel, nslab=nslab, bm=bm),
        grid_spec=pltpu.PrefetchScalarGridSpec(
            num_scalar_prefetch=5,
            grid=(n_exp,),
            in_specs=[hbm, hbm, hbm, hbm],
            out_specs=hbm,
            scratch_shapes=[pltpu.VMEM((2 * bm * nslab, LANES), jnp.int32),
                            pltpu.VMEM((2 * bm * nslab, LANES), jnp.int32),
                            pltpu.VMEM((2, d, de), F32), pltpu.VMEM((2, d, de), F32), pltpu.VMEM((2, de, d), F32),
                            pltpu.VMEM((d, de), BF16), pltpu.VMEM((d, de), BF16), pltpu.VMEM((de, d), BF16),
                            pltpu.SemaphoreType.DMA((2,)), pltpu.SemaphoreType.DMA((2,)),
                            pltpu.SemaphoreType.DMA((2, 3))]),
        out_shape=jax.ShapeDtypeStruct(xs2.shape, jnp.int32),
        compiler_params=_cparams(("arbitrary",)),
        name="routed_experts",
    )(visit, n_visit, pstart, pcounts, n_used, xs2, w1, w3, w2)


def _combine_kernel(slot_hbm, os_hbm, wt_ref, sh_ref, x1_ref, g2_ref, o_ref, slot_smem, buf, ssem, gsem,
                    *, nslab, tile0):
    i = pl.program_id(0)
    tt = x1_ref.shape[0]
    rows = TOP_K * tt * nslab

    def row_copy(half, t, kk):
        dst = buf.at[pl.ds(pl.multiple_of(half * rows + (kk * tt + t) * nslab, nslab), nslab), :]
        return pltpu.make_async_copy(os_hbm.at[slot_smem[half, t, kk]], dst, gsem.at[half])

    def fetch(tile, half):
        cp = pltpu.make_async_copy(slot_hbm.at[tile0 + tile], slot_smem.at[half], ssem)
        cp.start()
        cp.wait()

        def issue(t, carry):
            for kk in range(TOP_K):
                row_copy(half, t, kk).start(priority=kk % 2)
            return carry

        lax.fori_loop(0, tt, issue, 0)

    @pl.when(i == 0)
    def _():
        fetch(0, 0)

    def reduce_tile(half):
        def drain(t, carry):
            for kk in range(TOP_K):
                row_copy(half, t, kk).wait()
            return carry

        lax.fori_loop(0, tt, drain, 0)

        wt = wt_ref[...]
        g2 = _seq_rows(g2_ref[...], tt)
        hw = nslab * LANES
        for s in range(nslab):
            y_lo = sh_ref[:, s * LANES:(s + 1) * LANES]
            y_hi = sh_ref[:, hw + s * LANES:hw + (s + 1) * LANES]
            for kk in range(TOP_K):
                lo, hi = _unpack_words(buf[pl.ds(half * rows + kk * tt * nslab + s, tt, stride=nslab), :])
                y_lo = y_lo + wt[:, kk:kk + 1] * lo
                y_hi = y_hi + wt[:, kk:kk + 1] * hi
            for y, c0 in ((y_lo, s * LANES), (y_hi, hw + s * LANES)):
                cols = slice(c0, c0 + LANES)
                o_ref[:, cols] = x1_ref[:, cols] + g2[:, cols] * y

    for half in (0, 1):
        @pl.when(jnp.logical_and(i + 1 < pl.num_programs(0), lax.rem(i + 1, 2) == half))
        def _(half=half):
            fetch(i + 1, half)

    for half in (0, 1):
        @pl.when(lax.rem(i, 2) == half)
        def _(half=half):
            reduce_tile(half)


def _combine(slots, os3, wts_t, shared, x1, g2, seq_len, tok0):
    n, d = x1.shape
    nslab = d // (2 * LANES)
    tt = TOK_TILE
    tile0 = tok0 // tt
    return pl.pallas_call(
        functools.partial(_combine_kernel, nslab=nslab, tile0=tile0),
        grid=(n // tt,),
        in_specs=[pl.BlockSpec(memory_space=pl.ANY),
                  pl.BlockSpec(memory_space=pl.ANY),
                  pl.BlockSpec((tt, TOP_K), lambda i: (tile0 + i, 0)),
                  pl.BlockSpec((tt, d), lambda i: (tile0 + i, 0)),
                  pl.BlockSpec((tt, d), lambda i: (i, 0)),
                  _mod_spec(tt, seq_len, d)],
        out_specs=pl.BlockSpec((tt, d), lambda i: (i, 0)),
        out_shape=jax.ShapeDtypeStruct((n, d), F32),
        scratch_shapes=[pltpu.SMEM((2, tt, TOP_K), jnp.int32),
                        pltpu.VMEM((2 * TOP_K * tt * nslab, LANES), jnp.int32),
                        pltpu.SemaphoreType.DMA, pltpu.SemaphoreType.DMA((2,))],
        compiler_params=_cparams(("arbitrary",)),
        name="moe_combine",
    )(slots, os3, wts_t, shared, x1, g2)


def _mixer(x, mod, positions, s0, past_k, past_v, lb, p, w, moe_rows, moe_buf):
    bsz, seq_len, d = x.shape
    n = bsz * seq_len
    sh1, sc1, g1, sh2, sc2, g2 = mod
    tm = 256 if n % 256 == 0 else n
    x2 = x.reshape(n, d)
    n_kv = p["n_kv"]
    kd = n_kv * AT_HD
    hg_w = w["w_in_hg"].shape[1]
    qd = w["w_in_at"].shape[1] - 2 * kd
    h1, qn, kvn = _attn_in(x2, p["norm1_w"], sc1, sh1, w["w_in_at"], positions, p["q_norm_w"], p["k_norm_w"],
                           seq_len, qd, kd, min(seq_len, 512))
    tmm = 1024 if n % 1024 == 0 else tm
    z_hg = _matmul(h1, w["w_in_hg"], tmm, min(hg_w, 1024), name="w_in_hgrn")
    gates = _matmul(h1, w["w_in_gate"], tmm, min(2 * d, 1024), act="sigmoid", out_dtype=BF16, name="w_in_gates")

    o_hg, s_new = _hgrn(z_hg, lb, p["hg_norm_w"], s0, bsz, seq_len)

    if past_k is None:
        o_at = _swa_prompt(qn, kvn, p["attn_sinks"], bsz, seq_len, n_kv)
    else:
        o_at = _swa_sample(qn, kvn, past_k, past_v, p["attn_sinks"], bsz, seq_len, n_kv)

    x1, h2s = _merge(o_hg, o_at, gates, x2, g1, w["w_hg_out"], w["w_at_out"], w["w_o"],
                     p["norm2_w"], sc2, sh2, seq_len, tm, moe_rows, moe_buf)
    kv3 = kvn.reshape(bsz, seq_len, 2 * kd)
    return x1, h2s, s_new, kv3[:, :, :kd], kv3[:, :, kd:]


def kernel(x_prompt, x_sample, cache_k, cache_v, state_hgrn, c_prompt, c_sample, norm1_w, norm2_w, w_ada, b_ada,
           w_in, hg_lower_bounds, hg_norm_w, q_norm_w, k_norm_w, attn_sinks, w_hg_out, w_at_out, w_o, w_router,
           router_bias, w_exp_gate, w_exp_up, w_exp_down, w_sh_gate, w_sh_up, w_sh_down):
    depth = norm1_w.shape[0]
    assert depth == 1, "single trunk layer"
    bp, tp, d = x_prompt.shape
    bs, ts, _ = x_sample.shape
    window, n_kv = cache_k.shape[2], cache_k.shape[3]
    kd = n_kv * AT_HD
    hg_dim = w_hg_out.shape[1]
    qd = w_at_out.shape[1]
    n_exp = w_router.shape[2]
    nslab = d // (2 * LANES)
    l = 0

    lbs = jnp.cumsum(jax.nn.softmax(hg_lower_bounds.astype(F32), axis=0), axis=0)
    win = w_in[l]
    w = {
        "w_in_hg": win[:, :4 * hg_dim].astype(BF16),
        "w_in_at": win[:, 4 * hg_dim:4 * hg_dim + qd + 2 * kd].astype(BF16),
        "w_in_gate": win[:, 4 * hg_dim + qd + 2 * kd:].astype(BF16),
        "w_hg_out": w_hg_out[l].astype(BF16),
        "w_at_out": w_at_out[l].astype(BF16),
        "w_o": w_o[l].astype(BF16),
    }
    p = {"norm1_w": norm1_w[l], "norm2_w": norm2_w[l], "hg_norm_w": hg_norm_w[l], "q_norm_w": q_norm_w[l],
         "k_norm_w": k_norm_w[l], "attn_sinks": attn_sinks[l], "n_kv": n_kv}

    c_all = jnp.concatenate([c_prompt, c_sample], axis=0)
    mod_all = _ada(c_all, w_ada[l], b_ada[l])
    mod_all = mod_all.reshape(bp + bs, 6, 1, d)
    mod_p = tuple(mod_all[:bp, j] for j in range(6))
    mod_s = tuple(mod_all[bp:, j] for j in range(6))

    pos_p = jnp.arange(tp, dtype=jnp.int32)
    pos_s = PAST_LEN + jnp.arange(ts, dtype=jnp.int32)
    s0_p = jnp.zeros((bp,) + state_hgrn.shape[2:], F32)
    n_p, n_s = bp * tp, bs * ts
    n_tok = n_p + n_s
    x1_p, h2_p, sp, kp, vp = _mixer(x_prompt, mod_p, pos_p, s0_p, None, None, lbs[l], p, w, (0, n_tok), None)
    pk = cache_k[l].reshape(bs, window, kd)
    pv = cache_v[l].reshape(bs, window, kd)
    x1_s, h2s, ss, ks, vs = _mixer(x_sample, mod_s, pos_s, state_hgrn[l], pk, pv, lbs[l], p, w, (n_p, n_tok), h2_p)

    tr = 256 if n_tok % 256 == 0 else TOK_TILE
    eidx, wts, rank, counts = _router(h2s, w_router[l], router_bias[l], n_tok, d, tr)
    shared = _shared_expert(h2s, w_sh_gate[l].astype(BF16), w_sh_up[l].astype(BF16), w_sh_down[l].astype(BF16),
                            n_tok, d, tr)

    bm = MOE_ROWS
    counts = counts[:, 0]
    pcounts = (counts + bm - 1) // bm * bm
    pend = jnp.cumsum(pcounts)
    pstart = pend - pcounts
    nb = -(-(n_tok * TOP_K) // bm) + n_exp
    slots = _slots(eidx, rank, pstart, n_tok).transpose(0, 2, 1)
    n_used = (pend[-1:] // bm).astype(jnp.int32)

    xs = _dispatch(counts, pstart, slots, h2s, n_tok, nslab, nb * bm, bm)
    visit = jnp.argsort(counts == 0, stable=True).astype(jnp.int32)
    n_visit = jnp.sum(counts > 0).astype(jnp.int32).reshape(1)
    os_ = _experts(visit, n_visit, pstart, pcounts, n_used, xs.reshape(nb * bm * nslab, LANES),
                   w_exp_gate[l], w_exp_up[l], w_exp_down[l], d, bm)
    os3 = os_.reshape(nb * bm, nslab, LANES)
    wts_t = wts.T
    y_p = _combine(slots, os3, wts_t, shared, x1_p, mod_p[5], tp, 0)
    y_s = _combine(slots, os3, wts_t, shared, x1_s, mod_s[5], ts, n_p)

    def cache_out(a, b_, t):
        return a[:, t - window:].reshape(1, b_, window, n_kv, AT_HD)

    new_k_p = cache_out(kp, bp, tp)
    new_v_p = cache_out(vp, bp, tp)
    keys_s = jnp.concatenate([pk, ks], axis=1)
    vals_s = jnp.concatenate([pv, vs], axis=1)
    new_k_s = cache_out(keys_s, bs, window + ts)
    new_v_s = cache_out(vals_s, bs, window + ts)
    return (y_p.reshape(bp, tp, d), y_s.reshape(bs, ts, d), new_k_p, new_v_p, sp[None],
            new_k_s, new_v_s, ss[None])
```

```python
import functools
import math

import numpy as np
import jax
import jax.numpy as jnp
from jax import lax
from jax.experimental import pallas as pl
from jax.experimental.pallas import tpu as pltpu

EPS = 1e-6
CHUNK = 64
HG_CHUNK = 128
HG_DK = 128
AT_HD = 64
ROPE_DIM = 16
ROPE_THETA = 500000.0
TOP_K = 8
N_GROUPS = 8
TOPK_GROUPS = 4
ROUTED_SCALE = 2.5
PAST_LEN = 2048

LANES = 128
MOE_ROWS = 256
TOK_TILE = 128
VMEM_LIMIT = 56 * 1024 * 1024

F32 = jnp.float32
BF16 = jnp.bfloat16


def _cparams(semantics, vmem=VMEM_LIMIT):
    return pltpu.CompilerParams(dimension_semantics=semantics, vmem_limit_bytes=vmem)


def _sigmoid(x):
    return 1.0 / (1.0 + jnp.exp(-x))


def _silu(x):
    return x * _sigmoid(x)


def _dot(a, b):
    return jnp.dot(a, b, preferred_element_type=F32)


def _dot_nt(a, b):
    return lax.dot_general(a, b, (((1,), (1,)), ((), ())), preferred_element_type=F32)


def _dot_tn(a, b):
    return lax.dot_general(a, b, (((0,), (0,)), ((), ())), preferred_element_type=F32)


def _split_bf16(x):
    hi = x.astype(BF16)
    lo = (x - hi.astype(F32)).astype(BF16)
    return hi, lo


def _seq_rows(m, rows):
    s, _, d = m.shape
    if s == 1:
        return m[0]
    return jnp.broadcast_to(m, (s, rows // s, d)).reshape(rows, d)


def _mod_spec(tm, seq_len, d):
    if tm <= seq_len:
        return pl.BlockSpec((1, 1, d), lambda i: ((i * tm) // seq_len, 0, 0))
    s = tm // seq_len
    return pl.BlockSpec((s, 1, d), lambda i: (i, 0, 0))


def _ada_kernel(c_ref, w_ref, b_ref, o_ref):
    s = _silu(c_ref[...]).astype(BF16)
    o_ref[...] = _dot(s, w_ref[...].astype(BF16)) + b_ref[...]


def _ada(c, w, b):
    n, d = c.shape
    m = w.shape[1]
    tn = min(m, 1024)
    return pl.pallas_call(
        _ada_kernel,
        grid=(m // tn,),
        in_specs=[pl.BlockSpec((n, d), lambda j: (0, 0)),
                  pl.BlockSpec((d, tn), lambda j: (0, j)),
                  pl.BlockSpec((1, tn), lambda j: (0, j))],
        out_specs=pl.BlockSpec((n, tn), lambda j: (0, j)),
        out_shape=jax.ShapeDtypeStruct((n, m), F32),
        compiler_params=_cparams(("parallel",)),
        name="ada_mod",
    )(c, w, b.reshape(1, m))


def _mm_kernel(x_ref, w_ref, o_ref, *, act):
    y = _dot(x_ref[...], w_ref[...])
    if act == "sigmoid":
        y = _sigmoid(y)
    o_ref[...] = y.astype(o_ref.dtype)


def _matmul(x, w, tm, tn, act=None, out_dtype=F32, name="matmul"):
    n, k = x.shape
    m = w.shape[1]
    return pl.pallas_call(
        functools.partial(_mm_kernel, act=act),
        grid=(n // tm, m // tn),
        in_specs=[pl.BlockSpec((tm, k), lambda i, j: (i, 0)),
                  pl.BlockSpec((k, tn), lambda i, j: (0, j))],
        out_specs=pl.BlockSpec((tm, tn), lambda i, j: (i, j)),
        out_shape=jax.ShapeDtypeStruct((n, m), out_dtype),
        compiler_params=_cparams(("parallel", "arbitrary")),
        name=name,
    )(x, w)


def _hgrn_tables(c):
    nlev = int(math.log2(c))
    t = np.arange(c)[:, None]
    s = np.arange(c)[None, :]
    seg = []
    for l in range(1, nlev + 1):
        b = 1 << l
        seg.append(((s >= (t // b) * b) & (s <= t)).astype(np.float32))
        seg.append(((s > t) & (s <= (t // b) * b + b - 1)).astype(np.float32))
    masks = [(t == s).astype(np.float32)]
    for l in range(nlev):
        b = 1 << l
        masks.append(((t // (2 * b) == s // (2 * b)) & (t % (2 * b) >= b) & (s % (2 * b) < b)).astype(np.float32))
    return np.concatenate(seg, axis=0), np.stack(masks, axis=0)


def _hgrn_kernel(q_ref, f_ref, i_ref, g_ref, lb_ref, nw_ref, s0_ref, seg_ref, msk_ref,
                 o_ref, sn_ref, st_ref, *, c, nchunks, hb):
    nlev = int(math.log2(c))
    tstep = pl.program_id(2)

    @pl.when(tstep == 0)
    def _():
        for hh in range(hb):
            st_ref[hh] = s0_ref[0, hh].T

    nw = nw_ref[...]
    lb = lb_ref[...]
    head = lambda a, hh: a[:, hh * HG_DK:(hh + 1) * HG_DK]

    def decays(l, g_hi, g_lo):
        seg = seg_ref[2 * (l - 1) * c:2 * l * c, :]
        e = _dot(seg, g_hi) + _dot(seg, g_lo)
        return e[:c, :], e[c:, :]

    def chunk(ci, carry):
        rows = pl.ds(pl.multiple_of(ci * c, c), c)
        q = q_ref[rows, :]
        f = lb + (1.0 - lb) * _sigmoid(f_ref[rows, :])
        g = jnp.log(f)
        k = 1.0 - f
        g_hi, g_lo = _split_bf16(g)
        qb = q.astype(BF16)
        kb = k.astype(BF16)
        scores = [jnp.where(msk_ref[0] > 0, _dot_nt(head(qb, hh), head(kb, hh)), 0.0) for hh in range(hb)]
        for l in range(nlev):
            if l == 0:
                ql, kl = (q * f).astype(BF16), kb
            else:
                wl, vl = decays(l, g_hi, g_lo)
                ql = (q * jnp.exp(wl)).astype(BF16)
                kl = (k * jnp.exp(vl)).astype(BF16)
            for hh in range(hb):
                scores[hh] = scores[hh] + jnp.where(msk_ref[l + 1] > 0, _dot_nt(head(ql, hh), head(kl, hh)), 0.0)
        a_inc, v_end = decays(nlev, g_hi, g_lo)
        qa = (q * jnp.exp(a_inc)).astype(BF16)
        k_end = (k * jnp.exp(v_end)).astype(BF16)
        vb = i_ref[rows, :].astype(BF16)
        carry_decay = jnp.exp(a_inc[c - 1:c, :])
        gate = _silu(g_ref[rows, :])
        for hh in range(hb):
            st = st_ref[hh]
            o = _dot_nt(head(qa, hh), st.astype(BF16)) + _dot(scores[hh].astype(BF16), head(vb, hh))
            st_ref[hh] = st * head(carry_decay, hh) + _dot_tn(head(vb, hh), head(k_end, hh))
            on = o * lax.rsqrt(jnp.mean(o * o, axis=-1, keepdims=True) + EPS) * nw
            o_ref[rows, hh * HG_DK:(hh + 1) * HG_DK] = (on * head(gate, hh)).astype(o_ref.dtype)
        return carry

    lax.fori_loop(0, nchunks, chunk, 0)

    @pl.when(tstep == pl.num_programs(2) - 1)
    def _():
        for hh in range(hb):
            sn_ref[0, hh] = st_ref[hh].T


def _hgrn(z_hg, lb, norm_w, s0, bsz, seq_len):
    n, w4 = z_hg.shape
    nh = w4 // (4 * HG_DK)
    hb = min(nh, 8)
    ng = nh // hb
    c = min(HG_CHUNK, seq_len)
    tb = min(seq_len, 512)
    nt = seq_len // tb
    seg, masks = _hgrn_tables(c)

    def col(part):
        return pl.BlockSpec((tb, hb * HG_DK), lambda b, h, t: (b * nt + t, part * ng + h))

    return pl.pallas_call(
        functools.partial(_hgrn_kernel, c=c, nchunks=tb // c, hb=hb),
        grid=(bsz, ng, nt),
        in_specs=[col(0), col(1), col(2), col(3),
                  pl.BlockSpec((1, hb * HG_DK), lambda b, h, t: (0, h)),
                  pl.BlockSpec((1, HG_DK), lambda b, h, t: (0, 0)),
                  pl.BlockSpec((1, hb, HG_DK, HG_DK), lambda b, h, t: (b, h, 0, 0)),
                  pl.BlockSpec(seg.shape, lambda b, h, t: (0, 0)),
                  pl.BlockSpec(masks.shape, lambda b, h, t: (0, 0, 0))],
        out_specs=[pl.BlockSpec((tb, hb * HG_DK), lambda b, h, t: (b * nt + t, h)),
                   pl.BlockSpec((1, hb, HG_DK, HG_DK), lambda b, h, t: (b, h, 0, 0))],
        out_shape=[jax.ShapeDtypeStruct((n, nh * HG_DK), BF16),
                   jax.ShapeDtypeStruct((bsz, nh, HG_DK, HG_DK), F32)],
        scratch_shapes=[pltpu.VMEM((hb, HG_DK, HG_DK), F32)],
        compiler_params=_cparams(("parallel", "parallel", "arbitrary")),
        name="hgrn2",
    )(z_hg, z_hg, z_hg, z_hg, lb.reshape(1, -1), norm_w.reshape(1, HG_DK), s0,
      jnp.asarray(seg, BF16), jnp.asarray(masks, F32))


def _rope_tables(positions):
    half = ROPE_DIM // 2
    inv_freq = ROPE_THETA ** (-jnp.arange(0, ROPE_DIM, 2, dtype=F32) / ROPE_DIM)
    ang = positions.astype(F32)[:, None] * inv_freq[None, :]
    cos, sin = jnp.cos(ang), jnp.sin(ang)
    t = positions.shape[0]
    rest = AT_HD - ROPE_DIM
    c64 = jnp.concatenate([cos, cos, jnp.ones((t, rest), F32)], axis=1)
    sa64 = jnp.concatenate([-sin, jnp.zeros((t, half + rest), F32)], axis=1)
    sb64 = jnp.concatenate([jnp.zeros((t, half), F32), sin, jnp.zeros((t, rest), F32)], axis=1)
    rep = LANES // AT_HD
    return tuple(jnp.tile(a, (1, rep)) for a in (c64, sa64, sb64))


def _attn_in_kernel(x_ref, nw_ref, sc_ref, sh_ref, w_ref, cos_ref, sa_ref, sb_ref, qw_ref, kw_ref, bd_ref,
                    h_ref, q_ref, kv_ref, *, qd, kd):
    half = ROPE_DIM // 2
    x = x_ref[...]
    tm = x.shape[0]
    y = x * lax.rsqrt(jnp.mean(x * x, axis=-1, keepdims=True) + EPS) * nw_ref[...]
    h = (y * (1.0 + _seq_rows(sc_ref[...], tm)) + _seq_rows(sh_ref[...], tm)).astype(BF16)
    h_ref[...] = h
    z = _dot(h, w_ref[...])
    cos, sa, sb = cos_ref[...], sa_ref[...], sb_ref[...]

    def norm_rope(x, w, bd, reps):
        x2 = x * x
        hi, lo = _split_bf16(x2)
        ss = _dot(hi, bd) + _dot(lo, bd)
        xn = x * lax.rsqrt(ss * (1.0 / AT_HD) + EPS) * w
        width = x.shape[1]
        tile = lambda a: jnp.concatenate([a] * reps, axis=1) if reps > 1 else a
        return (xn * tile(cos) + pltpu.roll(xn, width - half, 1) * tile(sa)
                + pltpu.roll(xn, half, 1) * tile(sb))

    q = norm_rope(z[:, :qd], qw_ref[...], bd_ref[...], qd // LANES)
    k = norm_rope(z[:, qd:qd + kd], kw_ref[...], bd_ref[:kd, :kd], kd // LANES)
    q_ref[...] = q.astype(q_ref.dtype)
    kv_ref[:, :kd] = k
    kv_ref[:, kd:] = z[:, qd + kd:]


def _attn_in(x, norm_w, sc, sh, w_at, positions, q_norm_w, k_norm_w, seq_len, qd, kd, tm):
    n, d = x.shape
    cos, sa, sb = _rope_tables(positions)
    nt = seq_len // tm
    bd = np.kron(np.eye(qd // AT_HD, dtype=np.float32), np.ones((AT_HD, AT_HD), np.float32))
    tab = pl.BlockSpec((tm, LANES), lambda i: (i % nt, 0))
    const = lambda shape: pl.BlockSpec(shape, lambda i: (0,) * len(shape), pipeline_mode=pl.Buffered(1))
    return pl.pallas_call(
        functools.partial(_attn_in_kernel, qd=qd, kd=kd),
        grid=(n // tm,),
        in_specs=[pl.BlockSpec((tm, d), lambda i: (i, 0)), const((1, d)),
                  _mod_spec(tm, seq_len, d), _mod_spec(tm, seq_len, d),
                  const(w_at.shape), tab, tab, tab, const((1, qd)), const((1, kd)), const((qd, qd))],
        out_specs=[pl.BlockSpec((tm, d), lambda i: (i, 0)),
                   pl.BlockSpec((tm, qd), lambda i: (i, 0)),
                   pl.BlockSpec((tm, 2 * kd), lambda i: (i, 0))],
        out_shape=[jax.ShapeDtypeStruct((n, d), BF16),
                   jax.ShapeDtypeStruct((n, qd), BF16),
                   jax.ShapeDtypeStruct((n, 2 * kd), F32)],
        compiler_params=_cparams(("parallel",)),
        name="attn_in",
    )(x, norm_w.reshape(1, d), sc, sh, w_at, cos, sa, sb, jnp.tile(q_norm_w, qd // AT_HD).reshape(1, qd),
      jnp.tile(k_norm_w, kd // AT_HD).reshape(1, kd), jnp.asarray(bd, BF16))


def _attend(q, keys, vals, sinks_ref, col_valid, o_ref, n_kv, group):
    n_heads = n_kv * group
    batch = 16
    for h0 in range(0, n_heads, batch):
        heads = range(h0, min(h0 + batch, n_heads))
        scores = []
        for h in heads:
            j = h // group
            s = _dot_nt(q[:, h * AT_HD:(h + 1) * AT_HD], keys[:, j * AT_HD:(j + 1) * AT_HD]) * (AT_HD ** -0.5)
            scores.append(s if col_valid is None else jnp.where(col_valid, s, -jnp.inf))
        probs = []
        for h, s in zip(heads, scores):
            sink = sinks_ref[h]
            m = jnp.maximum(jnp.max(s, axis=-1, keepdims=True), sink)
            e = jnp.exp(s - m)
            den = jnp.sum(e, axis=-1, keepdims=True) + jnp.exp(sink - m)
            probs.append((e / den).astype(BF16))
        for h, p in zip(heads, probs):
            j = h // group
            o_ref[:, h * AT_HD:(h + 1) * AT_HD] = _dot(p, vals[:, j * AT_HD:(j + 1) * AT_HD]).astype(o_ref.dtype)


def _swa_prompt_kernel(sinks_ref, q_ref, kv0_ref, kv1_ref, kv2_ref, o_ref, *, n_kv, group, w_chunks):
    kd = n_kv * AT_HD
    n = pl.program_id(1)
    blocks = [kv0_ref[...], kv1_ref[...], kv2_ref[...]]
    keys = jnp.concatenate([b[:, :kd] for b in blocks], axis=0).astype(BF16)
    vals = jnp.concatenate([b[:, kd:] for b in blocks], axis=0).astype(BF16)
    col_chunk = lax.broadcasted_iota(jnp.int32, (1, keys.shape[0]), 1) // CHUNK
    col_valid = (col_chunk + n - w_chunks) >= 0
    _attend(q_ref[...], keys, vals, sinks_ref, col_valid, o_ref, n_kv, group)


def _swa_prompt(qn, kvn, sinks, bsz, seq_len, n_kv):
    n, qd = qn.shape
    kd = n_kv * AT_HD
    nc = seq_len // CHUNK
    group = qd // AT_HD // n_kv
    w_chunks = 2

    def kv_spec(j):
        return pl.BlockSpec((CHUNK, 2 * kd), lambda b, c, s: (b * nc + jnp.maximum(c - w_chunks + j, 0), 0))

    return pl.pallas_call(
        functools.partial(_swa_prompt_kernel, n_kv=n_kv, group=group, w_chunks=w_chunks),
        grid_spec=pltpu.PrefetchScalarGridSpec(
            num_scalar_prefetch=1,
            grid=(bsz, nc),
            in_specs=[pl.BlockSpec((CHUNK, qd), lambda b, c, s: (b * nc + c, 0)),
                      kv_spec(0), kv_spec(1), kv_spec(2)],
            out_specs=pl.BlockSpec((CHUNK, qd), lambda b, c, s: (b * nc + c, 0))),
        out_shape=jax.ShapeDtypeStruct((n, qd), BF16),
        compiler_params=_cparams(("parallel", "arbitrary")),
        name="swa_prompt",
    )(sinks, qn, kvn, kvn, kvn)


def _swa_sample_kernel(sinks_ref, q_ref, pk_ref, pv_ref, kv_ref, o_ref, *, n_kv, group):
    kd = n_kv * AT_HD
    kv = kv_ref[...]
    keys = jnp.concatenate([pk_ref[0], kv[:, :kd]], axis=0).astype(BF16)
    vals = jnp.concatenate([pv_ref[0], kv[:, kd:]], axis=0).astype(BF16)
    _attend(q_ref[...], keys, vals, sinks_ref, None, o_ref, n_kv, group)


def _swa_sample(qn, kvn, past_k, past_v, sinks, bsz, seq_len, n_kv):
    n, qd = qn.shape
    kd = n_kv * AT_HD
    window = past_k.shape[1]
    group = qd // AT_HD // n_kv
    return pl.pallas_call(
        functools.partial(_swa_sample_kernel, n_kv=n_kv, group=group),
        grid_spec=pltpu.PrefetchScalarGridSpec(
            num_scalar_prefetch=1,
            grid=(bsz,),
            in_specs=[pl.BlockSpec((seq_len, qd), lambda b, s: (b, 0)),
                      pl.BlockSpec((1, window, kd), lambda b, s: (b, 0, 0)),
                      pl.BlockSpec((1, window, kd), lambda b, s: (b, 0, 0)),
                      pl.BlockSpec((seq_len, 2 * kd), lambda b, s: (b, 0))],
            out_specs=pl.BlockSpec((seq_len, qd), lambda b, s: (b, 0))),
        out_shape=jax.ShapeDtypeStruct((n, qd), BF16),
        compiler_params=_cparams(("parallel",)),
        name="swa_sample",
    )(sinks, qn, past_k, past_v, kvn)


HI_MASK = np.uint32(0xFFFF0000)


def _pack_rows(x):
    half = x.shape[1] // 2
    bits = lambda a: lax.bitcast_convert_type(a.astype(BF16).astype(F32), jnp.uint32)
    word = (bits(x[:, half:]) & HI_MASK) | (bits(x[:, :half]) >> 16)
    return lax.bitcast_convert_type(word, jnp.int32)


def _unpack_words(w):
    u = lax.bitcast_convert_type(w, jnp.uint32)
    return lax.bitcast_convert_type(u << 16, F32), lax.bitcast_convert_type(u & HI_MASK, F32)


def _store_slabs(ref, words, row0=0):
    rows, width = words.shape
    nslab = width // LANES
    for s in range(nslab):
        ref[pl.ds(row0 + s, rows, stride=nslab), :] = words[:, s * LANES:(s + 1) * LANES]


def _load_rows(ref, rows, nslab, row0=0):
    lo, hi = [], []
    for s in range(nslab):
        a, b = _unpack_words(ref[pl.ds(row0 + s, rows, stride=nslab), :])
        lo.append(a)
        hi.append(b)
    return jnp.concatenate(lo + hi, axis=1).astype(BF16)


def _merge_kernel(ohg_ref, oat_ref, ga_ref, gb_ref, x_ref, g1_ref, whg_ref, wat_ref, wo_ref,
                  nw_ref, sc_ref, sh_ref, *rest):
    x1_ref, h2_ref = rest[-2:]
    tm = x_ref.shape[0]
    merged = ga_ref[...] * _dot(ohg_ref[...], whg_ref[...]) + gb_ref[...] * _dot(oat_ref[...], wat_ref[...])
    mix = _dot(merged.astype(BF16), wo_ref[...])
    x1 = x_ref[...] + _seq_rows(g1_ref[...], tm) * mix
    x1_ref[...] = x1
    y = x1 * lax.rsqrt(jnp.mean(x1 * x1, axis=-1, keepdims=True) + EPS) * nw_ref[...]
    h2 = y * (1.0 + _seq_rows(sc_ref[...], tm)) + _seq_rows(sh_ref[...], tm)
    _store_slabs(h2_ref, _pack_rows(h2))


def _merge(o_hg, o_at, gates, x, g1, w_hg_out, w_at_out, w_o, norm2_w, sc2, sh2, seq_len, tm, moe_rows, moe_buf):
    n, d = x.shape
    hd = o_hg.shape[1]
    ad = o_at.shape[1]
    nslab = d // (2 * LANES)
    tok0, n_all = moe_rows
    tile0 = tok0 // tm
    const = lambda shape: pl.BlockSpec(shape, lambda i: (0,) * len(shape), pipeline_mode=pl.Buffered(1))
    mod = lambda: _mod_spec(tm, seq_len, d)
    in_specs = [pl.BlockSpec((tm, hd), lambda i: (i, 0)),
                pl.BlockSpec((tm, ad), lambda i: (i, 0)),
                pl.BlockSpec((tm, d), lambda i: (i, 0)),
                pl.BlockSpec((tm, d), lambda i: (i, 1)),
                pl.BlockSpec((tm, d), lambda i: (i, 0)),
                mod(), const((hd, d)), const((ad, d)), const((d, d)), const((1, d)), mod(), mod()]
    args = [o_hg, o_at, gates, gates, x, g1, w_hg_out, w_at_out, w_o, norm2_w.reshape(1, d), sc2, sh2]
    aliases = {}
    if moe_buf is not None:
        in_specs.append(pl.BlockSpec(memory_space=pl.ANY))
        args.append(moe_buf)
        aliases = {len(args) - 1: 1}
    return pl.pallas_call(
        _merge_kernel,
        grid=(n // tm,),
        in_specs=in_specs,
        out_specs=[pl.BlockSpec((tm, d), lambda i: (i, 0)),
                   pl.BlockSpec((tm * nslab, LANES), lambda i: (tile0 + i, 0))],
        out_shape=[jax.ShapeDtypeStruct((n, d), F32),
                   jax.ShapeDtypeStruct((n_all * nslab, LANES), jnp.int32)],
        input_output_aliases=aliases,
        compiler_params=_cparams(("parallel",)),
        name="merge_norm2",
    )(*args)


def _router_kernel(h_ref, wr_ref, bias_ref, tri_ref, eidx_ref, wts_ref, rank_ref, cnt_ref, run_ref,
                   *, nslab, n_exp):
    tm = h_ref.shape[0] // nslab
    gsz = n_exp // N_GROUPS
    step = pl.program_id(0)

    @pl.when(step == 0)
    def _():
        run_ref[...] = jnp.zeros_like(run_ref)

    h = _load_rows(h_ref, tm, nslab)
    scores = _sigmoid(_dot_nt(wr_ref[...], h))
    choice = scores + bias_ref[...]
    neg = -jnp.inf
    row = lax.broadcasted_iota(jnp.int32, (gsz, tm), 0)

    gscore = []
    for gi in range(N_GROUPS):
        cg = choice[gi * gsz:(gi + 1) * gsz, :]
        m1 = jnp.max(cg, axis=0, keepdims=True)
        i1 = jnp.min(jnp.where(cg == m1, row, gsz), axis=0, keepdims=True)
        m2 = jnp.max(jnp.where(row == i1, neg, cg), axis=0, keepdims=True)
        gscore.append(m1 + m2)
    gs = jnp.concatenate(gscore, axis=0)
    grow = lax.broadcasted_iota(jnp.int32, (N_GROUPS, tm), 0)
    gsel = jnp.zeros((N_GROUPS, tm), F32)
    for _ in range(TOPK_GROUPS):
        gm = jnp.max(gs, axis=0, keepdims=True)
        gi = jnp.min(jnp.where(gs == gm, grow, N_GROUPS), axis=0, keepdims=True)
        hit = grow == gi
        gsel = jnp.where(hit, 1.0, gsel)
        gs = jnp.where(hit, neg, gs)
    masked = jnp.concatenate(
        [jnp.where(gsel[gi:gi + 1, :] > 0, choice[gi * gsz:(gi + 1) * gsz, :], neg) for gi in range(N_GROUPS)],
        axis=0)

    erow = lax.broadcasted_iota(jnp.int32, (n_exp, tm), 0)
    idxs, raw = [], []
    for _ in range(TOP_K):
        m = jnp.max(masked, axis=0, keepdims=True)
        i = jnp.min(jnp.where(masked == m, erow, n_exp), axis=0, keepdims=True)
        hit = erow == i
        raw.append(jnp.sum(jnp.where(hit, scores, 0.0), axis=0, keepdims=True))
        masked = jnp.where(hit, neg, masked)
        idxs.append(i)
    total = raw[0]
    for r in raw[1:]:
        total = total + r
    onehot = jnp.zeros((n_exp, tm), F32)
    for i in idxs:
        onehot = onehot + jnp.where(erow == i, 1.0, 0.0)
    before = _dot(onehot.astype(BF16), tri_ref[...]) + run_ref[:, 0:1]
    for kk in range(TOP_K):
        eidx_ref[kk:kk + 1, :] = idxs[kk]
        wts_ref[kk:kk + 1, :] = raw[kk] / total * ROUTED_SCALE
        rank_ref[kk:kk + 1, :] = jnp.sum(jnp.where(erow == idxs[kk], before, 0.0), axis=0, keepdims=True).astype(jnp.int32)
    run_ref[...] = run_ref[...] + jnp.sum(onehot, axis=1, keepdims=True)

    @pl.when(step == pl.num_programs(0) - 1)
    def _():
        cnt_ref[...] = run_ref[...].astype(jnp.int32)


def _slot_kernel(eidx_ref, rank_ref, pstart_ref, slot_ref):
    n_exp = pstart_ref.shape[0]
    tt = TOK_TILE
    erow = lax.broadcasted_iota(jnp.int32, (n_exp, tt), 0)
    pstart = pstart_ref[:, 0:1]
    for j in range(slot_ref.shape[0]):
        lanes = slice(j * tt, (j + 1) * tt)
        for kk in range(TOP_K):
            base = jnp.sum(jnp.where(erow == eidx_ref[kk:kk + 1, lanes], pstart, 0), axis=0, keepdims=True)
            slot_ref[j, kk:kk + 1, :] = base + rank_ref[kk:kk + 1, lanes]


def _slots(eidx, rank, pstart, n_tok):
    n_exp = pstart.shape[0]
    tt = TOK_TILE
    ntile = n_tok // tt
    per_step = max(g for g in range(1, 9) if ntile % g == 0)
    tok_spec = pl.BlockSpec((TOP_K, per_step * tt), lambda i: (0, i))
    return pl.pallas_call(
        _slot_kernel,
        grid=(ntile // per_step,),
        in_specs=[tok_spec, tok_spec, pl.BlockSpec((n_exp, LANES), lambda i: (0, 0))],
        out_specs=pl.BlockSpec((per_step, TOP_K, tt), lambda i: (i, 0, 0)),
        out_shape=jax.ShapeDtypeStruct((ntile, TOP_K, tt), jnp.int32),
        compiler_params=_cparams(("parallel",)),
        name="moe_slots",
    )(eidx, rank, jnp.broadcast_to(pstart[:, None], (n_exp, LANES)))


def _router(h2s, w_router, router_bias, n_tok, d, tm):
    n_exp = w_router.shape[1]
    nslab = d // (2 * LANES)
    tri = np.triu(np.ones((tm, tm), np.float32), 1)
    out_tok = lambda dt: jax.ShapeDtypeStruct((TOP_K, n_tok), dt)
    tok_spec = pl.BlockSpec((TOP_K, tm), lambda i: (0, i))
    return pl.pallas_call(
        functools.partial(_router_kernel, nslab=nslab, n_exp=n_exp),
        grid=(n_tok // tm,),
        in_specs=[pl.BlockSpec((tm * nslab, LANES), lambda i: (i, 0)),
                  pl.BlockSpec((n_exp, d), lambda i: (0, 0)),
                  pl.BlockSpec((n_exp, 1), lambda i: (0, 0)),
                  pl.BlockSpec((tm, tm), lambda i: (0, 0))],
        out_specs=[tok_spec, tok_spec, tok_spec, pl.BlockSpec((n_exp, LANES), lambda i: (0, 0))],
        out_shape=[out_tok(jnp.int32), out_tok(F32), out_tok(jnp.int32),
                   jax.ShapeDtypeStruct((n_exp, LANES), jnp.int32)],
        scratch_shapes=[pltpu.VMEM((n_exp, LANES), F32)],
        compiler_params=_cparams(("arbitrary",)),
        name="router_topk",
    )(h2s, w_router.T.astype(BF16), router_bias.reshape(n_exp, 1), jnp.asarray(tri, BF16))


def _shared_kernel(h_ref, w1_ref, w3_ref, w2_ref, o_ref, *, nslab):
    tm = h_ref.shape[0] // nslab
    h = _load_rows(h_ref, tm, nslab)
    a = (_silu(_dot(h, w1_ref[...])) * _dot(h, w3_ref[...])).astype(BF16)
    o_ref[...] = _dot(a, w2_ref[...])


def _shared_expert(h2s, ws1, ws3, ws2, n_tok, d, tm):
    nslab = d // (2 * LANES)
    ds = ws1.shape[1]
    return pl.pallas_call(
        functools.partial(_shared_kernel, nslab=nslab),
        grid=(n_tok // tm,),
        in_specs=[pl.BlockSpec((tm * nslab, LANES), lambda i: (i, 0)),
                  pl.BlockSpec((d, ds), lambda i: (0, 0)),
                  pl.BlockSpec((d, ds), lambda i: (0, 0)),
                  pl.BlockSpec((ds, d), lambda i: (0, 0))],
        out_specs=pl.BlockSpec((tm, d), lambda i: (i, 0)),
        out_shape=jax.ShapeDtypeStruct((n_tok, d), F32),
        compiler_params=_cparams(("parallel",)),
        name="shared_expert",
    )(h2s, ws1, ws3, ws2)


def _dispatch_kernel(cnt_ref, pstart_ref, slot_hbm, h_ref, xs_hbm, slot_smem, zbuf, ssem, dsem, zsem, *, bm):
    i = pl.program_id(0)
    nslab = xs_hbm.shape[1]
    tt = h_ref.shape[0] // nslab
    n_exp = cnt_ref.shape[0]

    def slot_copy(tile, half):
        return pltpu.make_async_copy(slot_hbm.at[tile], slot_smem.at[half], ssem.at[half])

    @pl.when(i == 0)
    def _():
        slot_copy(0, 0).start()

    @pl.when(i == 0)
    def _():
        zbuf[...] = jnp.zeros_like(zbuf)

        def walk(e, start):
            cnt = cnt_ref[e]
            pad = lax.rem(bm - lax.rem(cnt, bm), bm)
            base = pstart_ref[e] + cnt
            size = bm // 2
            while size >= 1:
                take = (pad & size) != 0

                @pl.when(take)
                def _(base=base, size=size):
                    cp = pltpu.make_async_copy(zbuf.at[pl.ds(0, size)], xs_hbm.at[pl.ds(base, size)], zsem)
                    if start:
                        cp.start()
                    else:
                        cp.wait()

                base = base + jnp.where(take, size, 0)
                size //= 2

        def start_e(e, carry):
            walk(e, True)
            return carry

        def wait_e(e, carry):
            walk(e, False)
            return carry

        lax.fori_loop(0, n_exp, start_e, 0)
        lax.fori_loop(0, n_exp, wait_e, 0)

    half = lax.rem(i, 2)
    slot_copy(i, half).wait()

    @pl.when(i + 1 < pl.num_programs(0))
    def _():
        slot_copy(i + 1, 1 - half).start()

    def row_copy(t, kk):
        src = h_ref.at[pl.ds(pl.multiple_of(t * nslab, nslab), nslab), :]
        return pltpu.make_async_copy(src, xs_hbm.at[slot_smem[half, kk, t]], dsem)

    def issue(t, carry):
        for kk in range(TOP_K):
            row_copy(t, kk).start(priority=kk % 2)
        return carry

    lax.fori_loop(0, tt, issue, 0)

    def drain(t, carry):
        for kk in range(TOP_K):
            row_copy(t, kk).wait()
        return carry

    lax.fori_loop(0, tt, drain, 0)


def _dispatch(counts, pstart, slots, h2s, n_tok, nslab, n_rows, bm):
    return pl.pallas_call(
        functools.partial(_dispatch_kernel, bm=bm),
        grid_spec=pltpu.PrefetchScalarGridSpec(
            num_scalar_prefetch=2,
            grid=(n_tok // TOK_TILE,),
            in_specs=[pl.BlockSpec(memory_space=pl.ANY),
                      pl.BlockSpec((TOK_TILE * nslab, LANES), lambda i, c, p: (i, 0))],
            out_specs=pl.BlockSpec(memory_space=pl.ANY),
            scratch_shapes=[pltpu.SMEM((2, TOP_K, TOK_TILE), jnp.int32),
                            pltpu.VMEM((bm // 2, nslab, LANES), jnp.int32),
                            pltpu.SemaphoreType.DMA((2,)), pltpu.SemaphoreType.DMA, pltpu.SemaphoreType.DMA]),
        out_shape=jax.ShapeDtypeStruct((n_rows, nslab, LANES), jnp.int32),
        compiler_params=_cparams(("arbitrary",)),
        name="moe_dispatch",
    )(counts, pstart, slots, h2s)


def _experts_kernel(vis_ref, nv_ref, pstart_ref, pcnt_ref, nu_ref, xs_hbm, w1_hbm, w3_hbm, w2_hbm, os_hbm,
                    xbuf, obuf, w1s, w3s, w2s, w1b, w3b, w2b, xsem, osem, wsem, *, nslab, bm):
    j = pl.program_id(0)
    rows = bm * nslab
    n_used = nu_ref[0]
    n_visit = nv_ref[0]

    def weight_copies(jj, slot):
        e = vis_ref[jj]
        return (pltpu.make_async_copy(w1_hbm.at[e], w1s.at[slot], wsem.at[slot, 0]),
                pltpu.make_async_copy(w3_hbm.at[e], w3s.at[slot], wsem.at[slot, 1]),
                pltpu.make_async_copy(w2_hbm.at[e], w2s.at[slot], wsem.at[slot, 2]))

    def x_copy(g, slot):
        src = xs_hbm.at[pl.ds(pl.multiple_of(g * rows, rows), rows), :]
        return pltpu.make_async_copy(src, xbuf.at[pl.ds(pl.multiple_of(slot * rows, rows), rows), :], xsem.at[slot])

    def o_copy(g, slot):
        dst = os_hbm.at[pl.ds(pl.multiple_of(g * rows, rows), rows), :]
        return pltpu.make_async_copy(obuf.at[pl.ds(pl.multiple_of(slot * rows, rows), rows), :], dst, osem.at[slot])

    @pl.when(j == 0)
    def _():
        x_copy(0, 0).start()
        for cp in weight_copies(0, 0):
            cp.start(priority=1)

    @pl.when(j < n_visit)
    def _():
        e = vis_ref[j]
        g0 = pstart_ref[e] // bm
        wslot = lax.rem(j, 2)
        for cp in weight_copies(j, wslot):
            cp.wait()

        @pl.when(j + 1 < n_visit)
        def _():
            for cp in weight_copies(j + 1, 1 - wslot):
                cp.start(priority=1)

        w1b[...] = w1s[wslot].astype(BF16)
        w3b[...] = w3s[wslot].astype(BF16)
        w2b[...] = w2s[wslot].astype(BF16)

        def block(b, carry):
            g = g0 + b
            slot = lax.rem(g, 2)
            x_copy(g, slot).wait()

            @pl.when(g + 1 < n_used)
            def _():
                x_copy(g + 1, 1 - slot).start()

            @pl.when(g >= 2)
            def _():
                o_copy(g - 2, slot).wait()

            x = _load_rows(xbuf, bm, nslab, slot * rows)
            a = (_silu(_dot(x, w1b[...])) * _dot(x, w3b[...])).astype(BF16)
            _store_slabs(obuf, _pack_rows(_dot(a, w2b[...])), slot * rows)
            o_copy(g, slot).start()
            return carry

        lax.fori_loop(0, pcnt_ref[e] // bm, block, 0)

    @pl.when(j == pl.num_programs(0) - 1)
    def _():
        @pl.when(n_used >= 2)
        def _():
            o_copy(n_used - 2, lax.rem(n_used, 2)).wait()

        o_copy(n_used - 1, lax.rem(n_used - 1, 2)).wait()


def _experts(visit, n_visit, pstart, pcounts, n_used, xs2, w1, w3, w2, d, bm):
    n_exp, _, de = w1.shape
    nslab = d // (2 * LANES)
    hbm = pl.BlockSpec(memory_space=pl.ANY)
    return pl.pallas_call(
        functools.partial(_experts_kernel, nslab=nslab, bm=bm),
        grid_spec=pltpu.PrefetchScalarGridSpec(
            num_scalar_prefetch=5,
            grid=(n_exp,),
            in_specs=[hbm, hbm, hbm, hbm],
            out_specs=hbm,
            scratch_shapes=[pltpu.VMEM((2 * bm * nslab, LANES), jnp.int32),
                            pltpu.VMEM((2 * bm * nslab, LANES), jnp.int32),
                            pltpu.VMEM((2, d, de), F32), pltpu.VMEM((2, d, de), F32), pltpu.VMEM((2, de, d), F32),
                            pltpu.VMEM((d, de), BF16), pltpu.VMEM((d, de), BF16), pltpu.VMEM((de, d), BF16),
                            pltpu.SemaphoreType.DMA((2,)), pltpu.SemaphoreType.DMA((2,)),
                            pltpu.SemaphoreType.DMA((2, 3))]),
        out_shape=jax.ShapeDtypeStruct(xs2.shape, jnp.int32),
        compiler_params=_cparams(("arbitrary",)),
        name="routed_experts",
    )(visit, n_visit, pstart, pcounts, n_used, xs2, w1, w3, w2)


def _combine_kernel(slot_hbm, os_hbm, wt_ref, sh_ref, x1_ref, g2_ref, o_ref, slot_smem, buf, ssem, gsem,
                    *, nslab, tile0):
    i = pl.program_id(0)
    tt = x1_ref.shape[0]
    rows = TOP_K * tt * nslab

    n_tiles = pl.num_programs(0)

    def row_copy(half, t, kk, row):
        dst = buf.at[pl.ds(pl.multiple_of(half * rows + (kk * tt + t) * nslab, nslab), nslab), :]
        return pltpu.make_async_copy(os_hbm.at[row], dst, gsem.at[half])

    def slot_copy(tile, half):
        return pltpu.make_async_copy(slot_hbm.at[tile0 + tile], slot_smem.at[half], ssem.at[half])

    def gather(half):
        def issue(t, carry):
            for kk in range(TOP_K):
                row_copy(half, t, kk, slot_smem[half, kk, t]).start(priority=kk % 2)
            return carry

        lax.fori_loop(0, tt, issue, 0)

    @pl.when(i == 0)
    def _():
        first = slot_copy(0, 0)
        first.start()
        first.wait()
        gather(0)

        @pl.when(1 < n_tiles)
        def _():
            slot_copy(1, 1).start()

    def reduce_tile(half):
        def drain(t, carry):
            for kk in range(TOP_K):
                row_copy(half, t, kk, 0).wait()
            return carry

        lax.fori_loop(0, tt, drain, 0)

        wt = wt_ref[...]
        g2 = _seq_rows(g2_ref[...], tt)
        hw = nslab * LANES
        for s in range(nslab):
            y_lo = sh_ref[:, s * LANES:(s + 1) * LANES]
            y_hi = sh_ref[:, hw + s * LANES:hw + (s + 1) * LANES]
            for kk in range(TOP_K):
                lo, hi = _unpack_words(buf[pl.ds(half * rows + kk * tt * nslab + s, tt, stride=nslab), :])
                y_lo = y_lo + wt[:, kk:kk + 1] * lo
                y_hi = y_hi + wt[:, kk:kk + 1] * hi
            for y, c0 in ((y_lo, s * LANES), (y_hi, hw + s * LANES)):
                cols = slice(c0, c0 + LANES)
                o_ref[:, cols] = x1_ref[:, cols] + g2[:, cols] * y

    for half in (0, 1):
        @pl.when(jnp.logical_and(i + 1 < n_tiles, lax.rem(i + 1, 2) == half))
        def _(half=half):
            slot_copy(i + 1, half).wait()
            gather(half)

            @pl.when(i + 2 < n_tiles)
            def _():
                slot_copy(i + 2, 1 - half).start()

    for half in (0, 1):
        @pl.when(lax.rem(i, 2) == half)
        def _(half=half):
            reduce_tile(half)


def _combine(slots, os3, wts_t, shared, x1, g2, seq_len, tok0):
    n, d = x1.shape
    nslab = d // (2 * LANES)
    tt = TOK_TILE
    tile0 = tok0 // tt
    return pl.pallas_call(
        functools.partial(_combine_kernel, nslab=nslab, tile0=tile0),
        grid=(n // tt,),
        in_specs=[pl.BlockSpec(memory_space=pl.ANY),
                  pl.BlockSpec(memory_space=pl.ANY),
                  pl.BlockSpec((tt, TOP_K), lambda i: (tile0 + i, 0)),
                  pl.BlockSpec((tt, d), lambda i: (tile0 + i, 0)),
                  pl.BlockSpec((tt, d), lambda i: (i, 0)),
                  _mod_spec(tt, seq_len, d)],
        out_specs=pl.BlockSpec((tt, d), lambda i: (i, 0)),
        out_shape=jax.ShapeDtypeStruct((n, d), F32),
        scratch_shapes=[pltpu.SMEM((2, TOP_K, tt), jnp.int32),
                        pltpu.VMEM((2 * TOP_K * tt * nslab, LANES), jnp.int32),
                        pltpu.SemaphoreType.DMA((2,)), pltpu.SemaphoreType.DMA((2,))],
        compiler_params=_cparams(("arbitrary",)),
        name="moe_combine",
    )(slots, os3, wts_t, shared, x1, g2)


def _mixer(x, mod, positions, s0, past_k, past_v, lb, p, w, moe_rows, moe_buf):
    bsz, seq_len, d = x.shape
    n = bsz * seq_len
    sh1, sc1, g1, sh2, sc2, g2 = mod
    tm = 256 if n % 256 == 0 else n
    x2 = x.reshape(n, d)
    n_kv = p["n_kv"]
    kd = n_kv * AT_HD
    hg_w = w["w_in_hg"].shape[1]
    qd = w["w_in_at"].shape[1] - 2 * kd
    h1, qn, kvn = _attn_in(x2, p["norm1_w"], sc1, sh1, w["w_in_at"], positions, p["q_norm_w"], p["k_norm_w"],
                           seq_len, qd, kd, min(seq_len, 512))
    tmm = 1024 if n % 1024 == 0 else tm
    z_hg = _matmul(h1, w["w_in_hg"], tmm, min(hg_w, 1024), name="w_in_hgrn")
    gates = _matmul(h1, w["w_in_gate"], tmm, min(2 * d, 1024), act="sigmoid", out_dtype=BF16, name="w_in_gates")

    o_hg, s_new = _hgrn(z_hg, lb, p["hg_norm_w"], s0, bsz, seq_len)

    if past_k is None:
        o_at = _swa_prompt(qn, kvn, p["attn_sinks"], bsz, seq_len, n_kv)
    else:
        o_at = _swa_sample(qn, kvn, past_k, past_v, p["attn_sinks"], bsz, seq_len, n_kv)

    x1, h2s = _merge(o_hg, o_at, gates, x2, g1, w["w_hg_out"], w["w_at_out"], w["w_o"],
                     p["norm2_w"], sc2, sh2, seq_len, tm, moe_rows, moe_buf)
    kv3 = kvn.reshape(bsz, seq_len, 2 * kd)
    return x1, h2s, s_new, kv3[:, :, :kd], kv3[:, :, kd:]


def kernel(x_prompt, x_sample, cache_k, cache_v, state_hgrn, c_prompt, c_sample, norm1_w, norm2_w, w_ada, b_ada,
           w_in, hg_lower_bounds, hg_norm_w, q_norm_w, k_norm_w, attn_sinks, w_hg_out, w_at_out, w_o, w_router,
           router_bias, w_exp_gate, w_exp_up, w_exp_down, w_sh_gate, w_sh_up, w_sh_down):
    depth = norm1_w.shape[0]
    assert depth == 1, "single trunk layer"
    bp, tp, d = x_prompt.shape
    bs, ts, _ = x_sample.shape
    window, n_kv = cache_k.shape[2], cache_k.shape[3]
    kd = n_kv * AT_HD
    hg_dim = w_hg_out.shape[1]
    qd = w_at_out.shape[1]
    n_exp = w_router.shape[2]
    nslab = d // (2 * LANES)
    l = 0

    lbs = jnp.cumsum(jax.nn.softmax(hg_lower_bounds.astype(F32), axis=0), axis=0)
    win = w_in[l]
    w = {
        "w_in_hg": win[:, :4 * hg_dim].astype(BF16),
        "w_in_at": win[:, 4 * hg_dim:4 * hg_dim + qd + 2 * kd].astype(BF16),
        "w_in_gate": win[:, 4 * hg_dim + qd + 2 * kd:].astype(BF16),
        "w_hg_out": w_hg_out[l].astype(BF16),
        "w_at_out": w_at_out[l].astype(BF16),
        "w_o": w_o[l].astype(BF16),
    }
    p = {"norm1_w": norm1_w[l], "norm2_w": norm2_w[l], "hg_norm_w": hg_norm_w[l], "q_norm_w": q_norm_w[l],
         "k_norm_w": k_norm_w[l], "attn_sinks": attn_sinks[l], "n_kv": n_kv}

    c_all = jnp.concatenate([c_prompt, c_sample], axis=0)
    mod_all = _ada(c_all, w_ada[l], b_ada[l])
    mod_all = mod_all.reshape(bp + bs, 6, 1, d)
    mod_p = tuple(mod_all[:bp, j] for j in range(6))
    mod_s = tuple(mod_all[bp:, j] for j in range(6))

    pos_p = jnp.arange(tp, dtype=jnp.int32)
    pos_s = PAST_LEN + jnp.arange(ts, dtype=jnp.int32)
    s0_p = jnp.zeros((bp,) + state_hgrn.shape[2:], F32)
    n_p, n_s = bp * tp, bs * ts
    n_tok = n_p + n_s
    x1_p, h2_p, sp, kp, vp = _mixer(x_prompt, mod_p, pos_p, s0_p, None, None, lbs[l], p, w, (0, n_tok), None)
    pk = cache_k[l].reshape(bs, window, kd)
    pv = cache_v[l].reshape(bs, window, kd)
    x1_s, h2s, ss, ks, vs = _mixer(x_sample, mod_s, pos_s, state_hgrn[l], pk, pv, lbs[l], p, w, (n_p, n_tok), h2_p)

    tr = 256 if n_tok % 256 == 0 else TOK_TILE
    eidx, wts, rank, counts = _router(h2s, w_router[l], router_bias[l], n_tok, d, tr)
    shared = _shared_expert(h2s, w_sh_gate[l].astype(BF16), w_sh_up[l].astype(BF16), w_sh_down[l].astype(BF16),
                            n_tok, d, tr)

    bm = MOE_ROWS
    counts = counts[:, 0]
    pcounts = (counts + bm - 1) // bm * bm
    pend = jnp.cumsum(pcounts)
    pstart = pend - pcounts
    nb = -(-(n_tok * TOP_K) // bm) + n_exp
    slots = _slots(eidx, rank, pstart, n_tok)
    n_used = (pend[-1:] // bm).astype(jnp.int32)

    xs = _dispatch(counts, pstart, slots, h2s, n_tok, nslab, nb * bm, bm)
    visit = jnp.argsort(counts == 0, stable=True).astype(jnp.int32)
    n_visit = jnp.sum(counts > 0).astype(jnp.int32).reshape(1)
    os_ = _experts(visit, n_visit, pstart, pcounts, n_used, xs.reshape(nb * bm * nslab, LANES),
                   w_exp_gate[l], w_exp_up[l], w_exp_down[l], d, bm)
    os3 = os_.reshape(nb * bm, nslab, LANES)
    wts_t = wts.T
    y_p = _combine(slots, os3, wts_t, shared, x1_p, mod_p[5], tp, 0)
    y_s = _combine(slots, os3, wts_t, shared, x1_s, mod_s[5], ts, n_p)

    def cache_out(a, b_, t):
        return a[:, t - window:].reshape(1, b_, window, n_kv, AT_HD)

    new_k_p = cache_out(kp, bp, tp)
    new_v_p = cache_out(vp, bp, tp)
    keys_s = jnp.concatenate([pk, ks], axis=1)
    vals_s = jnp.concatenate([pv, vs], axis=1)
    new_k_s = cache_out(keys_s, bs, window + ts)
    new_v_s = cache_out(vals_s, bs, window + ts)
    return (y_p.reshape(bp, tp, d), y_s.reshape(bs, ts, d), new_k_p, new_v_p, sp[None],
            new_k_s, new_v_s, ss[None])
```

```python
import functools
import math

import numpy as np
import jax
import jax.numpy as jnp
from jax import lax
from jax.experimental import pallas as pl
from jax.experimental.pallas import tpu as pltpu

EPS = 1e-6
CHUNK = 64
HG_CHUNK = 128
HG_DK = 128
AT_HD = 64
ROPE_DIM = 16
ROPE_THETA = 500000.0
TOP_K = 8
N_GROUPS = 8
TOPK_GROUPS = 4
ROUTED_SCALE = 2.5
PAST_LEN = 2048

LANES = 128
MOE_ROWS = 256
TOK_TILE = 128
VMEM_LIMIT = 56 * 1024 * 1024

F32 = jnp.float32
BF16 = jnp.bfloat16


def _cparams(semantics, vmem=VMEM_LIMIT):
    return pltpu.CompilerParams(dimension_semantics=semantics, vmem_limit_bytes=vmem)


def _sigmoid(x):
    return 1.0 / (1.0 + jnp.exp(-x))


def _silu(x):
    return x * _sigmoid(x)


def _dot(a, b):
    return jnp.dot(a, b, preferred_element_type=F32)


def _dot_nt(a, b):
    return lax.dot_general(a, b, (((1,), (1,)), ((), ())), preferred_element_type=F32)


def _dot_tn(a, b):
    return lax.dot_general(a, b, (((0,), (0,)), ((), ())), preferred_element_type=F32)


def _split_bf16(x):
    hi = x.astype(BF16)
    lo = (x - hi.astype(F32)).astype(BF16)
    return hi, lo


def _seq_rows(m, rows):
    s, _, d = m.shape
    if s == 1:
        return m[0]
    return jnp.broadcast_to(m, (s, rows // s, d)).reshape(rows, d)


def _mod_spec(tm, seq_len, d):
    if tm <= seq_len:
        return pl.BlockSpec((1, 1, d), lambda i: ((i * tm) // seq_len, 0, 0))
    s = tm // seq_len
    return pl.BlockSpec((s, 1, d), lambda i: (i, 0, 0))


def _ada_kernel(c_ref, w_ref, b_ref, o_ref):
    s = _silu(c_ref[...]).astype(BF16)
    o_ref[...] = _dot(s, w_ref[...].astype(BF16)) + b_ref[...]


def _ada(c, w, b):
    n, d = c.shape
    m = w.shape[1]
    tn = min(m, 1024)
    return pl.pallas_call(
        _ada_kernel,
        grid=(m // tn,),
        in_specs=[pl.BlockSpec((n, d), lambda j: (0, 0)),
                  pl.BlockSpec((d, tn), lambda j: (0, j)),
                  pl.BlockSpec((1, tn), lambda j: (0, j))],
        out_specs=pl.BlockSpec((n, tn), lambda j: (0, j)),
        out_shape=jax.ShapeDtypeStruct((n, m), F32),
        compiler_params=_cparams(("parallel",)),
        name="ada_mod",
    )(c, w, b.reshape(1, m))


def _mm_kernel(x_ref, w_ref, o_ref, *, act):
    y = _dot(x_ref[...], w_ref[...])
    if act == "sigmoid":
        y = _sigmoid(y)
    o_ref[...] = y.astype(o_ref.dtype)


def _matmul(x, w, tm, tn, act=None, out_dtype=F32, name="matmul"):
    n, k = x.shape
    m = w.shape[1]
    return pl.pallas_call(
        functools.partial(_mm_kernel, act=act),
        grid=(n // tm, m // tn),
        in_specs=[pl.BlockSpec((tm, k), lambda i, j: (i, 0)),
                  pl.BlockSpec((k, tn), lambda i, j: (0, j))],
        out_specs=pl.BlockSpec((tm, tn), lambda i, j: (i, j)),
        out_shape=jax.ShapeDtypeStruct((n, m), out_dtype),
        compiler_params=_cparams(("parallel", "arbitrary")),
        name=name,
    )(x, w)


def _hgrn_tables(c):
    nlev = int(math.log2(c))
    t = np.arange(c)[:, None]
    s = np.arange(c)[None, :]
    seg = []
    for l in range(1, nlev + 1):
        b = 1 << l
        seg.append(((s >= (t // b) * b) & (s <= t)).astype(np.float32))
        seg.append(((s > t) & (s <= (t // b) * b + b - 1)).astype(np.float32))
    masks = [(t == s).astype(np.float32)]
    for l in range(nlev):
        b = 1 << l
        masks.append(((t // (2 * b) == s // (2 * b)) & (t % (2 * b) >= b) & (s % (2 * b) < b)).astype(np.float32))
    return np.concatenate(seg, axis=0), np.stack(masks, axis=0)


def _hgrn_kernel(q_ref, f_ref, i_ref, g_ref, lb_ref, nw_ref, s0_ref, seg_ref, msk_ref,
                 o_ref, sn_ref, st_ref, *, c, nchunks, hb):
    nlev = int(math.log2(c))
    tstep = pl.program_id(2)

    @pl.when(tstep == 0)
    def _():
        for hh in range(hb):
            st_ref[hh] = s0_ref[0, hh].T

    nw = nw_ref[...]
    lb = lb_ref[...]
    head = lambda a, hh: a[:, hh * HG_DK:(hh + 1) * HG_DK]

    def decays(l, g_hi, g_lo):
        seg = seg_ref[2 * (l - 1) * c:2 * l * c, :]
        e = _dot(seg, g_hi) + _dot(seg, g_lo)
        return e[:c, :], e[c:, :]

    def chunk(ci, carry):
        rows = pl.ds(pl.multiple_of(ci * c, c), c)
        q = q_ref[rows, :]
        f = lb + (1.0 - lb) * _sigmoid(f_ref[rows, :])
        g = jnp.log(f)
        k = 1.0 - f
        g_hi, g_lo = _split_bf16(g)
        qb = q.astype(BF16)
        kb = k.astype(BF16)
        scores = [jnp.where(msk_ref[0] > 0, _dot_nt(head(qb, hh), head(kb, hh)), 0.0) for hh in range(hb)]
        for l in range(nlev):
            if l == 0:
                ql, kl = (q * f).astype(BF16), kb
            else:
                wl, vl = decays(l, g_hi, g_lo)
                ql = (q * jnp.exp(wl)).astype(BF16)
                kl = (k * jnp.exp(vl)).astype(BF16)
            for hh in range(hb):
                scores[hh] = scores[hh] + jnp.where(msk_ref[l + 1] > 0, _dot_nt(head(ql, hh), head(kl, hh)), 0.0)
        a_inc, v_end = decays(nlev, g_hi, g_lo)
        qa = (q * jnp.exp(a_inc)).astype(BF16)
        k_end = (k * jnp.exp(v_end)).astype(BF16)
        vb = i_ref[rows, :].astype(BF16)
        carry_decay = jnp.exp(a_inc[c - 1:c, :])
        gate = _silu(g_ref[rows, :])
        for hh in range(hb):
            st = st_ref[hh]
            o = _dot_nt(head(qa, hh), st.astype(BF16)) + _dot(scores[hh].astype(BF16), head(vb, hh))
            st_ref[hh] = st * head(carry_decay, hh) + _dot_tn(head(vb, hh), head(k_end, hh))
            on = o * lax.rsqrt(jnp.mean(o * o, axis=-1, keepdims=True) + EPS) * nw
            o_ref[rows, hh * HG_DK:(hh + 1) * HG_DK] = (on * head(gate, hh)).astype(o_ref.dtype)
        return carry

    lax.fori_loop(0, nchunks, chunk, 0)

    @pl.when(tstep == pl.num_programs(2) - 1)
    def _():
        for hh in range(hb):
            sn_ref[0, hh] = st_ref[hh].T


def _hgrn(z_hg, lb, norm_w, s0, bsz, seq_len):
    n, w4 = z_hg.shape
    nh = w4 // (4 * HG_DK)
    hb = min(nh, 8)
    ng = nh // hb
    c = min(HG_CHUNK, seq_len)
    tb = min(seq_len, 512)
    nt = seq_len // tb
    seg, masks = _hgrn_tables(c)

    def col(part):
        return pl.BlockSpec((tb, hb * HG_DK), lambda b, h, t: (b * nt + t, part * ng + h))

    return pl.pallas_call(
        functools.partial(_hgrn_kernel, c=c, nchunks=tb // c, hb=hb),
        grid=(bsz, ng, nt),
        in_specs=[col(0), col(1), col(2), col(3),
                  pl.BlockSpec((1, hb * HG_DK), lambda b, h, t: (0, h)),
                  pl.BlockSpec((1, HG_DK), lambda b, h, t: (0, 0)),
                  pl.BlockSpec((1, hb, HG_DK, HG_DK), lambda b, h, t: (b, h, 0, 0)),
                  pl.BlockSpec(seg.shape, lambda b, h, t: (0, 0)),
                  pl.BlockSpec(masks.shape, lambda b, h, t: (0, 0, 0))],
        out_specs=[pl.BlockSpec((tb, hb * HG_DK), lambda b, h, t: (b * nt + t, h)),
                   pl.BlockSpec((1, hb, HG_DK, HG_DK), lambda b, h, t: (b, h, 0, 0))],
        out_shape=[jax.ShapeDtypeStruct((n, nh * HG_DK), BF16),
                   jax.ShapeDtypeStruct((bsz, nh, HG_DK, HG_DK), F32)],
        scratch_shapes=[pltpu.VMEM((hb, HG_DK, HG_DK), F32)],
        compiler_params=_cparams(("parallel", "parallel", "arbitrary")),
        name="hgrn2",
    )(z_hg, z_hg, z_hg, z_hg, lb.reshape(1, -1), norm_w.reshape(1, HG_DK), s0,
      jnp.asarray(seg, BF16), jnp.asarray(masks, F32))


def _rope_tables(positions):
    half = ROPE_DIM // 2
    inv_freq = ROPE_THETA ** (-jnp.arange(0, ROPE_DIM, 2, dtype=F32) / ROPE_DIM)
    ang = positions.astype(F32)[:, None] * inv_freq[None, :]
    cos, sin = jnp.cos(ang), jnp.sin(ang)
    t = positions.shape[0]
    rest = AT_HD - ROPE_DIM
    c64 = jnp.concatenate([cos, cos, jnp.ones((t, rest), F32)], axis=1)
    sa64 = jnp.concatenate([-sin, jnp.zeros((t, half + rest), F32)], axis=1)
    sb64 = jnp.concatenate([jnp.zeros((t, half), F32), sin, jnp.zeros((t, rest), F32)], axis=1)
    rep = LANES // AT_HD
    return tuple(jnp.tile(a, (1, rep)) for a in (c64, sa64, sb64))


def _attn_in_kernel(x_ref, nw_ref, sc_ref, sh_ref, w_ref, cos_ref, sa_ref, sb_ref, qw_ref, kw_ref, bd_ref,
                    h_ref, q_ref, kv_ref, *, qd, kd):
    half = ROPE_DIM // 2
    x = x_ref[...]
    tm = x.shape[0]
    y = x * lax.rsqrt(jnp.mean(x * x, axis=-1, keepdims=True) + EPS) * nw_ref[...]
    h = (y * (1.0 + _seq_rows(sc_ref[...], tm)) + _seq_rows(sh_ref[...], tm)).astype(BF16)
    h_ref[...] = h
    z = _dot(h, w_ref[...])
    cos, sa, sb = cos_ref[...], sa_ref[...], sb_ref[...]

    def norm_rope(x, w, bd, reps):
        x2 = x * x
        hi, lo = _split_bf16(x2)
        ss = _dot(hi, bd) + _dot(lo, bd)
        xn = x * lax.rsqrt(ss * (1.0 / AT_HD) + EPS) * w
        width = x.shape[1]
        tile = lambda a: jnp.concatenate([a] * reps, axis=1) if reps > 1 else a
        return (xn * tile(cos) + pltpu.roll(xn, width - half, 1) * tile(sa)
                + pltpu.roll(xn, half, 1) * tile(sb))

    q = norm_rope(z[:, :qd], qw_ref[...], bd_ref[...], qd // LANES)
    k = norm_rope(z[:, qd:qd + kd], kw_ref[...], bd_ref[:kd, :kd], kd // LANES)
    q_ref[...] = q.astype(q_ref.dtype)
    kv_ref[:, :kd] = k
    kv_ref[:, kd:] = z[:, qd + kd:]


def _attn_in(x, norm_w, sc, sh, w_at, positions, q_norm_w, k_norm_w, seq_len, qd, kd, tm):
    n, d = x.shape
    cos, sa, sb = _rope_tables(positions)
    nt = seq_len // tm
    bd = np.kron(np.eye(qd // AT_HD, dtype=np.float32), np.ones((AT_HD, AT_HD), np.float32))
    tab = pl.BlockSpec((tm, LANES), lambda i: (i % nt, 0))
    const = lambda shape: pl.BlockSpec(shape, lambda i: (0,) * len(shape), pipeline_mode=pl.Buffered(1))
    return pl.pallas_call(
        functools.partial(_attn_in_kernel, qd=qd, kd=kd),
        grid=(n // tm,),
        in_specs=[pl.BlockSpec((tm, d), lambda i: (i, 0)), const((1, d)),
                  _mod_spec(tm, seq_len, d), _mod_spec(tm, seq_len, d),
                  const(w_at.shape), tab, tab, tab, const((1, qd)), const((1, kd)), const((qd, qd))],
        out_specs=[pl.BlockSpec((tm, d), lambda i: (i, 0)),
                   pl.BlockSpec((tm, qd), lambda i: (i, 0)),
                   pl.BlockSpec((tm, 2 * kd), lambda i: (i, 0))],
        out_shape=[jax.ShapeDtypeStruct((n, d), BF16),
                   jax.ShapeDtypeStruct((n, qd), BF16),
                   jax.ShapeDtypeStruct((n, 2 * kd), F32)],
        compiler_params=_cparams(("parallel",)),
        name="attn_in",
    )(x, norm_w.reshape(1, d), sc, sh, w_at, cos, sa, sb, jnp.tile(q_norm_w, qd // AT_HD).reshape(1, qd),
      jnp.tile(k_norm_w, kd // AT_HD).reshape(1, kd), jnp.asarray(bd, BF16))


def _attend(q, keys, vals, sinks_ref, col_valid, o_ref, n_kv, group):
    n_heads = n_kv * group
    batch = 16
    for h0 in range(0, n_heads, batch):
        heads = range(h0, min(h0 + batch, n_heads))
        scores = []
        for h in heads:
            j = h // group
            s = _dot_nt(q[:, h * AT_HD:(h + 1) * AT_HD], keys[:, j * AT_HD:(j + 1) * AT_HD]) * (AT_HD ** -0.5)
            scores.append(s if col_valid is None else jnp.where(col_valid, s, -jnp.inf))
        probs = []
        for h, s in zip(heads, scores):
            sink = sinks_ref[h]
            m = jnp.maximum(jnp.max(s, axis=-1, keepdims=True), sink)
            e = jnp.exp(s - m)
            den = jnp.sum(e, axis=-1, keepdims=True) + jnp.exp(sink - m)
            probs.append((e / den).astype(BF16))
        for h, p in zip(heads, probs):
            j = h // group
            o_ref[:, h * AT_HD:(h + 1) * AT_HD] = _dot(p, vals[:, j * AT_HD:(j + 1) * AT_HD]).astype(o_ref.dtype)


def _swa_prompt_kernel(sinks_ref, q_ref, kv0_ref, kv1_ref, kv2_ref, o_ref, *, n_kv, group, w_chunks):
    kd = n_kv * AT_HD
    n = pl.program_id(1)
    blocks = [kv0_ref[...], kv1_ref[...], kv2_ref[...]]
    keys = jnp.concatenate([b[:, :kd] for b in blocks], axis=0).astype(BF16)
    vals = jnp.concatenate([b[:, kd:] for b in blocks], axis=0).astype(BF16)
    col_chunk = lax.broadcasted_iota(jnp.int32, (1, keys.shape[0]), 1) // CHUNK
    col_valid = (col_chunk + n - w_chunks) >= 0
    _attend(q_ref[...], keys, vals, sinks_ref, col_valid, o_ref, n_kv, group)


def _swa_prompt(qn, kvn, sinks, bsz, seq_len, n_kv):
    n, qd = qn.shape
    kd = n_kv * AT_HD
    nc = seq_len // CHUNK
    group = qd // AT_HD // n_kv
    w_chunks = 2

    def kv_spec(j):
        return pl.BlockSpec((CHUNK, 2 * kd), lambda b, c, s: (b * nc + jnp.maximum(c - w_chunks + j, 0), 0))

    return pl.pallas_call(
        functools.partial(_swa_prompt_kernel, n_kv=n_kv, group=group, w_chunks=w_chunks),
        grid_spec=pltpu.PrefetchScalarGridSpec(
            num_scalar_prefetch=1,
            grid=(bsz, nc),
            in_specs=[pl.BlockSpec((CHUNK, qd), lambda b, c, s: (b * nc + c, 0)),
                      kv_spec(0), kv_spec(1), kv_spec(2)],
            out_specs=pl.BlockSpec((CHUNK, qd), lambda b, c, s: (b * nc + c, 0))),
        out_shape=jax.ShapeDtypeStruct((n, qd), BF16),
        compiler_params=_cparams(("parallel", "arbitrary")),
        name="swa_prompt",
    )(sinks, qn, kvn, kvn, kvn)


def _swa_sample_kernel(sinks_ref, q_ref, pk_ref, pv_ref, kv_ref, o_ref, *, n_kv, group):
    kd = n_kv * AT_HD
    kv = kv_ref[...]
    keys = jnp.concatenate([pk_ref[0], kv[:, :kd]], axis=0).astype(BF16)
    vals = jnp.concatenate([pv_ref[0], kv[:, kd:]], axis=0).astype(BF16)
    _attend(q_ref[...], keys, vals, sinks_ref, None, o_ref, n_kv, group)


def _swa_sample(qn, kvn, past_k, past_v, sinks, bsz, seq_len, n_kv):
    n, qd = qn.shape
    kd = n_kv * AT_HD
    window = past_k.shape[1]
    group = qd // AT_HD // n_kv
    return pl.pallas_call(
        functools.partial(_swa_sample_kernel, n_kv=n_kv, group=group),
        grid_spec=pltpu.PrefetchScalarGridSpec(
            num_scalar_prefetch=1,
            grid=(bsz,),
            in_specs=[pl.BlockSpec((seq_len, qd), lambda b, s: (b, 0)),
                      pl.BlockSpec((1, window, kd), lambda b, s: (b, 0, 0)),
                      pl.BlockSpec((1, window, kd), lambda b, s: (b, 0, 0)),
                      pl.BlockSpec((seq_len, 2 * kd), lambda b, s: (b, 0))],
            out_specs=pl.BlockSpec((seq_len, qd), lambda b, s: (b, 0))),
        out_shape=jax.ShapeDtypeStruct((n, qd), BF16),
        compiler_params=_cparams(("parallel",)),
        name="swa_sample",
    )(sinks, qn, past_k, past_v, kvn)


HI_MASK = np.uint32(0xFFFF0000)


def _pack_rows(x):
    half = x.shape[1] // 2
    bits = lambda a: lax.bitcast_convert_type(a.astype(BF16).astype(F32), jnp.uint32)
    word = (bits(x[:, half:]) & HI_MASK) | (bits(x[:, :half]) >> 16)
    return lax.bitcast_convert_type(word, jnp.int32)


def _unpack_words(w):
    u = lax.bitcast_convert_type(w, jnp.uint32)
    return lax.bitcast_convert_type(u << 16, F32), lax.bitcast_convert_type(u & HI_MASK, F32)


def _store_slabs(ref, words, row0=0):
    rows, width = words.shape
    nslab = width // LANES
    for s in range(nslab):
        ref[pl.ds(row0 + s, rows, stride=nslab), :] = words[:, s * LANES:(s + 1) * LANES]


def _load_rows(ref, rows, nslab, row0=0):
    lo, hi = [], []
    for s in range(nslab):
        a, b = _unpack_words(ref[pl.ds(row0 + s, rows, stride=nslab), :])
        lo.append(a)
        hi.append(b)
    return jnp.concatenate(lo + hi, axis=1).astype(BF16)


def _merge_kernel(ohg_ref, oat_ref, ga_ref, gb_ref, x_ref, g1_ref, whg_ref, wat_ref, wo_ref,
                  nw_ref, sc_ref, sh_ref, *rest):
    x1_ref, h2_ref = rest[-2:]
    tm = x_ref.shape[0]
    merged = ga_ref[...] * _dot(ohg_ref[...], whg_ref[...]) + gb_ref[...] * _dot(oat_ref[...], wat_ref[...])
    mix = _dot(merged.astype(BF16), wo_ref[...])
    x1 = x_ref[...] + _seq_rows(g1_ref[...], tm) * mix
    x1_ref[...] = x1
    y = x1 * lax.rsqrt(jnp.mean(x1 * x1, axis=-1, keepdims=True) + EPS) * nw_ref[...]
    h2 = y * (1.0 + _seq_rows(sc_ref[...], tm)) + _seq_rows(sh_ref[...], tm)
    _store_slabs(h2_ref, _pack_rows(h2))


def _merge(o_hg, o_at, gates, x, g1, w_hg_out, w_at_out, w_o, norm2_w, sc2, sh2, seq_len, tm, moe_rows, moe_buf):
    n, d = x.shape
    hd = o_hg.shape[1]
    ad = o_at.shape[1]
    nslab = d // (2 * LANES)
    tok0, n_all = moe_rows
    tile0 = tok0 // tm
    const = lambda shape: pl.BlockSpec(shape, lambda i: (0,) * len(shape), pipeline_mode=pl.Buffered(1))
    mod = lambda: _mod_spec(tm, seq_len, d)
    in_specs = [pl.BlockSpec((tm, hd), lambda i: (i, 0)),
                pl.BlockSpec((tm, ad), lambda i: (i, 0)),
                pl.BlockSpec((tm, d), lambda i: (i, 0)),
                pl.BlockSpec((tm, d), lambda i: (i, 1)),
                pl.BlockSpec((tm, d), lambda i: (i, 0)),
                mod(), const((hd, d)), const((ad, d)), const((d, d)), const((1, d)), mod(), mod()]
    args = [o_hg, o_at, gates, gates, x, g1, w_hg_out, w_at_out, w_o, norm2_w.reshape(1, d), sc2, sh2]
    aliases = {}
    if moe_buf is not None:
        in_specs.append(pl.BlockSpec(memory_space=pl.ANY))
        args.append(moe_buf)
        aliases = {len(args) - 1: 1}
    return pl.pallas_call(
        _merge_kernel,
        grid=(n // tm,),
        in_specs=in_specs,
        out_specs=[pl.BlockSpec((tm, d), lambda i: (i, 0)),
                   pl.BlockSpec((tm * nslab, LANES), lambda i: (tile0 + i, 0))],
        out_shape=[jax.ShapeDtypeStruct((n, d), F32),
                   jax.ShapeDtypeStruct((n_all * nslab, LANES), jnp.int32)],
        input_output_aliases=aliases,
        compiler_params=_cparams(("parallel",)),
        name="merge_norm2",
    )(*args)


def _router_kernel(h_ref, wr_ref, bias_ref, tri_ref, eidx_ref, wts_ref, rank_ref, cnt_ref, run_ref,
                   *, nslab, n_exp):
    tm = h_ref.shape[0] // nslab
    gsz = n_exp // N_GROUPS
    step = pl.program_id(0)

    @pl.when(step == 0)
    def _():
        run_ref[...] = jnp.zeros_like(run_ref)

    h = _load_rows(h_ref, tm, nslab)
    scores = _sigmoid(_dot_nt(wr_ref[...], h))
    choice = scores + bias_ref[...]
    neg = -jnp.inf
    row = lax.broadcasted_iota(jnp.int32, (gsz, tm), 0)

    gscore = []
    for gi in range(N_GROUPS):
        cg = choice[gi * gsz:(gi + 1) * gsz, :]
        m1 = jnp.max(cg, axis=0, keepdims=True)
        i1 = jnp.min(jnp.where(cg == m1, row, gsz), axis=0, keepdims=True)
        m2 = jnp.max(jnp.where(row == i1, neg, cg), axis=0, keepdims=True)
        gscore.append(m1 + m2)
    gs = jnp.concatenate(gscore, axis=0)
    grow = lax.broadcasted_iota(jnp.int32, (N_GROUPS, tm), 0)
    gsel = jnp.zeros((N_GROUPS, tm), F32)
    for _ in range(TOPK_GROUPS):
        gm = jnp.max(gs, axis=0, keepdims=True)
        gi = jnp.min(jnp.where(gs == gm, grow, N_GROUPS), axis=0, keepdims=True)
        hit = grow == gi
        gsel = jnp.where(hit, 1.0, gsel)
        gs = jnp.where(hit, neg, gs)
    masked = jnp.concatenate(
        [jnp.where(gsel[gi:gi + 1, :] > 0, choice[gi * gsz:(gi + 1) * gsz, :], neg) for gi in range(N_GROUPS)],
        axis=0)

    erow = lax.broadcasted_iota(jnp.int32, (n_exp, tm), 0)
    idxs, raw = [], []
    for _ in range(TOP_K):
        m = jnp.max(masked, axis=0, keepdims=True)
        i = jnp.min(jnp.where(masked == m, erow, n_exp), axis=0, keepdims=True)
        hit = erow == i
        raw.append(jnp.sum(jnp.where(hit, scores, 0.0), axis=0, keepdims=True))
        masked = jnp.where(hit, neg, masked)
        idxs.append(i)
    total = raw[0]
    for r in raw[1:]:
        total = total + r
    onehot = jnp.zeros((n_exp, tm), F32)
    for i in idxs:
        onehot = onehot + jnp.where(erow == i, 1.0, 0.0)
    before = _dot(onehot.astype(BF16), tri_ref[...]) + run_ref[:, 0:1]
    for kk in range(TOP_K):
        eidx_ref[kk:kk + 1, :] = idxs[kk]
        wts_ref[kk:kk + 1, :] = raw[kk] / total * ROUTED_SCALE
        rank_ref[kk:kk + 1, :] = jnp.sum(jnp.where(erow == idxs[kk], before, 0.0), axis=0, keepdims=True).astype(jnp.int32)
    run_ref[...] = run_ref[...] + jnp.sum(onehot, axis=1, keepdims=True)

    @pl.when(step == pl.num_programs(0) - 1)
    def _():
        cnt_ref[...] = run_ref[...].astype(jnp.int32)


def _slot_kernel(eidx_ref, rank_ref, pstart_ref, slot_ref):
    n_exp = pstart_ref.shape[0]
    tt = TOK_TILE
    erow = lax.broadcasted_iota(jnp.int32, (n_exp, tt), 0)
    pstart = pstart_ref[:, 0:1]
    for j in range(slot_ref.shape[0]):
        lanes = slice(j * tt, (j + 1) * tt)
        for kk in range(TOP_K):
            base = jnp.sum(jnp.where(erow == eidx_ref[kk:kk + 1, lanes], pstart, 0), axis=0, keepdims=True)
            slot_ref[j, kk:kk + 1, :] = base + rank_ref[kk:kk + 1, lanes]


def _slots(eidx, rank, pstart, n_tok):
    n_exp = pstart.shape[0]
    tt = TOK_TILE
    ntile = n_tok // tt
    per_step = max(g for g in range(1, 9) if ntile % g == 0)
    tok_spec = pl.BlockSpec((TOP_K, per_step * tt), lambda i: (0, i))
    return pl.pallas_call(
        _slot_kernel,
        grid=(ntile // per_step,),
        in_specs=[tok_spec, tok_spec, pl.BlockSpec((n_exp, LANES), lambda i: (0, 0))],
        out_specs=pl.BlockSpec((per_step, TOP_K, tt), lambda i: (i, 0, 0)),
        out_shape=jax.ShapeDtypeStruct((ntile, TOP_K, tt), jnp.int32),
        compiler_params=_cparams(("parallel",)),
        name="moe_slots",
    )(eidx, rank, jnp.broadcast_to(pstart[:, None], (n_exp, LANES)))


def _router(h2s, w_router, router_bias, n_tok, d, tm):
    n_exp = w_router.shape[1]
    nslab = d // (2 * LANES)
    tri = np.triu(np.ones((tm, tm), np.float32), 1)
    out_tok = lambda dt: jax.ShapeDtypeStruct((TOP_K, n_tok), dt)
    tok_spec = pl.BlockSpec((TOP_K, tm), lambda i: (0, i))
    return pl.pallas_call(
        functools.partial(_router_kernel, nslab=nslab, n_exp=n_exp),
        grid=(n_tok // tm,),
        in_specs=[pl.BlockSpec((tm * nslab, LANES), lambda i: (i, 0)),
                  pl.BlockSpec((n_exp, d), lambda i: (0, 0)),
                  pl.BlockSpec((n_exp, 1), lambda i: (0, 0)),
                  pl.BlockSpec((tm, tm), lambda i: (0, 0))],
        out_specs=[tok_spec, tok_spec, tok_spec, pl.BlockSpec((n_exp, LANES), lambda i: (0, 0))],
        out_shape=[out_tok(jnp.int32), out_tok(F32), out_tok(jnp.int32),
                   jax.ShapeDtypeStruct((n_exp, LANES), jnp.int32)],
        scratch_shapes=[pltpu.VMEM((n_exp, LANES), F32)],
        compiler_params=_cparams(("arbitrary",)),
        name="router_topk",
    )(h2s, w_router.T.astype(BF16), router_bias.reshape(n_exp, 1), jnp.asarray(tri, BF16))


def _shared_kernel(h_ref, w1_ref, w3_ref, w2_ref, o_ref, *, nslab):
    tm = h_ref.shape[0] // nslab
    h = _load_rows(h_ref, tm, nslab)
    a = (_silu(_dot(h, w1_ref[...])) * _dot(h, w3_ref[...])).astype(BF16)
    o_ref[...] = _dot(a, w2_ref[...])


def _shared_expert(h2s, ws1, ws3, ws2, n_tok, d, tm):
    nslab = d // (2 * LANES)
    ds = ws1.shape[1]
    return pl.pallas_call(
        functools.partial(_shared_kernel, nslab=nslab),
        grid=(n_tok // tm,),
        in_specs=[pl.BlockSpec((tm * nslab, LANES), lambda i: (i, 0)),
                  pl.BlockSpec((d, ds), lambda i: (0, 0)),
                  pl.BlockSpec((d, ds), lambda i: (0, 0)),
                  pl.BlockSpec((ds, d), lambda i: (0, 0))],
        out_specs=pl.BlockSpec((tm, d), lambda i: (i, 0)),
        out_shape=jax.ShapeDtypeStruct((n_tok, d), F32),
        compiler_params=_cparams(("parallel",)),
        name="shared_expert",
    )(h2s, ws1, ws3, ws2)


def _dispatch_kernel(cnt_ref, pstart_ref, slot_hbm, h_ref, xs_hbm, slot_smem, zbuf, ssem, dsem, zsem, *, bm):
    i = pl.program_id(0)
    nslab = xs_hbm.shape[1]
    tt = h_ref.shape[0] // nslab
    n_exp = cnt_ref.shape[0]

    def slot_copy(tile, half):
        return pltpu.make_async_copy(slot_hbm.at[tile], slot_smem.at[half], ssem.at[half])

    @pl.when(i == 0)
    def _():
        slot_copy(0, 0).start()

    @pl.when(i == 0)
    def _():
        zbuf[...] = jnp.zeros_like(zbuf)

        def walk(e, start):
            cnt = cnt_ref[e]
            pad = lax.rem(bm - lax.rem(cnt, bm), bm)
            base = pstart_ref[e] + cnt
            size = bm // 2
            while size >= 1:
                take = (pad & size) != 0

                @pl.when(take)
                def _(base=base, size=size):
                    cp = pltpu.make_async_copy(zbuf.at[pl.ds(0, size)], xs_hbm.at[pl.ds(base, size)], zsem)
                    if start:
                        cp.start()
                    else:
                        cp.wait()

                base = base + jnp.where(take, size, 0)
                size //= 2

        def start_e(e, carry):
            walk(e, True)
            return carry

        def wait_e(e, carry):
            walk(e, False)
            return carry

        lax.fori_loop(0, n_exp, start_e, 0)
        lax.fori_loop(0, n_exp, wait_e, 0)

    half = lax.rem(i, 2)
    slot_copy(i, half).wait()

    @pl.when(i + 1 < pl.num_programs(0))
    def _():
        slot_copy(i + 1, 1 - half).start()

    def row_copy(t, kk):
        src = h_ref.at[pl.ds(pl.multiple_of(t * nslab, nslab), nslab), :]
        return pltpu.make_async_copy(src, xs_hbm.at[slot_smem[half, kk, t]], dsem)

    def issue(t, carry):
        for kk in range(TOP_K):
            row_copy(t, kk).start(priority=kk % 2)
        return carry

    lax.fori_loop(0, tt, issue, 0)

    def drain(t, carry):
        for kk in range(TOP_K):
            row_copy(t, kk).wait()
        return carry

    lax.fori_loop(0, tt, drain, 0)


def _dispatch(counts, pstart, slots, h2s, n_tok, nslab, n_rows, bm):
    return pl.pallas_call(
        functools.partial(_dispatch_kernel, bm=bm),
        grid_spec=pltpu.PrefetchScalarGridSpec(
            num_scalar_prefetch=2,
            grid=(n_tok // TOK_TILE,),
            in_specs=[pl.BlockSpec(memory_space=pl.ANY),
                      pl.BlockSpec((TOK_TILE * nslab, LANES), lambda i, c, p: (i, 0))],
            out_specs=pl.BlockSpec(memory_space=pl.ANY),
            scratch_shapes=[pltpu.SMEM((2, TOP_K, TOK_TILE), jnp.int32),
                            pltpu.VMEM((bm // 2, nslab, LANES), jnp.int32),
                            pltpu.SemaphoreType.DMA((2,)), pltpu.SemaphoreType.DMA, pltpu.SemaphoreType.DMA]),
        out_shape=jax.ShapeDtypeStruct((n_rows, nslab, LANES), jnp.int32),
        compiler_params=_cparams(("arbitrary",)),
        name="moe_dispatch",
    )(counts, pstart, slots, h2s)


def _experts_kernel(vis_ref, nv_ref, pstart_ref, pcnt_ref, nu_ref, xs_hbm, w1_hbm, w3_hbm, w2_hbm, os_hbm,
                    xbuf, obuf, w1s, w3s, w2s, w1b, w3b, w2b, xsem, osem, wsem, *, nslab, bm):
    j = pl.program_id(0)
    rows = bm * nslab
    n_used = nu_ref[0]
    n_visit = nv_ref[0]

    def weight_copies(jj, slot):
        e = vis_ref[jj]
        return (pltpu.make_async_copy(w1_hbm.at[e], w1s.at[slot], wsem.at[slot, 0]),
                pltpu.make_async_copy(w3_hbm.at[e], w3s.at[slot], wsem.at[slot, 1]),
                pltpu.make_async_copy(w2_hbm.at[e], w2s.at[slot], wsem.at[slot, 2]))

    def x_copy(g, slot):
        src = xs_hbm.at[pl.ds(pl.multiple_of(g * rows, rows), rows), :]
        return pltpu.make_async_copy(src, xbuf.at[pl.ds(pl.multiple_of(slot * rows, rows), rows), :], xsem.at[slot])

    def o_copy(g, slot):
        dst = os_hbm.at[pl.ds(pl.multiple_of(g * rows, rows), rows), :]
        return pltpu.make_async_copy(obuf.at[pl.ds(pl.multiple_of(slot * rows, rows), rows), :], dst, osem.at[slot])

    @pl.when(j == 0)
    def _():
        x_copy(0, 0).start()
        for cp in weight_copies(0, 0):
            cp.start(priority=1)

    @pl.when(j < n_visit)
    def _():
        e = vis_ref[j]
        g0 = pstart_ref[e] // bm
        wslot = lax.rem(j, 2)
        for cp in weight_copies(j, wslot):
            cp.wait()

        @pl.when(j + 1 < n_visit)
        def _():
            for cp in weight_copies(j + 1, 1 - wslot):
                cp.start(priority=1)

        w1b[...] = w1s[wslot].astype(BF16)
        w3b[...] = w3s[wslot].astype(BF16)
        w2b[...] = w2s[wslot].astype(BF16)

        def block(b, carry):
            g = g0 + b
            slot = lax.rem(g, 2)
            x_copy(g, slot).wait()

            @pl.when(g + 1 < n_used)
            def _():
                x_copy(g + 1, 1 - slot).start()

            @pl.when(g >= 2)
            def _():
                o_copy(g - 2, slot).wait()

            x = _load_rows(xbuf, bm, nslab, slot * rows)
            a = (_silu(_dot(x, w1b[...])) * _dot(x, w3b[...])).astype(BF16)
            _store_slabs(obuf, _pack_rows(_dot(a, w2b[...])), slot * rows)
            o_copy(g, slot).start()
            return carry

        lax.fori_loop(0, pcnt_ref[e] // bm, block, 0)

    @pl.when(j == pl.num_programs(0) - 1)
    def _():
        @pl.when(n_used >= 2)
        def _():
            o_copy(n_used - 2, lax.rem(n_used, 2)).wait()

        o_copy(n_used - 1, lax.rem(n_used - 1, 2)).wait()


def _experts(visit, n_visit, pstart, pcounts, n_used, xs2, w1, w3, w2, d, bm):
    n_exp, _, de = w1.shape
    nslab = d // (2 * LANES)
    hbm = pl.BlockSpec(memory_space=pl.ANY)
    return pl.pallas_call(
        functools.partial(_experts_kernel, nslab=nslab, bm=bm),
        grid_spec=pltpu.PrefetchScalarGridSpec(
            num_scalar_prefetch=5,
            grid=(n_exp,),
            in_specs=[hbm, hbm, hbm, hbm],
            out_specs=hbm,
            scratch_shapes=[pltpu.VMEM((2 * bm * nslab, LANES), jnp.int32),
                            pltpu.VMEM((2 * bm * nslab, LANES), jnp.int32),
                            pltpu.VMEM((2, d, de), F32), pltpu.VMEM((2, d, de), F32), pltpu.VMEM((2, de, d), F32),
                            pltpu.VMEM((d, de), BF16), pltpu.VMEM((d, de), BF16), pltpu.VMEM((de, d), BF16),
                            pltpu.SemaphoreType.DMA((2,)), pltpu.SemaphoreType.DMA((2,)),
                            pltpu.SemaphoreType.DMA((2, 3))]),
        out_shape=jax.ShapeDtypeStruct(xs2.shape, jnp.int32),
        compiler_params=_cparams(("arbitrary",)),
        name="routed_experts",
    )(visit, n_visit, pstart, pcounts, n_used, xs2, w1, w3, w2)


def _combine_kernel(slot_hbm, os_hbm, wt_ref, sh_ref, x1_ref, g2_ref, o_ref, slot0, slot1, buf0, buf1, ssem, gsem,
                    *, nslab, tile0):
    i = pl.program_id(0)
    tt = x1_ref.shape[0]
    last = pl.num_programs(0) - 1
    slots = (slot0, slot1)
    bufs = (buf0, buf1)

    def row_copy(half, t, kk, row):
        dst = bufs[half].at[pl.ds((kk * tt + t) * nslab, nslab), :]
        return pltpu.make_async_copy(os_hbm.at[row], dst, gsem.at[half])

    def slot_copy(tile, half):
        return pltpu.make_async_copy(slot_hbm.at[tile0 + tile], slots[half], ssem.at[half])

    def drain(half):
        def body(t, carry):
            for kk in range(TOP_K):
                row_copy(half, t, kk, 0).wait()
            return carry

        lax.fori_loop(0, tt, body, 0)

    @pl.when(i == 0)
    def _():
        first = slot_copy(0, 0)
        first.start()
        first.wait()

        def issue(t, carry):
            for kk in range(TOP_K):
                row_copy(0, t, kk, slot0[kk, t]).start(priority=kk % 2)
            return carry

        lax.fori_loop(0, tt, issue, 0)
        slot_copy(jnp.minimum(1, last), 1).start()

    def step(half):
        other = 1 - half
        slot_copy(jnp.minimum(i + 1, last), other).wait()
        slot_copy(jnp.minimum(i + 2, last), half).start()
        drain(half)

        wt = wt_ref[...]
        g2 = _seq_rows(g2_ref[...], tt)
        hw = nslab * LANES
        per_slab = tt // nslab
        for s in range(nslab):
            y_lo = sh_ref[:, s * LANES:(s + 1) * LANES]
            y_hi = sh_ref[:, hw + s * LANES:hw + (s + 1) * LANES]
            for kk in range(TOP_K):
                lo, hi = _unpack_words(bufs[half][pl.ds(kk * tt * nslab + s, tt, stride=nslab), :])
                y_lo = y_lo + wt[:, kk:kk + 1] * lo
                y_hi = y_hi + wt[:, kk:kk + 1] * hi
            for y, c0 in ((y_lo, s * LANES), (y_hi, hw + s * LANES)):
                cols = slice(c0, c0 + LANES)
                o_ref[:, cols] = x1_ref[:, cols] + g2[:, cols] * y
            for t in range(s * per_slab, (s + 1) * per_slab):
                for kk in range(TOP_K):
                    row_copy(other, t, kk, slots[other][kk, t]).start(priority=kk % 2)

        @pl.when(i == last)
        def _():
            slot_copy(last, half).wait()
            drain(other)

    for half in (0, 1):
        @pl.when(lax.rem(i, 2) == half)
        def _(half=half):
            step(half)


def _combine(slots, os3, wts_t, shared, x1, g2, seq_len, tok0):
    n, d = x1.shape
    nslab = d // (2 * LANES)
    tt = TOK_TILE
    tile0 = tok0 // tt
    return pl.pallas_call(
        functools.partial(_combine_kernel, nslab=nslab, tile0=tile0),
        grid=(n // tt,),
        in_specs=[pl.BlockSpec(memory_space=pl.ANY),
                  pl.BlockSpec(memory_space=pl.ANY),
                  pl.BlockSpec((tt, TOP_K), lambda i: (tile0 + i, 0)),
                  pl.BlockSpec((tt, d), lambda i: (tile0 + i, 0)),
                  pl.BlockSpec((tt, d), lambda i: (i, 0)),
                  _mod_spec(tt, seq_len, d)],
        out_specs=pl.BlockSpec((tt, d), lambda i: (i, 0)),
        out_shape=jax.ShapeDtypeStruct((n, d), F32),
        scratch_shapes=[pltpu.SMEM((TOP_K, tt), jnp.int32), pltpu.SMEM((TOP_K, tt), jnp.int32),
                        pltpu.VMEM((TOP_K * tt * nslab, LANES), jnp.int32),
                        pltpu.VMEM((TOP_K * tt * nslab, LANES), jnp.int32),
                        pltpu.SemaphoreType.DMA((2,)), pltpu.SemaphoreType.DMA((2,))],
        compiler_params=_cparams(("arbitrary",)),
        name="moe_combine",
    )(slots, os3, wts_t, shared, x1, g2)


def _mixer(x, mod, positions, s0, past_k, past_v, lb, p, w, moe_rows, moe_buf):
    bsz, seq_len, d = x.shape
    n = bsz * seq_len
    sh1, sc1, g1, sh2, sc2, g2 = mod
    tm = 256 if n % 256 == 0 else n
    x2 = x.reshape(n, d)
    n_kv = p["n_kv"]
    kd = n_kv * AT_HD
    hg_w = w["w_in_hg"].shape[1]
    qd = w["w_in_at"].shape[1] - 2 * kd
    h1, qn, kvn = _attn_in(x2, p["norm1_w"], sc1, sh1, w["w_in_at"], positions, p["q_norm_w"], p["k_norm_w"],
                           seq_len, qd, kd, min(seq_len, 512))
    tmm = 1024 if n % 1024 == 0 else tm
    z_hg = _matmul(h1, w["w_in_hg"], tmm, min(hg_w, 1024), name="w_in_hgrn")
    gates = _matmul(h1, w["w_in_gate"], tmm, min(2 * d, 1024), act="sigmoid", out_dtype=BF16, name="w_in_gates")

    o_hg, s_new = _hgrn(z_hg, lb, p["hg_norm_w"], s0, bsz, seq_len)

    if past_k is None:
        o_at = _swa_prompt(qn, kvn, p["attn_sinks"], bsz, seq_len, n_kv)
    else:
        o_at = _swa_sample(qn, kvn, past_k, past_v, p["attn_sinks"], bsz, seq_len, n_kv)

    x1, h2s = _merge(o_hg, o_at, gates, x2, g1, w["w_hg_out"], w["w_at_out"], w["w_o"],
                     p["norm2_w"], sc2, sh2, seq_len, tm, moe_rows, moe_buf)
    kv3 = kvn.reshape(bsz, seq_len, 2 * kd)
    return x1, h2s, s_new, kv3[:, :, :kd], kv3[:, :, kd:]


def kernel(x_prompt, x_sample, cache_k, cache_v, state_hgrn, c_prompt, c_sample, norm1_w, norm2_w, w_ada, b_ada,
           w_in, hg_lower_bounds, hg_norm_w, q_norm_w, k_norm_w, attn_sinks, w_hg_out, w_at_out, w_o, w_router,
           router_bias, w_exp_gate, w_exp_up, w_exp_down, w_sh_gate, w_sh_up, w_sh_down):
    depth = norm1_w.shape[0]
    assert depth == 1, "single trunk layer"
    bp, tp, d = x_prompt.shape
    bs, ts, _ = x_sample.shape
    window, n_kv = cache_k.shape[2], cache_k.shape[3]
    kd = n_kv * AT_HD
    hg_dim = w_hg_out.shape[1]
    qd = w_at_out.shape[1]
    n_exp = w_router.shape[2]
    nslab = d // (2 * LANES)
    l = 0

    lbs = jnp.cumsum(jax.nn.softmax(hg_lower_bounds.astype(F32), axis=0), axis=0)
    win = w_in[l]
    w = {
        "w_in_hg": win[:, :4 * hg_dim].astype(BF16),
        "w_in_at": win[:, 4 * hg_dim:4 * hg_dim + qd + 2 * kd].astype(BF16),
        "w_in_gate": win[:, 4 * hg_dim + qd + 2 * kd:].astype(BF16),
        "w_hg_out": w_hg_out[l].astype(BF16),
        "w_at_out": w_at_out[l].astype(BF16),
        "w_o": w_o[l].astype(BF16),
    }
    p = {"norm1_w": norm1_w[l], "norm2_w": norm2_w[l], "hg_norm_w": hg_norm_w[l], "q_norm_w": q_norm_w[l],
         "k_norm_w": k_norm_w[l], "attn_sinks": attn_sinks[l], "n_kv": n_kv}

    c_all = jnp.concatenate([c_prompt, c_sample], axis=0)
    mod_all = _ada(c_all, w_ada[l], b_ada[l])
    mod_all = mod_all.reshape(bp + bs, 6, 1, d)
    mod_p = tuple(mod_all[:bp, j] for j in range(6))
    mod_s = tuple(mod_all[bp:, j] for j in range(6))

    pos_p = jnp.arange(tp, dtype=jnp.int32)
    pos_s = PAST_LEN + jnp.arange(ts, dtype=jnp.int32)
    s0_p = jnp.zeros((bp,) + state_hgrn.shape[2:], F32)
    n_p, n_s = bp * tp, bs * ts
    n_tok = n_p + n_s
    x1_p, h2_p, sp, kp, vp = _mixer(x_prompt, mod_p, pos_p, s0_p, None, None, lbs[l], p, w, (0, n_tok), None)
    pk = cache_k[l].reshape(bs, window, kd)
    pv = cache_v[l].reshape(bs, window, kd)
    x1_s, h2s, ss, ks, vs = _mixer(x_sample, mod_s, pos_s, state_hgrn[l], pk, pv, lbs[l], p, w, (n_p, n_tok), h2_p)

    tr = 256 if n_tok % 256 == 0 else TOK_TILE
    eidx, wts, rank, counts = _router(h2s, w_router[l], router_bias[l], n_tok, d, tr)
    shared = _shared_expert(h2s, w_sh_gate[l].astype(BF16), w_sh_up[l].astype(BF16), w_sh_down[l].astype(BF16),
                            n_tok, d, tr)

    bm = MOE_ROWS
    counts = counts[:, 0]
    pcounts = (counts + bm - 1) // bm * bm
    pend = jnp.cumsum(pcounts)
    pstart = pend - pcounts
    nb = -(-(n_tok * TOP_K) // bm) + n_exp
    slots = _slots(eidx, rank, pstart, n_tok)
    n_used = (pend[-1:] // bm).astype(jnp.int32)

    xs = _dispatch(counts, pstart, slots, h2s, n_tok, nslab, nb * bm, bm)
    visit = jnp.argsort(counts == 0, stable=True).astype(jnp.int32)
    n_visit = jnp.sum(counts > 0).astype(jnp.int32).reshape(1)
    os_ = _experts(visit, n_visit, pstart, pcounts, n_used, xs.reshape(nb * bm * nslab, LANES),
                   w_exp_gate[l], w_exp_up[l], w_exp_down[l], d, bm)
    os3 = os_.reshape(nb * bm, nslab, LANES)
    wts_t = wts.T
    y_p = _combine(slots, os3, wts_t, shared, x1_p, mod_p[5], tp, 0)
    y_s = _combine(slots, os3, wts_t, shared, x1_s, mod_s[5], ts, n_p)

    def cache_out(a, b_, t):
        return a[:, t - window:].reshape(1, b_, window, n_kv, AT_HD)

    new_k_p = cache_out(kp, bp, tp)
    new_v_p = cache_out(vp, bp, tp)
    keys_s = jnp.concatenate([pk, ks], axis=1)
    vals_s = jnp.concatenate([pv, vs], axis=1)
    new_k_s = cache_out(keys_s, bs, window + ts)
    new_v_s = cache_out(vals_s, bs, window + ts)
    return (y_p.reshape(bp, tp, d), y_s.reshape(bs, ts, d), new_k_p, new_v_p, sp[None],
            new_k_s, new_v_s, ss[None])
```

```python
import functools
import math

import numpy as np
import jax
import jax.numpy as jnp
from jax import lax
from jax.experimental import pallas as pl
from jax.experimental.pallas import tpu as pltpu

EPS = 1e-6
CHUNK = 64
HG_CHUNK = 128
HG_DK = 128
AT_HD = 64
ROPE_DIM = 16
ROPE_THETA = 500000.0
TOP_K = 8
N_GROUPS = 8
TOPK_GROUPS = 4
ROUTED_SCALE = 2.5
PAST_LEN = 2048

LANES = 128
MOE_ROWS = 256
TOK_TILE = 128
VMEM_LIMIT = 56 * 1024 * 1024

F32 = jnp.float32
BF16 = jnp.bfloat16


def _cparams(semantics, vmem=VMEM_LIMIT):
    return pltpu.CompilerParams(dimension_semantics=semantics, vmem_limit_bytes=vmem)


def _sigmoid(x):
    return 1.0 / (1.0 + jnp.exp(-x))


def _silu(x):
    return x * _sigmoid(x)


def _dot(a, b):
    return jnp.dot(a, b, preferred_element_type=F32)


def _dot_nt(a, b):
    return lax.dot_general(a, b, (((1,), (1,)), ((), ())), preferred_element_type=F32)


def _dot_tn(a, b):
    return lax.dot_general(a, b, (((0,), (0,)), ((), ())), preferred_element_type=F32)


def _split_bf16(x):
    hi = x.astype(BF16)
    lo = (x - hi.astype(F32)).astype(BF16)
    return hi, lo


def _seq_rows(m, rows):
    s, _, d = m.shape
    if s == 1:
        return m[0]
    return jnp.broadcast_to(m, (s, rows // s, d)).reshape(rows, d)


def _mod_spec(tm, seq_len, d):
    if tm <= seq_len:
        return pl.BlockSpec((1, 1, d), lambda i: ((i * tm) // seq_len, 0, 0))
    s = tm // seq_len
    return pl.BlockSpec((s, 1, d), lambda i: (i, 0, 0))


def _ada_kernel(c_ref, w_ref, b_ref, o_ref):
    s = _silu(c_ref[...]).astype(BF16)
    o_ref[...] = _dot(s, w_ref[...].astype(BF16)) + b_ref[...]


def _ada(c, w, b):
    n, d = c.shape
    m = w.shape[1]
    tn = min(m, 1024)
    return pl.pallas_call(
        _ada_kernel,
        grid=(m // tn,),
        in_specs=[pl.BlockSpec((n, d), lambda j: (0, 0)),
                  pl.BlockSpec((d, tn), lambda j: (0, j)),
                  pl.BlockSpec((1, tn), lambda j: (0, j))],
        out_specs=pl.BlockSpec((n, tn), lambda j: (0, j)),
        out_shape=jax.ShapeDtypeStruct((n, m), F32),
        compiler_params=_cparams(("parallel",)),
        name="ada_mod",
    )(c, w, b.reshape(1, m))


def _mm_kernel(x_ref, w_ref, o_ref, *, act):
    y = _dot(x_ref[...], w_ref[...])
    if act == "sigmoid":
        y = _sigmoid(y)
    o_ref[...] = y.astype(o_ref.dtype)


def _matmul(x, w, tm, tn, act=None, out_dtype=F32, name="matmul"):
    n, k = x.shape
    m = w.shape[1]
    return pl.pallas_call(
        functools.partial(_mm_kernel, act=act),
        grid=(n // tm, m // tn),
        in_specs=[pl.BlockSpec((tm, k), lambda i, j: (i, 0)),
                  pl.BlockSpec((k, tn), lambda i, j: (0, j))],
        out_specs=pl.BlockSpec((tm, tn), lambda i, j: (i, j)),
        out_shape=jax.ShapeDtypeStruct((n, m), out_dtype),
        compiler_params=_cparams(("parallel", "arbitrary")),
        name=name,
    )(x, w)


def _hgrn_tables(c):
    nlev = int(math.log2(c))
    t = np.arange(c)[:, None]
    s = np.arange(c)[None, :]
    seg = []
    for l in range(1, nlev + 1):
        b = 1 << l
        seg.append(((s >= (t // b) * b) & (s <= t)).astype(np.float32))
        seg.append(((s > t) & (s <= (t // b) * b + b - 1)).astype(np.float32))
    masks = [(t == s).astype(np.float32)]
    for l in range(nlev):
        b = 1 << l
        masks.append(((t // (2 * b) == s // (2 * b)) & (t % (2 * b) >= b) & (s % (2 * b) < b)).astype(np.float32))
    return np.concatenate(seg, axis=0), np.stack(masks, axis=0)


def _hgrn_kernel(q_ref, f_ref, i_ref, g_ref, lb_ref, nw_ref, s0_ref, seg_ref, msk_ref,
                 o_ref, sn_ref, st_ref, *, c, nchunks, hb):
    nlev = int(math.log2(c))
    tstep = pl.program_id(2)

    @pl.when(tstep == 0)
    def _():
        for hh in range(hb):
            st_ref[hh] = s0_ref[0, hh].T

    nw = nw_ref[...]
    lb = lb_ref[...]
    head = lambda a, hh: a[:, hh * HG_DK:(hh + 1) * HG_DK]

    def decays(l, g_hi, g_lo):
        seg = seg_ref[2 * (l - 1) * c:2 * l * c, :]
        e = _dot(seg, g_hi) + _dot(seg, g_lo)
        return e[:c, :], e[c:, :]

    def chunk(ci, carry):
        rows = pl.ds(pl.multiple_of(ci * c, c), c)
        q = q_ref[rows, :]
        f = lb + (1.0 - lb) * _sigmoid(f_ref[rows, :])
        g = jnp.log(f)
        k = 1.0 - f
        g_hi, g_lo = _split_bf16(g)
        qb = q.astype(BF16)
        kb = k.astype(BF16)
        scores = [jnp.where(msk_ref[0] > 0, _dot_nt(head(qb, hh), head(kb, hh)), 0.0) for hh in range(hb)]
        for l in range(nlev):
            if l == 0:
                ql, kl = (q * f).astype(BF16), kb
            else:
                wl, vl = decays(l, g_hi, g_lo)
                ql = (q * jnp.exp(wl)).astype(BF16)
                kl = (k * jnp.exp(vl)).astype(BF16)
            for hh in range(hb):
                scores[hh] = scores[hh] + jnp.where(msk_ref[l + 1] > 0, _dot_nt(head(ql, hh), head(kl, hh)), 0.0)
        a_inc, v_end = decays(nlev, g_hi, g_lo)
        qa = (q * jnp.exp(a_inc)).astype(BF16)
        k_end = (k * jnp.exp(v_end)).astype(BF16)
        vb = i_ref[rows, :].astype(BF16)
        carry_decay = jnp.exp(a_inc[c - 1:c, :])
        gate = _silu(g_ref[rows, :])
        for hh in range(hb):
            st = st_ref[hh]
            o = _dot_nt(head(qa, hh), st.astype(BF16)) + _dot(scores[hh].astype(BF16), head(vb, hh))
            st_ref[hh] = st * head(carry_decay, hh) + _dot_tn(head(vb, hh), head(k_end, hh))
            on = o * lax.rsqrt(jnp.mean(o * o, axis=-1, keepdims=True) + EPS) * nw
            o_ref[rows, hh * HG_DK:(hh + 1) * HG_DK] = (on * head(gate, hh)).astype(o_ref.dtype)
        return carry

    lax.fori_loop(0, nchunks, chunk, 0)

    @pl.when(tstep == pl.num_programs(2) - 1)
    def _():
        for hh in range(hb):
            sn_ref[0, hh] = st_ref[hh].T


def _hgrn(z_hg, lb, norm_w, s0, bsz, seq_len):
    n, w4 = z_hg.shape
    nh = w4 // (4 * HG_DK)
    hb = min(nh, 8)
    ng = nh // hb
    c = min(HG_CHUNK, seq_len)
    tb = min(seq_len, 512)
    nt = seq_len // tb
    seg, masks = _hgrn_tables(c)

    def col(part):
        return pl.BlockSpec((tb, hb * HG_DK), lambda b, h, t: (b * nt + t, part * ng + h))

    return pl.pallas_call(
        functools.partial(_hgrn_kernel, c=c, nchunks=tb // c, hb=hb),
        grid=(bsz, ng, nt),
        in_specs=[col(0), col(1), col(2), col(3),
                  pl.BlockSpec((1, hb * HG_DK), lambda b, h, t: (0, h)),
                  pl.BlockSpec((1, HG_DK), lambda b, h, t: (0, 0)),
                  pl.BlockSpec((1, hb, HG_DK, HG_DK), lambda b, h, t: (b, h, 0, 0)),
                  pl.BlockSpec(seg.shape, lambda b, h, t: (0, 0)),
                  pl.BlockSpec(masks.shape, lambda b, h, t: (0, 0, 0))],
        out_specs=[pl.BlockSpec((tb, hb * HG_DK), lambda b, h, t: (b * nt + t, h)),
                   pl.BlockSpec((1, hb, HG_DK, HG_DK), lambda b, h, t: (b, h, 0, 0))],
        out_shape=[jax.ShapeDtypeStruct((n, nh * HG_DK), BF16),
                   jax.ShapeDtypeStruct((bsz, nh, HG_DK, HG_DK), F32)],
        scratch_shapes=[pltpu.VMEM((hb, HG_DK, HG_DK), F32)],
        compiler_params=_cparams(("parallel", "parallel", "arbitrary")),
        name="hgrn2",
    )(z_hg, z_hg, z_hg, z_hg, lb.reshape(1, -1), norm_w.reshape(1, HG_DK), s0,
      jnp.asarray(seg, BF16), jnp.asarray(masks, F32))


def _rope_tables(positions):
    half = ROPE_DIM // 2
    inv_freq = ROPE_THETA ** (-jnp.arange(0, ROPE_DIM, 2, dtype=F32) / ROPE_DIM)
    ang = positions.astype(F32)[:, None] * inv_freq[None, :]
    cos, sin = jnp.cos(ang), jnp.sin(ang)
    t = positions.shape[0]
    rest = AT_HD - ROPE_DIM
    c64 = jnp.concatenate([cos, cos, jnp.ones((t, rest), F32)], axis=1)
    sa64 = jnp.concatenate([-sin, jnp.zeros((t, half + rest), F32)], axis=1)
    sb64 = jnp.concatenate([jnp.zeros((t, half), F32), sin, jnp.zeros((t, rest), F32)], axis=1)
    rep = LANES // AT_HD
    return tuple(jnp.tile(a, (1, rep)) for a in (c64, sa64, sb64))


def _attn_in_kernel(x_ref, nw_ref, sc_ref, sh_ref, w_ref, cos_ref, sa_ref, sb_ref, qw_ref, kw_ref, bd_ref,
                    h_ref, q_ref, kv_ref, *, qd, kd):
    half = ROPE_DIM // 2
    x = x_ref[...]
    tm = x.shape[0]
    y = x * lax.rsqrt(jnp.mean(x * x, axis=-1, keepdims=True) + EPS) * nw_ref[...]
    h = (y * (1.0 + _seq_rows(sc_ref[...], tm)) + _seq_rows(sh_ref[...], tm)).astype(BF16)
    h_ref[...] = h
    z = _dot(h, w_ref[...])
    cos, sa, sb = cos_ref[...], sa_ref[...], sb_ref[...]

    def norm_rope(x, w, bd, reps):
        x2 = x * x
        hi, lo = _split_bf16(x2)
        ss = _dot(hi, bd) + _dot(lo, bd)
        xn = x * lax.rsqrt(ss * (1.0 / AT_HD) + EPS) * w
        width = x.shape[1]
        tile = lambda a: jnp.concatenate([a] * reps, axis=1) if reps > 1 else a
        return (xn * tile(cos) + pltpu.roll(xn, width - half, 1) * tile(sa)
                + pltpu.roll(xn, half, 1) * tile(sb))

    q = norm_rope(z[:, :qd], qw_ref[...], bd_ref[...], qd // LANES)
    k = norm_rope(z[:, qd:qd + kd], kw_ref[...], bd_ref[:kd, :kd], kd // LANES)
    q_ref[...] = q.astype(q_ref.dtype)
    kv_ref[:, :kd] = k
    kv_ref[:, kd:] = z[:, qd + kd:]


def _attn_in(x, norm_w, sc, sh, w_at, positions, q_norm_w, k_norm_w, seq_len, qd, kd, tm):
    n, d = x.shape
    cos, sa, sb = _rope_tables(positions)
    nt = seq_len // tm
    bd = np.kron(np.eye(qd // AT_HD, dtype=np.float32), np.ones((AT_HD, AT_HD), np.float32))
    tab = pl.BlockSpec((tm, LANES), lambda i: (i % nt, 0))
    const = lambda shape: pl.BlockSpec(shape, lambda i: (0,) * len(shape), pipeline_mode=pl.Buffered(1))
    return pl.pallas_call(
        functools.partial(_attn_in_kernel, qd=qd, kd=kd),
        grid=(n // tm,),
        in_specs=[pl.BlockSpec((tm, d), lambda i: (i, 0)), const((1, d)),
                  _mod_spec(tm, seq_len, d), _mod_spec(tm, seq_len, d),
                  const(w_at.shape), tab, tab, tab, const((1, qd)), const((1, kd)), const((qd, qd))],
        out_specs=[pl.BlockSpec((tm, d), lambda i: (i, 0)),
                   pl.BlockSpec((tm, qd), lambda i: (i, 0)),
                   pl.BlockSpec((tm, 2 * kd), lambda i: (i, 0))],
        out_shape=[jax.ShapeDtypeStruct((n, d), BF16),
                   jax.ShapeDtypeStruct((n, qd), BF16),
                   jax.ShapeDtypeStruct((n, 2 * kd), F32)],
        compiler_params=_cparams(("parallel",)),
        name="attn_in",
    )(x, norm_w.reshape(1, d), sc, sh, w_at, cos, sa, sb, jnp.tile(q_norm_w, qd // AT_HD).reshape(1, qd),
      jnp.tile(k_norm_w, kd // AT_HD).reshape(1, kd), jnp.asarray(bd, BF16))


def _attend(q, keys, vals, sinks_ref, col_valid, o_ref, n_kv, group):
    n_heads = n_kv * group
    batch = 16
    for h0 in range(0, n_heads, batch):
        heads = range(h0, min(h0 + batch, n_heads))
        scores = []
        for h in heads:
            j = h // group
            s = _dot_nt(q[:, h * AT_HD:(h + 1) * AT_HD], keys[:, j * AT_HD:(j + 1) * AT_HD]) * (AT_HD ** -0.5)
            scores.append(s if col_valid is None else jnp.where(col_valid, s, -jnp.inf))
        probs = []
        for h, s in zip(heads, scores):
            sink = sinks_ref[h]
            m = jnp.maximum(jnp.max(s, axis=-1, keepdims=True), sink)
            e = jnp.exp(s - m)
            den = jnp.sum(e, axis=-1, keepdims=True) + jnp.exp(sink - m)
            probs.append((e / den).astype(BF16))
        for h, p in zip(heads, probs):
            j = h // group
            o_ref[:, h * AT_HD:(h + 1) * AT_HD] = _dot(p, vals[:, j * AT_HD:(j + 1) * AT_HD]).astype(o_ref.dtype)


def _swa_prompt_kernel(sinks_ref, q_ref, kv0_ref, kv1_ref, kv2_ref, o_ref, *, n_kv, group, w_chunks):
    kd = n_kv * AT_HD
    n = pl.program_id(1)
    blocks = [kv0_ref[...], kv1_ref[...], kv2_ref[...]]
    keys = jnp.concatenate([b[:, :kd] for b in blocks], axis=0).astype(BF16)
    vals = jnp.concatenate([b[:, kd:] for b in blocks], axis=0).astype(BF16)
    col_chunk = lax.broadcasted_iota(jnp.int32, (1, keys.shape[0]), 1) // CHUNK
    col_valid = (col_chunk + n - w_chunks) >= 0
    _attend(q_ref[...], keys, vals, sinks_ref, col_valid, o_ref, n_kv, group)


def _swa_prompt(qn, kvn, sinks, bsz, seq_len, n_kv):
    n, qd = qn.shape
    kd = n_kv * AT_HD
    nc = seq_len // CHUNK
    group = qd // AT_HD // n_kv
    w_chunks = 2

    def kv_spec(j):
        return pl.BlockSpec((CHUNK, 2 * kd), lambda b, c, s: (b * nc + jnp.maximum(c - w_chunks + j, 0), 0))

    return pl.pallas_call(
        functools.partial(_swa_prompt_kernel, n_kv=n_kv, group=group, w_chunks=w_chunks),
        grid_spec=pltpu.PrefetchScalarGridSpec(
            num_scalar_prefetch=1,
            grid=(bsz, nc),
            in_specs=[pl.BlockSpec((CHUNK, qd), lambda b, c, s: (b * nc + c, 0)),
                      kv_spec(0), kv_spec(1), kv_spec(2)],
            out_specs=pl.BlockSpec((CHUNK, qd), lambda b, c, s: (b * nc + c, 0))),
        out_shape=jax.ShapeDtypeStruct((n, qd), BF16),
        compiler_params=_cparams(("parallel", "arbitrary")),
        name="swa_prompt",
    )(sinks, qn, kvn, kvn, kvn)


def _swa_sample_kernel(sinks_ref, q_ref, pk_ref, pv_ref, kv_ref, o_ref, *, n_kv, group):
    kd = n_kv * AT_HD
    kv = kv_ref[...]
    keys = jnp.concatenate([pk_ref[0], kv[:, :kd]], axis=0).astype(BF16)
    vals = jnp.concatenate([pv_ref[0], kv[:, kd:]], axis=0).astype(BF16)
    _attend(q_ref[...], keys, vals, sinks_ref, None, o_ref, n_kv, group)


def _swa_sample(qn, kvn, past_k, past_v, sinks, bsz, seq_len, n_kv):
    n, qd = qn.shape
    kd = n_kv * AT_HD
    window = past_k.shape[1]
    group = qd // AT_HD // n_kv
    return pl.pallas_call(
        functools.partial(_swa_sample_kernel, n_kv=n_kv, group=group),
        grid_spec=pltpu.PrefetchScalarGridSpec(
            num_scalar_prefetch=1,
            grid=(bsz,),
            in_specs=[pl.BlockSpec((seq_len, qd), lambda b, s: (b, 0)),
                      pl.BlockSpec((1, window, kd), lambda b, s: (b, 0, 0)),
                      pl.BlockSpec((1, window, kd), lambda b, s: (b, 0, 0)),
                      pl.BlockSpec((seq_len, 2 * kd), lambda b, s: (b, 0))],
            out_specs=pl.BlockSpec((seq_len, qd), lambda b, s: (b, 0))),
        out_shape=jax.ShapeDtypeStruct((n, qd), BF16),
        compiler_params=_cparams(("parallel",)),
        name="swa_sample",
    )(sinks, qn, past_k, past_v, kvn)


HI_MASK = np.uint32(0xFFFF0000)


def _pack_rows(x):
    half = x.shape[1] // 2
    bits = lambda a: lax.bitcast_convert_type(a.astype(BF16).astype(F32), jnp.uint32)
    word = (bits(x[:, half:]) & HI_MASK) | (bits(x[:, :half]) >> 16)
    return lax.bitcast_convert_type(word, jnp.int32)


def _unpack_words(w):
    u = lax.bitcast_convert_type(w, jnp.uint32)
    return lax.bitcast_convert_type(u << 16, F32), lax.bitcast_convert_type(u & HI_MASK, F32)


def _store_slabs(ref, words, row0=0):
    rows, width = words.shape
    nslab = width // LANES
    for s in range(nslab):
        ref[pl.ds(row0 + s, rows, stride=nslab), :] = words[:, s * LANES:(s + 1) * LANES]


def _load_rows(ref, rows, nslab, row0=0):
    lo, hi = [], []
    for s in range(nslab):
        a, b = _unpack_words(ref[pl.ds(row0 + s, rows, stride=nslab), :])
        lo.append(a)
        hi.append(b)
    return jnp.concatenate(lo + hi, axis=1).astype(BF16)


def _merge_kernel(ohg_ref, oat_ref, ga_ref, gb_ref, x_ref, g1_ref, whg_ref, wat_ref, wo_ref,
                  nw_ref, sc_ref, sh_ref, *rest):
    x1_ref, h2_ref = rest[-2:]
    tm = x_ref.shape[0]
    merged = ga_ref[...] * _dot(ohg_ref[...], whg_ref[...]) + gb_ref[...] * _dot(oat_ref[...], wat_ref[...])
    mix = _dot(merged.astype(BF16), wo_ref[...])
    x1 = x_ref[...] + _seq_rows(g1_ref[...], tm) * mix
    x1_ref[...] = x1
    y = x1 * lax.rsqrt(jnp.mean(x1 * x1, axis=-1, keepdims=True) + EPS) * nw_ref[...]
    h2 = y * (1.0 + _seq_rows(sc_ref[...], tm)) + _seq_rows(sh_ref[...], tm)
    _store_slabs(h2_ref, _pack_rows(h2))


def _merge(o_hg, o_at, gates, x, g1, w_hg_out, w_at_out, w_o, norm2_w, sc2, sh2, seq_len, tm, moe_rows, moe_buf):
    n, d = x.shape
    hd = o_hg.shape[1]
    ad = o_at.shape[1]
    nslab = d // (2 * LANES)
    tok0, n_all = moe_rows
    tile0 = tok0 // tm
    const = lambda shape: pl.BlockSpec(shape, lambda i: (0,) * len(shape), pipeline_mode=pl.Buffered(1))
    mod = lambda: _mod_spec(tm, seq_len, d)
    in_specs = [pl.BlockSpec((tm, hd), lambda i: (i, 0)),
                pl.BlockSpec((tm, ad), lambda i: (i, 0)),
                pl.BlockSpec((tm, d), lambda i: (i, 0)),
                pl.BlockSpec((tm, d), lambda i: (i, 1)),
                pl.BlockSpec((tm, d), lambda i: (i, 0)),
                mod(), const((hd, d)), const((ad, d)), const((d, d)), const((1, d)), mod(), mod()]
    args = [o_hg, o_at, gates, gates, x, g1, w_hg_out, w_at_out, w_o, norm2_w.reshape(1, d), sc2, sh2]
    aliases = {}
    if moe_buf is not None:
        in_specs.append(pl.BlockSpec(memory_space=pl.ANY))
        args.append(moe_buf)
        aliases = {len(args) - 1: 1}
    return pl.pallas_call(
        _merge_kernel,
        grid=(n // tm,),
        in_specs=in_specs,
        out_specs=[pl.BlockSpec((tm, d), lambda i: (i, 0)),
                   pl.BlockSpec((tm * nslab, LANES), lambda i: (tile0 + i, 0))],
        out_shape=[jax.ShapeDtypeStruct((n, d), F32),
                   jax.ShapeDtypeStruct((n_all * nslab, LANES), jnp.int32)],
        input_output_aliases=aliases,
        compiler_params=_cparams(("parallel",)),
        name="merge_norm2",
    )(*args)


def _router_kernel(h_ref, wr_ref, bias_ref, tri_ref, eidx_ref, wts_ref, rank_ref, cnt_ref, run_ref,
                   *, nslab, n_exp):
    tm = h_ref.shape[0] // nslab
    gsz = n_exp // N_GROUPS
    step = pl.program_id(0)

    @pl.when(step == 0)
    def _():
        run_ref[...] = jnp.zeros_like(run_ref)

    h = _load_rows(h_ref, tm, nslab)
    scores = _sigmoid(_dot_nt(wr_ref[...], h))
    choice = scores + bias_ref[...]
    neg = -jnp.inf
    row = lax.broadcasted_iota(jnp.int32, (gsz, tm), 0)

    gscore = []
    for gi in range(N_GROUPS):
        cg = choice[gi * gsz:(gi + 1) * gsz, :]
        m1 = jnp.max(cg, axis=0, keepdims=True)
        i1 = jnp.min(jnp.where(cg == m1, row, gsz), axis=0, keepdims=True)
        m2 = jnp.max(jnp.where(row == i1, neg, cg), axis=0, keepdims=True)
        gscore.append(m1 + m2)
    gs = jnp.concatenate(gscore, axis=0)
    grow = lax.broadcasted_iota(jnp.int32, (N_GROUPS, tm), 0)
    gsel = jnp.zeros((N_GROUPS, tm), F32)
    for _ in range(TOPK_GROUPS):
        gm = jnp.max(gs, axis=0, keepdims=True)
        gi = jnp.min(jnp.where(gs == gm, grow, N_GROUPS), axis=0, keepdims=True)
        hit = grow == gi
        gsel = jnp.where(hit, 1.0, gsel)
        gs = jnp.where(hit, neg, gs)
    masked = jnp.concatenate(
        [jnp.where(gsel[gi:gi + 1, :] > 0, choice[gi * gsz:(gi + 1) * gsz, :], neg) for gi in range(N_GROUPS)],
        axis=0)

    erow = lax.broadcasted_iota(jnp.int32, (n_exp, tm), 0)
    idxs, raw = [], []
    for _ in range(TOP_K):
        m = jnp.max(masked, axis=0, keepdims=True)
        i = jnp.min(jnp.where(masked == m, erow, n_exp), axis=0, keepdims=True)
        hit = erow == i
        raw.append(jnp.sum(jnp.where(hit, scores, 0.0), axis=0, keepdims=True))
        masked = jnp.where(hit, neg, masked)
        idxs.append(i)
    total = raw[0]
    for r in raw[1:]:
        total = total + r
    onehot = jnp.zeros((n_exp, tm), F32)
    for i in idxs:
        onehot = onehot + jnp.where(erow == i, 1.0, 0.0)
    before = _dot(onehot.astype(BF16), tri_ref[...]) + run_ref[:, 0:1]
    for kk in range(TOP_K):
        eidx_ref[kk:kk + 1, :] = idxs[kk]
        wts_ref[kk:kk + 1, :] = raw[kk] / total * ROUTED_SCALE
        rank_ref[kk:kk + 1, :] = jnp.sum(jnp.where(erow == idxs[kk], before, 0.0), axis=0, keepdims=True).astype(jnp.int32)
    run_ref[...] = run_ref[...] + jnp.sum(onehot, axis=1, keepdims=True)

    @pl.when(step == pl.num_programs(0) - 1)
    def _():
        cnt_ref[...] = run_ref[...].astype(jnp.int32)


def _slot_kernel(eidx_ref, rank_ref, pstart_ref, slot_ref):
    n_exp = pstart_ref.shape[0]
    tt = TOK_TILE
    erow = lax.broadcasted_iota(jnp.int32, (n_exp, tt), 0)
    pstart = pstart_ref[:, 0:1]
    for j in range(slot_ref.shape[0]):
        lanes = slice(j * tt, (j + 1) * tt)
        for kk in range(TOP_K):
            base = jnp.sum(jnp.where(erow == eidx_ref[kk:kk + 1, lanes], pstart, 0), axis=0, keepdims=True)
            slot_ref[j, kk:kk + 1, :] = base + rank_ref[kk:kk + 1, lanes]


def _slots(eidx, rank, pstart, n_tok):
    n_exp = pstart.shape[0]
    tt = TOK_TILE
    ntile = n_tok // tt
    per_step = max(g for g in range(1, 9) if ntile % g == 0)
    tok_spec = pl.BlockSpec((TOP_K, per_step * tt), lambda i: (0, i))
    return pl.pallas_call(
        _slot_kernel,
        grid=(ntile // per_step,),
        in_specs=[tok_spec, tok_spec, pl.BlockSpec((n_exp, LANES), lambda i: (0, 0))],
        out_specs=pl.BlockSpec((per_step, TOP_K, tt), lambda i: (i, 0, 0)),
        out_shape=jax.ShapeDtypeStruct((ntile, TOP_K, tt), jnp.int32),
        compiler_params=_cparams(("parallel",)),
        name="moe_slots",
    )(eidx, rank, jnp.broadcast_to(pstart[:, None], (n_exp, LANES)))


def _router(h2s, w_router, router_bias, n_tok, d, tm):
    n_exp = w_router.shape[1]
    nslab = d // (2 * LANES)
    tri = np.triu(np.ones((tm, tm), np.float32), 1)
    out_tok = lambda dt: jax.ShapeDtypeStruct((TOP_K, n_tok), dt)
    tok_spec = pl.BlockSpec((TOP_K, tm), lambda i: (0, i))
    return pl.pallas_call(
        functools.partial(_router_kernel, nslab=nslab, n_exp=n_exp),
        grid=(n_tok // tm,),
        in_specs=[pl.BlockSpec((tm * nslab, LANES), lambda i: (i, 0)),
                  pl.BlockSpec((n_exp, d), lambda i: (0, 0)),
                  pl.BlockSpec((n_exp, 1), lambda i: (0, 0)),
                  pl.BlockSpec((tm, tm), lambda i: (0, 0))],
        out_specs=[tok_spec, tok_spec, tok_spec, pl.BlockSpec((n_exp, LANES), lambda i: (0, 0))],
        out_shape=[out_tok(jnp.int32), out_tok(F32), out_tok(jnp.int32),
                   jax.ShapeDtypeStruct((n_exp, LANES), jnp.int32)],
        scratch_shapes=[pltpu.VMEM((n_exp, LANES), F32)],
        compiler_params=_cparams(("arbitrary",)),
        name="router_topk",
    )(h2s, w_router.T.astype(BF16), router_bias.reshape(n_exp, 1), jnp.asarray(tri, BF16))


def _dispatch_kernel(cnt_ref, pstart_ref, slot_hbm, h_ref, xs_hbm, slot_smem, zbuf, ssem, dsem, zsem, *, bm):
    i = pl.program_id(0)
    nslab = xs_hbm.shape[1]
    tt = h_ref.shape[0] // nslab
    n_exp = cnt_ref.shape[0]

    def slot_copy(tile, half):
        return pltpu.make_async_copy(slot_hbm.at[tile], slot_smem.at[half], ssem.at[half])

    @pl.when(i == 0)
    def _():
        slot_copy(0, 0).start()

    @pl.when(i == 0)
    def _():
        zbuf[...] = jnp.zeros_like(zbuf)

        def walk(e, start):
            cnt = cnt_ref[e]
            pad = lax.rem(bm - lax.rem(cnt, bm), bm)
            base = pstart_ref[e] + cnt
            size = bm // 2
            while size >= 1:
                take = (pad & size) != 0

                @pl.when(take)
                def _(base=base, size=size):
                    cp = pltpu.make_async_copy(zbuf.at[pl.ds(0, size)], xs_hbm.at[pl.ds(base, size)], zsem)
                    if start:
                        cp.start()
                    else:
                        cp.wait()

                base = base + jnp.where(take, size, 0)
                size //= 2

        def start_e(e, carry):
            walk(e, True)
            return carry

        def wait_e(e, carry):
            walk(e, False)
            return carry

        lax.fori_loop(0, n_exp, start_e, 0)
        lax.fori_loop(0, n_exp, wait_e, 0)

    half = lax.rem(i, 2)
    slot_copy(i, half).wait()

    @pl.when(i + 1 < pl.num_programs(0))
    def _():
        slot_copy(i + 1, 1 - half).start()

    def row_copy(t, kk):
        src = h_ref.at[pl.ds(pl.multiple_of(t * nslab, nslab), nslab), :]
        return pltpu.make_async_copy(src, xs_hbm.at[slot_smem[half, kk, t]], dsem)

    def issue(t, carry):
        for kk in range(TOP_K):
            row_copy(t, kk).start(priority=kk % 2)
        return carry

    lax.fori_loop(0, tt, issue, 0)

    def drain(t, carry):
        for kk in range(TOP_K):
            row_copy(t, kk).wait()
        return carry

    lax.fori_loop(0, tt, drain, 0)


def _dispatch(counts, pstart, slots, h2s, n_tok, nslab, n_rows, bm):
    return pl.pallas_call(
        functools.partial(_dispatch_kernel, bm=bm),
        grid_spec=pltpu.PrefetchScalarGridSpec(
            num_scalar_prefetch=2,
            grid=(n_tok // TOK_TILE,),
            in_specs=[pl.BlockSpec(memory_space=pl.ANY),
                      pl.BlockSpec((TOK_TILE * nslab, LANES), lambda i, c, p: (i, 0))],
            out_specs=pl.BlockSpec(memory_space=pl.ANY),
            scratch_shapes=[pltpu.SMEM((2, TOP_K, TOK_TILE), jnp.int32),
                            pltpu.VMEM((bm // 2, nslab, LANES), jnp.int32),
                            pltpu.SemaphoreType.DMA((2,)), pltpu.SemaphoreType.DMA, pltpu.SemaphoreType.DMA]),
        out_shape=jax.ShapeDtypeStruct((n_rows, nslab, LANES), jnp.int32),
        compiler_params=_cparams(("arbitrary",)),
        name="moe_dispatch",
    )(counts, pstart, slots, h2s)


def _experts_kernel(vis_ref, nv_ref, pstart_ref, pcnt_ref, nu_ref, xs_hbm, w1_hbm, w3_hbm, w2_hbm, os_hbm,
                    xbuf, obuf, w1s, w3s, w2s, w1b, w3b, w2b, xsem, osem, wsem, *, nslab, bm):
    j = pl.program_id(0)
    rows = bm * nslab
    n_used = nu_ref[0]
    n_visit = nv_ref[0]

    def weight_copies(jj, slot):
        e = vis_ref[jj]
        return (pltpu.make_async_copy(w1_hbm.at[e], w1s.at[slot], wsem.at[slot, 0]),
                pltpu.make_async_copy(w3_hbm.at[e], w3s.at[slot], wsem.at[slot, 1]),
                pltpu.make_async_copy(w2_hbm.at[e], w2s.at[slot], wsem.at[slot, 2]))

    def x_copy(g, slot):
        src = xs_hbm.at[pl.ds(pl.multiple_of(g * rows, rows), rows), :]
        return pltpu.make_async_copy(src, xbuf.at[pl.ds(pl.multiple_of(slot * rows, rows), rows), :], xsem.at[slot])

    def o_copy(g, slot):
        dst = os_hbm.at[pl.ds(pl.multiple_of(g * rows, rows), rows), :]
        return pltpu.make_async_copy(obuf.at[pl.ds(pl.multiple_of(slot * rows, rows), rows), :], dst, osem.at[slot])

    @pl.when(j == 0)
    def _():
        x_copy(0, 0).start()
        for cp in weight_copies(0, 0):
            cp.start(priority=1)

    @pl.when(j < n_visit)
    def _():
        e = vis_ref[j]
        g0 = pstart_ref[e] // bm
        wslot = lax.rem(j, 2)
        for cp in weight_copies(j, wslot):
            cp.wait()

        @pl.when(j + 1 < n_visit)
        def _():
            for cp in weight_copies(j + 1, 1 - wslot):
                cp.start(priority=1)

        w1b[...] = w1s[wslot].astype(BF16)
        w3b[...] = w3s[wslot].astype(BF16)
        w2b[...] = w2s[wslot].astype(BF16)

        def block(b, carry):
            g = g0 + b
            slot = lax.rem(g, 2)
            x_copy(g, slot).wait()

            @pl.when(g + 1 < n_used)
            def _():
                x_copy(g + 1, 1 - slot).start()

            @pl.when(g >= 2)
            def _():
                o_copy(g - 2, slot).wait()

            x = _load_rows(xbuf, bm, nslab, slot * rows)
            a = (_silu(_dot(x, w1b[...])) * _dot(x, w3b[...])).astype(BF16)
            _store_slabs(obuf, _pack_rows(_dot(a, w2b[...])), slot * rows)
            o_copy(g, slot).start()
            return carry

        lax.fori_loop(0, pcnt_ref[e] // bm, block, 0)

    @pl.when(j == pl.num_programs(0) - 1)
    def _():
        @pl.when(n_used >= 2)
        def _():
            o_copy(n_used - 2, lax.rem(n_used, 2)).wait()

        o_copy(n_used - 1, lax.rem(n_used - 1, 2)).wait()


def _experts(visit, n_visit, pstart, pcounts, n_used, xs2, w1, w3, w2, d, bm):
    n_exp, _, de = w1.shape
    nslab = d // (2 * LANES)
    hbm = pl.BlockSpec(memory_space=pl.ANY)
    return pl.pallas_call(
        functools.partial(_experts_kernel, nslab=nslab, bm=bm),
        grid_spec=pltpu.PrefetchScalarGridSpec(
            num_scalar_prefetch=5,
            grid=(n_exp,),
            in_specs=[hbm, hbm, hbm, hbm],
            out_specs=hbm,
            scratch_shapes=[pltpu.VMEM((2 * bm * nslab, LANES), jnp.int32),
                            pltpu.VMEM((2 * bm * nslab, LANES), jnp.int32),
                            pltpu.VMEM((2, d, de), F32), pltpu.VMEM((2, d, de), F32), pltpu.VMEM((2, de, d), F32),
                            pltpu.VMEM((d, de), BF16), pltpu.VMEM((d, de), BF16), pltpu.VMEM((de, d), BF16),
                            pltpu.SemaphoreType.DMA((2,)), pltpu.SemaphoreType.DMA((2,)),
                            pltpu.SemaphoreType.DMA((2, 3))]),
        out_shape=jax.ShapeDtypeStruct(xs2.shape, jnp.int32),
        compiler_params=_cparams(("arbitrary",)),
        name="routed_experts",
    )(visit, n_visit, pstart, pcounts, n_used, xs2, w1, w3, w2)


def _combine_kernel(slot_hbm, os_hbm, wt_ref, h_ref, ws1_ref, ws3_ref, ws2_ref, x1_ref, g2_ref, o_ref,
                    slot0, slot1, buf0, buf1, sh_ref, ssem, gsem, *, nslab, tile0):
    i = pl.program_id(0)
    tt = x1_ref.shape[0]
    last = pl.num_programs(0) - 1
    slots = (slot0, slot1)
    bufs = (buf0, buf1)

    def row_copy(half, t, kk, row):
        dst = bufs[half].at[pl.ds((kk * tt + t) * nslab, nslab), :]
        return pltpu.make_async_copy(os_hbm.at[row], dst, gsem.at[half])

    def slot_copy(tile, half):
        return pltpu.make_async_copy(slot_hbm.at[tile0 + tile], slots[half], ssem.at[half])

    def drain(half):
        def body(t, carry):
            for kk in range(TOP_K):
                row_copy(half, t, kk, 0).wait()
            return carry

        lax.fori_loop(0, tt, body, 0)

    @pl.when(i == 0)
    def _():
        first = slot_copy(0, 0)
        first.start()
        first.wait()

        def issue(t, carry):
            for kk in range(TOP_K):
                row_copy(0, t, kk, slot0[kk, t]).start(priority=kk % 2)
            return carry

        lax.fori_loop(0, tt, issue, 0)
        slot_copy(jnp.minimum(1, last), 1).start()

    def step(half):
        other = 1 - half
        slot_copy(jnp.minimum(i + 1, last), other).wait()
        slot_copy(jnp.minimum(i + 2, last), half).start()
        drain(half)

        h = _load_rows(h_ref, tt, nslab)
        a = (_silu(_dot(h, ws1_ref[...])) * _dot(h, ws3_ref[...])).astype(BF16)
        sh_ref[...] = _dot(a, ws2_ref[...])

        wt = wt_ref[...]
        g2 = _seq_rows(g2_ref[...], tt)
        hw = nslab * LANES
        per_slab = tt // nslab
        for s in range(nslab):
            y_lo = sh_ref[:, s * LANES:(s + 1) * LANES]
            y_hi = sh_ref[:, hw + s * LANES:hw + (s + 1) * LANES]
            for kk in range(TOP_K):
                lo, hi = _unpack_words(bufs[half][pl.ds(kk * tt * nslab + s, tt, stride=nslab), :])
                y_lo = y_lo + wt[:, kk:kk + 1] * lo
                y_hi = y_hi + wt[:, kk:kk + 1] * hi
            for y, c0 in ((y_lo, s * LANES), (y_hi, hw + s * LANES)):
                cols = slice(c0, c0 + LANES)
                o_ref[:, cols] = x1_ref[:, cols] + g2[:, cols] * y
            for t in range(s * per_slab, (s + 1) * per_slab):
                for kk in range(TOP_K):
                    row_copy(other, t, kk, slots[other][kk, t]).start(priority=kk % 2)

        @pl.when(i == last)
        def _():
            slot_copy(last, half).wait()
            drain(other)

    for half in (0, 1):
        @pl.when(lax.rem(i, 2) == half)
        def _(half=half):
            step(half)


def _combine(slots, os3, wts_t, h2s, ws1, ws3, ws2, x1, g2, seq_len, tok0):
    n, d = x1.shape
    nslab = d // (2 * LANES)
    tt = TOK_TILE
    tile0 = tok0 // tt
    const = lambda a: pl.BlockSpec(a.shape, lambda i: (0, 0), pipeline_mode=pl.Buffered(1))
    return pl.pallas_call(
        functools.partial(_combine_kernel, nslab=nslab, tile0=tile0),
        grid=(n // tt,),
        in_specs=[pl.BlockSpec(memory_space=pl.ANY),
                  pl.BlockSpec(memory_space=pl.ANY),
                  pl.BlockSpec((tt, TOP_K), lambda i: (tile0 + i, 0)),
                  pl.BlockSpec((tt * nslab, LANES), lambda i: (tile0 + i, 0)),
                  const(ws1), const(ws3), const(ws2),
                  pl.BlockSpec((tt, d), lambda i: (i, 0)),
                  _mod_spec(tt, seq_len, d)],
        out_specs=pl.BlockSpec((tt, d), lambda i: (i, 0)),
        out_shape=jax.ShapeDtypeStruct((n, d), F32),
        scratch_shapes=[pltpu.SMEM((TOP_K, tt), jnp.int32), pltpu.SMEM((TOP_K, tt), jnp.int32),
                        pltpu.VMEM((TOP_K * tt * nslab, LANES), jnp.int32),
                        pltpu.VMEM((TOP_K * tt * nslab, LANES), jnp.int32),
                        pltpu.VMEM((tt, d), F32),
                        pltpu.SemaphoreType.DMA((2,)), pltpu.SemaphoreType.DMA((2,))],
        compiler_params=_cparams(("arbitrary",)),
        name="moe_combine",
    )(slots, os3, wts_t, h2s, ws1, ws3, ws2, x1, g2)


def _mixer(x, mod, positions, s0, past_k, past_v, lb, p, w, moe_rows, moe_buf):
    bsz, seq_len, d = x.shape
    n = bsz * seq_len
    sh1, sc1, g1, sh2, sc2, g2 = mod
    tm = 256 if n % 256 == 0 else n
    x2 = x.reshape(n, d)
    n_kv = p["n_kv"]
    kd = n_kv * AT_HD
    hg_w = w["w_in_hg"].shape[1]
    qd = w["w_in_at"].shape[1] - 2 * kd
    h1, qn, kvn = _attn_in(x2, p["norm1_w"], sc1, sh1, w["w_in_at"], positions, p["q_norm_w"], p["k_norm_w"],
                           seq_len, qd, kd, min(seq_len, 512))
    tmm = 1024 if n % 1024 == 0 else tm
    z_hg = _matmul(h1, w["w_in_hg"], tmm, min(hg_w, 1024), name="w_in_hgrn")
    gates = _matmul(h1, w["w_in_gate"], tmm, min(2 * d, 1024), act="sigmoid", out_dtype=BF16, name="w_in_gates")

    o_hg, s_new = _hgrn(z_hg, lb, p["hg_norm_w"], s0, bsz, seq_len)

    if past_k is None:
        o_at = _swa_prompt(qn, kvn, p["attn_sinks"], bsz, seq_len, n_kv)
    else:
        o_at = _swa_sample(qn, kvn, past_k, past_v, p["attn_sinks"], bsz, seq_len, n_kv)

    x1, h2s = _merge(o_hg, o_at, gates, x2, g1, w["w_hg_out"], w["w_at_out"], w["w_o"],
                     p["norm2_w"], sc2, sh2, seq_len, tm, moe_rows, moe_buf)
    kv3 = kvn.reshape(bsz, seq_len, 2 * kd)
    return x1, h2s, s_new, kv3[:, :, :kd], kv3[:, :, kd:]


def kernel(x_prompt, x_sample, cache_k, cache_v, state_hgrn, c_prompt, c_sample, norm1_w, norm2_w, w_ada, b_ada,
           w_in, hg_lower_bounds, hg_norm_w, q_norm_w, k_norm_w, attn_sinks, w_hg_out, w_at_out, w_o, w_router,
           router_bias, w_exp_gate, w_exp_up, w_exp_down, w_sh_gate, w_sh_up, w_sh_down):
    depth = norm1_w.shape[0]
    assert depth == 1, "single trunk layer"
    bp, tp, d = x_prompt.shape
    bs, ts, _ = x_sample.shape
    window, n_kv = cache_k.shape[2], cache_k.shape[3]
    kd = n_kv * AT_HD
    hg_dim = w_hg_out.shape[1]
    qd = w_at_out.shape[1]
    n_exp = w_router.shape[2]
    nslab = d // (2 * LANES)
    l = 0

    lbs = jnp.cumsum(jax.nn.softmax(hg_lower_bounds.astype(F32), axis=0), axis=0)
    win = w_in[l]
    w = {
        "w_in_hg": win[:, :4 * hg_dim].astype(BF16),
        "w_in_at": win[:, 4 * hg_dim:4 * hg_dim + qd + 2 * kd].astype(BF16),
        "w_in_gate": win[:, 4 * hg_dim + qd + 2 * kd:].astype(BF16),
        "w_hg_out": w_hg_out[l].astype(BF16),
        "w_at_out": w_at_out[l].astype(BF16),
        "w_o": w_o[l].astype(BF16),
    }
    p = {"norm1_w": norm1_w[l], "norm2_w": norm2_w[l], "hg_norm_w": hg_norm_w[l], "q_norm_w": q_norm_w[l],
         "k_norm_w": k_norm_w[l], "attn_sinks": attn_sinks[l], "n_kv": n_kv}

    c_all = jnp.concatenate([c_prompt, c_sample], axis=0)
    mod_all = _ada(c_all, w_ada[l], b_ada[l])
    mod_all = mod_all.reshape(bp + bs, 6, 1, d)
    mod_p = tuple(mod_all[:bp, j] for j in range(6))
    mod_s = tuple(mod_all[bp:, j] for j in range(6))

    pos_p = jnp.arange(tp, dtype=jnp.int32)
    pos_s = PAST_LEN + jnp.arange(ts, dtype=jnp.int32)
    s0_p = jnp.zeros((bp,) + state_hgrn.shape[2:], F32)
    n_p, n_s = bp * tp, bs * ts
    n_tok = n_p + n_s
    x1_p, h2_p, sp, kp, vp = _mixer(x_prompt, mod_p, pos_p, s0_p, None, None, lbs[l], p, w, (0, n_tok), None)
    pk = cache_k[l].reshape(bs, window, kd)
    pv = cache_v[l].reshape(bs, window, kd)
    x1_s, h2s, ss, ks, vs = _mixer(x_sample, mod_s, pos_s, state_hgrn[l], pk, pv, lbs[l], p, w, (n_p, n_tok), h2_p)

    tr = 256 if n_tok % 256 == 0 else TOK_TILE
    eidx, wts, rank, counts = _router(h2s, w_router[l], router_bias[l], n_tok, d, tr)

    bm = MOE_ROWS
    counts = counts[:, 0]
    pcounts = (counts + bm - 1) // bm * bm
    pend = jnp.cumsum(pcounts)
    pstart = pend - pcounts
    nb = -(-(n_tok * TOP_K) // bm) + n_exp
    slots = _slots(eidx, rank, pstart, n_tok)
    n_used = (pend[-1:] // bm).astype(jnp.int32)

    xs = _dispatch(counts, pstart, slots, h2s, n_tok, nslab, nb * bm, bm)
    visit = jnp.argsort(counts == 0, stable=True).astype(jnp.int32)
    n_visit = jnp.sum(counts > 0).astype(jnp.int32).reshape(1)
    os_ = _experts(visit, n_visit, pstart, pcounts, n_used, xs.reshape(nb * bm * nslab, LANES),
                   w_exp_gate[l], w_exp_up[l], w_exp_down[l], d, bm)
    os3 = os_.reshape(nb * bm, nslab, LANES)
    wts_t = wts.T
    ws = (w_sh_gate[l].astype(BF16), w_sh_up[l].astype(BF16), w_sh_down[l].astype(BF16))
    y_p = _combine(slots, os3, wts_t, h2s, *ws, x1_p, mod_p[5], tp, 0)
    y_s = _combine(slots, os3, wts_t, h2s, *ws, x1_s, mod_s[5], ts, n_p)

    def cache_out(a, b_, t):
        return a[:, t - window:].reshape(1, b_, window, n_kv, AT_HD)

    new_k_p = cache_out(kp, bp, tp)
    new_v_p = cache_out(vp, bp, tp)
    keys_s = jnp.concatenate([pk, ks], axis=1)
    vals_s = jnp.concatenate([pv, vs], axis=1)
    new_k_s = cache_out(keys_s, bs, window + ts)
    new_v_s = cache_out(vals_s, bs, window + ts)
    return (y_p.reshape(bp, tp, d), y_s.reshape(bs, ts, d), new_k_p, new_v_p, sp[None],
            new_k_s, new_v_s, ss[None])
```

```python
import functools
import math

import numpy as np
import jax
import jax.numpy as jnp
from jax import lax
from jax.experimental import pallas as pl
from jax.experimental.pallas import tpu as pltpu

EPS = 1e-6
CHUNK = 64
HG_CHUNK = 128
HG_DK = 128
AT_HD = 64
ROPE_DIM = 16
ROPE_THETA = 500000.0
TOP_K = 8
N_GROUPS = 8
TOPK_GROUPS = 4
ROUTED_SCALE = 2.5
PAST_LEN = 2048

LANES = 128
MOE_ROWS = 256
TOK_TILE = 128
VMEM_LIMIT = 56 * 1024 * 1024

F32 = jnp.float32
BF16 = jnp.bfloat16


def _cparams(semantics, vmem=VMEM_LIMIT):
    return pltpu.CompilerParams(dimension_semantics=semantics, vmem_limit_bytes=vmem)


def _sigmoid(x):
    return 1.0 / (1.0 + jnp.exp(-x))


def _silu(x):
    return x * _sigmoid(x)


def _dot(a, b):
    return jnp.dot(a, b, preferred_element_type=F32)


def _dot_nt(a, b):
    return lax.dot_general(a, b, (((1,), (1,)), ((), ())), preferred_element_type=F32)


def _dot_tn(a, b):
    return lax.dot_general(a, b, (((0,), (0,)), ((), ())), preferred_element_type=F32)


def _split_bf16(x):
    hi = x.astype(BF16)
    lo = (x - hi.astype(F32)).astype(BF16)
    return hi, lo


def _seq_rows(m, rows):
    s, _, d = m.shape
    if s == 1:
        return m[0]
    return jnp.broadcast_to(m, (s, rows // s, d)).reshape(rows, d)


def _mod_spec(tm, seq_len, d):
    if tm <= seq_len:
        return pl.BlockSpec((1, 1, d), lambda i: ((i * tm) // seq_len, 0, 0))
    s = tm // seq_len
    return pl.BlockSpec((s, 1, d), lambda i: (i, 0, 0))


def _ada_kernel(c_ref, w_ref, b_ref, o_ref):
    s = _silu(c_ref[...]).astype(BF16)
    o_ref[...] = _dot(s, w_ref[...].astype(BF16)) + b_ref[...]


def _ada(c, w, b):
    n, d = c.shape
    m = w.shape[1]
    tn = min(m, 1024)
    return pl.pallas_call(
        _ada_kernel,
        grid=(m // tn,),
        in_specs=[pl.BlockSpec((n, d), lambda j: (0, 0)),
                  pl.BlockSpec((d, tn), lambda j: (0, j)),
                  pl.BlockSpec((1, tn), lambda j: (0, j))],
        out_specs=pl.BlockSpec((n, tn), lambda j: (0, j)),
        out_shape=jax.ShapeDtypeStruct((n, m), F32),
        compiler_params=_cparams(("parallel",)),
        name="ada_mod",
    )(c, w, b.reshape(1, m))


def _mm_kernel(x_ref, w_ref, o_ref, *, act):
    y = _dot(x_ref[...], w_ref[...])
    if act == "sigmoid":
        y = _sigmoid(y)
    o_ref[...] = y.astype(o_ref.dtype)


def _matmul(x, w, tm, tn, act=None, out_dtype=F32, name="matmul"):
    n, k = x.shape
    m = w.shape[1]
    return pl.pallas_call(
        functools.partial(_mm_kernel, act=act),
        grid=(n // tm, m // tn),
        in_specs=[pl.BlockSpec((tm, k), lambda i, j: (i, 0)),
                  pl.BlockSpec((k, tn), lambda i, j: (0, j))],
        out_specs=pl.BlockSpec((tm, tn), lambda i, j: (i, j)),
        out_shape=jax.ShapeDtypeStruct((n, m), out_dtype),
        compiler_params=_cparams(("parallel", "arbitrary")),
        name=name,
    )(x, w)


def _hgrn_tables(c):
    nlev = int(math.log2(c))
    t = np.arange(c)[:, None]
    s = np.arange(c)[None, :]
    seg = []
    for l in range(1, nlev + 1):
        b = 1 << l
        seg.append(((s >= (t // b) * b) & (s <= t)).astype(np.float32))
        seg.append(((s > t) & (s <= (t // b) * b + b - 1)).astype(np.float32))
    masks = [(t == s).astype(np.float32)]
    for l in range(nlev):
        b = 1 << l
        masks.append(((t // (2 * b) == s // (2 * b)) & (t % (2 * b) >= b) & (s % (2 * b) < b)).astype(np.float32))
    return np.concatenate(seg, axis=0), np.stack(masks, axis=0)


def _hgrn_kernel(q_ref, f_ref, i_ref, g_ref, lb_ref, nw_ref, s0_ref, seg_ref, msk_ref,
                 o_ref, sn_ref, st_ref, *, c, nchunks, hb):
    nlev = int(math.log2(c))
    tstep = pl.program_id(2)

    @pl.when(tstep == 0)
    def _():
        for hh in range(hb):
            st_ref[hh] = s0_ref[0, hh].T

    nw = nw_ref[...]
    lb = lb_ref[...]
    head = lambda a, hh: a[:, hh * HG_DK:(hh + 1) * HG_DK]

    def decays(l, g_hi, g_lo):
        seg = seg_ref[2 * (l - 1) * c:2 * l * c, :]
        e = _dot(seg, g_hi) + _dot(seg, g_lo)
        return e[:c, :], e[c:, :]

    def block_decays(a_inc, b):
        width = a_inc.shape[1]
        fill = lambda r: jnp.broadcast_to(a_inc[r:r + 1, :], (b, width))
        before = jnp.concatenate([jnp.zeros((b, width), F32)] + [fill(j * b - 1) for j in range(1, c // b)], axis=0)
        last = jnp.concatenate([fill((j + 1) * b - 1) for j in range(c // b)], axis=0)
        return a_inc - before, last - a_inc

    def chunk(ci, carry):
        rows = pl.ds(pl.multiple_of(ci * c, c), c)
        q = q_ref[rows, :]
        f = lb + (1.0 - lb) * _sigmoid(f_ref[rows, :])
        g = jnp.log(f)
        k = 1.0 - f
        g_hi, g_lo = _split_bf16(g)
        qb = q.astype(BF16)
        kb = k.astype(BF16)
        a_inc, v_end = decays(nlev, g_hi, g_lo)
        scores = [jnp.where(msk_ref[0] > 0, _dot_nt(head(qb, hh), head(kb, hh)), 0.0) for hh in range(hb)]
        for l in range(nlev):
            if l == 0:
                ql, kl = (q * f).astype(BF16), kb
            else:
                wl, vl = block_decays(a_inc, 1 << l) if (1 << l) % 8 == 0 else decays(l, g_hi, g_lo)
                ql = (q * jnp.exp(wl)).astype(BF16)
                kl = (k * jnp.exp(vl)).astype(BF16)
            for hh in range(hb):
                scores[hh] = scores[hh] + jnp.where(msk_ref[l + 1] > 0, _dot_nt(head(ql, hh), head(kl, hh)), 0.0)
        qa = (q * jnp.exp(a_inc)).astype(BF16)
        k_end = (k * jnp.exp(v_end)).astype(BF16)
        vb = i_ref[rows, :].astype(BF16)
        carry_decay = jnp.exp(a_inc[c - 1:c, :])
        gate = _silu(g_ref[rows, :])
        for hh in range(hb):
            st = st_ref[hh]
            o = _dot_nt(head(qa, hh), st.astype(BF16)) + _dot(scores[hh].astype(BF16), head(vb, hh))
            st_ref[hh] = st * head(carry_decay, hh) + _dot_tn(head(vb, hh), head(k_end, hh))
            on = o * lax.rsqrt(jnp.mean(o * o, axis=-1, keepdims=True) + EPS) * nw
            o_ref[rows, hh * HG_DK:(hh + 1) * HG_DK] = (on * head(gate, hh)).astype(o_ref.dtype)
        return carry

    lax.fori_loop(0, nchunks, chunk, 0)

    @pl.when(tstep == pl.num_programs(2) - 1)
    def _():
        for hh in range(hb):
            sn_ref[0, hh] = st_ref[hh].T


def _hgrn(z_hg, lb, norm_w, s0, bsz, seq_len):
    n, w4 = z_hg.shape
    nh = w4 // (4 * HG_DK)
    hb = min(nh, 8)
    ng = nh // hb
    c = min(HG_CHUNK, seq_len)
    tb = min(seq_len, 512)
    nt = seq_len // tb
    seg, masks = _hgrn_tables(c)

    def col(part):
        return pl.BlockSpec((tb, hb * HG_DK), lambda b, h, t: (b * nt + t, part * ng + h))

    return pl.pallas_call(
        functools.partial(_hgrn_kernel, c=c, nchunks=tb // c, hb=hb),
        grid=(bsz, ng, nt),
        in_specs=[col(0), col(1), col(2), col(3),
                  pl.BlockSpec((1, hb * HG_DK), lambda b, h, t: (0, h)),
                  pl.BlockSpec((1, HG_DK), lambda b, h, t: (0, 0)),
                  pl.BlockSpec((1, hb, HG_DK, HG_DK), lambda b, h, t: (b, h, 0, 0)),
                  pl.BlockSpec(seg.shape, lambda b, h, t: (0, 0)),
                  pl.BlockSpec(masks.shape, lambda b, h, t: (0, 0, 0))],
        out_specs=[pl.BlockSpec((tb, hb * HG_DK), lambda b, h, t: (b * nt + t, h)),
                   pl.BlockSpec((1, hb, HG_DK, HG_DK), lambda b, h, t: (b, h, 0, 0))],
        out_shape=[jax.ShapeDtypeStruct((n, nh * HG_DK), BF16),
                   jax.ShapeDtypeStruct((bsz, nh, HG_DK, HG_DK), F32)],
        scratch_shapes=[pltpu.VMEM((hb, HG_DK, HG_DK), F32)],
        compiler_params=_cparams(("parallel", "parallel", "arbitrary")),
        name="hgrn2",
    )(z_hg, z_hg, z_hg, z_hg, lb.reshape(1, -1), norm_w.reshape(1, HG_DK), s0,
      jnp.asarray(seg, BF16), jnp.asarray(masks, F32))


def _rope_tables(positions):
    half = ROPE_DIM // 2
    inv_freq = ROPE_THETA ** (-jnp.arange(0, ROPE_DIM, 2, dtype=F32) / ROPE_DIM)
    ang = positions.astype(F32)[:, None] * inv_freq[None, :]
    cos, sin = jnp.cos(ang), jnp.sin(ang)
    t = positions.shape[0]
    rest = AT_HD - ROPE_DIM
    c64 = jnp.concatenate([cos, cos, jnp.ones((t, rest), F32)], axis=1)
    sa64 = jnp.concatenate([-sin, jnp.zeros((t, half + rest), F32)], axis=1)
    sb64 = jnp.concatenate([jnp.zeros((t, half), F32), sin, jnp.zeros((t, rest), F32)], axis=1)
    rep = LANES // AT_HD
    return tuple(jnp.tile(a, (1, rep)) for a in (c64, sa64, sb64))


def _attn_in_kernel(x_ref, nw_ref, sc_ref, sh_ref, w_ref, cos_ref, sa_ref, sb_ref, qw_ref, kw_ref, bd_ref,
                    h_ref, q_ref, kv_ref, *, qd, kd):
    half = ROPE_DIM // 2
    x = x_ref[...]
    tm = x.shape[0]
    y = x * lax.rsqrt(jnp.mean(x * x, axis=-1, keepdims=True) + EPS) * nw_ref[...]
    h = (y * (1.0 + _seq_rows(sc_ref[...], tm)) + _seq_rows(sh_ref[...], tm)).astype(BF16)
    h_ref[...] = h
    z = _dot(h, w_ref[...])
    cos, sa, sb = cos_ref[...], sa_ref[...], sb_ref[...]

    def norm_rope(x, w, bd, reps):
        x2 = x * x
        hi, lo = _split_bf16(x2)
        ss = _dot(hi, bd) + _dot(lo, bd)
        xn = x * lax.rsqrt(ss * (1.0 / AT_HD) + EPS) * w
        width = x.shape[1]
        tile = lambda a: jnp.concatenate([a] * reps, axis=1) if reps > 1 else a
        return (xn * tile(cos) + pltpu.roll(xn, width - half, 1) * tile(sa)
                + pltpu.roll(xn, half, 1) * tile(sb))

    q = norm_rope(z[:, :qd], qw_ref[...], bd_ref[...], qd // LANES)
    k = norm_rope(z[:, qd:qd + kd], kw_ref[...], bd_ref[:kd, :kd], kd // LANES)
    q_ref[...] = q.astype(q_ref.dtype)
    kv_ref[:, :kd] = k
    kv_ref[:, kd:] = z[:, qd + kd:]


def _attn_in(x, norm_w, sc, sh, w_at, positions, q_norm_w, k_norm_w, seq_len, qd, kd, tm):
    n, d = x.shape
    cos, sa, sb = _rope_tables(positions)
    nt = seq_len // tm
    bd = np.kron(np.eye(qd // AT_HD, dtype=np.float32), np.ones((AT_HD, AT_HD), np.float32))
    tab = pl.BlockSpec((tm, LANES), lambda i: (i % nt, 0))
    const = lambda shape: pl.BlockSpec(shape, lambda i: (0,) * len(shape), pipeline_mode=pl.Buffered(1))
    return pl.pallas_call(
        functools.partial(_attn_in_kernel, qd=qd, kd=kd),
        grid=(n // tm,),
        in_specs=[pl.BlockSpec((tm, d), lambda i: (i, 0)), const((1, d)),
                  _mod_spec(tm, seq_len, d), _mod_spec(tm, seq_len, d),
                  const(w_at.shape), tab, tab, tab, const((1, qd)), const((1, kd)), const((qd, qd))],
        out_specs=[pl.BlockSpec((tm, d), lambda i: (i, 0)),
                   pl.BlockSpec((tm, qd), lambda i: (i, 0)),
                   pl.BlockSpec((tm, 2 * kd), lambda i: (i, 0))],
        out_shape=[jax.ShapeDtypeStruct((n, d), BF16),
                   jax.ShapeDtypeStruct((n, qd), BF16),
                   jax.ShapeDtypeStruct((n, 2 * kd), F32)],
        compiler_params=_cparams(("parallel",)),
        name="attn_in",
    )(x, norm_w.reshape(1, d), sc, sh, w_at, cos, sa, sb, jnp.tile(q_norm_w, qd // AT_HD).reshape(1, qd),
      jnp.tile(k_norm_w, kd // AT_HD).reshape(1, kd), jnp.asarray(bd, BF16))


def _attend(q, keys, vals, sinks_ref, col_valid, o_ref, n_kv, group):
    n_heads = n_kv * group
    batch = 16
    for h0 in range(0, n_heads, batch):
        heads = range(h0, min(h0 + batch, n_heads))
        scores = []
        for h in heads:
            j = h // group
            s = _dot_nt(q[:, h * AT_HD:(h + 1) * AT_HD], keys[:, j * AT_HD:(j + 1) * AT_HD]) * (AT_HD ** -0.5)
            scores.append(s if col_valid is None else jnp.where(col_valid, s, -jnp.inf))
        probs = []
        for h, s in zip(heads, scores):
            sink = sinks_ref[h]
            m = jnp.maximum(jnp.max(s, axis=-1, keepdims=True), sink)
            e = jnp.exp(s - m)
            den = jnp.sum(e, axis=-1, keepdims=True) + jnp.exp(sink - m)
            probs.append((e / den).astype(BF16))
        for h, p in zip(heads, probs):
            j = h // group
            o_ref[:, h * AT_HD:(h + 1) * AT_HD] = _dot(p, vals[:, j * AT_HD:(j + 1) * AT_HD]).astype(o_ref.dtype)


def _swa_prompt_kernel(sinks_ref, q_ref, kv0_ref, kv1_ref, kv2_ref, o_ref, *, n_kv, group, w_chunks):
    kd = n_kv * AT_HD
    n = pl.program_id(1)
    blocks = [kv0_ref[...], kv1_ref[...], kv2_ref[...]]
    keys = jnp.concatenate([b[:, :kd] for b in blocks], axis=0).astype(BF16)
    vals = jnp.concatenate([b[:, kd:] for b in blocks], axis=0).astype(BF16)
    col_chunk = lax.broadcasted_iota(jnp.int32, (1, keys.shape[0]), 1) // CHUNK
    col_valid = (col_chunk + n - w_chunks) >= 0
    _attend(q_ref[...], keys, vals, sinks_ref, col_valid, o_ref, n_kv, group)


def _swa_prompt(qn, kvn, sinks, bsz, seq_len, n_kv):
    n, qd = qn.shape
    kd = n_kv * AT_HD
    nc = seq_len // CHUNK
    group = qd // AT_HD // n_kv
    w_chunks = 2

    def kv_spec(j):
        return pl.BlockSpec((CHUNK, 2 * kd), lambda b, c, s: (b * nc + jnp.maximum(c - w_chunks + j, 0), 0))

    return pl.pallas_call(
        functools.partial(_swa_prompt_kernel, n_kv=n_kv, group=group, w_chunks=w_chunks),
        grid_spec=pltpu.PrefetchScalarGridSpec(
            num_scalar_prefetch=1,
            grid=(bsz, nc),
            in_specs=[pl.BlockSpec((CHUNK, qd), lambda b, c, s: (b * nc + c, 0)),
                      kv_spec(0), kv_spec(1), kv_spec(2)],
            out_specs=pl.BlockSpec((CHUNK, qd), lambda b, c, s: (b * nc + c, 0))),
        out_shape=jax.ShapeDtypeStruct((n, qd), BF16),
        compiler_params=_cparams(("parallel", "arbitrary")),
        name="swa_prompt",
    )(sinks, qn, kvn, kvn, kvn)


def _swa_sample_kernel(sinks_ref, q_ref, pk_ref, pv_ref, kv_ref, o_ref, *, n_kv, group):
    kd = n_kv * AT_HD
    kv = kv_ref[...]
    keys = jnp.concatenate([pk_ref[0], kv[:, :kd]], axis=0).astype(BF16)
    vals = jnp.concatenate([pv_ref[0], kv[:, kd:]], axis=0).astype(BF16)
    _attend(q_ref[...], keys, vals, sinks_ref, None, o_ref, n_kv, group)


def _swa_sample(qn, kvn, past_k, past_v, sinks, bsz, seq_len, n_kv):
    n, qd = qn.shape
    kd = n_kv * AT_HD
    window = past_k.shape[1]
    group = qd // AT_HD // n_kv
    return pl.pallas_call(
        functools.partial(_swa_sample_kernel, n_kv=n_kv, group=group),
        grid_spec=pltpu.PrefetchScalarGridSpec(
            num_scalar_prefetch=1,
            grid=(bsz,),
            in_specs=[pl.BlockSpec((seq_len, qd), lambda b, s: (b, 0)),
                      pl.BlockSpec((1, window, kd), lambda b, s: (b, 0, 0)),
                      pl.BlockSpec((1, window, kd), lambda b, s: (b, 0, 0)),
                      pl.BlockSpec((seq_len, 2 * kd), lambda b, s: (b, 0))],
            out_specs=pl.BlockSpec((seq_len, qd), lambda b, s: (b, 0))),
        out_shape=jax.ShapeDtypeStruct((n, qd), BF16),
        compiler_params=_cparams(("parallel",)),
        name="swa_sample",
    )(sinks, qn, past_k, past_v, kvn)


HI_MASK = np.uint32(0xFFFF0000)


def _pack_rows(x):
    half = x.shape[1] // 2
    bits = lambda a: lax.bitcast_convert_type(a.astype(BF16).astype(F32), jnp.uint32)
    word = (bits(x[:, half:]) & HI_MASK) | (bits(x[:, :half]) >> 16)
    return lax.bitcast_convert_type(word, jnp.int32)


def _unpack_words(w):
    u = lax.bitcast_convert_type(w, jnp.uint32)
    return lax.bitcast_convert_type(u << 16, F32), lax.bitcast_convert_type(u & HI_MASK, F32)


def _store_slabs(ref, words, row0=0):
    rows, width = words.shape
    nslab = width // LANES
    for s in range(nslab):
        ref[pl.ds(row0 + s, rows, stride=nslab), :] = words[:, s * LANES:(s + 1) * LANES]


def _load_rows(ref, rows, nslab, row0=0):
    lo, hi = [], []
    for s in range(nslab):
        a, b = _unpack_words(ref[pl.ds(row0 + s, rows, stride=nslab), :])
        lo.append(a)
        hi.append(b)
    return jnp.concatenate(lo + hi, axis=1).astype(BF16)


def _merge_kernel(ohg_ref, oat_ref, ga_ref, gb_ref, x_ref, g1_ref, whg_ref, wat_ref, wo_ref,
                  nw_ref, sc_ref, sh_ref, *rest):
    x1_ref, h2_ref = rest[-2:]
    tm = x_ref.shape[0]
    merged = ga_ref[...] * _dot(ohg_ref[...], whg_ref[...]) + gb_ref[...] * _dot(oat_ref[...], wat_ref[...])
    mix = _dot(merged.astype(BF16), wo_ref[...])
    x1 = x_ref[...] + _seq_rows(g1_ref[...], tm) * mix
    x1_ref[...] = x1
    y = x1 * lax.rsqrt(jnp.mean(x1 * x1, axis=-1, keepdims=True) + EPS) * nw_ref[...]
    h2 = y * (1.0 + _seq_rows(sc_ref[...], tm)) + _seq_rows(sh_ref[...], tm)
    _store_slabs(h2_ref, _pack_rows(h2))


def _merge(o_hg, o_at, gates, x, g1, w_hg_out, w_at_out, w_o, norm2_w, sc2, sh2, seq_len, tm, moe_rows, moe_buf):
    n, d = x.shape
    hd = o_hg.shape[1]
    ad = o_at.shape[1]
    nslab = d // (2 * LANES)
    tok0, n_all = moe_rows
    tile0 = tok0 // tm
    const = lambda shape: pl.BlockSpec(shape, lambda i: (0,) * len(shape), pipeline_mode=pl.Buffered(1))
    mod = lambda: _mod_spec(tm, seq_len, d)
    in_specs = [pl.BlockSpec((tm, hd), lambda i: (i, 0)),
                pl.BlockSpec((tm, ad), lambda i: (i, 0)),
                pl.BlockSpec((tm, d), lambda i: (i, 0)),
                pl.BlockSpec((tm, d), lambda i: (i, 1)),
                pl.BlockSpec((tm, d), lambda i: (i, 0)),
                mod(), const((hd, d)), const((ad, d)), const((d, d)), const((1, d)), mod(), mod()]
    args = [o_hg, o_at, gates, gates, x, g1, w_hg_out, w_at_out, w_o, norm2_w.reshape(1, d), sc2, sh2]
    aliases = {}
    if moe_buf is not None:
        in_specs.append(pl.BlockSpec(memory_space=pl.ANY))
        args.append(moe_buf)
        aliases = {len(args) - 1: 1}
    return pl.pallas_call(
        _merge_kernel,
        grid=(n // tm,),
        in_specs=in_specs,
        out_specs=[pl.BlockSpec((tm, d), lambda i: (i, 0)),
                   pl.BlockSpec((tm * nslab, LANES), lambda i: (tile0 + i, 0))],
        out_shape=[jax.ShapeDtypeStruct((n, d), F32),
                   jax.ShapeDtypeStruct((n_all * nslab, LANES), jnp.int32)],
        input_output_aliases=aliases,
        compiler_params=_cparams(("parallel",)),
        name="merge_norm2",
    )(*args)


def _router_kernel(h_ref, wr_ref, bias_ref, tri_ref, eidx_ref, wts_ref, rank_ref, cnt_ref, run_ref,
                   *, nslab, n_exp):
    tm = h_ref.shape[0] // nslab
    gsz = n_exp // N_GROUPS
    step = pl.program_id(0)

    @pl.when(step == 0)
    def _():
        run_ref[...] = jnp.zeros_like(run_ref)

    h = _load_rows(h_ref, tm, nslab)
    scores = _sigmoid(_dot_nt(wr_ref[...], h))
    choice = scores + bias_ref[...]
    neg = -jnp.inf
    row = lax.broadcasted_iota(jnp.int32, (gsz, tm), 0)

    gscore = []
    for gi in range(N_GROUPS):
        cg = choice[gi * gsz:(gi + 1) * gsz, :]
        m1 = jnp.max(cg, axis=0, keepdims=True)
        i1 = jnp.min(jnp.where(cg == m1, row, gsz), axis=0, keepdims=True)
        m2 = jnp.max(jnp.where(row == i1, neg, cg), axis=0, keepdims=True)
        gscore.append(m1 + m2)
    gs = jnp.concatenate(gscore, axis=0)
    grow = lax.broadcasted_iota(jnp.int32, (N_GROUPS, tm), 0)
    gsel = jnp.zeros((N_GROUPS, tm), F32)
    for _ in range(TOPK_GROUPS):
        gm = jnp.max(gs, axis=0, keepdims=True)
        gi = jnp.min(jnp.where(gs == gm, grow, N_GROUPS), axis=0, keepdims=True)
        hit = grow == gi
        gsel = jnp.where(hit, 1.0, gsel)
        gs = jnp.where(hit, neg, gs)
    masked = jnp.concatenate(
        [jnp.where(gsel[gi:gi + 1, :] > 0, choice[gi * gsz:(gi + 1) * gsz, :], neg) for gi in range(N_GROUPS)],
        axis=0)

    erow = lax.broadcasted_iota(jnp.int32, (n_exp, tm), 0)
    idxs, raw = [], []
    for _ in range(TOP_K):
        m = jnp.max(masked, axis=0, keepdims=True)
        i = jnp.min(jnp.where(masked == m, erow, n_exp), axis=0, keepdims=True)
        hit = erow == i
        raw.append(jnp.sum(jnp.where(hit, scores, 0.0), axis=0, keepdims=True))
        masked = jnp.where(hit, neg, masked)
        idxs.append(i)
    total = raw[0]
    for r in raw[1:]:
        total = total + r
    onehot = jnp.zeros((n_exp, tm), F32)
    for i in idxs:
        onehot = onehot + jnp.where(erow == i, 1.0, 0.0)
    before = _dot(onehot.astype(BF16), tri_ref[...]) + run_ref[:, 0:1]
    for kk in range(TOP_K):
        eidx_ref[kk:kk + 1, :] = idxs[kk]
        wts_ref[kk:kk + 1, :] = raw[kk] / total * ROUTED_SCALE
        rank_ref[kk:kk + 1, :] = jnp.sum(jnp.where(erow == idxs[kk], before, 0.0), axis=0, keepdims=True).astype(jnp.int32)
    run_ref[...] = run_ref[...] + jnp.sum(onehot, axis=1, keepdims=True)

    @pl.when(step == pl.num_programs(0) - 1)
    def _():
        cnt_ref[...] = run_ref[...].astype(jnp.int32)


def _slot_kernel(eidx_ref, rank_ref, pstart_ref, slot_ref):
    n_exp = pstart_ref.shape[0]
    tt = TOK_TILE
    erow = lax.broadcasted_iota(jnp.int32, (n_exp, tt), 0)
    pstart = pstart_ref[:, 0:1]
    for j in range(slot_ref.shape[0]):
        lanes = slice(j * tt, (j + 1) * tt)
        for kk in range(TOP_K):
            base = jnp.sum(jnp.where(erow == eidx_ref[kk:kk + 1, lanes], pstart, 0), axis=0, keepdims=True)
            slot_ref[j, kk:kk + 1, :] = base + rank_ref[kk:kk + 1, lanes]


def _slots(eidx, rank, pstart, n_tok):
    n_exp = pstart.shape[0]
    tt = TOK_TILE
    ntile = n_tok // tt
    per_step = max(g for g in range(1, 9) if ntile % g == 0)
    tok_spec = pl.BlockSpec((TOP_K, per_step * tt), lambda i: (0, i))
    return pl.pallas_call(
        _slot_kernel,
        grid=(ntile // per_step,),
        in_specs=[tok_spec, tok_spec, pl.BlockSpec((n_exp, LANES), lambda i: (0, 0))],
        out_specs=pl.BlockSpec((per_step, TOP_K, tt), lambda i: (i, 0, 0)),
        out_shape=jax.ShapeDtypeStruct((ntile, TOP_K, tt), jnp.int32),
        compiler_params=_cparams(("parallel",)),
        name="moe_slots",
    )(eidx, rank, jnp.broadcast_to(pstart[:, None], (n_exp, LANES)))


def _router(h2s, w_router, router_bias, n_tok, d, tm):
    n_exp = w_router.shape[1]
    nslab = d // (2 * LANES)
    tri = np.triu(np.ones((tm, tm), np.float32), 1)
    out_tok = lambda dt: jax.ShapeDtypeStruct((TOP_K, n_tok), dt)
    tok_spec = pl.BlockSpec((TOP_K, tm), lambda i: (0, i))
    return pl.pallas_call(
        functools.partial(_router_kernel, nslab=nslab, n_exp=n_exp),
        grid=(n_tok // tm,),
        in_specs=[pl.BlockSpec((tm * nslab, LANES), lambda i: (i, 0)),
                  pl.BlockSpec((n_exp, d), lambda i: (0, 0)),
                  pl.BlockSpec((n_exp, 1), lambda i: (0, 0)),
                  pl.BlockSpec((tm, tm), lambda i: (0, 0))],
        out_specs=[tok_spec, tok_spec, tok_spec, pl.BlockSpec((n_exp, LANES), lambda i: (0, 0))],
        out_shape=[out_tok(jnp.int32), out_tok(F32), out_tok(jnp.int32),
                   jax.ShapeDtypeStruct((n_exp, LANES), jnp.int32)],
        scratch_shapes=[pltpu.VMEM((n_exp, LANES), F32)],
        compiler_params=_cparams(("arbitrary",)),
        name="router_topk",
    )(h2s, w_router.T.astype(BF16), router_bias.reshape(n_exp, 1), jnp.asarray(tri, BF16))


def _dispatch_kernel(cnt_ref, pstart_ref, slot_hbm, h_ref, xs_hbm, slot_smem, zbuf, ssem, dsem, zsem, *, bm):
    i = pl.program_id(0)
    nslab = xs_hbm.shape[1]
    tt = h_ref.shape[0] // nslab
    n_exp = cnt_ref.shape[0]

    def slot_copy(tile, half):
        return pltpu.make_async_copy(slot_hbm.at[tile], slot_smem.at[half], ssem.at[half])

    @pl.when(i == 0)
    def _():
        slot_copy(0, 0).start()

    @pl.when(i == 0)
    def _():
        zbuf[...] = jnp.zeros_like(zbuf)

        def walk(e, start):
            cnt = cnt_ref[e]
            pad = lax.rem(bm - lax.rem(cnt, bm), bm)
            base = pstart_ref[e] + cnt
            size = bm // 2
            while size >= 1:
                take = (pad & size) != 0

                @pl.when(take)
                def _(base=base, size=size):
                    cp = pltpu.make_async_copy(zbuf.at[pl.ds(0, size)], xs_hbm.at[pl.ds(base, size)], zsem)
                    if start:
                        cp.start()
                    else:
                        cp.wait()

                base = base + jnp.where(take, size, 0)
                size //= 2

        def start_e(e, carry):
            walk(e, True)
            return carry

        def wait_e(e, carry):
            walk(e, False)
            return carry

        lax.fori_loop(0, n_exp, start_e, 0)
        lax.fori_loop(0, n_exp, wait_e, 0)

    half = lax.rem(i, 2)
    slot_copy(i, half).wait()

    @pl.when(i + 1 < pl.num_programs(0))
    def _():
        slot_copy(i + 1, 1 - half).start()

    def row_copy(t, kk):
        src = h_ref.at[pl.ds(pl.multiple_of(t * nslab, nslab), nslab), :]
        return pltpu.make_async_copy(src, xs_hbm.at[slot_smem[half, kk, t]], dsem)

    def issue(t, carry):
        for kk in range(TOP_K):
            row_copy(t, kk).start(priority=kk % 2)
        return carry

    lax.fori_loop(0, tt, issue, 0)

    def drain(t, carry):
        for kk in range(TOP_K):
            row_copy(t, kk).wait()
        return carry

    lax.fori_loop(0, tt, drain, 0)


def _dispatch(counts, pstart, slots, h2s, n_tok, nslab, n_rows, bm):
    return pl.pallas_call(
        functools.partial(_dispatch_kernel, bm=bm),
        grid_spec=pltpu.PrefetchScalarGridSpec(
            num_scalar_prefetch=2,
            grid=(n_tok // TOK_TILE,),
            in_specs=[pl.BlockSpec(memory_space=pl.ANY),
                      pl.BlockSpec((TOK_TILE * nslab, LANES), lambda i, c, p: (i, 0))],
            out_specs=pl.BlockSpec(memory_space=pl.ANY),
            scratch_shapes=[pltpu.SMEM((2, TOP_K, TOK_TILE), jnp.int32),
                            pltpu.VMEM((bm // 2, nslab, LANES), jnp.int32),
                            pltpu.SemaphoreType.DMA((2,)), pltpu.SemaphoreType.DMA, pltpu.SemaphoreType.DMA]),
        out_shape=jax.ShapeDtypeStruct((n_rows, nslab, LANES), jnp.int32),
        compiler_params=_cparams(("arbitrary",)),
        name="moe_dispatch",
    )(counts, pstart, slots, h2s)


def _experts_kernel(vis_ref, nv_ref, pstart_ref, pcnt_ref, nu_ref, xs_hbm, w1_hbm, w3_hbm, w2_hbm, os_hbm,
                    xbuf, obuf, w1s, w3s, w2s, w1b, w3b, w2b, xsem, osem, wsem, *, nslab, bm):
    j = pl.program_id(0)
    rows = bm * nslab
    n_used = nu_ref[0]
    n_visit = nv_ref[0]

    def weight_copies(jj, slot):
        e = vis_ref[jj]
        return (pltpu.make_async_copy(w1_hbm.at[e], w1s.at[slot], wsem.at[slot, 0]),
                pltpu.make_async_copy(w3_hbm.at[e], w3s.at[slot], wsem.at[slot, 1]),
                pltpu.make_async_copy(w2_hbm.at[e], w2s.at[slot], wsem.at[slot, 2]))

    def x_copy(g, slot):
        src = xs_hbm.at[pl.ds(pl.multiple_of(g * rows, rows), rows), :]
        return pltpu.make_async_copy(src, xbuf.at[pl.ds(pl.multiple_of(slot * rows, rows), rows), :], xsem.at[slot])

    def o_copy(g, slot):
        dst = os_hbm.at[pl.ds(pl.multiple_of(g * rows, rows), rows), :]
        return pltpu.make_async_copy(obuf.at[pl.ds(pl.multiple_of(slot * rows, rows), rows), :], dst, osem.at[slot])

    @pl.when(j == 0)
    def _():
        x_copy(0, 0).start()
        for cp in weight_copies(0, 0):
            cp.start(priority=1)

    @pl.when(j < n_visit)
    def _():
        e = vis_ref[j]
        g0 = pstart_ref[e] // bm
        wslot = lax.rem(j, 2)
        for cp in weight_copies(j, wslot):
            cp.wait()

        @pl.when(j + 1 < n_visit)
        def _():
            for cp in weight_copies(j + 1, 1 - wslot):
                cp.start(priority=1)

        w1b[...] = w1s[wslot].astype(BF16)
        w3b[...] = w3s[wslot].astype(BF16)
        w2b[...] = w2s[wslot].astype(BF16)

        def block(b, carry):
            g = g0 + b
            slot = lax.rem(g, 2)
            x_copy(g, slot).wait()

            @pl.when(g + 1 < n_used)
            def _():
                x_copy(g + 1, 1 - slot).start()

            @pl.when(g >= 2)
            def _():
                o_copy(g - 2, slot).wait()

            x = _load_rows(xbuf, bm, nslab, slot * rows)
            a = (_silu(_dot(x, w1b[...])) * _dot(x, w3b[...])).astype(BF16)
            _store_slabs(obuf, _pack_rows(_dot(a, w2b[...])), slot * rows)
            o_copy(g, slot).start()
            return carry

        lax.fori_loop(0, pcnt_ref[e] // bm, block, 0)

    @pl.when(j == pl.num_programs(0) - 1)
    def _():
        @pl.when(n_used >= 2)
        def _():
            o_copy(n_used - 2, lax.rem(n_used, 2)).wait()

        o_copy(n_used - 1, lax.rem(n_used - 1, 2)).wait()


def _experts(visit, n_visit, pstart, pcounts, n_used, xs2, w1, w3, w2, d, bm):
    n_exp, _, de = w1.shape
    nslab = d // (2 * LANES)
    hbm = pl.BlockSpec(memory_space=pl.ANY)
    return pl.pallas_call(
        functools.partial(_experts_kernel, nslab=nslab, bm=bm),
        grid_spec=pltpu.PrefetchScalarGridSpec(
            num_scalar_prefetch=5,
            grid=(n_exp,),
            in_specs=[hbm, hbm, hbm, hbm],
            out_specs=hbm,
            scratch_shapes=[pltpu.VMEM((2 * bm * nslab, LANES), jnp.int32),
                            pltpu.VMEM((2 * bm * nslab, LANES), jnp.int32),
                            pltpu.VMEM((2, d, de), F32), pltpu.VMEM((2, d, de), F32), pltpu.VMEM((2, de, d), F32),
                            pltpu.VMEM((d, de), BF16), pltpu.VMEM((d, de), BF16), pltpu.VMEM((de, d), BF16),
                            pltpu.SemaphoreType.DMA((2,)), pltpu.SemaphoreType.DMA((2,)),
                            pltpu.SemaphoreType.DMA((2, 3))]),
        out_shape=jax.ShapeDtypeStruct(xs2.shape, jnp.int32),
        compiler_params=_cparams(("arbitrary",)),
        name="routed_experts",
    )(visit, n_visit, pstart, pcounts, n_used, xs2, w1, w3, w2)


def _combine_kernel(slot_hbm, os_hbm, wt_ref, h_ref, ws1_ref, ws3_ref, ws2_ref, x1_ref, g2_ref, o_ref,
                    slot0, slot1, buf0, buf1, sh_ref, ssem, gsem, *, nslab, tile0):
    i = pl.program_id(0)
    tt = x1_ref.shape[0]
    last = pl.num_programs(0) - 1
    slots = (slot0, slot1)
    bufs = (buf0, buf1)

    def row_copy(half, t, kk, row):
        dst = bufs[half].at[pl.ds((kk * tt + t) * nslab, nslab), :]
        return pltpu.make_async_copy(os_hbm.at[row], dst, gsem.at[half])

    def slot_copy(tile, half):
        return pltpu.make_async_copy(slot_hbm.at[tile0 + tile], slots[half], ssem.at[half])

    def drain(half):
        def body(t, carry):
            for kk in range(TOP_K):
                row_copy(half, t, kk, 0).wait()
            return carry

        lax.fori_loop(0, tt, body, 0)

    @pl.when(i == 0)
    def _():
        first = slot_copy(0, 0)
        first.start()
        first.wait()

        def issue(t, carry):
            for kk in range(TOP_K):
                row_copy(0, t, kk, slot0[kk, t]).start(priority=kk % 2)
            return carry

        lax.fori_loop(0, tt, issue, 0)
        slot_copy(jnp.minimum(1, last), 1).start()

    def step(half):
        other = 1 - half
        slot_copy(jnp.minimum(i + 1, last), other).wait()
        slot_copy(jnp.minimum(i + 2, last), half).start()
        drain(half)

        h = _load_rows(h_ref, tt, nslab)
        a = (_silu(_dot(h, ws1_ref[...])) * _dot(h, ws3_ref[...])).astype(BF16)
        sh_ref[...] = _dot(a, ws2_ref[...])

        wt = wt_ref[...]
        g2 = _seq_rows(g2_ref[...], tt)
        hw = nslab * LANES
        per_slab = tt // nslab
        for s in range(nslab):
            y_lo = sh_ref[:, s * LANES:(s + 1) * LANES]
            y_hi = sh_ref[:, hw + s * LANES:hw + (s + 1) * LANES]
            for kk in range(TOP_K):
                lo, hi = _unpack_words(bufs[half][pl.ds(kk * tt * nslab + s, tt, stride=nslab), :])
                y_lo = y_lo + wt[:, kk:kk + 1] * lo
                y_hi = y_hi + wt[:, kk:kk + 1] * hi
            for y, c0 in ((y_lo, s * LANES), (y_hi, hw + s * LANES)):
                cols = slice(c0, c0 + LANES)
                o_ref[:, cols] = x1_ref[:, cols] + g2[:, cols] * y
            for t in range(s * per_slab, (s + 1) * per_slab):
                for kk in range(TOP_K):
                    row_copy(other, t, kk, slots[other][kk, t]).start(priority=kk % 2)

        @pl.when(i == last)
        def _():
            slot_copy(last, half).wait()
            drain(other)

    for half in (0, 1):
        @pl.when(lax.rem(i, 2) == half)
        def _(half=half):
            step(half)


def _combine(slots, os3, wts_t, h2s, ws1, ws3, ws2, x1, g2, seq_len, tok0):
    n, d = x1.shape
    nslab = d // (2 * LANES)
    tt = TOK_TILE
    tile0 = tok0 // tt
    const = lambda a: pl.BlockSpec(a.shape, lambda i: (0, 0), pipeline_mode=pl.Buffered(1))
    return pl.pallas_call(
        functools.partial(_combine_kernel, nslab=nslab, tile0=tile0),
        grid=(n // tt,),
        in_specs=[pl.BlockSpec(memory_space=pl.ANY),
                  pl.BlockSpec(memory_space=pl.ANY),
                  pl.BlockSpec((tt, TOP_K), lambda i: (tile0 + i, 0)),
                  pl.BlockSpec((tt * nslab, LANES), lambda i: (tile0 + i, 0)),
                  const(ws1), const(ws3), const(ws2),
                  pl.BlockSpec((tt, d), lambda i: (i, 0)),
                  _mod_spec(tt, seq_len, d)],
        out_specs=pl.BlockSpec((tt, d), lambda i: (i, 0)),
        out_shape=jax.ShapeDtypeStruct((n, d), F32),
        scratch_shapes=[pltpu.SMEM((TOP_K, tt), jnp.int32), pltpu.SMEM((TOP_K, tt), jnp.int32),
                        pltpu.VMEM((TOP_K * tt * nslab, LANES), jnp.int32),
                        pltpu.VMEM((TOP_K * tt * nslab, LANES), jnp.int32),
                        pltpu.VMEM((tt, d), F32),
                        pltpu.SemaphoreType.DMA((2,)), pltpu.SemaphoreType.DMA((2,))],
        compiler_params=_cparams(("arbitrary",)),
        name="moe_combine",
    )(slots, os3, wts_t, h2s, ws1, ws3, ws2, x1, g2)


def _mixer(x, mod, positions, s0, past_k, past_v, lb, p, w, moe_rows, moe_buf):
    bsz, seq_len, d = x.shape
    n = bsz * seq_len
    sh1, sc1, g1, sh2, sc2, g2 = mod
    tm = 256 if n % 256 == 0 else n
    x2 = x.reshape(n, d)
    n_kv = p["n_kv"]
    kd = n_kv * AT_HD
    hg_w = w["w_in_hg"].shape[1]
    qd = w["w_in_at"].shape[1] - 2 * kd
    h1, qn, kvn = _attn_in(x2, p["norm1_w"], sc1, sh1, w["w_in_at"], positions, p["q_norm_w"], p["k_norm_w"],
                           seq_len, qd, kd, min(seq_len, 512))
    tmm = 1024 if n % 1024 == 0 else tm
    z_hg = _matmul(h1, w["w_in_hg"], tmm, min(hg_w, 1024), name="w_in_hgrn")
    gates = _matmul(h1, w["w_in_gate"], tmm, min(2 * d, 1024), act="sigmoid", out_dtype=BF16, name="w_in_gates")

    o_hg, s_new = _hgrn(z_hg, lb, p["hg_norm_w"], s0, bsz, seq_len)

    if past_k is None:
        o_at = _swa_prompt(qn, kvn, p["attn_sinks"], bsz, seq_len, n_kv)
    else:
        o_at = _swa_sample(qn, kvn, past_k, past_v, p["attn_sinks"], bsz, seq_len, n_kv)

    x1, h2s = _merge(o_hg, o_at, gates, x2, g1, w["w_hg_out"], w["w_at_out"], w["w_o"],
                     p["norm2_w"], sc2, sh2, seq_len, tm, moe_rows, moe_buf)
    kv3 = kvn.reshape(bsz, seq_len, 2 * kd)
    return x1, h2s, s_new, kv3[:, :, :kd], kv3[:, :, kd:]


def kernel(x_prompt, x_sample, cache_k, cache_v, state_hgrn, c_prompt, c_sample, norm1_w, norm2_w, w_ada, b_ada,
           w_in, hg_lower_bounds, hg_norm_w, q_norm_w, k_norm_w, attn_sinks, w_hg_out, w_at_out, w_o, w_router,
           router_bias, w_exp_gate, w_exp_up, w_exp_down, w_sh_gate, w_sh_up, w_sh_down):
    depth = norm1_w.shape[0]
    assert depth == 1, "single trunk layer"
    bp, tp, d = x_prompt.shape
    bs, ts, _ = x_sample.shape
    window, n_kv = cache_k.shape[2], cache_k.shape[3]
    kd = n_kv * AT_HD
    hg_dim = w_hg_out.shape[1]
    qd = w_at_out.shape[1]
    n_exp = w_router.shape[2]
    nslab = d // (2 * LANES)
    l = 0

    lbs = jnp.cumsum(jax.nn.softmax(hg_lower_bounds.astype(F32), axis=0), axis=0)
    win = w_in[l]
    w = {
        "w_in_hg": win[:, :4 * hg_dim].astype(BF16),
        "w_in_at": win[:, 4 * hg_dim:4 * hg_dim + qd + 2 * kd].astype(BF16),
        "w_in_gate": win[:, 4 * hg_dim + qd + 2 * kd:].astype(BF16),
        "w_hg_out": w_hg_out[l].astype(BF16),
        "w_at_out": w_at_out[l].astype(BF16),
        "w_o": w_o[l].astype(BF16),
    }
    p = {"norm1_w": norm1_w[l], "norm2_w": norm2_w[l], "hg_norm_w": hg_norm_w[l], "q_norm_w": q_norm_w[l],
         "k_norm_w": k_norm_w[l], "attn_sinks": attn_sinks[l], "n_kv": n_kv}

    c_all = jnp.concatenate([c_prompt, c_sample], axis=0)
    mod_all = _ada(c_all, w_ada[l], b_ada[l])
    mod_all = mod_all.reshape(bp + bs, 6, 1, d)
    mod_p = tuple(mod_all[:bp, j] for j in range(6))
    mod_s = tuple(mod_all[bp:, j] for j in range(6))

    pos_p = jnp.arange(tp, dtype=jnp.int32)
    pos_s = PAST_LEN + jnp.arange(ts, dtype=jnp.int32)
    s0_p = jnp.zeros((bp,) + state_hgrn.shape[2:], F32)
    n_p, n_s = bp * tp, bs * ts
    n_tok = n_p + n_s
    x1_p, h2_p, sp, kp, vp = _mixer(x_prompt, mod_p, pos_p, s0_p, None, None, lbs[l], p, w, (0, n_tok), None)
    pk = cache_k[l].reshape(bs, window, kd)
    pv = cache_v[l].reshape(bs, window, kd)
    x1_s, h2s, ss, ks, vs = _mixer(x_sample, mod_s, pos_s, state_hgrn[l], pk, pv, lbs[l], p, w, (n_p, n_tok), h2_p)

    tr = 256 if n_tok % 256 == 0 else TOK_TILE
    eidx, wts, rank, counts = _router(h2s, w_router[l], router_bias[l], n_tok, d, tr)

    bm = MOE_ROWS
    counts = counts[:, 0]
    pcounts = (counts + bm - 1) // bm * bm
    pend = jnp.cumsum(pcounts)
    pstart = pend - pcounts
    nb = -(-(n_tok * TOP_K) // bm) + n_exp
    slots = _slots(eidx, rank, pstart, n_tok)
    n_used = (pend[-1:] // bm).astype(jnp.int32)

    xs = _dispatch(counts, pstart, slots, h2s, n_tok, nslab, nb * bm, bm)
    visit = jnp.argsort(counts == 0, stable=True).astype(jnp.int32)
    n_visit = jnp.sum(counts > 0).astype(jnp.int32).reshape(1)
    os_ = _experts(visit, n_visit, pstart, pcounts, n_used, xs.reshape(nb * bm * nslab, LANES),
                   w_exp_gate[l], w_exp_up[l], w_exp_down[l], d, bm)
    os3 = os_.reshape(nb * bm, nslab, LANES)
    wts_t = wts.T
    ws = (w_sh_gate[l].astype(BF16), w_sh_up[l].astype(BF16), w_sh_down[l].astype(BF16))
    y_p = _combine(slots, os3, wts_t, h2s, *ws, x1_p, mod_p[5], tp, 0)
    y_s = _combine(slots, os3, wts_t, h2s, *ws, x1_s, mod_s[5], ts, n_p)

    def cache_out(a, b_, t):
        return a[:, t - window:].reshape(1, b_, window, n_kv, AT_HD)

    new_k_p = cache_out(kp, bp, tp)
    new_v_p = cache_out(vp, bp, tp)
    keys_s = jnp.concatenate([pk, ks], axis=1)
    vals_s = jnp.concatenate([pv, vs], axis=1)
    new_k_s = cache_out(keys_s, bs, window + ts)
    new_v_s = cache_out(vals_s, bs, window + ts)
    return (y_p.reshape(bp, tp, d), y_s.reshape(bs, ts, d), new_k_p, new_v_p, sp[None],
            new_k_s, new_v_s, ss[None])
```

```python
import functools
import math

import numpy as np
import jax
import jax.numpy as jnp
from jax import lax
from jax.experimental import pallas as pl
from jax.experimental.pallas import tpu as pltpu

EPS = 1e-6
CHUNK = 64
HG_CHUNK = 128
HG_DK = 128
AT_HD = 64
ROPE_DIM = 16
ROPE_THETA = 500000.0
TOP_K = 8
N_GROUPS = 8
TOPK_GROUPS = 4
ROUTED_SCALE = 2.5
PAST_LEN = 2048

LANES = 128
MOE_ROWS = 256
TOK_TILE = 128
VMEM_LIMIT = 56 * 1024 * 1024

F32 = jnp.float32
BF16 = jnp.bfloat16


def _cparams(semantics, vmem=VMEM_LIMIT):
    return pltpu.CompilerParams(dimension_semantics=semantics, vmem_limit_bytes=vmem)


def _sigmoid(x):
    return 1.0 / (1.0 + jnp.exp(-x))


def _silu(x):
    return x * _sigmoid(x)


def _dot(a, b):
    return jnp.dot(a, b, preferred_element_type=F32)


def _dot_nt(a, b):
    return lax.dot_general(a, b, (((1,), (1,)), ((), ())), preferred_element_type=F32)


def _dot_tn(a, b):
    return lax.dot_general(a, b, (((0,), (0,)), ((), ())), preferred_element_type=F32)


def _split_bf16(x):
    hi = x.astype(BF16)
    lo = (x - hi.astype(F32)).astype(BF16)
    return hi, lo


def _seq_rows(m, rows):
    s, _, d = m.shape
    if s == 1:
        return m[0]
    return jnp.broadcast_to(m, (s, rows // s, d)).reshape(rows, d)


def _mod_spec(tm, seq_len, d):
    if tm <= seq_len:
        return pl.BlockSpec((1, 1, d), lambda i: ((i * tm) // seq_len, 0, 0))
    s = tm // seq_len
    return pl.BlockSpec((s, 1, d), lambda i: (i, 0, 0))


def _ada_kernel(c_ref, w_ref, b_ref, o_ref):
    s = _silu(c_ref[...]).astype(BF16)
    o_ref[...] = _dot(s, w_ref[...].astype(BF16)) + b_ref[...]


def _ada(c, w, b):
    n, d = c.shape
    m = w.shape[1]
    tn = min(m, 1024)
    return pl.pallas_call(
        _ada_kernel,
        grid=(m // tn,),
        in_specs=[pl.BlockSpec((n, d), lambda j: (0, 0)),
                  pl.BlockSpec((d, tn), lambda j: (0, j)),
                  pl.BlockSpec((1, tn), lambda j: (0, j))],
        out_specs=pl.BlockSpec((n, tn), lambda j: (0, j)),
        out_shape=jax.ShapeDtypeStruct((n, m), F32),
        compiler_params=_cparams(("parallel",)),
        name="ada_mod",
    )(c, w, b.reshape(1, m))


def _mm_kernel(x_ref, w_ref, o_ref, *, act):
    y = _dot(x_ref[...], w_ref[...])
    if act == "sigmoid":
        y = _sigmoid(y)
    o_ref[...] = y.astype(o_ref.dtype)


def _matmul(x, w, tm, tn, act=None, out_dtype=F32, name="matmul"):
    n, k = x.shape
    m = w.shape[1]
    return pl.pallas_call(
        functools.partial(_mm_kernel, act=act),
        grid=(n // tm, m // tn),
        in_specs=[pl.BlockSpec((tm, k), lambda i, j: (i, 0)),
                  pl.BlockSpec((k, tn), lambda i, j: (0, j))],
        out_specs=pl.BlockSpec((tm, tn), lambda i, j: (i, j)),
        out_shape=jax.ShapeDtypeStruct((n, m), out_dtype),
        compiler_params=_cparams(("parallel", "arbitrary")),
        name=name,
    )(x, w)


def _hgrn_tables(c):
    nlev = int(math.log2(c))
    t = np.arange(c)[:, None]
    s = np.arange(c)[None, :]
    seg = []
    for l in range(1, nlev + 1):
        b = 1 << l
        seg.append(((s >= (t // b) * b) & (s <= t)).astype(np.float32))
        seg.append(((s > t) & (s <= (t // b) * b + b - 1)).astype(np.float32))
    masks = [(t == s).astype(np.float32)]
    for l in range(nlev):
        b = 1 << l
        masks.append(((t // (2 * b) == s // (2 * b)) & (t % (2 * b) >= b) & (s % (2 * b) < b)).astype(np.float32))
    return np.concatenate(seg, axis=0), np.stack(masks, axis=0)


def _hgrn_kernel(q_ref, f_ref, i_ref, g_ref, lb_ref, nw_ref, s0_ref, seg_ref, msk_ref,
                 o_ref, sn_ref, st_ref, *, c, nchunks, hb):
    nlev = int(math.log2(c))
    tstep = pl.program_id(2)

    @pl.when(tstep == 0)
    def _():
        for hh in range(hb):
            st_ref[hh] = s0_ref[0, hh].T

    nw = nw_ref[...]
    lb = lb_ref[...]
    head = lambda a, hh: a[:, hh * HG_DK:(hh + 1) * HG_DK]

    def decays(l, g_hi, g_lo):
        seg = seg_ref[2 * (l - 1) * c:2 * l * c, :]
        e = _dot(seg, g_hi) + _dot(seg, g_lo)
        return e[:c, :], e[c:, :]

    def block_decays(a_inc, b):
        width = a_inc.shape[1]
        fill = lambda r: jnp.broadcast_to(a_inc[r:r + 1, :], (b, width))
        before = jnp.concatenate([jnp.zeros((b, width), F32)] + [fill(j * b - 1) for j in range(1, c // b)], axis=0)
        last = jnp.concatenate([fill((j + 1) * b - 1) for j in range(c // b)], axis=0)
        return a_inc - before, last - a_inc

    def chunk(ci, carry):
        rows = pl.ds(pl.multiple_of(ci * c, c), c)
        q = q_ref[rows, :]
        f = lb + (1.0 - lb) * _sigmoid(f_ref[rows, :])
        g = jnp.log(f)
        k = 1.0 - f
        g_hi, g_lo = _split_bf16(g)
        qb = q.astype(BF16)
        kb = k.astype(BF16)
        a_inc, v_end = decays(nlev, g_hi, g_lo)
        scores = [jnp.where(msk_ref[0] > 0, _dot_nt(head(qb, hh), head(kb, hh)), 0.0) for hh in range(hb)]
        for l in range(nlev):
            if l == 0:
                ql, kl = (q * f).astype(BF16), kb
            else:
                wl, vl = block_decays(a_inc, 1 << l) if (1 << l) % 8 == 0 else decays(l, g_hi, g_lo)
                ql = (q * jnp.exp(wl)).astype(BF16)
                kl = (k * jnp.exp(vl)).astype(BF16)
            for hh in range(hb):
                scores[hh] = scores[hh] + jnp.where(msk_ref[l + 1] > 0, _dot_nt(head(ql, hh), head(kl, hh)), 0.0)
        qa = (q * jnp.exp(a_inc)).astype(BF16)
        k_end = (k * jnp.exp(v_end)).astype(BF16)
        vb = i_ref[rows, :].astype(BF16)
        carry_decay = jnp.exp(a_inc[c - 1:c, :])
        gate = _silu(g_ref[rows, :])
        for hh in range(hb):
            st = st_ref[hh]
            o = _dot_nt(head(qa, hh), st.astype(BF16)) + _dot(scores[hh].astype(BF16), head(vb, hh))
            st_ref[hh] = st * head(carry_decay, hh) + _dot_tn(head(vb, hh), head(k_end, hh))
            on = o * lax.rsqrt(jnp.mean(o * o, axis=-1, keepdims=True) + EPS) * nw
            o_ref[rows, hh * HG_DK:(hh + 1) * HG_DK] = (on * head(gate, hh)).astype(o_ref.dtype)
        return carry

    lax.fori_loop(0, nchunks, chunk, 0)

    @pl.when(tstep == pl.num_programs(2) - 1)
    def _():
        for hh in range(hb):
            sn_ref[0, hh] = st_ref[hh].T


def _hgrn(z_hg, lb, norm_w, s0, bsz, seq_len):
    n, w4 = z_hg.shape
    nh = w4 // (4 * HG_DK)
    hb = min(nh, 8)
    ng = nh // hb
    c = min(HG_CHUNK, seq_len)
    tb = min(seq_len, 512)
    nt = seq_len // tb
    seg, masks = _hgrn_tables(c)

    def col(part):
        return pl.BlockSpec((tb, hb * HG_DK), lambda b, h, t: (b * nt + t, part * ng + h))

    return pl.pallas_call(
        functools.partial(_hgrn_kernel, c=c, nchunks=tb // c, hb=hb),
        grid=(bsz, ng, nt),
        in_specs=[col(0), col(1), col(2), col(3),
                  pl.BlockSpec((1, hb * HG_DK), lambda b, h, t: (0, h)),
                  pl.BlockSpec((1, HG_DK), lambda b, h, t: (0, 0)),
                  pl.BlockSpec((1, hb, HG_DK, HG_DK), lambda b, h, t: (b, h, 0, 0)),
                  pl.BlockSpec(seg.shape, lambda b, h, t: (0, 0)),
                  pl.BlockSpec(masks.shape, lambda b, h, t: (0, 0, 0))],
        out_specs=[pl.BlockSpec((tb, hb * HG_DK), lambda b, h, t: (b * nt + t, h)),
                   pl.BlockSpec((1, hb, HG_DK, HG_DK), lambda b, h, t: (b, h, 0, 0))],
        out_shape=[jax.ShapeDtypeStruct((n, nh * HG_DK), BF16),
                   jax.ShapeDtypeStruct((bsz, nh, HG_DK, HG_DK), F32)],
        scratch_shapes=[pltpu.VMEM((hb, HG_DK, HG_DK), F32)],
        compiler_params=_cparams(("parallel", "parallel", "arbitrary")),
        name="hgrn2",
    )(z_hg, z_hg, z_hg, z_hg, lb.reshape(1, -1), norm_w.reshape(1, HG_DK), s0,
      jnp.asarray(seg, BF16), jnp.asarray(masks, F32))


def _rope_tables(positions):
    half = ROPE_DIM // 2
    inv_freq = ROPE_THETA ** (-jnp.arange(0, ROPE_DIM, 2, dtype=F32) / ROPE_DIM)
    ang = positions.astype(F32)[:, None] * inv_freq[None, :]
    cos, sin = jnp.cos(ang), jnp.sin(ang)
    t = positions.shape[0]
    rest = AT_HD - ROPE_DIM
    c64 = jnp.concatenate([cos, cos, jnp.ones((t, rest), F32)], axis=1)
    sa64 = jnp.concatenate([-sin, jnp.zeros((t, half + rest), F32)], axis=1)
    sb64 = jnp.concatenate([jnp.zeros((t, half), F32), sin, jnp.zeros((t, rest), F32)], axis=1)
    rep = LANES // AT_HD
    return tuple(jnp.tile(a, (1, rep)) for a in (c64, sa64, sb64))


def _attn_in_kernel(x_ref, nw_ref, sc_ref, sh_ref, w_ref, cos_ref, sa_ref, sb_ref, qw_ref, kw_ref, hsum_ref,
                    hexp_ref, h_ref, q_ref, kv_ref, *, qd, kd):
    half = ROPE_DIM // 2
    x = x_ref[...]
    tm = x.shape[0]
    y = x * lax.rsqrt(jnp.mean(x * x, axis=-1, keepdims=True) + EPS) * nw_ref[...]
    h = (y * (1.0 + _seq_rows(sc_ref[...], tm)) + _seq_rows(sh_ref[...], tm)).astype(BF16)
    h_ref[...] = h
    z = _dot(h, w_ref[...])
    cos, sa, sb = cos_ref[...], sa_ref[...], sb_ref[...]

    def norm_rope(x, w, reps):
        width = x.shape[1]
        hi, lo = _split_bf16(x * x)
        hsum = hsum_ref[:width, :]
        ss = _dot(hi, hsum) + _dot(lo, hsum)
        r_hi, r_lo = _split_bf16(lax.rsqrt(ss * (1.0 / AT_HD) + EPS))
        hexp = hexp_ref[:, :width]
        xn = x * (_dot(r_hi, hexp) + _dot(r_lo, hexp)) * w
        tile = lambda a: jnp.concatenate([a] * reps, axis=1) if reps > 1 else a
        return (xn * tile(cos) + pltpu.roll(xn, width - half, 1) * tile(sa)
                + pltpu.roll(xn, half, 1) * tile(sb))

    q = norm_rope(z[:, :qd], qw_ref[...], qd // LANES)
    k = norm_rope(z[:, qd:qd + kd], kw_ref[...], kd // LANES)
    q_ref[...] = q.astype(q_ref.dtype)
    kv_ref[:, :kd] = k
    kv_ref[:, kd:] = z[:, qd + kd:]


def _attn_in(x, norm_w, sc, sh, w_at, positions, q_norm_w, k_norm_w, seq_len, qd, kd, tm):
    n, d = x.shape
    cos, sa, sb = _rope_tables(positions)
    nt = seq_len // tm
    hsum = np.zeros((qd, LANES), np.float32)
    hsum[np.arange(qd), np.arange(qd) // AT_HD] = 1.0
    tab = pl.BlockSpec((tm, LANES), lambda i: (i % nt, 0))
    const = lambda shape: pl.BlockSpec(shape, lambda i: (0,) * len(shape), pipeline_mode=pl.Buffered(1))
    return pl.pallas_call(
        functools.partial(_attn_in_kernel, qd=qd, kd=kd),
        grid=(n // tm,),
        in_specs=[pl.BlockSpec((tm, d), lambda i: (i, 0)), const((1, d)),
                  _mod_spec(tm, seq_len, d), _mod_spec(tm, seq_len, d),
                  const(w_at.shape), tab, tab, tab, const((1, qd)), const((1, kd)),
                  const((qd, LANES)), const((LANES, qd))],
        out_specs=[pl.BlockSpec((tm, d), lambda i: (i, 0)),
                   pl.BlockSpec((tm, qd), lambda i: (i, 0)),
                   pl.BlockSpec((tm, 2 * kd), lambda i: (i, 0))],
        out_shape=[jax.ShapeDtypeStruct((n, d), BF16),
                   jax.ShapeDtypeStruct((n, qd), BF16),
                   jax.ShapeDtypeStruct((n, 2 * kd), F32)],
        compiler_params=_cparams(("parallel",)),
        name="attn_in",
    )(x, norm_w.reshape(1, d), sc, sh, w_at, cos, sa, sb, jnp.tile(q_norm_w, qd // AT_HD).reshape(1, qd),
      jnp.tile(k_norm_w, kd // AT_HD).reshape(1, kd), jnp.asarray(hsum, BF16), jnp.asarray(hsum.T, BF16))


def _attend(q, keys, vals, sinks_ref, col_valid, o_ref, n_kv, group):
    n_heads = n_kv * group
    batch = 16
    for h0 in range(0, n_heads, batch):
        heads = range(h0, min(h0 + batch, n_heads))
        scores = []
        for h in heads:
            j = h // group
            s = _dot_nt(q[:, h * AT_HD:(h + 1) * AT_HD], keys[:, j * AT_HD:(j + 1) * AT_HD]) * (AT_HD ** -0.5)
            scores.append(s if col_valid is None else jnp.where(col_valid, s, -jnp.inf))
        probs = []
        for h, s in zip(heads, scores):
            sink = sinks_ref[h]
            m = jnp.maximum(jnp.max(s, axis=-1, keepdims=True), sink)
            e = jnp.exp(s - m)
            den = jnp.sum(e, axis=-1, keepdims=True) + jnp.exp(sink - m)
            probs.append((e / den).astype(BF16))
        for h, p in zip(heads, probs):
            j = h // group
            o_ref[:, h * AT_HD:(h + 1) * AT_HD] = _dot(p, vals[:, j * AT_HD:(j + 1) * AT_HD]).astype(o_ref.dtype)


def _swa_prompt_kernel(sinks_ref, q_ref, kv0_ref, kv1_ref, kv2_ref, o_ref, *, n_kv, group, w_chunks):
    kd = n_kv * AT_HD
    n = pl.program_id(1)
    blocks = [kv0_ref[...], kv1_ref[...], kv2_ref[...]]
    keys = jnp.concatenate([b[:, :kd] for b in blocks], axis=0).astype(BF16)
    vals = jnp.concatenate([b[:, kd:] for b in blocks], axis=0).astype(BF16)
    col_chunk = lax.broadcasted_iota(jnp.int32, (1, keys.shape[0]), 1) // CHUNK
    col_valid = (col_chunk + n - w_chunks) >= 0
    _attend(q_ref[...], keys, vals, sinks_ref, col_valid, o_ref, n_kv, group)


def _swa_prompt(qn, kvn, sinks, bsz, seq_len, n_kv):
    n, qd = qn.shape
    kd = n_kv * AT_HD
    nc = seq_len // CHUNK
    group = qd // AT_HD // n_kv
    w_chunks = 2

    def kv_spec(j):
        return pl.BlockSpec((CHUNK, 2 * kd), lambda b, c, s: (b * nc + jnp.maximum(c - w_chunks + j, 0), 0))

    return pl.pallas_call(
        functools.partial(_swa_prompt_kernel, n_kv=n_kv, group=group, w_chunks=w_chunks),
        grid_spec=pltpu.PrefetchScalarGridSpec(
            num_scalar_prefetch=1,
            grid=(bsz, nc),
            in_specs=[pl.BlockSpec((CHUNK, qd), lambda b, c, s: (b * nc + c, 0)),
                      kv_spec(0), kv_spec(1), kv_spec(2)],
            out_specs=pl.BlockSpec((CHUNK, qd), lambda b, c, s: (b * nc + c, 0))),
        out_shape=jax.ShapeDtypeStruct((n, qd), BF16),
        compiler_params=_cparams(("parallel", "arbitrary")),
        name="swa_prompt",
    )(sinks, qn, kvn, kvn, kvn)


def _swa_sample_kernel(sinks_ref, q_ref, pk_ref, pv_ref, kv_ref, o_ref, *, n_kv, group):
    kd = n_kv * AT_HD
    kv = kv_ref[...]
    keys = jnp.concatenate([pk_ref[0], kv[:, :kd]], axis=0).astype(BF16)
    vals = jnp.concatenate([pv_ref[0], kv[:, kd:]], axis=0).astype(BF16)
    _attend(q_ref[...], keys, vals, sinks_ref, None, o_ref, n_kv, group)


def _swa_sample(qn, kvn, past_k, past_v, sinks, bsz, seq_len, n_kv):
    n, qd = qn.shape
    kd = n_kv * AT_HD
    window = past_k.shape[1]
    group = qd // AT_HD // n_kv
    return pl.pallas_call(
        functools.partial(_swa_sample_kernel, n_kv=n_kv, group=group),
        grid_spec=pltpu.PrefetchScalarGridSpec(
            num_scalar_prefetch=1,
            grid=(bsz,),
            in_specs=[pl.BlockSpec((seq_len, qd), lambda b, s: (b, 0)),
                      pl.BlockSpec((1, window, kd), lambda b, s: (b, 0, 0)),
                      pl.BlockSpec((1, window, kd), lambda b, s: (b, 0, 0)),
                      pl.BlockSpec((seq_len, 2 * kd), lambda b, s: (b, 0))],
            out_specs=pl.BlockSpec((seq_len, qd), lambda b, s: (b, 0))),
        out_shape=jax.ShapeDtypeStruct((n, qd), BF16),
        compiler_params=_cparams(("parallel",)),
        name="swa_sample",
    )(sinks, qn, past_k, past_v, kvn)


HI_MASK = np.uint32(0xFFFF0000)


def _pack_rows(x):
    half = x.shape[1] // 2
    bits = lambda a: lax.bitcast_convert_type(a.astype(BF16).astype(F32), jnp.uint32)
    word = (bits(x[:, half:]) & HI_MASK) | (bits(x[:, :half]) >> 16)
    return lax.bitcast_convert_type(word, jnp.int32)


def _unpack_words(w):
    u = lax.bitcast_convert_type(w, jnp.uint32)
    return lax.bitcast_convert_type(u << 16, F32), lax.bitcast_convert_type(u & HI_MASK, F32)


def _store_slabs(ref, words, row0=0):
    rows, width = words.shape
    nslab = width // LANES
    for s in range(nslab):
        ref[pl.ds(row0 + s, rows, stride=nslab), :] = words[:, s * LANES:(s + 1) * LANES]


def _load_rows(ref, rows, nslab, row0=0):
    lo, hi = [], []
    for s in range(nslab):
        a, b = _unpack_words(ref[pl.ds(row0 + s, rows, stride=nslab), :])
        lo.append(a)
        hi.append(b)
    return jnp.concatenate(lo + hi, axis=1).astype(BF16)


def _merge_kernel(ohg_ref, oat_ref, ga_ref, gb_ref, x_ref, g1_ref, whg_ref, wat_ref, wo_ref,
                  nw_ref, sc_ref, sh_ref, *rest):
    x1_ref, h2_ref = rest[-2:]
    tm = x_ref.shape[0]
    merged = ga_ref[...] * _dot(ohg_ref[...], whg_ref[...]) + gb_ref[...] * _dot(oat_ref[...], wat_ref[...])
    mix = _dot(merged.astype(BF16), wo_ref[...])
    x1 = x_ref[...] + _seq_rows(g1_ref[...], tm) * mix
    x1_ref[...] = x1
    y = x1 * lax.rsqrt(jnp.mean(x1 * x1, axis=-1, keepdims=True) + EPS) * nw_ref[...]
    h2 = y * (1.0 + _seq_rows(sc_ref[...], tm)) + _seq_rows(sh_ref[...], tm)
    _store_slabs(h2_ref, _pack_rows(h2))


def _merge(o_hg, o_at, gates, x, g1, w_hg_out, w_at_out, w_o, norm2_w, sc2, sh2, seq_len, tm, moe_rows, moe_buf):
    n, d = x.shape
    hd = o_hg.shape[1]
    ad = o_at.shape[1]
    nslab = d // (2 * LANES)
    tok0, n_all = moe_rows
    tile0 = tok0 // tm
    const = lambda shape: pl.BlockSpec(shape, lambda i: (0,) * len(shape), pipeline_mode=pl.Buffered(1))
    mod = lambda: _mod_spec(tm, seq_len, d)
    in_specs = [pl.BlockSpec((tm, hd), lambda i: (i, 0)),
                pl.BlockSpec((tm, ad), lambda i: (i, 0)),
                pl.BlockSpec((tm, d), lambda i: (i, 0)),
                pl.BlockSpec((tm, d), lambda i: (i, 1)),
                pl.BlockSpec((tm, d), lambda i: (i, 0)),
                mod(), const((hd, d)), const((ad, d)), const((d, d)), const((1, d)), mod(), mod()]
    args = [o_hg, o_at, gates, gates, x, g1, w_hg_out, w_at_out, w_o, norm2_w.reshape(1, d), sc2, sh2]
    aliases = {}
    if moe_buf is not None:
        in_specs.append(pl.BlockSpec(memory_space=pl.ANY))
        args.append(moe_buf)
        aliases = {len(args) - 1: 1}
    return pl.pallas_call(
        _merge_kernel,
        grid=(n // tm,),
        in_specs=in_specs,
        out_specs=[pl.BlockSpec((tm, d), lambda i: (i, 0)),
                   pl.BlockSpec((tm * nslab, LANES), lambda i: (tile0 + i, 0))],
        out_shape=[jax.ShapeDtypeStruct((n, d), F32),
                   jax.ShapeDtypeStruct((n_all * nslab, LANES), jnp.int32)],
        input_output_aliases=aliases,
        compiler_params=_cparams(("parallel",)),
        name="merge_norm2",
    )(*args)


def _router_kernel(h_ref, wr_ref, bias_ref, tri_ref, eidx_ref, wts_ref, rank_ref, cnt_ref, run_ref,
                   *, nslab, n_exp):
    tm = h_ref.shape[0] // nslab
    gsz = n_exp // N_GROUPS
    step = pl.program_id(0)

    @pl.when(step == 0)
    def _():
        run_ref[...] = jnp.zeros_like(run_ref)

    h = _load_rows(h_ref, tm, nslab)
    scores = _sigmoid(_dot_nt(wr_ref[...], h))
    choice = scores + bias_ref[...]
    neg = -jnp.inf
    row = lax.broadcasted_iota(jnp.int32, (gsz, tm), 0)

    gscore = []
    for gi in range(N_GROUPS):
        cg = choice[gi * gsz:(gi + 1) * gsz, :]
        m1 = jnp.max(cg, axis=0, keepdims=True)
        i1 = jnp.min(jnp.where(cg == m1, row, gsz), axis=0, keepdims=True)
        m2 = jnp.max(jnp.where(row == i1, neg, cg), axis=0, keepdims=True)
        gscore.append(m1 + m2)
    gs = jnp.concatenate(gscore, axis=0)
    grow = lax.broadcasted_iota(jnp.int32, (N_GROUPS, tm), 0)
    gsel = jnp.zeros((N_GROUPS, tm), F32)
    for _ in range(TOPK_GROUPS):
        gm = jnp.max(gs, axis=0, keepdims=True)
        gi = jnp.min(jnp.where(gs == gm, grow, N_GROUPS), axis=0, keepdims=True)
        hit = grow == gi
        gsel = jnp.where(hit, 1.0, gsel)
        gs = jnp.where(hit, neg, gs)
    masked = jnp.concatenate(
        [jnp.where(gsel[gi:gi + 1, :] > 0, choice[gi * gsz:(gi + 1) * gsz, :], neg) for gi in range(N_GROUPS)],
        axis=0)

    erow = lax.broadcasted_iota(jnp.int32, (n_exp, tm), 0)
    idxs, raw = [], []
    onehot = jnp.zeros((n_exp, tm), F32)
    for _ in range(TOP_K):
        m = jnp.max(masked, axis=0, keepdims=True)
        i = jnp.min(jnp.where(masked == m, erow, n_exp), axis=0, keepdims=True)
        hit = erow == i
        raw.append(jnp.sum(jnp.where(hit, scores, 0.0), axis=0, keepdims=True))
        masked = jnp.where(hit, neg, masked)
        onehot = jnp.where(hit, 1.0, onehot)
        idxs.append(i)
    total = raw[0]
    for r in raw[1:]:
        total = total + r
    before = _dot(onehot.astype(BF16), tri_ref[...]) + run_ref[:, 0:1]
    for kk in range(TOP_K):
        eidx_ref[kk:kk + 1, :] = idxs[kk]
        wts_ref[kk:kk + 1, :] = raw[kk] / total * ROUTED_SCALE
        rank_ref[kk:kk + 1, :] = jnp.sum(jnp.where(erow == idxs[kk], before, 0.0), axis=0, keepdims=True).astype(jnp.int32)
    run_ref[...] = run_ref[...] + jnp.sum(onehot, axis=1, keepdims=True)

    @pl.when(step == pl.num_programs(0) - 1)
    def _():
        cnt_ref[...] = run_ref[...].astype(jnp.int32)


def _slot_kernel(eidx_ref, rank_ref, pstart_ref, slot_ref):
    n_exp = pstart_ref.shape[0]
    tt = TOK_TILE
    erow = lax.broadcasted_iota(jnp.int32, (n_exp, tt), 0)
    pstart = pstart_ref[:, 0:1]
    for j in range(slot_ref.shape[0]):
        lanes = slice(j * tt, (j + 1) * tt)
        for kk in range(TOP_K):
            base = jnp.sum(jnp.where(erow == eidx_ref[kk:kk + 1, lanes], pstart, 0), axis=0, keepdims=True)
            slot_ref[j, kk:kk + 1, :] = base + rank_ref[kk:kk + 1, lanes]


def _slots(eidx, rank, pstart, n_tok):
    n_exp = pstart.shape[0]
    tt = TOK_TILE
    ntile = n_tok // tt
    per_step = max(g for g in range(1, 9) if ntile % g == 0)
    tok_spec = pl.BlockSpec((TOP_K, per_step * tt), lambda i: (0, i))
    return pl.pallas_call(
        _slot_kernel,
        grid=(ntile // per_step,),
        in_specs=[tok_spec, tok_spec, pl.BlockSpec((n_exp, LANES), lambda i: (0, 0))],
        out_specs=pl.BlockSpec((per_step, TOP_K, tt), lambda i: (i, 0, 0)),
        out_shape=jax.ShapeDtypeStruct((ntile, TOP_K, tt), jnp.int32),
        compiler_params=_cparams(("parallel",)),
        name="moe_slots",
    )(eidx, rank, jnp.broadcast_to(pstart[:, None], (n_exp, LANES)))


def _router(h2s, w_router, router_bias, n_tok, d, tm):
    n_exp = w_router.shape[1]
    nslab = d // (2 * LANES)
    tri = np.triu(np.ones((tm, tm), np.float32), 1)
    out_tok = lambda dt: jax.ShapeDtypeStruct((TOP_K, n_tok), dt)
    tok_spec = pl.BlockSpec((TOP_K, tm), lambda i: (0, i))
    return pl.pallas_call(
        functools.partial(_router_kernel, nslab=nslab, n_exp=n_exp),
        grid=(n_tok // tm,),
        in_specs=[pl.BlockSpec((tm * nslab, LANES), lambda i: (i, 0)),
                  pl.BlockSpec((n_exp, d), lambda i: (0, 0)),
                  pl.BlockSpec((n_exp, 1), lambda i: (0, 0)),
                  pl.BlockSpec((tm, tm), lambda i: (0, 0))],
        out_specs=[tok_spec, tok_spec, tok_spec, pl.BlockSpec((n_exp, LANES), lambda i: (0, 0))],
        out_shape=[out_tok(jnp.int32), out_tok(F32), out_tok(jnp.int32),
                   jax.ShapeDtypeStruct((n_exp, LANES), jnp.int32)],
        scratch_shapes=[pltpu.VMEM((n_exp, LANES), F32)],
        compiler_params=_cparams(("arbitrary",)),
        name="router_topk",
    )(h2s, w_router.T.astype(BF16), router_bias.reshape(n_exp, 1), jnp.asarray(tri, BF16))


def _dispatch_kernel(cnt_ref, pstart_ref, slot_hbm, h_ref, xs_hbm, slot_smem, zbuf, ssem, dsem, zsem, *, bm):
    i = pl.program_id(0)
    nslab = xs_hbm.shape[1]
    tt = h_ref.shape[0] // nslab
    n_exp = cnt_ref.shape[0]

    def slot_copy(tile, half):
        return pltpu.make_async_copy(slot_hbm.at[tile], slot_smem.at[half], ssem.at[half])

    @pl.when(i == 0)
    def _():
        slot_copy(0, 0).start()

    @pl.when(i == 0)
    def _():
        zbuf[...] = jnp.zeros_like(zbuf)

        def walk(e, start):
            cnt = cnt_ref[e]
            pad = lax.rem(bm - lax.rem(cnt, bm), bm)
            base = pstart_ref[e] + cnt
            size = bm // 2
            while size >= 1:
                take = (pad & size) != 0

                @pl.when(take)
                def _(base=base, size=size):
                    cp = pltpu.make_async_copy(zbuf.at[pl.ds(0, size)], xs_hbm.at[pl.ds(base, size)], zsem)
                    if start:
                        cp.start()
                    else:
                        cp.wait()

                base = base + jnp.where(take, size, 0)
                size //= 2

        def start_e(e, carry):
            walk(e, True)
            return carry

        def wait_e(e, carry):
            walk(e, False)
            return carry

        lax.fori_loop(0, n_exp, start_e, 0)
        lax.fori_loop(0, n_exp, wait_e, 0)

    half = lax.rem(i, 2)
    slot_copy(i, half).wait()

    @pl.when(i + 1 < pl.num_programs(0))
    def _():
        slot_copy(i + 1, 1 - half).start()

    def row_copy(t, kk):
        src = h_ref.at[pl.ds(pl.multiple_of(t * nslab, nslab), nslab), :]
        return pltpu.make_async_copy(src, xs_hbm.at[slot_smem[half, kk, t]], dsem)

    def issue(t, carry):
        for kk in range(TOP_K):
            row_copy(t, kk).start(priority=kk % 2)
        return carry

    lax.fori_loop(0, tt, issue, 0)

    def drain(t, carry):
        for kk in range(TOP_K):
            row_copy(t, kk).wait()
        return carry

    lax.fori_loop(0, tt, drain, 0)


def _dispatch(counts, pstart, slots, h2s, n_tok, nslab, n_rows, bm):
    return pl.pallas_call(
        functools.partial(_dispatch_kernel, bm=bm),
        grid_spec=pltpu.PrefetchScalarGridSpec(
            num_scalar_prefetch=2,
            grid=(n_tok // TOK_TILE,),
            in_specs=[pl.BlockSpec(memory_space=pl.ANY),
                      pl.BlockSpec((TOK_TILE * nslab, LANES), lambda i, c, p: (i, 0))],
            out_specs=pl.BlockSpec(memory_space=pl.ANY),
            scratch_shapes=[pltpu.SMEM((2, TOP_K, TOK_TILE), jnp.int32),
                            pltpu.VMEM((bm // 2, nslab, LANES), jnp.int32),
                            pltpu.SemaphoreType.DMA((2,)), pltpu.SemaphoreType.DMA, pltpu.SemaphoreType.DMA]),
        out_shape=jax.ShapeDtypeStruct((n_rows, nslab, LANES), jnp.int32),
        compiler_params=_cparams(("arbitrary",)),
        name="moe_dispatch",
    )(counts, pstart, slots, h2s)


def _experts_kernel(vis_ref, nv_ref, pstart_ref, pcnt_ref, nu_ref, xs_hbm, w1_hbm, w3_hbm, w2_hbm, os_hbm,
                    xbuf, obuf, w1s, w3s, w2s, w1b, w3b, w2b, xsem, osem, wsem, *, nslab, bm):
    j = pl.program_id(0)
    rows = bm * nslab
    n_used = nu_ref[0]
    n_visit = nv_ref[0]

    def weight_copies(jj, slot):
        e = vis_ref[jj]
        return (pltpu.make_async_copy(w1_hbm.at[e], w1s.at[slot], wsem.at[slot, 0]),
                pltpu.make_async_copy(w3_hbm.at[e], w3s.at[slot], wsem.at[slot, 1]),
                pltpu.make_async_copy(w2_hbm.at[e], w2s.at[slot], wsem.at[slot, 2]))

    def x_copy(g, slot):
        src = xs_hbm.at[pl.ds(pl.multiple_of(g * rows, rows), rows), :]
        return pltpu.make_async_copy(src, xbuf.at[pl.ds(pl.multiple_of(slot * rows, rows), rows), :], xsem.at[slot])

    def o_copy(g, slot):
        dst = os_hbm.at[pl.ds(pl.multiple_of(g * rows, rows), rows), :]
        return pltpu.make_async_copy(obuf.at[pl.ds(pl.multiple_of(slot * rows, rows), rows), :], dst, osem.at[slot])

    @pl.when(j == 0)
    def _():
        x_copy(0, 0).start()
        for cp in weight_copies(0, 0):
            cp.start(priority=1)

    @pl.when(j < n_visit)
    def _():
        e = vis_ref[j]
        g0 = pstart_ref[e] // bm
        wslot = lax.rem(j, 2)
        for cp in weight_copies(j, wslot):
            cp.wait()

        @pl.when(j + 1 < n_visit)
        def _():
            for cp in weight_copies(j + 1, 1 - wslot):
                cp.start(priority=1)

        w1b[...] = w1s[wslot].astype(BF16)
        w3b[...] = w3s[wslot].astype(BF16)
        w2b[...] = w2s[wslot].astype(BF16)

        def block(b, carry):
            g = g0 + b
            slot = lax.rem(g, 2)
            x_copy(g, slot).wait()

            @pl.when(g + 1 < n_used)
            def _():
                x_copy(g + 1, 1 - slot).start()

            @pl.when(g >= 2)
            def _():
                o_copy(g - 2, slot).wait()

            x = _load_rows(xbuf, bm, nslab, slot * rows)
            a = (_silu(_dot(x, w1b[...])) * _dot(x, w3b[...])).astype(BF16)
            _store_slabs(obuf, _pack_rows(_dot(a, w2b[...])), slot * rows)
            o_copy(g, slot).start()
            return carry

        lax.fori_loop(0, pcnt_ref[e] // bm, block, 0)

    @pl.when(j == pl.num_programs(0) - 1)
    def _():
        @pl.when(n_used >= 2)
        def _():
            o_copy(n_used - 2, lax.rem(n_used, 2)).wait()

        o_copy(n_used - 1, lax.rem(n_used - 1, 2)).wait()


def _experts(visit, n_visit, pstart, pcounts, n_used, xs2, w1, w3, w2, d, bm):
    n_exp, _, de = w1.shape
    nslab = d // (2 * LANES)
    hbm = pl.BlockSpec(memory_space=pl.ANY)
    return pl.pallas_call(
        functools.partial(_experts_kernel, nslab=nslab, bm=bm),
        grid_spec=pltpu.PrefetchScalarGridSpec(
            num_scalar_prefetch=5,
            grid=(n_exp,),
            in_specs=[hbm, hbm, hbm, hbm],
            out_specs=hbm,
            scratch_shapes=[pltpu.VMEM((2 * bm * nslab, LANES), jnp.int32),
                            pltpu.VMEM((2 * bm * nslab, LANES), jnp.int32),
                            pltpu.VMEM((2, d, de), F32), pltpu.VMEM((2, d, de), F32), pltpu.VMEM((2, de, d), F32),
                            pltpu.VMEM((d, de), BF16), pltpu.VMEM((d, de), BF16), pltpu.VMEM((de, d), BF16),
                            pltpu.SemaphoreType.DMA((2,)), pltpu.SemaphoreType.DMA((2,)),
                            pltpu.SemaphoreType.DMA((2, 3))]),
        out_shape=jax.ShapeDtypeStruct(xs2.shape, jnp.int32),
        compiler_params=_cparams(("arbitrary",)),
        name="routed_experts",
    )(visit, n_visit, pstart, pcounts, n_used, xs2, w1, w3, w2)


def _combine_kernel(slot_hbm, os_hbm, wt_ref, h_ref, ws1_ref, ws3_ref, ws2_ref, x1_ref, g2_ref, o_ref,
                    slot0, slot1, buf0, buf1, sh_ref, ssem, gsem, *, nslab, tile0):
    i = pl.program_id(0)
    tt = x1_ref.shape[0]
    last = pl.num_programs(0) - 1
    slots = (slot0, slot1)
    bufs = (buf0, buf1)

    def row_copy(half, t, kk, row):
        dst = bufs[half].at[pl.ds((kk * tt + t) * nslab, nslab), :]
        return pltpu.make_async_copy(os_hbm.at[row], dst, gsem.at[half])

    def slot_copy(tile, half):
        return pltpu.make_async_copy(slot_hbm.at[tile0 + tile], slots[half], ssem.at[half])

    def drain(half):
        def body(t, carry):
            for kk in range(TOP_K):
                row_copy(half, t, kk, 0).wait()
            return carry

        lax.fori_loop(0, tt, body, 0)

    @pl.when(i == 0)
    def _():
        first = slot_copy(0, 0)
        first.start()
        first.wait()

        def issue(t, carry):
            for kk in range(TOP_K):
                row_copy(0, t, kk, slot0[kk, t]).start(priority=kk % 2)
            return carry

        lax.fori_loop(0, tt, issue, 0)
        slot_copy(jnp.minimum(1, last), 1).start()

    def step(half):
        other = 1 - half
        slot_copy(jnp.minimum(i + 1, last), other).wait()
        slot_copy(jnp.minimum(i + 2, last), half).start()
        drain(half)

        h = _load_rows(h_ref, tt, nslab)
        a = (_silu(_dot(h, ws1_ref[...])) * _dot(h, ws3_ref[...])).astype(BF16)
        sh_ref[...] = _dot(a, ws2_ref[...])

        wt = wt_ref[...]
        g2 = _seq_rows(g2_ref[...], tt)
        hw = nslab * LANES
        per_slab = tt // nslab
        for s in range(nslab):
            y_lo = sh_ref[:, s * LANES:(s + 1) * LANES]
            y_hi = sh_ref[:, hw + s * LANES:hw + (s + 1) * LANES]
            for kk in range(TOP_K):
                lo, hi = _unpack_words(bufs[half][pl.ds(kk * tt * nslab + s, tt, stride=nslab), :])
                y_lo = y_lo + wt[:, kk:kk + 1] * lo
                y_hi = y_hi + wt[:, kk:kk + 1] * hi
            for y, c0 in ((y_lo, s * LANES), (y_hi, hw + s * LANES)):
                cols = slice(c0, c0 + LANES)
                o_ref[:, cols] = x1_ref[:, cols] + g2[:, cols] * y
            for t in range(s * per_slab, (s + 1) * per_slab):
                for kk in range(TOP_K):
                    row_copy(other, t, kk, slots[other][kk, t]).start(priority=kk % 2)

        @pl.when(i == last)
        def _():
            slot_copy(last, half).wait()
            drain(other)

    for half in (0, 1):
        @pl.when(lax.rem(i, 2) == half)
        def _(half=half):
            step(half)


def _combine(slots, os3, wts_t, h2s, ws1, ws3, ws2, x1, g2, seq_len, tok0):
    n, d = x1.shape
    nslab = d // (2 * LANES)
    tt = TOK_TILE
    tile0 = tok0 // tt
    const = lambda a: pl.BlockSpec(a.shape, lambda i: (0, 0), pipeline_mode=pl.Buffered(1))
    return pl.pallas_call(
        functools.partial(_combine_kernel, nslab=nslab, tile0=tile0),
        grid=(n // tt,),
        in_specs=[pl.BlockSpec(memory_space=pl.ANY),
                  pl.BlockSpec(memory_space=pl.ANY),
                  pl.BlockSpec((tt, TOP_K), lambda i: (tile0 + i, 0)),
                  pl.BlockSpec((tt * nslab, LANES), lambda i: (tile0 + i, 0)),
                  const(ws1), const(ws3), const(ws2),
                  pl.BlockSpec((tt, d), lambda i: (i, 0)),
                  _mod_spec(tt, seq_len, d)],
        out_specs=pl.BlockSpec((tt, d), lambda i: (i, 0)),
        out_shape=jax.ShapeDtypeStruct((n, d), F32),
        scratch_shapes=[pltpu.SMEM((TOP_K, tt), jnp.int32), pltpu.SMEM((TOP_K, tt), jnp.int32),
                        pltpu.VMEM((TOP_K * tt * nslab, LANES), jnp.int32),
                        pltpu.VMEM((TOP_K * tt * nslab, LANES), jnp.int32),
                        pltpu.VMEM((tt, d), F32),
                        pltpu.SemaphoreType.DMA((2,)), pltpu.SemaphoreType.DMA((2,))],
        compiler_params=_cparams(("arbitrary",)),
        name="moe_combine",
    )(slots, os3, wts_t, h2s, ws1, ws3, ws2, x1, g2)


def _mixer(x, mod, positions, s0, past_k, past_v, lb, p, w, moe_rows, moe_buf):
    bsz, seq_len, d = x.shape
    n = bsz * seq_len
    sh1, sc1, g1, sh2, sc2, g2 = mod
    tm = 256 if n % 256 == 0 else n
    x2 = x.reshape(n, d)
    n_kv = p["n_kv"]
    kd = n_kv * AT_HD
    hg_w = w["w_in_hg"].shape[1]
    qd = w["w_in_at"].shape[1] - 2 * kd
    h1, qn, kvn = _attn_in(x2, p["norm1_w"], sc1, sh1, w["w_in_at"], positions, p["q_norm_w"], p["k_norm_w"],
                           seq_len, qd, kd, min(seq_len, 512))
    tmm = 1024 if n % 1024 == 0 else tm
    z_hg = _matmul(h1, w["w_in_hg"], tmm, min(hg_w, 1024), name="w_in_hgrn")
    gates = _matmul(h1, w["w_in_gate"], tmm, min(2 * d, 1024), act="sigmoid", out_dtype=BF16, name="w_in_gates")

    o_hg, s_new = _hgrn(z_hg, lb, p["hg_norm_w"], s0, bsz, seq_len)

    if past_k is None:
        o_at = _swa_prompt(qn, kvn, p["attn_sinks"], bsz, seq_len, n_kv)
    else:
        o_at = _swa_sample(qn, kvn, past_k, past_v, p["attn_sinks"], bsz, seq_len, n_kv)

    x1, h2s = _merge(o_hg, o_at, gates, x2, g1, w["w_hg_out"], w["w_at_out"], w["w_o"],
                     p["norm2_w"], sc2, sh2, seq_len, tm, moe_rows, moe_buf)
    kv3 = kvn.reshape(bsz, seq_len, 2 * kd)
    return x1, h2s, s_new, kv3[:, :, :kd], kv3[:, :, kd:]


def kernel(x_prompt, x_sample, cache_k, cache_v, state_hgrn, c_prompt, c_sample, norm1_w, norm2_w, w_ada, b_ada,
           w_in, hg_lower_bounds, hg_norm_w, q_norm_w, k_norm_w, attn_sinks, w_hg_out, w_at_out, w_o, w_router,
           router_bias, w_exp_gate, w_exp_up, w_exp_down, w_sh_gate, w_sh_up, w_sh_down):
    depth = norm1_w.shape[0]
    assert depth == 1, "single trunk layer"
    bp, tp, d = x_prompt.shape
    bs, ts, _ = x_sample.shape
    window, n_kv = cache_k.shape[2], cache_k.shape[3]
    kd = n_kv * AT_HD
    hg_dim = w_hg_out.shape[1]
    qd = w_at_out.shape[1]
    n_exp = w_router.shape[2]
    nslab = d // (2 * LANES)
    l = 0

    lbs = jnp.cumsum(jax.nn.softmax(hg_lower_bounds.astype(F32), axis=0), axis=0)
    win = w_in[l]
    w = {
        "w_in_hg": win[:, :4 * hg_dim].astype(BF16),
        "w_in_at": win[:, 4 * hg_dim:4 * hg_dim + qd + 2 * kd].astype(BF16),
        "w_in_gate": win[:, 4 * hg_dim + qd + 2 * kd:].astype(BF16),
        "w_hg_out": w_hg_out[l].astype(BF16),
        "w_at_out": w_at_out[l].astype(BF16),
        "w_o": w_o[l].astype(BF16),
    }
    p = {"norm1_w": norm1_w[l], "norm2_w": norm2_w[l], "hg_norm_w": hg_norm_w[l], "q_norm_w": q_norm_w[l],
         "k_norm_w": k_norm_w[l], "attn_sinks": attn_sinks[l], "n_kv": n_kv}

    c_all = jnp.concatenate([c_prompt, c_sample], axis=0)
    mod_all = _ada(c_all, w_ada[l], b_ada[l])
    mod_all = mod_all.reshape(bp + bs, 6, 1, d)
    mod_p = tuple(mod_all[:bp, j] for j in range(6))
    mod_s = tuple(mod_all[bp:, j] for j in range(6))

    pos_p = jnp.arange(tp, dtype=jnp.int32)
    pos_s = PAST_LEN + jnp.arange(ts, dtype=jnp.int32)
    s0_p = jnp.zeros((bp,) + state_hgrn.shape[2:], F32)
    n_p, n_s = bp * tp, bs * ts
    n_tok = n_p + n_s
    x1_p, h2_p, sp, kp, vp = _mixer(x_prompt, mod_p, pos_p, s0_p, None, None, lbs[l], p, w, (0, n_tok), None)
    pk = cache_k[l].reshape(bs, window, kd)
    pv = cache_v[l].reshape(bs, window, kd)
    x1_s, h2s, ss, ks, vs = _mixer(x_sample, mod_s, pos_s, state_hgrn[l], pk, pv, lbs[l], p, w, (n_p, n_tok), h2_p)

    tr = 256 if n_tok % 256 == 0 else TOK_TILE
    eidx, wts, rank, counts = _router(h2s, w_router[l], router_bias[l], n_tok, d, tr)

    bm = MOE_ROWS
    counts = counts[:, 0]
    pcounts = (counts + bm - 1) // bm * bm
    pend = jnp.cumsum(pcounts)
    pstart = pend - pcounts
    nb = -(-(n_tok * TOP_K) // bm) + n_exp
    slots = _slots(eidx, rank, pstart, n_tok)
    n_used = (pend[-1:] // bm).astype(jnp.int32)

    xs = _dispatch(counts, pstart, slots, h2s, n_tok, nslab, nb * bm, bm)
    visit = jnp.argsort(counts == 0, stable=True).astype(jnp.int32)
    n_visit = jnp.sum(counts > 0).astype(jnp.int32).reshape(1)
    os_ = _experts(visit, n_visit, pstart, pcounts, n_used, xs.reshape(nb * bm * nslab, LANES),
                   w_exp_gate[l], w_exp_up[l], w_exp_down[l], d, bm)
    os3 = os_.reshape(nb * bm, nslab, LANES)
    wts_t = wts.T
    ws = (w_sh_gate[l].astype(BF16), w_sh_up[l].astype(BF16), w_sh_down[l].astype(BF16))
    y_p = _combine(slots, os3, wts_t, h2s, *ws, x1_p, mod_p[5], tp, 0)
    y_s = _combine(slots, os3, wts_t, h2s, *ws, x1_s, mod_s[5], ts, n_p)

    def cache_out(a, b_, t):
        return a[:, t - window:].reshape(1, b_, window, n_kv, AT_HD)

    new_k_p = cache_out(kp, bp, tp)
    new_v_p = cache_out(vp, bp, tp)
    keys_s = jnp.concatenate([pk, ks], axis=1)
    vals_s = jnp.concatenate([pv, vs], axis=1)
    new_k_s = cache_out(keys_s, bs, window + ts)
    new_v_s = cache_out(vals_s, bs, window + ts)
    return (y_p.reshape(bp, tp, d), y_s.reshape(bs, ts, d), new_k_p, new_v_p, sp[None],
            new_k_s, new_v_s, ss[None])
```

```python
import functools
import math

import numpy as np
import jax
import jax.numpy as jnp
from jax import lax
from jax.experimental import pallas as pl
from jax.experimental.pallas import tpu as pltpu

EPS = 1e-6
CHUNK = 64
HG_CHUNK = 128
HG_DK = 128
AT_HD = 64
ROPE_DIM = 16
ROPE_THETA = 500000.0
TOP_K = 8
N_GROUPS = 8
TOPK_GROUPS = 4
ROUTED_SCALE = 2.5
PAST_LEN = 2048

LANES = 128
MOE_ROWS = 256
TOK_TILE = 128
VMEM_LIMIT = 56 * 1024 * 1024

F32 = jnp.float32
BF16 = jnp.bfloat16


def _cparams(semantics, vmem=VMEM_LIMIT):
    return pltpu.CompilerParams(dimension_semantics=semantics, vmem_limit_bytes=vmem)


def _sigmoid(x):
    return 1.0 / (1.0 + jnp.exp(-x))


def _silu(x):
    return x * _sigmoid(x)


def _dot(a, b):
    return jnp.dot(a, b, preferred_element_type=F32)


def _dot_nt(a, b):
    return lax.dot_general(a, b, (((1,), (1,)), ((), ())), preferred_element_type=F32)


def _dot_tn(a, b):
    return lax.dot_general(a, b, (((0,), (0,)), ((), ())), preferred_element_type=F32)


def _split_bf16(x):
    hi = x.astype(BF16)
    lo = (x - hi.astype(F32)).astype(BF16)
    return hi, lo


def _seq_rows(m, rows):
    s, _, d = m.shape
    if s == 1:
        return m[0]
    return jnp.broadcast_to(m, (s, rows // s, d)).reshape(rows, d)


def _mod_spec(tm, seq_len, d):
    if tm <= seq_len:
        return pl.BlockSpec((1, 1, d), lambda i: ((i * tm) // seq_len, 0, 0))
    s = tm // seq_len
    return pl.BlockSpec((s, 1, d), lambda i: (i, 0, 0))


def _ada_kernel(c_ref, w_ref, b_ref, o_ref):
    s = _silu(c_ref[...]).astype(BF16)
    o_ref[...] = _dot(s, w_ref[...].astype(BF16)) + b_ref[...]


def _ada(c, w, b):
    n, d = c.shape
    m = w.shape[1]
    tn = min(m, 1024)
    return pl.pallas_call(
        _ada_kernel,
        grid=(m // tn,),
        in_specs=[pl.BlockSpec((n, d), lambda j: (0, 0)),
                  pl.BlockSpec((d, tn), lambda j: (0, j)),
                  pl.BlockSpec((1, tn), lambda j: (0, j))],
        out_specs=pl.BlockSpec((n, tn), lambda j: (0, j)),
        out_shape=jax.ShapeDtypeStruct((n, m), F32),
        compiler_params=_cparams(("parallel",)),
        name="ada_mod",
    )(c, w, b.reshape(1, m))


def _mm_kernel(x_ref, w_ref, o_ref, *, act):
    y = _dot(x_ref[...], w_ref[...])
    if act == "sigmoid":
        y = _sigmoid(y)
    o_ref[...] = y.astype(o_ref.dtype)


def _matmul(x, w, tm, tn, act=None, out_dtype=F32, name="matmul"):
    n, k = x.shape
    m = w.shape[1]
    return pl.pallas_call(
        functools.partial(_mm_kernel, act=act),
        grid=(n // tm, m // tn),
        in_specs=[pl.BlockSpec((tm, k), lambda i, j: (i, 0)),
                  pl.BlockSpec((k, tn), lambda i, j: (0, j))],
        out_specs=pl.BlockSpec((tm, tn), lambda i, j: (i, j)),
        out_shape=jax.ShapeDtypeStruct((n, m), out_dtype),
        compiler_params=_cparams(("parallel", "arbitrary")),
        name=name,
    )(x, w)


def _hgrn_tables(c):
    nlev = int(math.log2(c))
    t = np.arange(c)[:, None]
    s = np.arange(c)[None, :]
    seg = []
    for l in range(1, nlev + 1):
        b = 1 << l
        seg.append(((s >= (t // b) * b) & (s <= t)).astype(np.float32))
        seg.append(((s > t) & (s <= (t // b) * b + b - 1)).astype(np.float32))
    masks = [(t == s).astype(np.float32)]
    for l in range(nlev):
        b = 1 << l
        masks.append(((t // (2 * b) == s // (2 * b)) & (t % (2 * b) >= b) & (s % (2 * b) < b)).astype(np.float32))
    return np.concatenate(seg, axis=0), np.stack(masks, axis=0)


def _hgrn_kernel(q_ref, f_ref, i_ref, g_ref, lb_ref, nw_ref, s0_ref, seg_ref, msk_ref,
                 o_ref, sn_ref, st_ref, *, c, nchunks, hb):
    nlev = int(math.log2(c))
    tstep = pl.program_id(2)

    @pl.when(tstep == 0)
    def _():
        for hh in range(hb):
            st_ref[hh] = s0_ref[0, hh].T

    nw = nw_ref[...]
    lb = lb_ref[...]
    head = lambda a, hh: a[:, hh * HG_DK:(hh + 1) * HG_DK]

    def decays(l, g_hi, g_lo):
        seg = seg_ref[2 * (l - 1) * c:2 * l * c, :]
        e = _dot(seg, g_hi) + _dot(seg, g_lo)
        return e[:c, :], e[c:, :]

    def block_decays(a_inc, b):
        width = a_inc.shape[1]
        fill = lambda r: jnp.broadcast_to(a_inc[r:r + 1, :], (b, width))
        before = jnp.concatenate([jnp.zeros((b, width), F32)] + [fill(j * b - 1) for j in range(1, c // b)], axis=0)
        last = jnp.concatenate([fill((j + 1) * b - 1) for j in range(c // b)], axis=0)
        return a_inc - before, last - a_inc

    def chunk(ci, carry):
        rows = pl.ds(pl.multiple_of(ci * c, c), c)
        q = q_ref[rows, :]
        f = lb + (1.0 - lb) * _sigmoid(f_ref[rows, :])
        g = jnp.log(f)
        k = 1.0 - f
        g_hi, g_lo = _split_bf16(g)
        qb = q.astype(BF16)
        kb = k.astype(BF16)
        a_inc, v_end = decays(nlev, g_hi, g_lo)
        scores = [jnp.where(msk_ref[0] > 0, _dot_nt(head(qb, hh), head(kb, hh)), 0.0) for hh in range(hb)]
        for l in range(nlev):
            if l == 0:
                ql, kl = (q * f).astype(BF16), kb
            else:
                wl, vl = block_decays(a_inc, 1 << l) if (1 << l) % 8 == 0 else decays(l, g_hi, g_lo)
                ql = (q * jnp.exp(wl)).astype(BF16)
                kl = (k * jnp.exp(vl)).astype(BF16)
            for hh in range(hb):
                scores[hh] = scores[hh] + jnp.where(msk_ref[l + 1] > 0, _dot_nt(head(ql, hh), head(kl, hh)), 0.0)
        qa = (q * jnp.exp(a_inc)).astype(BF16)
        k_end = (k * jnp.exp(v_end)).astype(BF16)
        vb = i_ref[rows, :].astype(BF16)
        carry_decay = jnp.exp(a_inc[c - 1:c, :])
        gate = _silu(g_ref[rows, :])
        for hh in range(hb):
            st = st_ref[hh]
            o = _dot_nt(head(qa, hh), st.astype(BF16)) + _dot(scores[hh].astype(BF16), head(vb, hh))
            st_ref[hh] = st * head(carry_decay, hh) + _dot_tn(head(vb, hh), head(k_end, hh))
            on = o * lax.rsqrt(jnp.mean(o * o, axis=-1, keepdims=True) + EPS) * nw
            o_ref[rows, hh * HG_DK:(hh + 1) * HG_DK] = (on * head(gate, hh)).astype(o_ref.dtype)
        return carry

    lax.fori_loop(0, nchunks, chunk, 0)

    @pl.when(tstep == pl.num_programs(2) - 1)
    def _():
        for hh in range(hb):
            sn_ref[0, hh] = st_ref[hh].T


def _hgrn(z_hg, lb, norm_w, s0, bsz, seq_len):
    n, w4 = z_hg.shape
    nh = w4 // (4 * HG_DK)
    hb = min(nh, 8)
    ng = nh // hb
    c = min(HG_CHUNK, seq_len)
    tb = min(seq_len, 512)
    nt = seq_len // tb
    seg, masks = _hgrn_tables(c)

    def col(part):
        return pl.BlockSpec((tb, hb * HG_DK), lambda b, h, t: (b * nt + t, part * ng + h))

    return pl.pallas_call(
        functools.partial(_hgrn_kernel, c=c, nchunks=tb // c, hb=hb),
        grid=(bsz, ng, nt),
        in_specs=[col(0), col(1), col(2), col(3),
                  pl.BlockSpec((1, hb * HG_DK), lambda b, h, t: (0, h)),
                  pl.BlockSpec((1, HG_DK), lambda b, h, t: (0, 0)),
                  pl.BlockSpec((1, hb, HG_DK, HG_DK), lambda b, h, t: (b, h, 0, 0)),
                  pl.BlockSpec(seg.shape, lambda b, h, t: (0, 0)),
                  pl.BlockSpec(masks.shape, lambda b, h, t: (0, 0, 0))],
        out_specs=[pl.BlockSpec((tb, hb * HG_DK), lambda b, h, t: (b * nt + t, h)),
                   pl.BlockSpec((1, hb, HG_DK, HG_DK), lambda b, h, t: (b, h, 0, 0))],
        out_shape=[jax.ShapeDtypeStruct((n, nh * HG_DK), BF16),
                   jax.ShapeDtypeStruct((bsz, nh, HG_DK, HG_DK), F32)],
        scratch_shapes=[pltpu.VMEM((hb, HG_DK, HG_DK), F32)],
        compiler_params=_cparams(("parallel", "parallel", "arbitrary")),
        name="hgrn2",
    )(z_hg, z_hg, z_hg, z_hg, lb.reshape(1, -1), norm_w.reshape(1, HG_DK), s0,
      jnp.asarray(seg, BF16), jnp.asarray(masks, F32))


def _rope_tables(positions):
    half = ROPE_DIM // 2
    inv_freq = ROPE_THETA ** (-jnp.arange(0, ROPE_DIM, 2, dtype=F32) / ROPE_DIM)
    ang = positions.astype(F32)[:, None] * inv_freq[None, :]
    cos, sin = jnp.cos(ang), jnp.sin(ang)
    t = positions.shape[0]
    rest = AT_HD - ROPE_DIM
    c64 = jnp.concatenate([cos, cos, jnp.ones((t, rest), F32)], axis=1)
    sa64 = jnp.concatenate([-sin, jnp.zeros((t, half + rest), F32)], axis=1)
    sb64 = jnp.concatenate([jnp.zeros((t, half), F32), sin, jnp.zeros((t, rest), F32)], axis=1)
    rep = LANES // AT_HD
    return tuple(jnp.tile(a, (1, rep)) for a in (c64, sa64, sb64))


def _attn_in_kernel(x_ref, nw_ref, sc_ref, sh_ref, w_ref, cos_ref, sa_ref, sb_ref, qw_ref, kw_ref, hsum_ref,
                    hexp_ref, h_ref, q_ref, kv_ref, *, qd, kd):
    half = ROPE_DIM // 2
    x = x_ref[...]
    tm = x.shape[0]
    y = x * lax.rsqrt(jnp.mean(x * x, axis=-1, keepdims=True) + EPS) * nw_ref[...]
    h = (y * (1.0 + _seq_rows(sc_ref[...], tm)) + _seq_rows(sh_ref[...], tm)).astype(BF16)
    h_ref[...] = h
    z = _dot(h, w_ref[...])
    cos, sa, sb = cos_ref[...], sa_ref[...], sb_ref[...]

    def norm_rope(x, w, reps):
        width = x.shape[1]
        hi, lo = _split_bf16(x * x)
        hsum = hsum_ref[:width, :]
        ss = _dot(hi, hsum) + _dot(lo, hsum)
        r_hi, r_lo = _split_bf16(lax.rsqrt(ss * (1.0 / AT_HD) + EPS))
        hexp = hexp_ref[:, :width]
        xn = x * (_dot(r_hi, hexp) + _dot(r_lo, hexp)) * w
        tile = lambda a: jnp.concatenate([a] * reps, axis=1) if reps > 1 else a
        return (xn * tile(cos) + pltpu.roll(xn, width - half, 1) * tile(sa)
                + pltpu.roll(xn, half, 1) * tile(sb))

    q = norm_rope(z[:, :qd], qw_ref[...], qd // LANES)
    k = norm_rope(z[:, qd:qd + kd], kw_ref[...], kd // LANES)
    q_ref[...] = q.astype(q_ref.dtype)
    kv_ref[:, :kd] = k
    kv_ref[:, kd:] = z[:, qd + kd:]


def _attn_in(x, norm_w, sc, sh, w_at, positions, q_norm_w, k_norm_w, seq_len, qd, kd, tm):
    n, d = x.shape
    cos, sa, sb = _rope_tables(positions)
    nt = seq_len // tm
    hsum = np.zeros((qd, LANES), np.float32)
    hsum[np.arange(qd), np.arange(qd) // AT_HD] = 1.0
    tab = pl.BlockSpec((tm, LANES), lambda i: (i % nt, 0))
    const = lambda shape: pl.BlockSpec(shape, lambda i: (0,) * len(shape), pipeline_mode=pl.Buffered(1))
    return pl.pallas_call(
        functools.partial(_attn_in_kernel, qd=qd, kd=kd),
        grid=(n // tm,),
        in_specs=[pl.BlockSpec((tm, d), lambda i: (i, 0)), const((1, d)),
                  _mod_spec(tm, seq_len, d), _mod_spec(tm, seq_len, d),
                  const(w_at.shape), tab, tab, tab, const((1, qd)), const((1, kd)),
                  const((qd, LANES)), const((LANES, qd))],
        out_specs=[pl.BlockSpec((tm, d), lambda i: (i, 0)),
                   pl.BlockSpec((tm, qd), lambda i: (i, 0)),
                   pl.BlockSpec((tm, 2 * kd), lambda i: (i, 0))],
        out_shape=[jax.ShapeDtypeStruct((n, d), BF16),
                   jax.ShapeDtypeStruct((n, qd), BF16),
                   jax.ShapeDtypeStruct((n, 2 * kd), F32)],
        compiler_params=_cparams(("parallel",)),
        name="attn_in",
    )(x, norm_w.reshape(1, d), sc, sh, w_at, cos, sa, sb, jnp.tile(q_norm_w, qd // AT_HD).reshape(1, qd),
      jnp.tile(k_norm_w, kd // AT_HD).reshape(1, kd), jnp.asarray(hsum, BF16), jnp.asarray(hsum.T, BF16))


def _attend(q, keys, vals, sinks_ref, col_valid, o_ref, n_kv, group, row0=0):
    tq = q.shape[0]
    n_heads = n_kv * group
    batch = 16
    for h0 in range(0, n_heads, batch):
        heads = range(h0, min(h0 + batch, n_heads))
        scores = []
        for h in heads:
            j = h // group
            s = _dot_nt(q[:, h * AT_HD:(h + 1) * AT_HD], keys[:, j * AT_HD:(j + 1) * AT_HD]) * (AT_HD ** -0.5)
            scores.append(s if col_valid is None else jnp.where(col_valid, s, -jnp.inf))
        probs = []
        for h, s in zip(heads, scores):
            sink = sinks_ref[h]
            m = jnp.maximum(jnp.max(s, axis=-1, keepdims=True), sink)
            e = jnp.exp(s - m)
            den = jnp.sum(e, axis=-1, keepdims=True) + jnp.exp(sink - m)
            probs.append((e / den).astype(BF16))
        for h, p in zip(heads, probs):
            j = h // group
            out = _dot(p, vals[:, j * AT_HD:(j + 1) * AT_HD]).astype(o_ref.dtype)
            o_ref[row0:row0 + tq, h * AT_HD:(h + 1) * AT_HD] = out


def _swa_prompt_kernel(sinks_ref, q_ref, *rest, n_kv, group, w_chunks, cps):
    kv_refs, o_ref = rest[:-1], rest[-1]
    kd = n_kv * AT_HD
    first = pl.program_id(1) * cps
    blocks = [r[...] for r in kv_refs]
    for ci in range(cps):
        window = blocks[ci:ci + w_chunks + 1]
        keys = jnp.concatenate([b[:, :kd] for b in window], axis=0).astype(BF16)
        vals = jnp.concatenate([b[:, kd:] for b in window], axis=0).astype(BF16)
        col_chunk = lax.broadcasted_iota(jnp.int32, (1, keys.shape[0]), 1) // CHUNK
        col_valid = (col_chunk + first + ci - w_chunks) >= 0
        _attend(q_ref[ci * CHUNK:(ci + 1) * CHUNK, :], keys, vals, sinks_ref, col_valid, o_ref, n_kv, group,
                row0=ci * CHUNK)


def _swa_prompt(qn, kvn, sinks, bsz, seq_len, n_kv):
    n, qd = qn.shape
    kd = n_kv * AT_HD
    nc = seq_len // CHUNK
    group = qd // AT_HD // n_kv
    w_chunks = 2
    cps = 2 if nc % 2 == 0 else 1
    steps = nc // cps

    def kv_spec(j):
        return pl.BlockSpec((CHUNK, 2 * kd), lambda b, c, s: (b * nc + jnp.maximum(c * cps - w_chunks + j, 0), 0))

    nkv_blocks = w_chunks + cps
    return pl.pallas_call(
        functools.partial(_swa_prompt_kernel, n_kv=n_kv, group=group, w_chunks=w_chunks, cps=cps),
        grid_spec=pltpu.PrefetchScalarGridSpec(
            num_scalar_prefetch=1,
            grid=(bsz, steps),
            in_specs=[pl.BlockSpec((cps * CHUNK, qd), lambda b, c, s: (b * steps + c, 0))]
                     + [kv_spec(j) for j in range(nkv_blocks)],
            out_specs=pl.BlockSpec((cps * CHUNK, qd), lambda b, c, s: (b * steps + c, 0))),
        out_shape=jax.ShapeDtypeStruct((n, qd), BF16),
        compiler_params=_cparams(("parallel", "arbitrary")),
        name="swa_prompt",
    )(sinks, qn, *([kvn] * nkv_blocks))


def _swa_sample_kernel(sinks_ref, q_ref, pk_ref, pv_ref, kv_ref, o_ref, *, n_kv, group):
    kd = n_kv * AT_HD
    kv = kv_ref[...]
    keys = jnp.concatenate([pk_ref[0], kv[:, :kd]], axis=0).astype(BF16)
    vals = jnp.concatenate([pv_ref[0], kv[:, kd:]], axis=0).astype(BF16)
    _attend(q_ref[...], keys, vals, sinks_ref, None, o_ref, n_kv, group)


def _swa_sample(qn, kvn, past_k, past_v, sinks, bsz, seq_len, n_kv):
    n, qd = qn.shape
    kd = n_kv * AT_HD
    window = past_k.shape[1]
    group = qd // AT_HD // n_kv
    return pl.pallas_call(
        functools.partial(_swa_sample_kernel, n_kv=n_kv, group=group),
        grid_spec=pltpu.PrefetchScalarGridSpec(
            num_scalar_prefetch=1,
            grid=(bsz,),
            in_specs=[pl.BlockSpec((seq_len, qd), lambda b, s: (b, 0)),
                      pl.BlockSpec((1, window, kd), lambda b, s: (b, 0, 0)),
                      pl.BlockSpec((1, window, kd), lambda b, s: (b, 0, 0)),
                      pl.BlockSpec((seq_len, 2 * kd), lambda b, s: (b, 0))],
            out_specs=pl.BlockSpec((seq_len, qd), lambda b, s: (b, 0))),
        out_shape=jax.ShapeDtypeStruct((n, qd), BF16),
        compiler_params=_cparams(("parallel",)),
        name="swa_sample",
    )(sinks, qn, past_k, past_v, kvn)


HI_MASK = np.uint32(0xFFFF0000)


def _pack_rows(x):
    half = x.shape[1] // 2
    bits = lambda a: lax.bitcast_convert_type(a.astype(BF16).astype(F32), jnp.uint32)
    word = (bits(x[:, half:]) & HI_MASK) | (bits(x[:, :half]) >> 16)
    return lax.bitcast_convert_type(word, jnp.int32)


def _unpack_words(w):
    u = lax.bitcast_convert_type(w, jnp.uint32)
    return lax.bitcast_convert_type(u << 16, F32), lax.bitcast_convert_type(u & HI_MASK, F32)


def _store_slabs(ref, words, row0=0):
    rows, width = words.shape
    nslab = width // LANES
    for s in range(nslab):
        ref[pl.ds(row0 + s, rows, stride=nslab), :] = words[:, s * LANES:(s + 1) * LANES]


def _load_rows(ref, rows, nslab, row0=0):
    lo, hi = [], []
    for s in range(nslab):
        a, b = _unpack_words(ref[pl.ds(row0 + s, rows, stride=nslab), :])
        lo.append(a)
        hi.append(b)
    return jnp.concatenate(lo + hi, axis=1).astype(BF16)


def _merge_kernel(ohg_ref, oat_ref, ga_ref, gb_ref, x_ref, g1_ref, whg_ref, wat_ref, wo_ref,
                  nw_ref, sc_ref, sh_ref, *rest):
    x1_ref, h2_ref = rest[-2:]
    tm = x_ref.shape[0]
    merged = ga_ref[...] * _dot(ohg_ref[...], whg_ref[...]) + gb_ref[...] * _dot(oat_ref[...], wat_ref[...])
    mix = _dot(merged.astype(BF16), wo_ref[...])
    x1 = x_ref[...] + _seq_rows(g1_ref[...], tm) * mix
    x1_ref[...] = x1
    y = x1 * lax.rsqrt(jnp.mean(x1 * x1, axis=-1, keepdims=True) + EPS) * nw_ref[...]
    h2 = y * (1.0 + _seq_rows(sc_ref[...], tm)) + _seq_rows(sh_ref[...], tm)
    _store_slabs(h2_ref, _pack_rows(h2))


def _merge(o_hg, o_at, gates, x, g1, w_hg_out, w_at_out, w_o, norm2_w, sc2, sh2, seq_len, tm, moe_rows, moe_buf):
    n, d = x.shape
    hd = o_hg.shape[1]
    ad = o_at.shape[1]
    nslab = d // (2 * LANES)
    tok0, n_all = moe_rows
    tile0 = tok0 // tm
    const = lambda shape: pl.BlockSpec(shape, lambda i: (0,) * len(shape), pipeline_mode=pl.Buffered(1))
    mod = lambda: _mod_spec(tm, seq_len, d)
    in_specs = [pl.BlockSpec((tm, hd), lambda i: (i, 0)),
                pl.BlockSpec((tm, ad), lambda i: (i, 0)),
                pl.BlockSpec((tm, d), lambda i: (i, 0)),
                pl.BlockSpec((tm, d), lambda i: (i, 1)),
                pl.BlockSpec((tm, d), lambda i: (i, 0)),
                mod(), const((hd, d)), const((ad, d)), const((d, d)), const((1, d)), mod(), mod()]
    args = [o_hg, o_at, gates, gates, x, g1, w_hg_out, w_at_out, w_o, norm2_w.reshape(1, d), sc2, sh2]
    aliases = {}
    if moe_buf is not None:
        in_specs.append(pl.BlockSpec(memory_space=pl.ANY))
        args.append(moe_buf)
        aliases = {len(args) - 1: 1}
    return pl.pallas_call(
        _merge_kernel,
        grid=(n // tm,),
        in_specs=in_specs,
        out_specs=[pl.BlockSpec((tm, d), lambda i: (i, 0)),
                   pl.BlockSpec((tm * nslab, LANES), lambda i: (tile0 + i, 0))],
        out_shape=[jax.ShapeDtypeStruct((n, d), F32),
                   jax.ShapeDtypeStruct((n_all * nslab, LANES), jnp.int32)],
        input_output_aliases=aliases,
        compiler_params=_cparams(("parallel",)),
        name="merge_norm2",
    )(*args)


def _router_kernel(h_ref, wr_ref, bias_ref, tri_ref, eidx_ref, wts_ref, rank_ref, cnt_ref, run_ref,
                   *, nslab, n_exp):
    tm = h_ref.shape[0] // nslab
    gsz = n_exp // N_GROUPS
    step = pl.program_id(0)

    @pl.when(step == 0)
    def _():
        run_ref[...] = jnp.zeros_like(run_ref)

    h = _load_rows(h_ref, tm, nslab)
    scores = _sigmoid(_dot_nt(wr_ref[...], h))
    choice = scores + bias_ref[...]
    neg = -jnp.inf
    row = lax.broadcasted_iota(jnp.int32, (gsz, tm), 0)

    gscore = []
    for gi in range(N_GROUPS):
        cg = choice[gi * gsz:(gi + 1) * gsz, :]
        m1 = jnp.max(cg, axis=0, keepdims=True)
        i1 = jnp.min(jnp.where(cg == m1, row, gsz), axis=0, keepdims=True)
        m2 = jnp.max(jnp.where(row == i1, neg, cg), axis=0, keepdims=True)
        gscore.append(m1 + m2)
    gs = jnp.concatenate(gscore, axis=0)
    grow = lax.broadcasted_iota(jnp.int32, (N_GROUPS, tm), 0)
    gsel = jnp.zeros((N_GROUPS, tm), F32)
    for _ in range(TOPK_GROUPS):
        gm = jnp.max(gs, axis=0, keepdims=True)
        gi = jnp.min(jnp.where(gs == gm, grow, N_GROUPS), axis=0, keepdims=True)
        hit = grow == gi
        gsel = jnp.where(hit, 1.0, gsel)
        gs = jnp.where(hit, neg, gs)
    masked = jnp.concatenate(
        [jnp.where(gsel[gi:gi + 1, :] > 0, choice[gi * gsz:(gi + 1) * gsz, :], neg) for gi in range(N_GROUPS)],
        axis=0)

    erow = lax.broadcasted_iota(jnp.int32, (n_exp, tm), 0)
    idxs, raw = [], []
    onehot = jnp.zeros((n_exp, tm), F32)
    for _ in range(TOP_K):
        m = jnp.max(masked, axis=0, keepdims=True)
        i = jnp.min(jnp.where(masked == m, erow, n_exp), axis=0, keepdims=True)
        hit = erow == i
        raw.append(jnp.sum(jnp.where(hit, scores, 0.0), axis=0, keepdims=True))
        masked = jnp.where(hit, neg, masked)
        onehot = jnp.where(hit, 1.0, onehot)
        idxs.append(i)
    total = raw[0]
    for r in raw[1:]:
        total = total + r
    before = _dot(onehot.astype(BF16), tri_ref[...]) + run_ref[:, 0:1]
    for kk in range(TOP_K):
        eidx_ref[kk:kk + 1, :] = idxs[kk]
        wts_ref[kk:kk + 1, :] = raw[kk] / total * ROUTED_SCALE
        rank_ref[kk:kk + 1, :] = jnp.sum(jnp.where(erow == idxs[kk], before, 0.0), axis=0, keepdims=True).astype(jnp.int32)
    run_ref[...] = run_ref[...] + jnp.sum(onehot, axis=1, keepdims=True)

    @pl.when(step == pl.num_programs(0) - 1)
    def _():
        cnt_ref[...] = run_ref[...].astype(jnp.int32)


def _slot_kernel(eidx_ref, rank_ref, pstart_ref, slot_ref):
    n_exp = pstart_ref.shape[0]
    tt = TOK_TILE
    erow = lax.broadcasted_iota(jnp.int32, (n_exp, tt), 0)
    pstart = pstart_ref[:, 0:1]
    for j in range(slot_ref.shape[0]):
        lanes = slice(j * tt, (j + 1) * tt)
        for kk in range(TOP_K):
            base = jnp.sum(jnp.where(erow == eidx_ref[kk:kk + 1, lanes], pstart, 0), axis=0, keepdims=True)
            slot_ref[j, kk:kk + 1, :] = base + rank_ref[kk:kk + 1, lanes]


def _slots(eidx, rank, pstart, n_tok):
    n_exp = pstart.shape[0]
    tt = TOK_TILE
    ntile = n_tok // tt
    per_step = max(g for g in range(1, 9) if ntile % g == 0)
    tok_spec = pl.BlockSpec((TOP_K, per_step * tt), lambda i: (0, i))
    return pl.pallas_call(
        _slot_kernel,
        grid=(ntile // per_step,),
        in_specs=[tok_spec, tok_spec, pl.BlockSpec((n_exp, LANES), lambda i: (0, 0))],
        out_specs=pl.BlockSpec((per_step, TOP_K, tt), lambda i: (i, 0, 0)),
        out_shape=jax.ShapeDtypeStruct((ntile, TOP_K, tt), jnp.int32),
        compiler_params=_cparams(("parallel",)),
        name="moe_slots",
    )(eidx, rank, jnp.broadcast_to(pstart[:, None], (n_exp, LANES)))


def _router(h2s, w_router, router_bias, n_tok, d, tm):
    n_exp = w_router.shape[1]
    nslab = d // (2 * LANES)
    tri = np.triu(np.ones((tm, tm), np.float32), 1)
    out_tok = lambda dt: jax.ShapeDtypeStruct((TOP_K, n_tok), dt)
    tok_spec = pl.BlockSpec((TOP_K, tm), lambda i: (0, i))
    return pl.pallas_call(
        functools.partial(_router_kernel, nslab=nslab, n_exp=n_exp),
        grid=(n_tok // tm,),
        in_specs=[pl.BlockSpec((tm * nslab, LANES), lambda i: (i, 0)),
                  pl.BlockSpec((n_exp, d), lambda i: (0, 0)),
                  pl.BlockSpec((n_exp, 1), lambda i: (0, 0)),
                  pl.BlockSpec((tm, tm), lambda i: (0, 0))],
        out_specs=[tok_spec, tok_spec, tok_spec, pl.BlockSpec((n_exp, LANES), lambda i: (0, 0))],
        out_shape=[out_tok(jnp.int32), out_tok(F32), out_tok(jnp.int32),
                   jax.ShapeDtypeStruct((n_exp, LANES), jnp.int32)],
        scratch_shapes=[pltpu.VMEM((n_exp, LANES), F32)],
        compiler_params=_cparams(("arbitrary",)),
        name="router_topk",
    )(h2s, w_router.T.astype(BF16), router_bias.reshape(n_exp, 1), jnp.asarray(tri, BF16))


def _dispatch_kernel(cnt_ref, pstart_ref, slot_hbm, h_ref, xs_hbm, slot_smem, zbuf, ssem, dsem, zsem, *, bm):
    i = pl.program_id(0)
    nslab = xs_hbm.shape[1]
    tt = h_ref.shape[0] // nslab
    n_exp = cnt_ref.shape[0]

    def slot_copy(tile, half):
        return pltpu.make_async_copy(slot_hbm.at[tile], slot_smem.at[half], ssem.at[half])

    @pl.when(i == 0)
    def _():
        slot_copy(0, 0).start()

    @pl.when(i == 0)
    def _():
        zbuf[...] = jnp.zeros_like(zbuf)

        def walk(e, start):
            cnt = cnt_ref[e]
            pad = lax.rem(bm - lax.rem(cnt, bm), bm)
            base = pstart_ref[e] + cnt
            size = bm // 2
            while size >= 1:
                take = (pad & size) != 0

                @pl.when(take)
                def _(base=base, size=size):
                    cp = pltpu.make_async_copy(zbuf.at[pl.ds(0, size)], xs_hbm.at[pl.ds(base, size)], zsem)
                    if start:
                        cp.start()
                    else:
                        cp.wait()

                base = base + jnp.where(take, size, 0)
                size //= 2

        def start_e(e, carry):
            walk(e, True)
            return carry

        def wait_e(e, carry):
            walk(e, False)
            return carry

        lax.fori_loop(0, n_exp, start_e, 0)
        lax.fori_loop(0, n_exp, wait_e, 0)

    half = lax.rem(i, 2)
    slot_copy(i, half).wait()

    @pl.when(i + 1 < pl.num_programs(0))
    def _():
        slot_copy(i + 1, 1 - half).start()

    def row_copy(t, kk):
        src = h_ref.at[pl.ds(pl.multiple_of(t * nslab, nslab), nslab), :]
        return pltpu.make_async_copy(src, xs_hbm.at[slot_smem[half, kk, t]], dsem)

    def issue(t, carry):
        for kk in range(TOP_K):
            row_copy(t, kk).start(priority=kk % 2)
        return carry

    lax.fori_loop(0, tt, issue, 0)

    def drain(t, carry):
        for kk in range(TOP_K):
            row_copy(t, kk).wait()
        return carry

    lax.fori_loop(0, tt, drain, 0)


def _dispatch(counts, pstart, slots, h2s, n_tok, nslab, n_rows, bm):
    return pl.pallas_call(
        functools.partial(_dispatch_kernel, bm=bm),
        grid_spec=pltpu.PrefetchScalarGridSpec(
            num_scalar_prefetch=2,
            grid=(n_tok // TOK_TILE,),
            in_specs=[pl.BlockSpec(memory_space=pl.ANY),
                      pl.BlockSpec((TOK_TILE * nslab, LANES), lambda i, c, p: (i, 0))],
            out_specs=pl.BlockSpec(memory_space=pl.ANY),
            scratch_shapes=[pltpu.SMEM((2, TOP_K, TOK_TILE), jnp.int32),
                            pltpu.VMEM((bm // 2, nslab, LANES), jnp.int32),
                            pltpu.SemaphoreType.DMA((2,)), pltpu.SemaphoreType.DMA, pltpu.SemaphoreType.DMA]),
        out_shape=jax.ShapeDtypeStruct((n_rows, nslab, LANES), jnp.int32),
        compiler_params=_cparams(("arbitrary",)),
        name="moe_dispatch",
    )(counts, pstart, slots, h2s)


def _experts_kernel(vis_ref, nv_ref, pstart_ref, pcnt_ref, nu_ref, xs_hbm, w1_hbm, w3_hbm, w2_hbm, os_hbm,
                    xbuf, obuf, w1s, w3s, w2s, w1b, w3b, w2b, xsem, osem, wsem, *, nslab, bm):
    j = pl.program_id(0)
    rows = bm * nslab
    n_used = nu_ref[0]
    n_visit = nv_ref[0]

    def weight_copies(jj, slot):
        e = vis_ref[jj]
        return (pltpu.make_async_copy(w1_hbm.at[e], w1s.at[slot], wsem.at[slot, 0]),
                pltpu.make_async_copy(w3_hbm.at[e], w3s.at[slot], wsem.at[slot, 1]),
                pltpu.make_async_copy(w2_hbm.at[e], w2s.at[slot], wsem.at[slot, 2]))

    def x_copy(g, slot):
        src = xs_hbm.at[pl.ds(pl.multiple_of(g * rows, rows), rows), :]
        return pltpu.make_async_copy(src, xbuf.at[pl.ds(pl.multiple_of(slot * rows, rows), rows), :], xsem.at[slot])

    def o_copy(g, slot):
        dst = os_hbm.at[pl.ds(pl.multiple_of(g * rows, rows), rows), :]
        return pltpu.make_async_copy(obuf.at[pl.ds(pl.multiple_of(slot * rows, rows), rows), :], dst, osem.at[slot])

    @pl.when(j == 0)
    def _():
        x_copy(0, 0).start()
        for cp in weight_copies(0, 0):
            cp.start(priority=1)

    @pl.when(j < n_visit)
    def _():
        e = vis_ref[j]
        g0 = pstart_ref[e] // bm
        wslot = lax.rem(j, 2)
        for cp in weight_copies(j, wslot):
            cp.wait()

        @pl.when(j + 1 < n_visit)
        def _():
            for cp in weight_copies(j + 1, 1 - wslot):
                cp.start(priority=1)

        w1b[...] = w1s[wslot].astype(BF16)
        w3b[...] = w3s[wslot].astype(BF16)
        w2b[...] = w2s[wslot].astype(BF16)

        def block(b, carry):
            g = g0 + b
            slot = lax.rem(g, 2)
            x_copy(g, slot).wait()

            @pl.when(g + 1 < n_used)
            def _():
                x_copy(g + 1, 1 - slot).start()

            @pl.when(g >= 2)
            def _():
                o_copy(g - 2, slot).wait()

            x = _load_rows(xbuf, bm, nslab, slot * rows)
            a = (_silu(_dot(x, w1b[...])) * _dot(x, w3b[...])).astype(BF16)
            _store_slabs(obuf, _pack_rows(_dot(a, w2b[...])), slot * rows)
            o_copy(g, slot).start()
            return carry

        lax.fori_loop(0, pcnt_ref[e] // bm, block, 0)

    @pl.when(j == pl.num_programs(0) - 1)
    def _():
        @pl.when(n_used >= 2)
        def _():
            o_copy(n_used - 2, lax.rem(n_used, 2)).wait()

        o_copy(n_used - 1, lax.rem(n_used - 1, 2)).wait()


def _experts(visit, n_visit, pstart, pcounts, n_used, xs2, w1, w3, w2, d, bm):
    n_exp, _, de = w1.shape
    nslab = d // (2 * LANES)
    hbm = pl.BlockSpec(memory_space=pl.ANY)
    return pl.pallas_call(
        functools.partial(_experts_kernel, nslab=nslab, bm=bm),
        grid_spec=pltpu.PrefetchScalarGridSpec(
            num_scalar_prefetch=5,
            grid=(n_exp,),
            in_specs=[hbm, hbm, hbm, hbm],
            out_specs=hbm,
            scratch_shapes=[pltpu.VMEM((2 * bm * nslab, LANES), jnp.int32),
                            pltpu.VMEM((2 * bm * nslab, LANES), jnp.int32),
                            pltpu.VMEM((2, d, de), F32), pltpu.VMEM((2, d, de), F32), pltpu.VMEM((2, de, d), F32),
                            pltpu.VMEM((d, de), BF16), pltpu.VMEM((d, de), BF16), pltpu.VMEM((de, d), BF16),
                            pltpu.SemaphoreType.DMA((2,)), pltpu.SemaphoreType.DMA((2,)),
                            pltpu.SemaphoreType.DMA((2, 3))]),
        out_shape=jax.ShapeDtypeStruct(xs2.shape, jnp.int32),
        compiler_params=_cparams(("arbitrary",)),
        name="routed_experts",
    )(visit, n_visit, pstart, pcounts, n_used, xs2, w1, w3, w2)


def _combine_kernel(slot_hbm, os_hbm, wt_ref, h_ref, ws1_ref, ws3_ref, ws2_ref, x1_ref, g2_ref, o_ref,
                    slot0, slot1, buf0, buf1, sh_ref, ssem, gsem, *, nslab, tile0):
    i = pl.program_id(0)
    tt = x1_ref.shape[0]
    last = pl.num_programs(0) - 1
    slots = (slot0, slot1)
    bufs = (buf0, buf1)

    def row_copy(half, t, kk, row):
        dst = bufs[half].at[pl.ds((kk * tt + t) * nslab, nslab), :]
        return pltpu.make_async_copy(os_hbm.at[row], dst, gsem.at[half])

    def slot_copy(tile, half):
        return pltpu.make_async_copy(slot_hbm.at[tile0 + tile], slots[half], ssem.at[half])

    def drain(half):
        def body(t, carry):
            for kk in range(TOP_K):
                row_copy(half, t, kk, 0).wait()
            return carry

        lax.fori_loop(0, tt, body, 0)

    @pl.when(i == 0)
    def _():
        first = slot_copy(0, 0)
        first.start()
        first.wait()

        def issue(t, carry):
            for kk in range(TOP_K):
                row_copy(0, t, kk, slot0[kk, t]).start(priority=kk % 2)
            return carry

        lax.fori_loop(0, tt, issue, 0)
        slot_copy(jnp.minimum(1, last), 1).start()

    def step(half):
        other = 1 - half
        slot_copy(jnp.minimum(i + 1, last), other).wait()
        slot_copy(jnp.minimum(i + 2, last), half).start()
        drain(half)

        h = _load_rows(h_ref, tt, nslab)
        a = (_silu(_dot(h, ws1_ref[...])) * _dot(h, ws3_ref[...])).astype(BF16)
        sh_ref[...] = _dot(a, ws2_ref[...])

        wt = wt_ref[...]
        g2 = _seq_rows(g2_ref[...], tt)
        hw = nslab * LANES
        per_slab = tt // nslab
        for s in range(nslab):
            y_lo = sh_ref[:, s * LANES:(s + 1) * LANES]
            y_hi = sh_ref[:, hw + s * LANES:hw + (s + 1) * LANES]
            for kk in range(TOP_K):
                lo, hi = _unpack_words(bufs[half][pl.ds(kk * tt * nslab + s, tt, stride=nslab), :])
                y_lo = y_lo + wt[:, kk:kk + 1] * lo
                y_hi = y_hi + wt[:, kk:kk + 1] * hi
            for y, c0 in ((y_lo, s * LANES), (y_hi, hw + s * LANES)):
                cols = slice(c0, c0 + LANES)
                o_ref[:, cols] = x1_ref[:, cols] + g2[:, cols] * y
            for t in range(s * per_slab, (s + 1) * per_slab):
                for kk in range(TOP_K):
                    row_copy(other, t, kk, slots[other][kk, t]).start(priority=kk % 2)

        @pl.when(i == last)
        def _():
            slot_copy(last, half).wait()
            drain(other)

    for half in (0, 1):
        @pl.when(lax.rem(i, 2) == half)
        def _(half=half):
            step(half)


def _combine(slots, os3, wts_t, h2s, ws1, ws3, ws2, x1, g2, seq_len, tok0):
    n, d = x1.shape
    nslab = d // (2 * LANES)
    tt = TOK_TILE
    tile0 = tok0 // tt
    const = lambda a: pl.BlockSpec(a.shape, lambda i: (0, 0), pipeline_mode=pl.Buffered(1))
    return pl.pallas_call(
        functools.partial(_combine_kernel, nslab=nslab, tile0=tile0),
        grid=(n // tt,),
        in_specs=[pl.BlockSpec(memory_space=pl.ANY),
                  pl.BlockSpec(memory_space=pl.ANY),
                  pl.BlockSpec((tt, TOP_K), lambda i: (tile0 + i, 0)),
                  pl.BlockSpec((tt * nslab, LANES), lambda i: (tile0 + i, 0)),
                  const(ws1), const(ws3), const(ws2),
                  pl.BlockSpec((tt, d), lambda i: (i, 0)),
                  _mod_spec(tt, seq_len, d)],
        out_specs=pl.BlockSpec((tt, d), lambda i: (i, 0)),
        out_shape=jax.ShapeDtypeStruct((n, d), F32),
        scratch_shapes=[pltpu.SMEM((TOP_K, tt), jnp.int32), pltpu.SMEM((TOP_K, tt), jnp.int32),
                        pltpu.VMEM((TOP_K * tt * nslab, LANES), jnp.int32),
                        pltpu.VMEM((TOP_K * tt * nslab, LANES), jnp.int32),
                        pltpu.VMEM((tt, d), F32),
                        pltpu.SemaphoreType.DMA((2,)), pltpu.SemaphoreType.DMA((2,))],
        compiler_params=_cparams(("arbitrary",)),
        name="moe_combine",
    )(slots, os3, wts_t, h2s, ws1, ws3, ws2, x1, g2)


def _mixer(x, mod, positions, s0, past_k, past_v, lb, p, w, moe_rows, moe_buf):
    bsz, seq_len, d = x.shape
    n = bsz * seq_len
    sh1, sc1, g1, sh2, sc2, g2 = mod
    tm = 256 if n % 256 == 0 else n
    x2 = x.reshape(n, d)
    n_kv = p["n_kv"]
    kd = n_kv * AT_HD
    hg_w = w["w_in_hg"].shape[1]
    qd = w["w_in_at"].shape[1] - 2 * kd
    h1, qn, kvn = _attn_in(x2, p["norm1_w"], sc1, sh1, w["w_in_at"], positions, p["q_norm_w"], p["k_norm_w"],
                           seq_len, qd, kd, min(seq_len, 512))
    tmm = 1024 if n % 1024 == 0 else tm
    z_hg = _matmul(h1, w["w_in_hg"], tmm, min(hg_w, 1024), name="w_in_hgrn")
    gates = _matmul(h1, w["w_in_gate"], tmm, min(2 * d, 1024), act="sigmoid", out_dtype=BF16, name="w_in_gates")

    o_hg, s_new = _hgrn(z_hg, lb, p["hg_norm_w"], s0, bsz, seq_len)

    if past_k is None:
        o_at = _swa_prompt(qn, kvn, p["attn_sinks"], bsz, seq_len, n_kv)
    else:
        o_at = _swa_sample(qn, kvn, past_k, past_v, p["attn_sinks"], bsz, seq_len, n_kv)

    x1, h2s = _merge(o_hg, o_at, gates, x2, g1, w["w_hg_out"], w["w_at_out"], w["w_o"],
                     p["norm2_w"], sc2, sh2, seq_len, tm, moe_rows, moe_buf)
    kv3 = kvn.reshape(bsz, seq_len, 2 * kd)
    return x1, h2s, s_new, kv3[:, :, :kd], kv3[:, :, kd:]


def kernel(x_prompt, x_sample, cache_k, cache_v, state_hgrn, c_prompt, c_sample, norm1_w, norm2_w, w_ada, b_ada,
           w_in, hg_lower_bounds, hg_norm_w, q_norm_w, k_norm_w, attn_sinks, w_hg_out, w_at_out, w_o, w_router,
           router_bias, w_exp_gate, w_exp_up, w_exp_down, w_sh_gate, w_sh_up, w_sh_down):
    depth = norm1_w.shape[0]
    assert depth == 1, "single trunk layer"
    bp, tp, d = x_prompt.shape
    bs, ts, _ = x_sample.shape
    window, n_kv = cache_k.shape[2], cache_k.shape[3]
    kd = n_kv * AT_HD
    hg_dim = w_hg_out.shape[1]
    qd = w_at_out.shape[1]
    n_exp = w_router.shape[2]
    nslab = d // (2 * LANES)
    l = 0

    lbs = jnp.cumsum(jax.nn.softmax(hg_lower_bounds.astype(F32), axis=0), axis=0)
    win = w_in[l]
    w = {
        "w_in_hg": win[:, :4 * hg_dim].astype(BF16),
        "w_in_at": win[:, 4 * hg_dim:4 * hg_dim + qd + 2 * kd].astype(BF16),
        "w_in_gate": win[:, 4 * hg_dim + qd + 2 * kd:].astype(BF16),
        "w_hg_out": w_hg_out[l].astype(BF16),
        "w_at_out": w_at_out[l].astype(BF16),
        "w_o": w_o[l].astype(BF16),
    }
    p = {"norm1_w": norm1_w[l], "norm2_w": norm2_w[l], "hg_norm_w": hg_norm_w[l], "q_norm_w": q_norm_w[l],
         "k_norm_w": k_norm_w[l], "attn_sinks": attn_sinks[l], "n_kv": n_kv}

    c_all = jnp.concatenate([c_prompt, c_sample], axis=0)
    mod_all = _ada(c_all, w_ada[l], b_ada[l])
    mod_all = mod_all.reshape(bp + bs, 6, 1, d)
    mod_p = tuple(mod_all[:bp, j] for j in range(6))
    mod_s = tuple(mod_all[bp:, j] for j in range(6))

    pos_p = jnp.arange(tp, dtype=jnp.int32)
    pos_s = PAST_LEN + jnp.arange(ts, dtype=jnp.int32)
    s0_p = jnp.zeros((bp,) + state_hgrn.shape[2:], F32)
    n_p, n_s = bp * tp, bs * ts
    n_tok = n_p + n_s
    x1_p, h2_p, sp, kp, vp = _mixer(x_prompt, mod_p, pos_p, s0_p, None, None, lbs[l], p, w, (0, n_tok), None)
    pk = cache_k[l].reshape(bs, window, kd)
    pv = cache_v[l].reshape(bs, window, kd)
    x1_s, h2s, ss, ks, vs = _mixer(x_sample, mod_s, pos_s, state_hgrn[l], pk, pv, lbs[l], p, w, (n_p, n_tok), h2_p)

    tr = 256 if n_tok % 256 == 0 else TOK_TILE
    eidx, wts, rank, counts = _router(h2s, w_router[l], router_bias[l], n_tok, d, tr)

    bm = MOE_ROWS
    counts = counts[:, 0]
    pcounts = (counts + bm - 1) // bm * bm
    pend = jnp.cumsum(pcounts)
    pstart = pend - pcounts
    nb = -(-(n_tok * TOP_K) // bm) + n_exp
    slots = _slots(eidx, rank, pstart, n_tok)
    n_used = (pend[-1:] // bm).astype(jnp.int32)

    xs = _dispatch(counts, pstart, slots, h2s, n_tok, nslab, nb * bm, bm)
    visit = jnp.argsort(counts == 0, stable=True).astype(jnp.int32)
    n_visit = jnp.sum(counts > 0).astype(jnp.int32).reshape(1)
    os_ = _experts(visit, n_visit, pstart, pcounts, n_used, xs.reshape(nb * bm * nslab, LANES),
                   w_exp_gate[l], w_exp_up[l], w_exp_down[l], d, bm)
    os3 = os_.reshape(nb * bm, nslab, LANES)
    wts_t = wts.T
    ws = (w_sh_gate[l].astype(BF16), w_sh_up[l].astype(BF16), w_sh_down[l].astype(BF16))
    y_p = _combine(slots, os3, wts_t, h2s, *ws, x1_p, mod_p[5], tp, 0)
    y_s = _combine(slots, os3, wts_t, h2s, *ws, x1_s, mod_s[5], ts, n_p)

    def cache_out(a, b_, t):
        return a[:, t - window:].reshape(1, b_, window, n_kv, AT_HD)

    new_k_p = cache_out(kp, bp, tp)
    new_v_p = cache_out(vp, bp, tp)
    keys_s = jnp.concatenate([pk, ks], axis=1)
    vals_s = jnp.concatenate([pv, vs], axis=1)
    new_k_s = cache_out(keys_s, bs, window + ts)
    new_v_s = cache_out(vals_s, bs, window + ts)
    return (y_p.reshape(bp, tp, d), y_s.reshape(bs, ts, d), new_k_p, new_v_p, sp[None],
            new_k_s, new_v_s, ss[None])
```

```python
import functools
import math

import numpy as np
import jax
import jax.numpy as jnp
from jax import lax
from jax.experimental import pallas as pl
from jax.experimental.pallas import tpu as pltpu

EPS = 1e-6
CHUNK = 64
HG_CHUNK = 128
HG_DK = 128
AT_HD = 64
ROPE_DIM = 16
ROPE_THETA = 500000.0
TOP_K = 8
N_GROUPS = 8
TOPK_GROUPS = 4
ROUTED_SCALE = 2.5
PAST_LEN = 2048

LANES = 128
MOE_ROWS = 256
TOK_TILE = 128
VMEM_LIMIT = 56 * 1024 * 1024

F32 = jnp.float32
BF16 = jnp.bfloat16


def _cparams(semantics, vmem=VMEM_LIMIT):
    return pltpu.CompilerParams(dimension_semantics=semantics, vmem_limit_bytes=vmem)


def _sigmoid(x):
    return 1.0 / (1.0 + jnp.exp(-x))


def _silu(x):
    return x * _sigmoid(x)


def _dot(a, b):
    return jnp.dot(a, b, preferred_element_type=F32)


def _dot_nt(a, b):
    return lax.dot_general(a, b, (((1,), (1,)), ((), ())), preferred_element_type=F32)


def _dot_tn(a, b):
    return lax.dot_general(a, b, (((0,), (0,)), ((), ())), preferred_element_type=F32)


def _split_bf16(x):
    hi = x.astype(BF16)
    lo = (x - hi.astype(F32)).astype(BF16)
    return hi, lo


def _seq_rows(m, rows):
    s, _, d = m.shape
    if s == 1:
        return m[0]
    return jnp.broadcast_to(m, (s, rows // s, d)).reshape(rows, d)


def _mod_spec(tm, seq_len, d):
    if tm <= seq_len:
        return pl.BlockSpec((1, 1, d), lambda i: ((i * tm) // seq_len, 0, 0))
    s = tm // seq_len
    return pl.BlockSpec((s, 1, d), lambda i: (i, 0, 0))


def _ada_kernel(c_ref, w_ref, b_ref, o_ref):
    s = _silu(c_ref[...]).astype(BF16)
    o_ref[...] = _dot(s, w_ref[...].astype(BF16)) + b_ref[...]


def _ada(c, w, b):
    n, d = c.shape
    m = w.shape[1]
    tn = min(m, 1024)
    return pl.pallas_call(
        _ada_kernel,
        grid=(m // tn,),
        in_specs=[pl.BlockSpec((n, d), lambda j: (0, 0)),
                  pl.BlockSpec((d, tn), lambda j: (0, j)),
                  pl.BlockSpec((1, tn), lambda j: (0, j))],
        out_specs=pl.BlockSpec((n, tn), lambda j: (0, j)),
        out_shape=jax.ShapeDtypeStruct((n, m), F32),
        compiler_params=_cparams(("parallel",)),
        name="ada_mod",
    )(c, w, b.reshape(1, m))


def _mm_kernel(x_ref, w_ref, o_ref, *, act):
    y = _dot(x_ref[...], w_ref[...])
    if act == "sigmoid":
        y = _sigmoid(y)
    o_ref[...] = y.astype(o_ref.dtype)


def _matmul(x, w, tm, tn, act=None, out_dtype=F32, name="matmul"):
    n, k = x.shape
    m = w.shape[1]
    return pl.pallas_call(
        functools.partial(_mm_kernel, act=act),
        grid=(n // tm, m // tn),
        in_specs=[pl.BlockSpec((tm, k), lambda i, j: (i, 0)),
                  pl.BlockSpec((k, tn), lambda i, j: (0, j))],
        out_specs=pl.BlockSpec((tm, tn), lambda i, j: (i, j)),
        out_shape=jax.ShapeDtypeStruct((n, m), out_dtype),
        compiler_params=_cparams(("parallel", "arbitrary")),
        name=name,
    )(x, w)


def _hgrn_tables(c):
    nlev = int(math.log2(c))
    t = np.arange(c)[:, None]
    s = np.arange(c)[None, :]
    seg = []
    for l in range(1, nlev + 1):
        b = 1 << l
        seg.append(((s >= (t // b) * b) & (s <= t)).astype(np.float32))
        seg.append(((s > t) & (s <= (t // b) * b + b - 1)).astype(np.float32))
    masks = [(t == s).astype(np.float32)]
    for l in range(nlev):
        b = 1 << l
        masks.append(((t // (2 * b) == s // (2 * b)) & (t % (2 * b) >= b) & (s % (2 * b) < b)).astype(np.float32))
    return np.concatenate(seg, axis=0), np.stack(masks, axis=0)


def _hgrn_kernel(h_ref, w_ref, lb_ref, nw_ref, s0_ref, seg_ref, msk_ref,
                 o_ref, sn_ref, st_ref, z_ref, *, c, nchunks, hb):
    nlev = int(math.log2(c))
    tstep = pl.program_id(1)

    width = hb * HG_DK
    h = h_ref[...]
    for part in range(4):
        z_ref[:, part * width:(part + 1) * width] = _dot(h, w_ref[:, part * width:(part + 1) * width])
    q_ref, f_ref, i_ref, g_ref = (z_ref.at[:, part * width:(part + 1) * width] for part in range(4))

    @pl.when(tstep == 0)
    def _():
        for hh in range(hb):
            st_ref[hh] = s0_ref[0, hh].T

    nw = nw_ref[...]
    lb = lb_ref[...]
    head = lambda a, hh: a[:, hh * HG_DK:(hh + 1) * HG_DK]

    def decays(l, g_hi, g_lo):
        seg = seg_ref[2 * (l - 1) * c:2 * l * c, :]
        e = _dot(seg, g_hi) + _dot(seg, g_lo)
        return e[:c, :], e[c:, :]

    def block_decays(a_inc, b):
        width = a_inc.shape[1]
        fill = lambda r: jnp.broadcast_to(a_inc[r:r + 1, :], (b, width))
        before = jnp.concatenate([jnp.zeros((b, width), F32)] + [fill(j * b - 1) for j in range(1, c // b)], axis=0)
        last = jnp.concatenate([fill((j + 1) * b - 1) for j in range(c // b)], axis=0)
        return a_inc - before, last - a_inc

    def chunk(ci, carry):
        rows = pl.ds(pl.multiple_of(ci * c, c), c)
        q = q_ref[rows, :]
        f = lb + (1.0 - lb) * _sigmoid(f_ref[rows, :])
        g = jnp.log(f)
        k = 1.0 - f
        g_hi, g_lo = _split_bf16(g)
        qb = q.astype(BF16)
        kb = k.astype(BF16)
        a_inc, v_end = decays(nlev, g_hi, g_lo)
        scores = [jnp.where(msk_ref[0] > 0, _dot_nt(head(qb, hh), head(kb, hh)), 0.0) for hh in range(hb)]
        for l in range(nlev):
            if l == 0:
                ql, kl = (q * f).astype(BF16), kb
            else:
                wl, vl = block_decays(a_inc, 1 << l) if (1 << l) % 8 == 0 else decays(l, g_hi, g_lo)
                ql = (q * jnp.exp(wl)).astype(BF16)
                kl = (k * jnp.exp(vl)).astype(BF16)
            for hh in range(hb):
                scores[hh] = scores[hh] + jnp.where(msk_ref[l + 1] > 0, _dot_nt(head(ql, hh), head(kl, hh)), 0.0)
        qa = (q * jnp.exp(a_inc)).astype(BF16)
        k_end = (k * jnp.exp(v_end)).astype(BF16)
        vb = i_ref[rows, :].astype(BF16)
        carry_decay = jnp.exp(a_inc[c - 1:c, :])
        gate = _silu(g_ref[rows, :])
        for hh in range(hb):
            st = st_ref[hh]
            o = _dot_nt(head(qa, hh), st.astype(BF16)) + _dot(scores[hh].astype(BF16), head(vb, hh))
            st_ref[hh] = st * head(carry_decay, hh) + _dot_tn(head(vb, hh), head(k_end, hh))
            on = o * lax.rsqrt(jnp.mean(o * o, axis=-1, keepdims=True) + EPS) * nw
            o_ref[rows, hh * HG_DK:(hh + 1) * HG_DK] = (on * head(gate, hh)).astype(o_ref.dtype)
        return carry

    lax.fori_loop(0, nchunks, chunk, 0)

    @pl.when(tstep == pl.num_programs(1) - 1)
    def _():
        for hh in range(hb):
            sn_ref[0, hh] = st_ref[hh].T


def _hgrn(h1, w_hg, lb, norm_w, s0, bsz, seq_len):
    n, d = h1.shape
    w4 = w_hg.shape[1]
    hb = w4 // (4 * HG_DK)
    c = min(HG_CHUNK, seq_len)
    tb = min(seq_len, 512)
    nt = seq_len // tb
    seg, masks = _hgrn_tables(c)
    const = lambda shape: pl.BlockSpec(shape, lambda b, t: (0,) * len(shape), pipeline_mode=pl.Buffered(1))
    return pl.pallas_call(
        functools.partial(_hgrn_kernel, c=c, nchunks=tb // c, hb=hb),
        grid=(bsz, nt),
        in_specs=[pl.BlockSpec((tb, d), lambda b, t: (b * nt + t, 0)),
                  const((d, w4)), const((1, hb * HG_DK)), const((1, HG_DK)),
                  pl.BlockSpec((1, hb, HG_DK, HG_DK), lambda b, t: (b, 0, 0, 0)),
                  const(seg.shape), const(masks.shape)],
        out_specs=[pl.BlockSpec((tb, hb * HG_DK), lambda b, t: (b * nt + t, 0)),
                   pl.BlockSpec((1, hb, HG_DK, HG_DK), lambda b, t: (b, 0, 0, 0))],
        out_shape=[jax.ShapeDtypeStruct((n, hb * HG_DK), BF16),
                   jax.ShapeDtypeStruct((bsz, hb, HG_DK, HG_DK), F32)],
        scratch_shapes=[pltpu.VMEM((hb, HG_DK, HG_DK), F32), pltpu.VMEM((tb, w4), F32)],
        compiler_params=_cparams(("parallel", "arbitrary")),
        name="hgrn2",
    )(h1, w_hg, lb.reshape(1, -1), norm_w.reshape(1, HG_DK), s0, jnp.asarray(seg, BF16), jnp.asarray(masks, F32))


def _rope_tables(positions):
    half = ROPE_DIM // 2
    inv_freq = ROPE_THETA ** (-jnp.arange(0, ROPE_DIM, 2, dtype=F32) / ROPE_DIM)
    ang = positions.astype(F32)[:, None] * inv_freq[None, :]
    cos, sin = jnp.cos(ang), jnp.sin(ang)
    t = positions.shape[0]
    rest = AT_HD - ROPE_DIM
    c64 = jnp.concatenate([cos, cos, jnp.ones((t, rest), F32)], axis=1)
    sa64 = jnp.concatenate([-sin, jnp.zeros((t, half + rest), F32)], axis=1)
    sb64 = jnp.concatenate([jnp.zeros((t, half), F32), sin, jnp.zeros((t, rest), F32)], axis=1)
    rep = LANES // AT_HD
    return tuple(jnp.tile(a, (1, rep)) for a in (c64, sa64, sb64))


def _attn_in_kernel(x_ref, nw_ref, sc_ref, sh_ref, w_ref, cos_ref, sa_ref, sb_ref, qw_ref, kw_ref, hsum_ref,
                    hexp_ref, h_ref, q_ref, kv_ref, *, qd, kd):
    half = ROPE_DIM // 2
    x = x_ref[...]
    tm = x.shape[0]
    y = x * lax.rsqrt(jnp.mean(x * x, axis=-1, keepdims=True) + EPS) * nw_ref[...]
    h = (y * (1.0 + _seq_rows(sc_ref[...], tm)) + _seq_rows(sh_ref[...], tm)).astype(BF16)
    h_ref[...] = h
    z = _dot(h, w_ref[...])
    cos, sa, sb = cos_ref[...], sa_ref[...], sb_ref[...]

    def norm_rope(x, w, reps):
        width = x.shape[1]
        hi, lo = _split_bf16(x * x)
        hsum = hsum_ref[:width, :]
        ss = _dot(hi, hsum) + _dot(lo, hsum)
        r_hi, r_lo = _split_bf16(lax.rsqrt(ss * (1.0 / AT_HD) + EPS))
        hexp = hexp_ref[:, :width]
        xn = x * (_dot(r_hi, hexp) + _dot(r_lo, hexp)) * w
        tile = lambda a: jnp.concatenate([a] * reps, axis=1) if reps > 1 else a
        return (xn * tile(cos) + pltpu.roll(xn, width - half, 1) * tile(sa)
                + pltpu.roll(xn, half, 1) * tile(sb))

    q = norm_rope(z[:, :qd], qw_ref[...], qd // LANES)
    k = norm_rope(z[:, qd:qd + kd], kw_ref[...], kd // LANES)
    q_ref[...] = q.astype(q_ref.dtype)
    kv_ref[:, :kd] = k
    kv_ref[:, kd:] = z[:, qd + kd:]


def _attn_in(x, norm_w, sc, sh, w_at, positions, q_norm_w, k_norm_w, seq_len, qd, kd, tm):
    n, d = x.shape
    cos, sa, sb = _rope_tables(positions)
    nt = seq_len // tm
    hsum = np.zeros((qd, LANES), np.float32)
    hsum[np.arange(qd), np.arange(qd) // AT_HD] = 1.0
    tab = pl.BlockSpec((tm, LANES), lambda i: (i % nt, 0))
    const = lambda shape: pl.BlockSpec(shape, lambda i: (0,) * len(shape), pipeline_mode=pl.Buffered(1))
    return pl.pallas_call(
        functools.partial(_attn_in_kernel, qd=qd, kd=kd),
        grid=(n // tm,),
        in_specs=[pl.BlockSpec((tm, d), lambda i: (i, 0)), const((1, d)),
                  _mod_spec(tm, seq_len, d), _mod_spec(tm, seq_len, d),
                  const(w_at.shape), tab, tab, tab, const((1, qd)), const((1, kd)),
                  const((qd, LANES)), const((LANES, qd))],
        out_specs=[pl.BlockSpec((tm, d), lambda i: (i, 0)),
                   pl.BlockSpec((tm, qd), lambda i: (i, 0)),
                   pl.BlockSpec((tm, 2 * kd), lambda i: (i, 0))],
        out_shape=[jax.ShapeDtypeStruct((n, d), BF16),
                   jax.ShapeDtypeStruct((n, qd), BF16),
                   jax.ShapeDtypeStruct((n, 2 * kd), F32)],
        compiler_params=_cparams(("parallel",)),
        name="attn_in",
    )(x, norm_w.reshape(1, d), sc, sh, w_at, cos, sa, sb, jnp.tile(q_norm_w, qd // AT_HD).reshape(1, qd),
      jnp.tile(k_norm_w, kd // AT_HD).reshape(1, kd), jnp.asarray(hsum, BF16), jnp.asarray(hsum.T, BF16))


def _attend(q, keys, vals, sinks_ref, col_valid, o_ref, n_kv, group, row0=0):
    tq = q.shape[0]
    n_heads = n_kv * group
    batch = 16
    for h0 in range(0, n_heads, batch):
        heads = range(h0, min(h0 + batch, n_heads))
        scores = []
        for h in heads:
            j = h // group
            s = _dot_nt(q[:, h * AT_HD:(h + 1) * AT_HD], keys[:, j * AT_HD:(j + 1) * AT_HD]) * (AT_HD ** -0.5)
            scores.append(s if col_valid is None else jnp.where(col_valid, s, -jnp.inf))
        probs = []
        for h, s in zip(heads, scores):
            sink = sinks_ref[h]
            m = jnp.maximum(jnp.max(s, axis=-1, keepdims=True), sink)
            e = jnp.exp(s - m)
            den = jnp.sum(e, axis=-1, keepdims=True) + jnp.exp(sink - m)
            probs.append((e / den).astype(BF16))
        for h, p in zip(heads, probs):
            j = h // group
            out = _dot(p, vals[:, j * AT_HD:(j + 1) * AT_HD]).astype(o_ref.dtype)
            o_ref[row0:row0 + tq, h * AT_HD:(h + 1) * AT_HD] = out


def _swa_prompt_kernel(sinks_ref, q_ref, *rest, n_kv, group, w_chunks, cps):
    kv_refs, o_ref = rest[:-1], rest[-1]
    kd = n_kv * AT_HD
    first = pl.program_id(1) * cps
    blocks = [r[...] for r in kv_refs]
    for ci in range(cps):
        window = blocks[ci:ci + w_chunks + 1]
        keys = jnp.concatenate([b[:, :kd] for b in window], axis=0).astype(BF16)
        vals = jnp.concatenate([b[:, kd:] for b in window], axis=0).astype(BF16)
        col_chunk = lax.broadcasted_iota(jnp.int32, (1, keys.shape[0]), 1) // CHUNK
        col_valid = (col_chunk + first + ci - w_chunks) >= 0
        _attend(q_ref[ci * CHUNK:(ci + 1) * CHUNK, :], keys, vals, sinks_ref, col_valid, o_ref, n_kv, group,
                row0=ci * CHUNK)


def _swa_prompt(qn, kvn, sinks, bsz, seq_len, n_kv):
    n, qd = qn.shape
    kd = n_kv * AT_HD
    nc = seq_len // CHUNK
    group = qd // AT_HD // n_kv
    w_chunks = 2
    cps = 2 if nc % 2 == 0 else 1
    steps = nc // cps

    def kv_spec(j):
        return pl.BlockSpec((CHUNK, 2 * kd), lambda b, c, s: (b * nc + jnp.maximum(c * cps - w_chunks + j, 0), 0))

    nkv_blocks = w_chunks + cps
    return pl.pallas_call(
        functools.partial(_swa_prompt_kernel, n_kv=n_kv, group=group, w_chunks=w_chunks, cps=cps),
        grid_spec=pltpu.PrefetchScalarGridSpec(
            num_scalar_prefetch=1,
            grid=(bsz, steps),
            in_specs=[pl.BlockSpec((cps * CHUNK, qd), lambda b, c, s: (b * steps + c, 0))]
                     + [kv_spec(j) for j in range(nkv_blocks)],
            out_specs=pl.BlockSpec((cps * CHUNK, qd), lambda b, c, s: (b * steps + c, 0))),
        out_shape=jax.ShapeDtypeStruct((n, qd), BF16),
        compiler_params=_cparams(("parallel", "arbitrary")),
        name="swa_prompt",
    )(sinks, qn, *([kvn] * nkv_blocks))


def _swa_sample_kernel(sinks_ref, q_ref, pk_ref, pv_ref, kv_ref, o_ref, *, n_kv, group):
    kd = n_kv * AT_HD
    kv = kv_ref[...]
    keys = jnp.concatenate([pk_ref[0], kv[:, :kd]], axis=0).astype(BF16)
    vals = jnp.concatenate([pv_ref[0], kv[:, kd:]], axis=0).astype(BF16)
    _attend(q_ref[...], keys, vals, sinks_ref, None, o_ref, n_kv, group)


def _swa_sample(qn, kvn, past_k, past_v, sinks, bsz, seq_len, n_kv):
    n, qd = qn.shape
    kd = n_kv * AT_HD
    window = past_k.shape[1]
    group = qd // AT_HD // n_kv
    return pl.pallas_call(
        functools.partial(_swa_sample_kernel, n_kv=n_kv, group=group),
        grid_spec=pltpu.PrefetchScalarGridSpec(
            num_scalar_prefetch=1,
            grid=(bsz,),
            in_specs=[pl.BlockSpec((seq_len, qd), lambda b, s: (b, 0)),
                      pl.BlockSpec((1, window, kd), lambda b, s: (b, 0, 0)),
                      pl.BlockSpec((1, window, kd), lambda b, s: (b, 0, 0)),
                      pl.BlockSpec((seq_len, 2 * kd), lambda b, s: (b, 0))],
            out_specs=pl.BlockSpec((seq_len, qd), lambda b, s: (b, 0))),
        out_shape=jax.ShapeDtypeStruct((n, qd), BF16),
        compiler_params=_cparams(("parallel",)),
        name="swa_sample",
    )(sinks, qn, past_k, past_v, kvn)


HI_MASK = np.uint32(0xFFFF0000)


def _pack_rows(x):
    half = x.shape[1] // 2
    bits = lambda a: lax.bitcast_convert_type(a.astype(BF16).astype(F32), jnp.uint32)
    word = (bits(x[:, half:]) & HI_MASK) | (bits(x[:, :half]) >> 16)
    return lax.bitcast_convert_type(word, jnp.int32)


def _unpack_words(w):
    u = lax.bitcast_convert_type(w, jnp.uint32)
    return lax.bitcast_convert_type(u << 16, F32), lax.bitcast_convert_type(u & HI_MASK, F32)


def _store_slabs(ref, words, row0=0):
    rows, width = words.shape
    nslab = width // LANES
    for s in range(nslab):
        ref[pl.ds(row0 + s, rows, stride=nslab), :] = words[:, s * LANES:(s + 1) * LANES]


def _load_rows(ref, rows, nslab, row0=0):
    lo, hi = [], []
    for s in range(nslab):
        a, b = _unpack_words(ref[pl.ds(row0 + s, rows, stride=nslab), :])
        lo.append(a)
        hi.append(b)
    return jnp.concatenate(lo + hi, axis=1).astype(BF16)


def _merge_kernel(ohg_ref, oat_ref, ga_ref, gb_ref, x_ref, g1_ref, whg_ref, wat_ref, wo_ref,
                  nw_ref, sc_ref, sh_ref, *rest):
    x1_ref, h2_ref = rest[-2:]
    tm = x_ref.shape[0]
    merged = ga_ref[...] * _dot(ohg_ref[...], whg_ref[...]) + gb_ref[...] * _dot(oat_ref[...], wat_ref[...])
    mix = _dot(merged.astype(BF16), wo_ref[...])
    x1 = x_ref[...] + _seq_rows(g1_ref[...], tm) * mix
    x1_ref[...] = x1
    y = x1 * lax.rsqrt(jnp.mean(x1 * x1, axis=-1, keepdims=True) + EPS) * nw_ref[...]
    h2 = y * (1.0 + _seq_rows(sc_ref[...], tm)) + _seq_rows(sh_ref[...], tm)
    _store_slabs(h2_ref, _pack_rows(h2))


def _merge(o_hg, o_at, gates, x, g1, w_hg_out, w_at_out, w_o, norm2_w, sc2, sh2, seq_len, tm, moe_rows, moe_buf):
    n, d = x.shape
    hd = o_hg.shape[1]
    ad = o_at.shape[1]
    nslab = d // (2 * LANES)
    tok0, n_all = moe_rows
    tile0 = tok0 // tm
    const = lambda shape: pl.BlockSpec(shape, lambda i: (0,) * len(shape), pipeline_mode=pl.Buffered(1))
    mod = lambda: _mod_spec(tm, seq_len, d)
    in_specs = [pl.BlockSpec((tm, hd), lambda i: (i, 0)),
                pl.BlockSpec((tm, ad), lambda i: (i, 0)),
                pl.BlockSpec((tm, d), lambda i: (i, 0)),
                pl.BlockSpec((tm, d), lambda i: (i, 1)),
                pl.BlockSpec((tm, d), lambda i: (i, 0)),
                mod(), const((hd, d)), const((ad, d)), const((d, d)), const((1, d)), mod(), mod()]
    args = [o_hg, o_at, gates, gates, x, g1, w_hg_out, w_at_out, w_o, norm2_w.reshape(1, d), sc2, sh2]
    aliases = {}
    if moe_buf is not None:
        in_specs.append(pl.BlockSpec(memory_space=pl.ANY))
        args.append(moe_buf)
        aliases = {len(args) - 1: 1}
    return pl.pallas_call(
        _merge_kernel,
        grid=(n // tm,),
        in_specs=in_specs,
        out_specs=[pl.BlockSpec((tm, d), lambda i: (i, 0)),
                   pl.BlockSpec((tm * nslab, LANES), lambda i: (tile0 + i, 0))],
        out_shape=[jax.ShapeDtypeStruct((n, d), F32),
                   jax.ShapeDtypeStruct((n_all * nslab, LANES), jnp.int32)],
        input_output_aliases=aliases,
        compiler_params=_cparams(("parallel",)),
        name="merge_norm2",
    )(*args)


def _router_kernel(h_ref, wr_ref, bias_ref, tri_ref, eidx_ref, wts_ref, rank_ref, cnt_ref, run_ref,
                   *, nslab, n_exp):
    tm = h_ref.shape[0] // nslab
    gsz = n_exp // N_GROUPS
    step = pl.program_id(0)

    @pl.when(step == 0)
    def _():
        run_ref[...] = jnp.zeros_like(run_ref)

    h = _load_rows(h_ref, tm, nslab)
    scores = _sigmoid(_dot_nt(wr_ref[...], h))
    choice = scores + bias_ref[...]
    neg = -jnp.inf
    row = lax.broadcasted_iota(jnp.int32, (gsz, tm), 0)

    gscore = []
    for gi in range(N_GROUPS):
        cg = choice[gi * gsz:(gi + 1) * gsz, :]
        m1 = jnp.max(cg, axis=0, keepdims=True)
        i1 = jnp.min(jnp.where(cg == m1, row, gsz), axis=0, keepdims=True)
        m2 = jnp.max(jnp.where(row == i1, neg, cg), axis=0, keepdims=True)
        gscore.append(m1 + m2)
    gs = jnp.concatenate(gscore, axis=0)
    grow = lax.broadcasted_iota(jnp.int32, (N_GROUPS, tm), 0)
    gsel = jnp.zeros((N_GROUPS, tm), F32)
    for _ in range(TOPK_GROUPS):
        gm = jnp.max(gs, axis=0, keepdims=True)
        gi = jnp.min(jnp.where(gs == gm, grow, N_GROUPS), axis=0, keepdims=True)
        hit = grow == gi
        gsel = jnp.where(hit, 1.0, gsel)
        gs = jnp.where(hit, neg, gs)
    masked = jnp.concatenate(
        [jnp.where(gsel[gi:gi + 1, :] > 0, choice[gi * gsz:(gi + 1) * gsz, :], neg) for gi in range(N_GROUPS)],
        axis=0)

    erow = lax.broadcasted_iota(jnp.int32, (n_exp, tm), 0)
    idxs, raw = [], []
    onehot = jnp.zeros((n_exp, tm), F32)
    for _ in range(TOP_K):
        m = jnp.max(masked, axis=0, keepdims=True)
        i = jnp.min(jnp.where(masked == m, erow, n_exp), axis=0, keepdims=True)
        hit = erow == i
        raw.append(jnp.sum(jnp.where(hit, scores, 0.0), axis=0, keepdims=True))
        masked = jnp.where(hit, neg, masked)
        onehot = jnp.where(hit, 1.0, onehot)
        idxs.append(i)
    total = raw[0]
    for r in raw[1:]:
        total = total + r
    before = _dot(onehot.astype(BF16), tri_ref[...]) + run_ref[:, 0:1]
    for kk in range(TOP_K):
        eidx_ref[kk:kk + 1, :] = idxs[kk]
        wts_ref[kk:kk + 1, :] = raw[kk] / total * ROUTED_SCALE
        rank_ref[kk:kk + 1, :] = jnp.sum(jnp.where(erow == idxs[kk], before, 0.0), axis=0, keepdims=True).astype(jnp.int32)
    run_ref[...] = run_ref[...] + jnp.sum(onehot, axis=1, keepdims=True)

    @pl.when(step == pl.num_programs(0) - 1)
    def _():
        cnt_ref[...] = run_ref[...].astype(jnp.int32)


def _slot_kernel(eidx_ref, rank_ref, pstart_ref, slot_ref):
    n_exp = pstart_ref.shape[0]
    tt = TOK_TILE
    erow = lax.broadcasted_iota(jnp.int32, (n_exp, tt), 0)
    pstart = pstart_ref[:, 0:1]
    for j in range(slot_ref.shape[0]):
        lanes = slice(j * tt, (j + 1) * tt)
        for kk in range(TOP_K):
            base = jnp.sum(jnp.where(erow == eidx_ref[kk:kk + 1, lanes], pstart, 0), axis=0, keepdims=True)
            slot_ref[j, kk:kk + 1, :] = base + rank_ref[kk:kk + 1, lanes]


def _slots(eidx, rank, pstart, n_tok):
    n_exp = pstart.shape[0]
    tt = TOK_TILE
    ntile = n_tok // tt
    per_step = max(g for g in range(1, 9) if ntile % g == 0)
    tok_spec = pl.BlockSpec((TOP_K, per_step * tt), lambda i: (0, i))
    return pl.pallas_call(
        _slot_kernel,
        grid=(ntile // per_step,),
        in_specs=[tok_spec, tok_spec, pl.BlockSpec((n_exp, LANES), lambda i: (0, 0))],
        out_specs=pl.BlockSpec((per_step, TOP_K, tt), lambda i: (i, 0, 0)),
        out_shape=jax.ShapeDtypeStruct((ntile, TOP_K, tt), jnp.int32),
        compiler_params=_cparams(("parallel",)),
        name="moe_slots",
    )(eidx, rank, jnp.broadcast_to(pstart[:, None], (n_exp, LANES)))


def _router(h2s, w_router, router_bias, n_tok, d, tm):
    n_exp = w_router.shape[1]
    nslab = d // (2 * LANES)
    tri = np.triu(np.ones((tm, tm), np.float32), 1)
    out_tok = lambda dt: jax.ShapeDtypeStruct((TOP_K, n_tok), dt)
    tok_spec = pl.BlockSpec((TOP_K, tm), lambda i: (0, i))
    return pl.pallas_call(
        functools.partial(_router_kernel, nslab=nslab, n_exp=n_exp),
        grid=(n_tok // tm,),
        in_specs=[pl.BlockSpec((tm * nslab, LANES), lambda i: (i, 0)),
                  pl.BlockSpec((n_exp, d), lambda i: (0, 0)),
                  pl.BlockSpec((n_exp, 1), lambda i: (0, 0)),
                  pl.BlockSpec((tm, tm), lambda i: (0, 0))],
        out_specs=[tok_spec, tok_spec, tok_spec, pl.BlockSpec((n_exp, LANES), lambda i: (0, 0))],
        out_shape=[out_tok(jnp.int32), out_tok(F32), out_tok(jnp.int32),
                   jax.ShapeDtypeStruct((n_exp, LANES), jnp.int32)],
        scratch_shapes=[pltpu.VMEM((n_exp, LANES), F32)],
        compiler_params=_cparams(("arbitrary",)),
        name="router_topk",
    )(h2s, w_router.T.astype(BF16), router_bias.reshape(n_exp, 1), jnp.asarray(tri, BF16))


def _dispatch_kernel(cnt_ref, pstart_ref, slot_hbm, h_ref, xs_hbm, slot_smem, zbuf, ssem, dsem, zsem, *, bm):
    i = pl.program_id(0)
    nslab = xs_hbm.shape[1]
    tt = h_ref.shape[0] // nslab
    n_exp = cnt_ref.shape[0]

    def slot_copy(tile, half):
        return pltpu.make_async_copy(slot_hbm.at[tile], slot_smem.at[half], ssem.at[half])

    @pl.when(i == 0)
    def _():
        slot_copy(0, 0).start()

    @pl.when(i == 0)
    def _():
        zbuf[...] = jnp.zeros_like(zbuf)

        def walk(e, start):
            cnt = cnt_ref[e]
            pad = lax.rem(bm - lax.rem(cnt, bm), bm)
            base = pstart_ref[e] + cnt
            size = bm // 2
            while size >= 1:
                take = (pad & size) != 0

                @pl.when(take)
                def _(base=base, size=size):
                    cp = pltpu.make_async_copy(zbuf.at[pl.ds(0, size)], xs_hbm.at[pl.ds(base, size)], zsem)
                    if start:
                        cp.start()
                    else:
                        cp.wait()

                base = base + jnp.where(take, size, 0)
                size //= 2

        def start_e(e, carry):
            walk(e, True)
            return carry

        def wait_e(e, carry):
            walk(e, False)
            return carry

        lax.fori_loop(0, n_exp, start_e, 0)
        lax.fori_loop(0, n_exp, wait_e, 0)

    half = lax.rem(i, 2)
    slot_copy(i, half).wait()

    @pl.when(i + 1 < pl.num_programs(0))
    def _():
        slot_copy(i + 1, 1 - half).start()

    def row_copy(t, kk):
        src = h_ref.at[pl.ds(pl.multiple_of(t * nslab, nslab), nslab), :]
        return pltpu.make_async_copy(src, xs_hbm.at[slot_smem[half, kk, t]], dsem)

    def issue(t, carry):
        for kk in range(TOP_K):
            row_copy(t, kk).start(priority=kk % 2)
        return carry

    lax.fori_loop(0, tt, issue, 0)

    def drain(t, carry):
        for kk in range(TOP_K):
            row_copy(t, kk).wait()
        return carry

    lax.fori_loop(0, tt, drain, 0)


def _dispatch(counts, pstart, slots, h2s, n_tok, nslab, n_rows, bm):
    return pl.pallas_call(
        functools.partial(_dispatch_kernel, bm=bm),
        grid_spec=pltpu.PrefetchScalarGridSpec(
            num_scalar_prefetch=2,
            grid=(n_tok // TOK_TILE,),
            in_specs=[pl.BlockSpec(memory_space=pl.ANY),
                      pl.BlockSpec((TOK_TILE * nslab, LANES), lambda i, c, p: (i, 0))],
            out_specs=pl.BlockSpec(memory_space=pl.ANY),
            scratch_shapes=[pltpu.SMEM((2, TOP_K, TOK_TILE), jnp.int32),
                            pltpu.VMEM((bm // 2, nslab, LANES), jnp.int32),
                            pltpu.SemaphoreType.DMA((2,)), pltpu.SemaphoreType.DMA, pltpu.SemaphoreType.DMA]),
        out_shape=jax.ShapeDtypeStruct((n_rows, nslab, LANES), jnp.int32),
        compiler_params=_cparams(("arbitrary",)),
        name="moe_dispatch",
    )(counts, pstart, slots, h2s)


def _experts_kernel(vis_ref, nv_ref, pstart_ref, pcnt_ref, nu_ref, xs_hbm, w1_hbm, w3_hbm, w2_hbm, os_hbm,
                    xbuf, obuf, w1s, w3s, w2s, w1b, w3b, w2b, xsem, osem, wsem, *, nslab, bm):
    j = pl.program_id(0)
    rows = bm * nslab
    n_used = nu_ref[0]
    n_visit = nv_ref[0]

    def weight_copies(jj, slot):
        e = vis_ref[jj]
        return (pltpu.make_async_copy(w1_hbm.at[e], w1s.at[slot], wsem.at[slot, 0]),
                pltpu.make_async_copy(w3_hbm.at[e], w3s.at[slot], wsem.at[slot, 1]),
                pltpu.make_async_copy(w2_hbm.at[e], w2s.at[slot], wsem.at[slot, 2]))

    def x_copy(g, slot):
        src = xs_hbm.at[pl.ds(pl.multiple_of(g * rows, rows), rows), :]
        return pltpu.make_async_copy(src, xbuf.at[pl.ds(pl.multiple_of(slot * rows, rows), rows), :], xsem.at[slot])

    def o_copy(g, slot):
        dst = os_hbm.at[pl.ds(pl.multiple_of(g * rows, rows), rows), :]
        return pltpu.make_async_copy(obuf.at[pl.ds(pl.multiple_of(slot * rows, rows), rows), :], dst, osem.at[slot])

    @pl.when(j == 0)
    def _():
        x_copy(0, 0).start()
        for cp in weight_copies(0, 0):
            cp.start(priority=1)

    @pl.when(j < n_visit)
    def _():
        e = vis_ref[j]
        g0 = pstart_ref[e] // bm
        wslot = lax.rem(j, 2)
        for cp in weight_copies(j, wslot):
            cp.wait()

        @pl.when(j + 1 < n_visit)
        def _():
            for cp in weight_copies(j + 1, 1 - wslot):
                cp.start(priority=1)

        w1b[...] = w1s[wslot].astype(BF16)
        w3b[...] = w3s[wslot].astype(BF16)
        w2b[...] = w2s[wslot].astype(BF16)

        def block(b, carry):
            g = g0 + b
            slot = lax.rem(g, 2)
            x_copy(g, slot).wait()

            @pl.when(g + 1 < n_used)
            def _():
                x_copy(g + 1, 1 - slot).start()

            @pl.when(g >= 2)
            def _():
                o_copy(g - 2, slot).wait()

            x = _load_rows(xbuf, bm, nslab, slot * rows)
            a = (_silu(_dot(x, w1b[...])) * _dot(x, w3b[...])).astype(BF16)
            _store_slabs(obuf, _pack_rows(_dot(a, w2b[...])), slot * rows)
            o_copy(g, slot).start()
            return carry

        lax.fori_loop(0, pcnt_ref[e] // bm, block, 0)

    @pl.when(j == pl.num_programs(0) - 1)
    def _():
        @pl.when(n_used >= 2)
        def _():
            o_copy(n_used - 2, lax.rem(n_used, 2)).wait()

        o_copy(n_used - 1, lax.rem(n_used - 1, 2)).wait()


def _experts(visit, n_visit, pstart, pcounts, n_used, xs2, w1, w3, w2, d, bm):
    n_exp, _, de = w1.shape
    nslab = d // (2 * LANES)
    hbm = pl.BlockSpec(memory_space=pl.ANY)
    return pl.pallas_call(
        functools.partial(_experts_kernel, nslab=nslab, bm=bm),
        grid_spec=pltpu.PrefetchScalarGridSpec(
            num_scalar_prefetch=5,
            grid=(n_exp,),
            in_specs=[hbm, hbm, hbm, hbm],
            out_specs=hbm,
            scratch_shapes=[pltpu.VMEM((2 * bm * nslab, LANES), jnp.int32),
                            pltpu.VMEM((2 * bm * nslab, LANES), jnp.int32),
                            pltpu.VMEM((2, d, de), F32), pltpu.VMEM((2, d, de), F32), pltpu.VMEM((2, de, d), F32),
                            pltpu.VMEM((d, de), BF16), pltpu.VMEM((d, de), BF16), pltpu.VMEM((de, d), BF16),
                            pltpu.SemaphoreType.DMA((2,)), pltpu.SemaphoreType.DMA((2,)),
                            pltpu.SemaphoreType.DMA((2, 3))]),
        out_shape=jax.ShapeDtypeStruct(xs2.shape, jnp.int32),
        compiler_params=_cparams(("arbitrary",)),
        name="routed_experts",
    )(visit, n_visit, pstart, pcounts, n_used, xs2, w1, w3, w2)


def _combine_kernel(slot_hbm, os_hbm, wt_ref, h_ref, ws1_ref, ws3_ref, ws2_ref, x1_ref, g2_ref, o_ref,
                    slot0, slot1, buf0, buf1, sh_ref, ssem, gsem, *, nslab, tile0):
    i = pl.program_id(0)
    tt = x1_ref.shape[0]
    last = pl.num_programs(0) - 1
    slots = (slot0, slot1)
    bufs = (buf0, buf1)

    def row_copy(half, t, kk, row):
        dst = bufs[half].at[pl.ds((kk * tt + t) * nslab, nslab), :]
        return pltpu.make_async_copy(os_hbm.at[row], dst, gsem.at[half])

    def slot_copy(tile, half):
        return pltpu.make_async_copy(slot_hbm.at[tile0 + tile], slots[half], ssem.at[half])

    def drain(half):
        def body(t, carry):
            for kk in range(TOP_K):
                row_copy(half, t, kk, 0).wait()
            return carry

        lax.fori_loop(0, tt, body, 0)

    @pl.when(i == 0)
    def _():
        first = slot_copy(0, 0)
        first.start()
        first.wait()

        def issue(t, carry):
            for kk in range(TOP_K):
                row_copy(0, t, kk, slot0[kk, t]).start(priority=kk % 2)
            return carry

        lax.fori_loop(0, tt, issue, 0)
        slot_copy(jnp.minimum(1, last), 1).start()

    def step(half):
        other = 1 - half
        slot_copy(jnp.minimum(i + 1, last), other).wait()
        slot_copy(jnp.minimum(i + 2, last), half).start()
        drain(half)

        h = _load_rows(h_ref, tt, nslab)
        a = (_silu(_dot(h, ws1_ref[...])) * _dot(h, ws3_ref[...])).astype(BF16)
        sh_ref[...] = _dot(a, ws2_ref[...])

        wt = wt_ref[...]
        g2 = _seq_rows(g2_ref[...], tt)
        hw = nslab * LANES
        per_slab = tt // nslab
        for s in range(nslab):
            y_lo = sh_ref[:, s * LANES:(s + 1) * LANES]
            y_hi = sh_ref[:, hw + s * LANES:hw + (s + 1) * LANES]
            for kk in range(TOP_K):
                lo, hi = _unpack_words(bufs[half][pl.ds(kk * tt * nslab + s, tt, stride=nslab), :])
                y_lo = y_lo + wt[:, kk:kk + 1] * lo
                y_hi = y_hi + wt[:, kk:kk + 1] * hi
            for y, c0 in ((y_lo, s * LANES), (y_hi, hw + s * LANES)):
                cols = slice(c0, c0 + LANES)
                o_ref[:, cols] = x1_ref[:, cols] + g2[:, cols] * y
            for t in range(s * per_slab, (s + 1) * per_slab):
                for kk in range(TOP_K):
                    row_copy(other, t, kk, slots[other][kk, t]).start(priority=kk % 2)

        @pl.when(i == last)
        def _():
            slot_copy(last, half).wait()
            drain(other)

    for half in (0, 1):
        @pl.when(lax.rem(i, 2) == half)
        def _(half=half):
            step(half)


def _combine(slots, os3, wts_t, h2s, ws1, ws3, ws2, x1, g2, seq_len, tok0):
    n, d = x1.shape
    nslab = d // (2 * LANES)
    tt = TOK_TILE
    tile0 = tok0 // tt
    const = lambda a: pl.BlockSpec(a.shape, lambda i: (0, 0), pipeline_mode=pl.Buffered(1))
    return pl.pallas_call(
        functools.partial(_combine_kernel, nslab=nslab, tile0=tile0),
        grid=(n // tt,),
        in_specs=[pl.BlockSpec(memory_space=pl.ANY),
                  pl.BlockSpec(memory_space=pl.ANY),
                  pl.BlockSpec((tt, TOP_K), lambda i: (tile0 + i, 0)),
                  pl.BlockSpec((tt * nslab, LANES), lambda i: (tile0 + i, 0)),
                  const(ws1), const(ws3), const(ws2),
                  pl.BlockSpec((tt, d), lambda i: (i, 0)),
                  _mod_spec(tt, seq_len, d)],
        out_specs=pl.BlockSpec((tt, d), lambda i: (i, 0)),
        out_shape=jax.ShapeDtypeStruct((n, d), F32),
        scratch_shapes=[pltpu.SMEM((TOP_K, tt), jnp.int32), pltpu.SMEM((TOP_K, tt), jnp.int32),
                        pltpu.VMEM((TOP_K * tt * nslab, LANES), jnp.int32),
                        pltpu.VMEM((TOP_K * tt * nslab, LANES), jnp.int32),
                        pltpu.VMEM((tt, d), F32),
                        pltpu.SemaphoreType.DMA((2,)), pltpu.SemaphoreType.DMA((2,))],
        compiler_params=_cparams(("arbitrary",)),
        name="moe_combine",
    )(slots, os3, wts_t, h2s, ws1, ws3, ws2, x1, g2)


def _mixer(x, mod, positions, s0, past_k, past_v, lb, p, w, moe_rows, moe_buf):
    bsz, seq_len, d = x.shape
    n = bsz * seq_len
    sh1, sc1, g1, sh2, sc2, g2 = mod
    tm = 256 if n % 256 == 0 else n
    x2 = x.reshape(n, d)
    n_kv = p["n_kv"]
    kd = n_kv * AT_HD
    qd = w["w_in_at"].shape[1] - 2 * kd
    h1, qn, kvn = _attn_in(x2, p["norm1_w"], sc1, sh1, w["w_in_at"], positions, p["q_norm_w"], p["k_norm_w"],
                           seq_len, qd, kd, min(seq_len, 512))
    tmm = 1024 if n % 1024 == 0 else tm
    gates = _matmul(h1, w["w_in_gate"], tmm, min(2 * d, 1024), act="sigmoid", out_dtype=BF16, name="w_in_gates")

    o_hg, s_new = _hgrn(h1, w["w_in_hg"], lb, p["hg_norm_w"], s0, bsz, seq_len)

    if past_k is None:
        o_at = _swa_prompt(qn, kvn, p["attn_sinks"], bsz, seq_len, n_kv)
    else:
        o_at = _swa_sample(qn, kvn, past_k, past_v, p["attn_sinks"], bsz, seq_len, n_kv)

    x1, h2s = _merge(o_hg, o_at, gates, x2, g1, w["w_hg_out"], w["w_at_out"], w["w_o"],
                     p["norm2_w"], sc2, sh2, seq_len, tm, moe_rows, moe_buf)
    kv3 = kvn.reshape(bsz, seq_len, 2 * kd)
    return x1, h2s, s_new, kv3[:, :, :kd], kv3[:, :, kd:]


def kernel(x_prompt, x_sample, cache_k, cache_v, state_hgrn, c_prompt, c_sample, norm1_w, norm2_w, w_ada, b_ada,
           w_in, hg_lower_bounds, hg_norm_w, q_norm_w, k_norm_w, attn_sinks, w_hg_out, w_at_out, w_o, w_router,
           router_bias, w_exp_gate, w_exp_up, w_exp_down, w_sh_gate, w_sh_up, w_sh_down):
    depth = norm1_w.shape[0]
    assert depth == 1, "single trunk layer"
    bp, tp, d = x_prompt.shape
    bs, ts, _ = x_sample.shape
    window, n_kv = cache_k.shape[2], cache_k.shape[3]
    kd = n_kv * AT_HD
    hg_dim = w_hg_out.shape[1]
    qd = w_at_out.shape[1]
    n_exp = w_router.shape[2]
    nslab = d // (2 * LANES)
    l = 0

    lbs = jnp.cumsum(jax.nn.softmax(hg_lower_bounds.astype(F32), axis=0), axis=0)
    win = w_in[l]
    w = {
        "w_in_hg": win[:, :4 * hg_dim].astype(BF16),
        "w_in_at": win[:, 4 * hg_dim:4 * hg_dim + qd + 2 * kd].astype(BF16),
        "w_in_gate": win[:, 4 * hg_dim + qd + 2 * kd:].astype(BF16),
        "w_hg_out": w_hg_out[l].astype(BF16),
        "w_at_out": w_at_out[l].astype(BF16),
        "w_o": w_o[l].astype(BF16),
    }
    p = {"norm1_w": norm1_w[l], "norm2_w": norm2_w[l], "hg_norm_w": hg_norm_w[l], "q_norm_w": q_norm_w[l],
         "k_norm_w": k_norm_w[l], "attn_sinks": attn_sinks[l], "n_kv": n_kv}

    c_all = jnp.concatenate([c_prompt, c_sample], axis=0)
    mod_all = _ada(c_all, w_ada[l], b_ada[l])
    mod_all = mod_all.reshape(bp + bs, 6, 1, d)
    mod_p = tuple(mod_all[:bp, j] for j in range(6))
    mod_s = tuple(mod_all[bp:, j] for j in range(6))

    pos_p = jnp.arange(tp, dtype=jnp.int32)
    pos_s = PAST_LEN + jnp.arange(ts, dtype=jnp.int32)
    s0_p = jnp.zeros((bp,) + state_hgrn.shape[2:], F32)
    n_p, n_s = bp * tp, bs * ts
    n_tok = n_p + n_s
    x1_p, h2_p, sp, kp, vp = _mixer(x_prompt, mod_p, pos_p, s0_p, None, None, lbs[l], p, w, (0, n_tok), None)
    pk = cache_k[l].reshape(bs, window, kd)
    pv = cache_v[l].reshape(bs, window, kd)
    x1_s, h2s, ss, ks, vs = _mixer(x_sample, mod_s, pos_s, state_hgrn[l], pk, pv, lbs[l], p, w, (n_p, n_tok), h2_p)

    tr = 256 if n_tok % 256 == 0 else TOK_TILE
    eidx, wts, rank, counts = _router(h2s, w_router[l], router_bias[l], n_tok, d, tr)

    bm = MOE_ROWS
    counts = counts[:, 0]
    pcounts = (counts + bm - 1) // bm * bm
    pend = jnp.cumsum(pcounts)
    pstart = pend - pcounts
    nb = -(-(n_tok * TOP_K) // bm) + n_exp
    slots = _slots(eidx, rank, pstart, n_tok)
    n_used = (pend[-1:] // bm).astype(jnp.int32)

    xs = _dispatch(counts, pstart, slots, h2s, n_tok, nslab, nb * bm, bm)
    visit = jnp.argsort(counts == 0, stable=True).astype(jnp.int32)
    n_visit = jnp.sum(counts > 0).astype(jnp.int32).reshape(1)
    os_ = _experts(visit, n_visit, pstart, pcounts, n_used, xs.reshape(nb * bm * nslab, LANES),
                   w_exp_gate[l], w_exp_up[l], w_exp_down[l], d, bm)
    os3 = os_.reshape(nb * bm, nslab, LANES)
    wts_t = wts.T
    ws = (w_sh_gate[l].astype(BF16), w_sh_up[l].astype(BF16), w_sh_down[l].astype(BF16))
    y_p = _combine(slots, os3, wts_t, h2s, *ws, x1_p, mod_p[5], tp, 0)
    y_s = _combine(slots, os3, wts_t, h2s, *ws, x1_s, mod_s[5], ts, n_p)

    def cache_out(a, b_, t):
        return a[:, t - window:].reshape(1, b_, window, n_kv, AT_HD)

    new_k_p = cache_out(kp, bp, tp)
    new_v_p = cache_out(vp, bp, tp)
    keys_s = jnp.concatenate([pk, ks], axis=1)
    vals_s = jnp.concatenate([pv, vs], axis=1)
    new_k_s = cache_out(keys_s, bs, window + ts)
    new_v_s = cache_out(vals_s, bs, window + ts)
    return (y_p.reshape(bp, tp, d), y_s.reshape(bs, ts, d), new_k_p, new_v_p, sp[None],
            new_k_s, new_v_s, ss[None])
```
